```python
import math
import jax
import jax.numpy as jnp
from jax import lax
import numpy as np

D_MODEL = 1024
BATCH = 8
SEQ = 2048
DEPTH = 2

GROUP_WIDTH = D_MODEL // 4
HEAD_DIM = 64
N_HEADS_A = GROUP_WIDTH // HEAD_DIM
DIFF_DH = HEAD_DIM // 2
N_HEADS_B = GROUP_WIDTH // HEAD_DIM
CONV_CH = GROUP_WIDTH
CONV_GROUPS = 4
CONV_K = 31
N_HEADS_D = GROUP_WIDTH // HEAD_DIM
DILATED_CFG = ((128, 1), (512, 4), (2048, 16))
GRID_W = 64
NA_KH_MAX = 8
NA_KW = 16
ROPE_THETA = 10000.0
Q_BLOCK = 128
N_GROUPS = 4
EXPERTS_PER_GROUP = 8
N_EXPERTS = N_GROUPS * EXPERTS_PER_GROUP
TOP_K_INNER = 2
D_EXPERT = D_MODEL // 2
MOE_BLOCK = 256
EPS = 1e-6
NEG_INF = -1e30

W_A = 3 * N_HEADS_A * HEAD_DIM
W_B = 3 * N_HEADS_B * HEAD_DIM
W_C = 2 * CONV_CH
W_D = 3 * N_HEADS_D * HEAD_DIM
P_IN = W_A + W_B + W_C + W_D

kernel_name = "hybrid_parallel_mixer_hmoe_block"

F32 = jnp.float32


def rms_norm(x, g):
    xf = x.astype(F32)
    y = xf * lax.rsqrt(jnp.mean(xf * xf, axis=-1, keepdims=True) + EPS)
    return (y * g.astype(F32)).astype(x.dtype)


def rotary(x):
    s, d = x.shape[1], x.shape[-1]
    inv = ROPE_THETA ** (-jnp.arange(0, d, 2, dtype=F32) / d)
    ang = jnp.arange(s, dtype=F32)[:, None] * inv[None, :]
    shp = (1, s) + (1,) * (x.ndim - 3) + (d // 2,)
    cos, sin = jnp.cos(ang).reshape(shp), jnp.sin(ang).reshape(shp)
    xf = x.astype(F32)
    x1, x2 = xf[..., : d // 2], xf[..., d // 2:]
    return jnp.concatenate([x1 * cos - x2 * sin, x2 * cos + x1 * sin], axis=-1).astype(x.dtype)


def diff_attention(q, k, v, lam_params, subln_g, layer_idx):
    b, s, h = q.shape[:3]
    lam_init = 0.8 - 0.6 * math.exp(-0.3 * layer_idx)
    lp = lam_params.astype(F32)
    lam = jnp.exp(jnp.sum(lp[0] * lp[1])) - jnp.exp(jnp.sum(lp[2] * lp[3])) + lam_init
    q, k = rotary(q), rotary(k)
    scale = DIFF_DH ** -0.5
    nqb = s // Q_BLOCK
    qb = jnp.moveaxis(q.reshape(b, nqb, Q_BLOCK, h, 2, DIFF_DH), 1, 0)

    def block(qi):
        sc = jnp.einsum('bqhce,bkhce->bhcqk', qi, k).astype(F32) * scale
        p = jax.nn.softmax(sc, axis=-1)
        a = p[:, :, 0] - lam * p[:, :, 1]
        return jnp.einsum('bhqk,bkhe->bqhe', a.astype(v.dtype), v)

    o = jnp.moveaxis(lax.map(block, qb), 0, 1).reshape(b, s, h, 2 * DIFF_DH)
    o = rms_norm(o, subln_g) * (1.0 - lam_init)
    return o.reshape(b, s, h * 2 * DIFF_DH)


def neighborhood_attention(q, k, v, rpb):
    b, s, h, e = q.shape
    rows = s // GRID_W
    kh = min(NA_KH_MAX, rows)
    kw = NA_KW
    qg = q.reshape(b, rows, GRID_W, h, e)
    kg = k.reshape(b, rows, GRID_W, h, e)
    vg = v.reshape(b, rows, GRID_W, h, e)
    ri = jnp.arange(rows)
    ci = jnp.arange(GRID_W)
    row_start = jnp.clip(ri - kh // 2, 0, rows - kh)
    row_idx = row_start[:, None] + jnp.arange(kh)[None, :]
    col_start = jnp.clip(ci - kw // 2, 0, GRID_W - kw)
    col_ok = (ci[None, :] >= col_start[:, None]) & (ci[None, :] < col_start[:, None] + kw)
    kr = kg[:, row_idx]
    vr = vg[:, row_idx]
    dr = row_idx - ri[:, None] + NA_KH_MAX - 1
    dc = jnp.clip(ci[None, :] - ci[:, None], -(kw - 1), kw - 1) + kw - 1
    bias = rpb.astype(F32)[:, dr[:, None, :, None], dc[None, :, None, :]]
    sc = jnp.einsum('bijhe,biawhe->bhijaw', qg, kr).astype(F32) * (e ** -0.5) + bias[None]
    sc = jnp.where(col_ok[:, None, :], sc, NEG_INF)
    p = jax.nn.softmax(sc.reshape(b, h, rows, GRID_W, kh * GRID_W), axis=-1).reshape(sc.shape)
    o = jnp.einsum('bhijaw,biawhe->bijhe', p.astype(v.dtype), vr)
    return o.reshape(b, s, h * e)


def conformer_conv(u, w_dw, b_dw, g_n, b_n):
    a, g = jnp.split(u, 2, axis=-1)
    z = a * jax.nn.sigmoid(g)
    z = lax.conv_general_dilated(
        z, w_dw[:, None, :].astype(z.dtype), window_strides=(1,),
        padding=[(CONV_K // 2, CONV_K // 2)], dimension_numbers=('NWC', 'WIO', 'NWC'),
        feature_group_count=CONV_CH) + b_dw.astype(z.dtype)
    b, s, ch = z.shape
    zf = z.astype(F32).reshape(b, s, CONV_GROUPS, ch // CONV_GROUPS)
    mu = jnp.mean(zf, axis=-1, keepdims=True)
    var = jnp.mean(jnp.square(zf - mu), axis=-1, keepdims=True)
    zn = ((zf - mu) * lax.rsqrt(var + EPS)).reshape(b, s, ch) * g_n.astype(F32) + b_n.astype(F32)
    return jax.nn.silu(zn).astype(u.dtype)


def dilated_branch(q, k, v, dilation, n_side):
    b, s, h, e = q.shape
    L = s // dilation
    blk = n_side
    nb = -(-L // blk)
    Lp = nb * blk

    def split(t):
        return t.reshape(b, L, dilation, h, e).transpose(0, 2, 1, 3, 4).reshape(b * dilation, L, h, e)

    qs, ks, vs = split(q), split(k), split(v)
    qb = jnp.pad(qs, ((0, 0), (0, Lp - L), (0, 0), (0, 0))).reshape(-1, nb, blk, h, e)

    def band(t):
        tp = jnp.pad(t, ((0, 0), (blk, Lp - L + blk), (0, 0), (0, 0))).reshape(-1, nb + 2, blk, h, e)
        return jnp.concatenate([tp[:, :-2], tp[:, 1:-1], tp[:, 2:]], axis=2)

    kb, vb = band(ks), band(vs)
    qpos = jnp.arange(nb)[:, None] * blk + jnp.arange(blk)[None, :]
    kpos = jnp.arange(nb)[:, None] * blk - blk + jnp.arange(3 * blk)[None, :]
    rel = kpos[:, None, :] - qpos[:, :, None]
    ok = (jnp.abs(rel) <= n_side) & (kpos[:, None, :] >= 0) & (kpos[:, None, :] < L)
    sc = jnp.einsum('ncqhe,nckhe->nhcqk', qb, kb).astype(F32) * (e ** -0.5)
    sc = jnp.where(ok, sc, NEG_INF)
    m = jnp.max(sc, axis=-1, keepdims=True)
    pe = jnp.exp(sc - m)
    l = jnp.sum(pe, axis=-1, keepdims=True)
    o = jnp.einsum('nhcqk,nckhe->ncqhe', (pe / l).astype(v.dtype), vb)
    lse = (m + jnp.log(l))[..., 0]

    def merge(t):
        t = t.reshape((b, dilation, Lp) + t.shape[2:])[:, :, :L]
        t = jnp.swapaxes(t, 1, 2)
        return t.reshape((b, s) + t.shape[3:])

    o = merge(o.reshape(b * dilation, Lp, h, e))
    lse = merge(jnp.transpose(lse, (0, 2, 3, 1)).reshape(b * dilation, Lp, h))
    return o, lse


def dilated_attention(q, k, v):
    b, s, h, e = q.shape
    q, k = rotary(q), rotary(k)
    outs, lses = [], []
    for window, dil in DILATED_CFG:
        o, lse = dilated_branch(q, k, v, dil, window // (2 * dil))
        outs.append(o)
        lses.append(lse)
    wts = jax.nn.softmax(jnp.stack(lses, axis=0), axis=0)
    o = jnp.einsum('gbsh,gbshe->bshe', wts, jnp.stack(outs, axis=0).astype(F32))
    return o.astype(v.dtype).reshape(b, s, h * e)


def expert_dispatch(xt, eid, gates, w_gate, w_up, w_down):
    T, d = xt.shape
    A = T * TOP_K_INNER
    flat_e = eid.reshape(-1)
    flat_tok = jnp.repeat(jnp.arange(T), TOP_K_INNER, total_repeat_length=A)
    flat_g = gates.reshape(-1)
    order = jnp.argsort(flat_e)
    se, stok, sg = flat_e[order], flat_tok[order], flat_g[order]
    counts = jnp.bincount(flat_e, length=N_EXPERTS)
    seg_start = jnp.cumsum(counts) - counts
    padded = (counts + MOE_BLOCK - 1) // MOE_BLOCK * MOE_BLOCK
    pad_end = jnp.cumsum(padded)
    pad_start = pad_end - padded
    dest = pad_start[se] + jnp.arange(A) - seg_start[se]
    nblk = A // MOE_BLOCK + N_EXPERTS
    buf_tok = jnp.zeros((nblk * MOE_BLOCK,), jnp.int32).at[dest].set(stok.astype(jnp.int32))
    blk_e = jnp.minimum(jnp.searchsorted(pad_end, jnp.arange(nblk) * MOE_BLOCK, side='right'), N_EXPERTS - 1)
    xb = xt[buf_tok].reshape(nblk, MOE_BLOCK, d)

    def run(args):
        xe, e_i = args
        hdn = jax.nn.silu(xe @ w_gate[e_i]) * (xe @ w_up[e_i])
        return hdn @ w_down[e_i]

    yb = lax.map(run, (xb, blk_e)).reshape(nblk * MOE_BLOCK, d)
    contrib = yb[dest].astype(F32) * sg[:, None]
    return jnp.zeros((T, d), F32).at[stok].add(contrib).astype(xt.dtype)


def hier_moe(h, w_rg, w_re, w_gate, w_up, w_down):
    b, s, d = h.shape
    xt = h.reshape(-1, d)
    lg = (xt @ w_rg).astype(F32)
    pg = jax.nn.softmax(lg, axis=-1)
    grp = jnp.argmax(lg, axis=-1)
    p_grp = jnp.take_along_axis(pg, grp[:, None], axis=-1)
    le = jnp.einsum('td,gde->tge', xt, w_re).astype(F32)
    le = jnp.take_along_axis(le, grp[:, None, None], axis=1)[:, 0]
    top_v, top_i = lax.top_k(le, TOP_K_INNER)
    gates = p_grp * jax.nn.softmax(top_v, axis=-1)
    eid = grp[:, None] * EXPERTS_PER_GROUP + top_i
    return expert_dispatch(xt, eid, gates, w_gate, w_up, w_down).reshape(b, s, d)


def setup_inputs(seed: int = 0) -> dict:
    key = jax.random.key(seed)
    ks = jax.random.split(key, 24)
    L, D = DEPTH, D_MODEL
    nrm = lambda k, shp, sc: jax.random.normal(k, shp, F32) * sc
    return {
        "x": nrm(ks[0], (BATCH, SEQ, D), 1.0),
        "c": nrm(ks[1], (BATCH, D), 1.0),
        "w_ada": nrm(ks[2], (L, D, 6 * D), 0.5 * D ** -0.5),
        "b_ada": nrm(ks[3], (L, 6 * D), 0.01),
        "g_norm1": 1.0 + nrm(ks[4], (L, D), 0.05),
        "g_norm2": 1.0 + nrm(ks[5], (L, D), 0.05),
        "w_in": nrm(ks[6], (L, D, P_IN), D ** -0.5),
        "diff_lambda": nrm(ks[7], (L, 4, DIFF_DH), 0.1),
        "diff_subln": 1.0 + nrm(ks[8], (L, 2 * DIFF_DH), 0.05),
        "na_rpb": nrm(ks[9], (L, N_HEADS_B, 2 * NA_KH_MAX - 1, 2 * NA_KW - 1), 0.05),
        "conv_dw": nrm(ks[10], (L, CONV_K, CONV_CH), CONV_K ** -0.5),
        "conv_b": nrm(ks[11], (L, CONV_CH), 0.01),
        "conv_norm_g": 1.0 + nrm(ks[12], (L, CONV_CH), 0.05),
        "conv_norm_b": nrm(ks[13], (L, CONV_CH), 0.01),
        "w_out": nrm(ks[14], (L, D, D), D ** -0.5),
        "w_router_group": nrm(ks[15], (L, D, N_GROUPS), D ** -0.5),
        "w_router_expert": nrm(ks[16], (L, N_GROUPS, D, EXPERTS_PER_GROUP), D ** -0.5),
        "w_exp_gate": nrm(ks[17], (L, N_EXPERTS, D, D_EXPERT), D ** -0.5),
        "w_exp_up": nrm(ks[18], (L, N_EXPERTS, D, D_EXPERT), D ** -0.5),
        "w_exp_down": nrm(ks[19], (L, N_EXPERTS, D_EXPERT, D), D_EXPERT ** -0.5),
        "g_final": 1.0 + nrm(ks[20], (D,), 0.05),
    }


def reference(x, c, w_ada, b_ada, g_norm1, g_norm2, w_in, diff_lambda, diff_subln, na_rpb,
              conv_dw, conv_b, conv_norm_g, conv_norm_b, w_out, w_router_group, w_router_expert,
              w_exp_gate, w_exp_up, w_exp_down, g_final):
    b, s, d = x.shape
    ca = jax.nn.silu(c)
    for l in range(DEPTH):
        mod = ca @ w_ada[l] + b_ada[l]
        sh1, sc1, gt1, sh2, sc2, gt2 = [m[:, None, :] for m in jnp.split(mod, 6, axis=-1)]
        h = rms_norm(x, g_norm1[l]) * (1.0 + sc1) + sh1
        proj = h @ w_in[l]
        pa, pb, pc, pd = jnp.split(proj, [W_A, W_A + W_B, W_A + W_B + W_C], axis=-1)
        qa, ka, va = jnp.split(pa, 3, axis=-1)
        ya = diff_attention(qa.reshape(b, s, N_HEADS_A, 2, DIFF_DH),
                            ka.reshape(b, s, N_HEADS_A, 2, DIFF_DH),
                            va.reshape(b, s, N_HEADS_A, HEAD_DIM),
                            diff_lambda[l], diff_subln[l], l)
        qb_, kb_, vb_ = [t.reshape(b, s, N_HEADS_B, HEAD_DIM) for t in jnp.split(pb, 3, axis=-1)]
        yb = neighborhood_attention(qb_, kb_, vb_, na_rpb[l])
        yc = conformer_conv(pc, conv_dw[l], conv_b[l], conv_norm_g[l], conv_norm_b[l])
        qd, kd, vd = [t.reshape(b, s, N_HEADS_D, HEAD_DIM) for t in jnp.split(pd, 3, axis=-1)]
        yd = dilated_attention(qd, kd, vd)
        mix = jnp.concatenate([ya, yb, yc, yd], axis=-1) @ w_out[l]
        x = x + gt1 * mix
        h2 = rms_norm(x, g_norm2[l]) * (1.0 + sc2) + sh2
        x = x + gt2 * hier_moe(h2, w_router_group[l], w_router_expert[l],
                               w_exp_gate[l], w_exp_up[l], w_exp_down[l])
    return rms_norm(x, g_final)
```

```python
import functools
import math

import numpy as np
import jax
import jax.numpy as jnp
from jax import lax
from jax.experimental import pallas as pl
from jax.experimental.pallas import tpu as pltpu

F32 = jnp.float32
BF16 = jnp.bfloat16

D_MODEL = 1024
GROUP_WIDTH = 256
HEAD_DIM = 64
N_HEADS = 4
DIFF_DH = 32
CONV_K = 31
CONV_GROUP_CH = 64
GRID_W = 64
NA_KH = 8
NA_KW = 16
NA_ROWS_PER_STEP = 4
NA_SLAB_ROWS = 12
ROPE_THETA = 10000.0
N_GROUPS = 4
EXPERTS_PER_GROUP = 8
N_EXPERTS = 32
D_EXPERT = 512
MOE_BLOCK = 256
EPS = 1e-6
NEG_INF = -1e30
LOG2E = 1.4426950408889634
ROUTER_LANES = 128
DILATED_CFG = ((128, 1), (512, 4), (2048, 16))

VMEM_LIMIT = 56 * 1024 * 1024


def _cparams(*sem):
    return pltpu.CompilerParams(dimension_semantics=sem, vmem_limit_bytes=VMEM_LIMIT)


def _dot(a, b):
    return jnp.dot(a, b, preferred_element_type=F32)


def _dot_nt(a, b):
    return lax.dot_general(a, b, (((1,), (1,)), ((), ())), preferred_element_type=F32)


def _split(a):
    hi = a.astype(BF16)
    lo = (a - hi.astype(F32)).astype(BF16)
    return hi, lo


def _dot3(a, b):
    ah, al = _split(a)
    bh, bl = _split(b)
    return _dot(ah, bh) + (_dot(ah, bl) + _dot(al, bh))


def _group_mean(v, gmat, width):
    hi, lo = _split(v)
    return (_dot(hi, gmat) + _dot(lo, gmat)) * (1.0 / width)


def _block_diag_ones(n, width):
    idx = np.arange(n) // width
    return jnp.asarray((idx[:, None] == idx[None, :]).astype(np.float32), dtype=BF16)


def _ada_kernel(c_ref, w_ref, b_ref, o_ref):
    c = c_ref[...]
    ca = c * jax.nn.sigmoid(c)
    o_ref[0] = _dot3(ca, w_ref[0]) + b_ref[0]


def _ada_modulation(c, w_ada, b_ada):
    depth, d, n = w_ada.shape
    b = c.shape[0]
    bn = 1024
    return pl.pallas_call(
        _ada_kernel,
        grid=(depth, n // bn),
        in_specs=[
            pl.BlockSpec((b, d), lambda l, j: (0, 0)),
            pl.BlockSpec((1, d, bn), lambda l, j: (l, 0, j)),
            pl.BlockSpec((1, 1, bn), lambda l, j: (l, 0, j)),
        ],
        out_specs=pl.BlockSpec((1, b, bn), lambda l, j: (l, 0, j)),
        out_shape=jax.ShapeDtypeStruct((depth, b, n), F32),
        compiler_params=_cparams("arbitrary", "arbitrary"),
        name="ada_modulation",
    )(c, w_ada, b_ada.reshape(depth, 1, n))


def _rope_tables(seq, dim):
    half = dim // 2
    inv = ROPE_THETA ** (-jnp.arange(0, dim, 2, dtype=F32) / dim)
    ang = jnp.arange(seq, dtype=F32)[:, None] * inv[None, :]
    cos, sin = jnp.cos(ang), jnp.sin(ang)
    reps = 128 // dim
    zeros = jnp.zeros_like(sin)
    cos_t = jnp.tile(jnp.concatenate([cos, cos], axis=1), (1, reps))
    sin_hi = jnp.tile(jnp.concatenate([zeros, sin], axis=1), (1, reps))
    sin_lo = jnp.tile(jnp.concatenate([-sin, zeros], axis=1), (1, reps))
    return cos_t, sin_hi, sin_lo


def _rotary(v, cos_t, sin_hi, sin_lo, half):
    outs = []
    for j in range(v.shape[1] // 128):
        vj = v[:, j * 128:(j + 1) * 128]
        outs.append(vj * cos_t + pltpu.roll(vj, half, 1) * sin_hi + pltpu.roll(vj, 128 - half, 1) * sin_lo)
    return jnp.concatenate(outs, axis=1)


def _inproj_kernel(x_ref, mod_ref, g_ref, w_ref, ca_ref, sha_ref, sla_ref, cd_ref, shd_ref, sld_ref,
                   qa_ref, ka_ref, va_ref, qb_ref, kb_ref, vb_ref, pc_ref, qd_ref, kd_ref, vd_ref):
    x = x_ref[...]
    ms = jnp.mean(x * x, axis=-1, keepdims=True)
    y = x * lax.rsqrt(ms + EPS)
    mod = mod_ref[0]
    h = (y * g_ref[...]) * (1.0 + mod[1:2]) + mod[0:1]
    hb = h.astype(BF16)
    gw = GROUP_WIDTH

    def proj(col):
        return _dot(hb, w_ref[:, col * gw:(col + 1) * gw])

    rot_a = functools.partial(_rotary, cos_t=ca_ref[...], sin_hi=sha_ref[...], sin_lo=sla_ref[...],
                              half=DIFF_DH // 2)
    rot_d = functools.partial(_rotary, cos_t=cd_ref[...], sin_hi=shd_ref[...], sin_lo=sld_ref[...],
                              half=HEAD_DIM // 2)
    na_scale = HEAD_DIM ** -0.5
    qa_ref[...] = rot_a(proj(0)).astype(BF16)
    ka_ref[...] = rot_a(proj(1)).astype(BF16)
    va_ref[...] = proj(2).astype(BF16)
    qb_ref[...] = (proj(3) * na_scale).astype(BF16)
    kb_ref[...] = proj(4).astype(BF16)
    vb_ref[...] = proj(5).astype(BF16)
    pc_ref[:, 0:gw] = proj(6)
    pc_ref[:, gw:2 * gw] = proj(7)
    qd_ref[...] = (rot_d(proj(8)) * na_scale).astype(BF16)
    kd_ref[...] = rot_d(proj(9)).astype(BF16)
    vd_ref[...] = proj(10).astype(BF16)


def _input_projection(x2d, mod_l, g1, w_in_bf16, rope_a, rope_d, seq):
    t, d = x2d.shape
    tm = 512
    tiles_per_batch = seq // tm
    p_in = w_in_bf16.shape[1]
    gw = GROUP_WIDTH
    row_spec = lambda width: pl.BlockSpec((tm, width), lambda i: (i, 0))
    tab_spec = pl.BlockSpec((tm, 128), lambda i: (i % tiles_per_batch, 0))
    out_shapes = []
    out_specs = []
    for name in ("qa", "ka", "va", "qb", "kb", "vb", "pc", "qd", "kd", "vd"):
        if name == "pc":
            out_shapes.append(jax.ShapeDtypeStruct((t, 2 * gw), F32))
            out_specs.append(row_spec(2 * gw))
        else:
            out_shapes.append(jax.ShapeDtypeStruct((t, gw), BF16))
            out_specs.append(row_spec(gw))
    return pl.pallas_call(
        _inproj_kernel,
        grid=(t // tm,),
        in_specs=[
            row_spec(d),
            pl.BlockSpec((1, 6, d), lambda i: (i // tiles_per_batch, 0, 0)),
            pl.BlockSpec((1, d), lambda i: (0, 0)),
            pl.BlockSpec((d, p_in), lambda i: (0, 0)),
            tab_spec, tab_spec, tab_spec, tab_spec, tab_spec, tab_spec,
        ],
        out_specs=out_specs,
        out_shape=out_shapes,
        compiler_params=_cparams("arbitrary"),
        name="input_projection",
    )(x2d, mod_l, g1.reshape(1, d), w_in_bf16, *rope_a, *rope_d)


def _head_masks():
    lane = np.arange(GROUP_WIDTH)
    head = np.stack([(lane // HEAD_DIM == h) for h in range(N_HEADS)]).astype(np.float32)
    diff = np.stack([(lane // DIFF_DH == j) for j in range(2 * N_HEADS)]).astype(np.float32)
    return (jnp.asarray(head[:, None, :], dtype=BF16), jnp.asarray(head[:, None, :], dtype=F32),
            jnp.asarray(diff[:, None, :], dtype=BF16))


def _diff_attn_kernel(lam_init, q_ref, k_ref, v_ref, lp_ref, g_ref, dmask_ref, hmask_ref, gmat_ref,
                      o_ref, acc_ref):
    q = q_ref[...]
    k = k_ref[...]
    v = v_ref[...]
    lp = lp_ref[...]
    lam = (jnp.exp(jnp.sum(lp[0:1] * lp[1:2], axis=-1, keepdims=True))
           - jnp.exp(jnp.sum(lp[2:3] * lp[3:4], axis=-1, keepdims=True)) + lam_init)
    exp_scale = (DIFF_DH ** -0.5) * LOG2E
    acc_ref[...] = jnp.zeros_like(acc_ref)

    def head(h, carry):
        parts = []
        for c in range(2):
            s = _dot_nt(q * dmask_ref[2 * h + c], k)
            m = jnp.max(s, axis=-1, keepdims=True)
            e = jnp.exp2((s - m) * exp_scale)
            parts.append((e, jnp.sum(e, axis=-1, keepdims=True)))
        a = parts[0][0] * (1.0 / parts[0][1]) - parts[1][0] * (lam / parts[1][1])
        acc_ref[...] += _dot(a.astype(BF16), v) * hmask_ref[h]
        return carry

    lax.fori_loop(0, N_HEADS, head, 0)
    o = acc_ref[...]
    ms = _group_mean(o * o, gmat_ref[...], HEAD_DIM)
    o_ref[...] = ((o * lax.rsqrt(ms + EPS) * g_ref[...]) * (1.0 - lam_init)).astype(BF16)


def _diff_attention(qa, ka, va, lam_params, subln_g, layer_idx, batch, seq, masks, gmat):
    t, gw = qa.shape
    tq = 256
    nq = seq // tq
    lam_init = 0.8 - 0.6 * math.exp(-0.3 * layer_idx)
    _, hmask_f32, dmask = masks
    g_tiled = jnp.tile(subln_g, N_HEADS).reshape(1, gw)
    kv_spec = pl.BlockSpec((seq, gw), lambda b, i: (b, 0))
    full = lambda shape: pl.BlockSpec(shape, lambda b, i: (0,) * len(shape))
    return pl.pallas_call(
        functools.partial(_diff_attn_kernel, lam_init),
        grid=(batch, nq),
        in_specs=[
            pl.BlockSpec((tq, gw), lambda b, i: (b * nq + i, 0)),
            kv_spec, kv_spec,
            full(lam_params.shape), full((1, gw)), full(dmask.shape), full(hmask_f32.shape), full(gmat.shape),
        ],
        out_specs=pl.BlockSpec((tq, gw), lambda b, i: (b * nq + i, 0)),
        out_shape=jax.ShapeDtypeStruct((t, gw), BF16),
        scratch_shapes=[pltpu.VMEM((tq, gw), F32)],
        compiler_params=_cparams("arbitrary", "arbitrary"),
        name="diff_attention",
    )(qa, ka, va, lam_params, g_tiled, dmask, hmask_f32, gmat)


def _na_slab_start_rows():
    g = np.arange(GRID_ROWS_STEPS)
    return np.clip(NA_ROWS_PER_STEP * g - NA_KH // 2, 0, GRID_ROWS - NA_SLAB_ROWS)


GRID_ROWS = 32
GRID_ROWS_STEPS = GRID_ROWS // NA_ROWS_PER_STEP


def _na_bias_table(rpb):
    w = GRID_W
    cq = np.arange(w)[:, None]
    ck = np.arange(w)[None, :]
    dc = np.clip(ck - cq, -(NA_KW - 1), NA_KW - 1) + NA_KW - 1
    col_start = np.clip(cq - NA_KW // 2, 0, w - NA_KW)
    col_ok = (ck >= col_start) & (ck < col_start + NA_KW)
    toep = rpb.astype(F32)[:, :, dc]
    toep = jnp.where(col_ok[None, None], toep, NEG_INF)
    masked = jnp.full((rpb.shape[0], 1, w, w), NEG_INF, F32)
    toep = jnp.concatenate([toep, masked], axis=1)
    start = _na_slab_start_rows()
    g = np.arange(GRID_ROWS_STEPS)[:, None, None]
    a = np.arange(NA_ROWS_PER_STEP)[None, :, None]
    kr = np.arange(NA_SLAB_ROWS)[None, None, :]
    qrow = NA_ROWS_PER_STEP * g + a
    krow = start[:, None, None] + kr
    row_start = np.clip(qrow - NA_KH // 2, 0, GRID_ROWS - NA_KH)
    row_ok = (krow >= row_start) & (krow < row_start + NA_KH)
    dr = np.where(row_ok, krow - qrow + NA_KH - 1, 2 * NA_KH - 1)
    blocks = toep[:, dr]
    blocks = jnp.transpose(blocks, (1, 0, 2, 4, 3, 5))
    return blocks.reshape(GRID_ROWS_STEPS, rpb.shape[0], NA_ROWS_PER_STEP * w, NA_SLAB_ROWS * w)


def _na_kernel(start_ref, q_ref, k_ref, v_ref, bias_ref, hmask_ref, hmask_f32_ref, o_ref, acc_ref):
    g = pl.program_id(0)
    start = pl.multiple_of(start_ref[g] * GRID_W, GRID_W)
    nk = NA_SLAB_ROWS * GRID_W
    q = q_ref[...]
    k = k_ref[pl.ds(start, nk), :]
    v = v_ref[pl.ds(start, nk), :]
    acc_ref[...] = jnp.zeros_like(acc_ref)

    def head(h, carry):
        s = _dot_nt(q * hmask_ref[h], k) + bias_ref[0, h]
        m = jnp.max(s, axis=-1, keepdims=True)
        e = jnp.exp(s - m)
        p = e * (1.0 / jnp.sum(e, axis=-1, keepdims=True))
        acc_ref[...] += _dot(p.astype(BF16), v) * hmask_f32_ref[h]
        return carry

    lax.fori_loop(0, N_HEADS, head, 0)
    o_ref[...] = acc_ref[...].astype(BF16)


def _neighborhood_attention(qb, kb, vb, bias_table, batch, seq, masks):
    t, gw = qb.shape
    tq = NA_ROWS_PER_STEP * GRID_W
    steps = GRID_ROWS_STEPS
    hmask, hmask_f32, _ = masks
    start_rows = jnp.asarray(_na_slab_start_rows(), dtype=jnp.int32)
    kv_spec = pl.BlockSpec((seq, gw), lambda g, b, s: (b, 0))
    grid_spec = pltpu.PrefetchScalarGridSpec(
        num_scalar_prefetch=1,
        grid=(steps, batch),
        in_specs=[
            pl.BlockSpec((tq, gw), lambda g, b, s: (b * steps + g, 0)),
            kv_spec, kv_spec,
            pl.BlockSpec((1,) + bias_table.shape[1:], lambda g, b, s: (g, 0, 0, 0)),
            pl.BlockSpec(hmask.shape, lambda g, b, s: (0, 0, 0)),
            pl.BlockSpec(hmask_f32.shape, lambda g, b, s: (0, 0, 0)),
        ],
        out_specs=pl.BlockSpec((tq, gw), lambda g, b, s: (b * steps + g, 0)),
        scratch_shapes=[pltpu.VMEM((tq, gw), F32)],
    )
    return pl.pallas_call(
        _na_kernel,
        grid_spec=grid_spec,
        out_shape=jax.ShapeDtypeStruct((t, gw), BF16),
        compiler_params=_cparams("arbitrary", "arbitrary"),
        name="neighborhood_attention",
    )(start_rows, qb, kb, vb, bias_table, hmask, hmask_f32)


CONV_PAD = 16
CONV_CHUNK = 128


def _conv_kernel(pc_ref, w_ref, b_ref, gn_ref, bn_ref, gmat_ref, o_ref, zp_ref):
    seq, ch = o_ref.shape
    a = pc_ref[:, 0:ch]
    gate = pc_ref[:, ch:2 * ch]
    zp_ref[0:CONV_PAD, :] = jnp.zeros((CONV_PAD, ch), F32)
    zp_ref[CONV_PAD + seq:2 * CONV_PAD + seq, :] = jnp.zeros((CONV_PAD, ch), F32)
    zp_ref[CONV_PAD:CONV_PAD + seq, :] = a * jax.nn.sigmoid(gate)
    gmat = gmat_ref[...]
    first = CONV_PAD - CONV_K // 2
    for c in range(seq // CONV_CHUNK):
        r0 = c * CONV_CHUNK
        acc = jnp.zeros((CONV_CHUNK, ch), F32)
        for j in range(CONV_K):
            acc = acc + w_ref[j] * zp_ref[r0 + first + j:r0 + first + j + CONV_CHUNK, :]
        z = acc + b_ref[...]
        mu = _group_mean(z, gmat, CONV_GROUP_CH)
        dz = z - mu
        var = _group_mean(dz * dz, gmat, CONV_GROUP_CH)
        zn = dz * lax.rsqrt(var + EPS) * gn_ref[...] + bn_ref[...]
        o_ref[r0:r0 + CONV_CHUNK, :] = (zn * jax.nn.sigmoid(zn)).astype(BF16)


def _conformer_conv(pc, w_dw, b_dw, g_n, b_n, batch, seq, gmat):
    t = pc.shape[0]
    ch = GROUP_WIDTH
    full = lambda shape: pl.BlockSpec(shape, lambda b: (0,) * len(shape))
    return pl.pallas_call(
        _conv_kernel,
        grid=(batch,),
        in_specs=[
            pl.BlockSpec((seq, 2 * ch), lambda b: (b, 0)),
            full((CONV_K, 1, ch)), full((1, ch)), full((1, ch)), full((1, ch)), full(gmat.shape),
        ],
        out_specs=pl.BlockSpec((seq, ch), lambda b: (b, 0)),
        out_shape=jax.ShapeDtypeStruct((t, ch), BF16),
        scratch_shapes=[pltpu.VMEM((seq + 2 * CONV_PAD, ch), F32)],
        compiler_params=_cparams("arbitrary"),
        name="conformer_conv",
    )(pc, w_dw.reshape(CONV_K, 1, ch), b_dw.reshape(1, ch), g_n.reshape(1, ch), b_n.reshape(1, ch), gmat)


def _dilated_kernel(q_ref, k_ref, v_ref, hmask_ref, hmask_f32_ref, o_ref, acc_ref):
    tq = q_ref.shape[0]
    seq = k_ref.shape[0]
    q = q_ref[...]
    k = k_ref[...]
    v = v_ref[...]
    qpos = pl.program_id(1) * tq + lax.broadcasted_iota(jnp.int32, (tq, seq), 0)
    delta = lax.broadcasted_iota(jnp.int32, (tq, seq), 1) - qpos
    dist = jnp.abs(delta)
    mult = jnp.zeros((tq, seq), F32)
    for window, dil in DILATED_CFG:
        inside = (dist <= window // 2) & ((delta & (dil - 1)) == 0)
        mult = mult + jnp.where(inside, 1.0, 0.0)
    acc_ref[...] = jnp.zeros_like(acc_ref)

    def head(h, carry):
        s = jnp.where(mult > 0.0, _dot_nt(q * hmask_ref[h], k), NEG_INF)
        m = jnp.max(s, axis=-1, keepdims=True)
        e = jnp.exp(s - m) * mult
        p = e * (1.0 / jnp.sum(e, axis=-1, keepdims=True))
        acc_ref[...] += _dot(p.astype(BF16), v) * hmask_f32_ref[h]
        return carry

    lax.fori_loop(0, N_HEADS, head, 0)
    o_ref[...] = acc_ref[...].astype(BF16)


def _dilated_attention(qd, kd, vd, batch, seq, masks):
    t, gw = qd.shape
    tq = 256
    nq = seq // tq
    hmask, hmask_f32, _ = masks
    kv_spec = pl.BlockSpec((seq, gw), lambda b, i: (b, 0))
    return pl.pallas_call(
        _dilated_kernel,
        grid=(batch, nq),
        in_specs=[
            pl.BlockSpec((tq, gw), lambda b, i: (b * nq + i, 0)),
            kv_spec, kv_spec,
            pl.BlockSpec(hmask.shape, lambda b, i: (0, 0, 0)),
            pl.BlockSpec(hmask_f32.shape, lambda b, i: (0, 0, 0)),
        ],
        out_specs=pl.BlockSpec((tq, gw), lambda b, i: (b * nq + i, 0)),
        out_shape=jax.ShapeDtypeStruct((t, gw), BF16),
        scratch_shapes=[pltpu.VMEM((tq, gw), F32)],
        compiler_params=_cparams("arbitrary", "arbitrary"),
        name="dilated_attention",
    )(qd, kd, vd, hmask, hmask_f32)


def _outproj_kernel(ya_ref, yb_ref, yc_ref, yd_ref, w_ref, x_ref, mod_ref, g_ref, wr_ref,
                    x1_ref, h2_ref, lg_ref):
    gw = GROUP_WIDTH
    mix = _dot(ya_ref[...], w_ref[0:gw, :])
    mix = mix + _dot(yb_ref[...], w_ref[gw:2 * gw, :])
    mix = mix + _dot(yc_ref[...], w_ref[2 * gw:3 * gw, :])
    mix = mix + _dot(yd_ref[...], w_ref[3 * gw:4 * gw, :])
    mod = mod_ref[0]
    x1 = x_ref[...] + mod[2:3] * mix
    x1_ref[...] = x1
    ms = jnp.mean(x1 * x1, axis=-1, keepdims=True)
    h2 = (x1 * lax.rsqrt(ms + EPS) * g_ref[...]) * (1.0 + mod[4:5]) + mod[3:4]
    h2_ref[...] = h2
    lg_ref[...] = _dot3(h2, wr_ref[...])


def _output_projection(ys, w_out_bf16, x2d, mod_l, g2, w_router, seq):
    t, d = x2d.shape
    tm = 512
    tiles_per_batch = seq // tm
    gw = GROUP_WIDTH
    row_spec = lambda width: pl.BlockSpec((tm, width), lambda i: (i, 0))
    return pl.pallas_call(
        _outproj_kernel,
        grid=(t // tm,),
        in_specs=[
            row_spec(gw), row_spec(gw), row_spec(gw), row_spec(gw),
            pl.BlockSpec((d, d), lambda i: (0, 0)),
            row_spec(d),
            pl.BlockSpec((1, 6, d), lambda i: (i // tiles_per_batch, 0, 0)),
            pl.BlockSpec((1, d), lambda i: (0, 0)),
            pl.BlockSpec((d, ROUTER_LANES), lambda i: (0, 0)),
        ],
        out_specs=[row_spec(d), row_spec(d), row_spec(ROUTER_LANES)],
        out_shape=[jax.ShapeDtypeStruct((t, d), F32), jax.ShapeDtypeStruct((t, d), F32),
                   jax.ShapeDtypeStruct((t, ROUTER_LANES), F32)],
        compiler_params=_cparams("arbitrary"),
        name="output_projection",
    )(*ys, w_out_bf16, x2d, mod_l, g2.reshape(1, d), w_router)


def _routing_kernel(lg_ref, info_ref, cnt_ref, carry_ref):
    tr = lg_ref.shape[0]

    @pl.when(pl.program_id(0) == 0)
    def _():
        carry_ref[...] = jnp.zeros_like(carry_ref)

    lg = lg_ref[...]
    lane = lax.broadcasted_iota(jnp.int32, lg.shape, 1).astype(F32)
    big = float(ROUTER_LANES)
    glog = jnp.where(lane < N_GROUPS, lg, -jnp.inf)
    gmax = jnp.max(glog, axis=-1, keepdims=True)
    p_grp = 1.0 / jnp.sum(jnp.exp(glog - gmax), axis=-1, keepdims=True)
    grp = jnp.min(jnp.where(glog == gmax, lane, big), axis=-1, keepdims=True)
    lo = N_GROUPS + EXPERTS_PER_GROUP * grp
    elog = jnp.where((lane >= lo) & (lane < lo + EXPERTS_PER_GROUP), lg, -jnp.inf)
    v1 = jnp.max(elog, axis=-1, keepdims=True)
    i1 = jnp.min(jnp.where(elog == v1, lane, big), axis=-1, keepdims=True)
    elog2 = jnp.where(lane == i1, -jnp.inf, elog)
    v2 = jnp.max(elog2, axis=-1, keepdims=True)
    i2 = jnp.min(jnp.where(elog2 == v2, lane, big), axis=-1, keepdims=True)
    d = jnp.exp(v2 - v1)
    gate1 = p_grp / (1.0 + d)
    gate2 = p_grp * d / (1.0 + d)
    sel1 = lane == i1
    sel2 = lane == i2
    sel = jnp.where(sel1 | sel2, 1.0, 0.0)
    row = lax.broadcasted_iota(jnp.int32, (tr, tr), 0)
    col = lax.broadcasted_iota(jnp.int32, (tr, tr), 1)
    before = jnp.where(col < row, 1.0, 0.0).astype(BF16)
    rank = _dot(before, sel.astype(BF16)) + carry_ref[...]
    r1 = jnp.sum(jnp.where(sel1, rank, 0.0), axis=-1, keepdims=True)
    r2 = jnp.sum(jnp.where(sel2, rank, 0.0), axis=-1, keepdims=True)
    carry_ref[...] += jnp.sum(sel, axis=0, keepdims=True)
    cnt_ref[...] = carry_ref[...]
    info = jnp.zeros_like(lg)
    for idx, val in enumerate((i1 - N_GROUPS, i2 - N_GROUPS, r1, r2, gate1, gate2)):
        info = jnp.where(lane == idx, val, info)
    info_ref[...] = info


def _routing(logits):
    t = logits.shape[0]
    tr = 512
    return pl.pallas_call(
        _routing_kernel,
        grid=(t // tr,),
        in_specs=[pl.BlockSpec((tr, ROUTER_LANES), lambda i: (i, 0))],
        out_specs=[pl.BlockSpec((tr, ROUTER_LANES), lambda i: (i, 0)),
                   pl.BlockSpec((1, ROUTER_LANES), lambda i: (0, 0))],
        out_shape=[jax.ShapeDtypeStruct((t, ROUTER_LANES), F32),
                   jax.ShapeDtypeStruct((1, ROUTER_LANES), F32)],
        scratch_shapes=[pltpu.VMEM((1, ROUTER_LANES), F32)],
        compiler_params=_cparams("arbitrary"),
        name="routing",
    )(logits)


def _row_copy(src_ref, src_row, dst_ref, dst_row, sem):
    return pltpu.make_async_copy(src_ref.at[pl.ds(src_row, 1)], dst_ref.at[pl.ds(dst_row, 1)], sem)


def _dispatch_kernel(d1_ref, d2_ref, h_ref, xs_in_ref, xs_ref, sem):
    del xs_in_ref
    td = h_ref.shape[0]
    base = pl.program_id(0) * td

    def issue(r, carry):
        for dref in (d1_ref, d2_ref):
            _row_copy(h_ref, r, xs_ref, dref[base + r], sem).start()
        return carry

    lax.fori_loop(0, td, issue, 0)

    def drain(r, carry):
        for _ in range(2):
            _row_copy(h_ref, 0, xs_ref, 0, sem).wait()
        return carry

    lax.fori_loop(0, td, drain, 0)


def _dispatch(h2, dest1, dest2, n_rows):
    t, d = h2.shape
    td = 256
    xs_init = jnp.zeros((n_rows, d), h2.dtype)
    grid_spec = pltpu.PrefetchScalarGridSpec(
        num_scalar_prefetch=2,
        grid=(t // td,),
        in_specs=[pl.BlockSpec((td, d), lambda i, d1, d2: (i, 0)),
                  pl.BlockSpec(memory_space=pl.ANY)],
        out_specs=pl.BlockSpec(memory_space=pl.ANY),
        scratch_shapes=[pltpu.SemaphoreType.DMA(())],
    )
    return pl.pallas_call(
        _dispatch_kernel,
        grid_spec=grid_spec,
        out_shape=jax.ShapeDtypeStruct((n_rows, d), h2.dtype),
        input_output_aliases={3: 0},
        compiler_params=_cparams("arbitrary"),
        name="moe_dispatch",
    )(dest1, dest2, h2, xs_init)


def _expert_kernel(blk_e_ref, nvalid_ref, xs_ref, wg_ref, wu_ref, wd_ref, ys_ref):
    j = pl.program_id(0)

    @pl.when(j < nvalid_ref[0])
    def _():
        xb = xs_ref[...].astype(BF16)
        gate = _dot(xb, wg_ref[0])
        up = _dot(xb, wu_ref[0])
        hdn = (gate * jax.nn.sigmoid(gate)) * up
        ys_ref[...] = _dot(hdn.astype(BF16), wd_ref[0])

    @pl.when(j >= nvalid_ref[0])
    def _():
        ys_ref[...] = jnp.zeros_like(ys_ref)


def _expert_mlp(xs, blk_e, nvalid, wg, wu, wd):
    n_rows, d = xs.shape
    nblk = n_rows // MOE_BLOCK
    de = wg.shape[2]

    def x_map(j, be, nv):
        return (jnp.minimum(j, nv[0] - 1), 0)

    grid_spec = pltpu.PrefetchScalarGridSpec(
        num_scalar_prefetch=2,
        grid=(nblk,),
        in_specs=[
            pl.BlockSpec((MOE_BLOCK, d), x_map),
            pl.BlockSpec((1, d, de), lambda j, be, nv: (be[j], 0, 0)),
            pl.BlockSpec((1, d, de), lambda j, be, nv: (be[j], 0, 0)),
            pl.BlockSpec((1, de, d), lambda j, be, nv: (be[j], 0, 0)),
        ],
        out_specs=pl.BlockSpec((MOE_BLOCK, d), lambda j, be, nv: (j, 0)),
    )
    return pl.pallas_call(
        _expert_kernel,
        grid_spec=grid_spec,
        out_shape=jax.ShapeDtypeStruct((n_rows, d), F32),
        compiler_params=_cparams("arbitrary"),
        name="expert_mlp",
    )(blk_e, nvalid, xs, wg, wu, wd)


def _combine_kernel(final_norm, d1_ref, d2_ref, x_ref, mod_ref, info_ref, gf_ref, ys_ref, o_ref, buf_ref, sem):
    tc = x_ref.shape[0]
    base = pl.program_id(0) * tc

    def issue(r, carry):
        for slot, dref in enumerate((d1_ref, d2_ref)):
            _row_copy(ys_ref, dref[base + r], buf_ref.at[slot], r, sem).start()
        return carry

    lax.fori_loop(0, tc, issue, 0)

    def drain(r, carry):
        for slot in range(2):
            _row_copy(ys_ref, 0, buf_ref.at[slot], 0, sem).wait()
        return carry

    lax.fori_loop(0, tc, drain, 0)
    info = info_ref[...]
    moe = info[:, 4:5] * buf_ref[0] + info[:, 5:6] * buf_ref[1]
    x2 = x_ref[...] + mod_ref[0][5:6] * moe
    if final_norm:
        ms = jnp.mean(x2 * x2, axis=-1, keepdims=True)
        x2 = x2 * lax.rsqrt(ms + EPS) * gf_ref[...]
    o_ref[...] = x2


def _combine(x1, mod_l, info, ys, dest1, dest2, g_final, seq, final_norm):
    t, d = x1.shape
    tc = 256
    tiles_per_batch = seq // tc
    grid_spec = pltpu.PrefetchScalarGridSpec(
        num_scalar_prefetch=2,
        grid=(t // tc,),
        in_specs=[
            pl.BlockSpec((tc, d), lambda i, d1, d2: (i, 0)),
            pl.BlockSpec((1, 6, d), lambda i, d1, d2: (i // tiles_per_batch, 0, 0)),
            pl.BlockSpec((tc, ROUTER_LANES), lambda i, d1, d2: (i, 0)),
            pl.BlockSpec((1, d), lambda i, d1, d2: (0, 0)),
            pl.BlockSpec(memory_space=pl.ANY),
        ],
        out_specs=pl.BlockSpec((tc, d), lambda i, d1, d2: (i, 0)),
        scratch_shapes=[pltpu.VMEM((2, tc, d), F32), pltpu.SemaphoreType.DMA(())],
    )
    return pl.pallas_call(
        functools.partial(_combine_kernel, final_norm),
        grid_spec=grid_spec,
        out_shape=jax.ShapeDtypeStruct((t, d), F32),
        compiler_params=_cparams("arbitrary"),
        name="moe_combine",
    )(dest1, dest2, x1, mod_l, info, g_final.reshape(1, d), ys)


def _router_weights(w_rg, w_re):
    d = w_rg.shape[0]
    w_experts = jnp.transpose(w_re, (1, 0, 2)).reshape(d, N_EXPERTS)
    pad = jnp.zeros((d, ROUTER_LANES - N_GROUPS - N_EXPERTS), F32)
    return jnp.concatenate([w_rg, w_experts, pad], axis=1)


def _block_layout(counts_row, n_blocks):
    counts = counts_row[0, N_GROUPS:N_GROUPS + N_EXPERTS].astype(jnp.int32)
    padded = (counts + MOE_BLOCK - 1) // MOE_BLOCK * MOE_BLOCK
    pad_end = jnp.cumsum(padded)
    pad_start = pad_end - padded
    blk_e = jnp.minimum(jnp.searchsorted(pad_end, jnp.arange(n_blocks) * MOE_BLOCK, side='right'),
                        N_EXPERTS - 1).astype(jnp.int32)
    nvalid = (pad_end[-1:] // MOE_BLOCK).astype(jnp.int32)
    return pad_start, blk_e, nvalid


def kernel(x, c, w_ada, b_ada, g_norm1, g_norm2, w_in, diff_lambda, diff_subln, na_rpb, conv_dw, conv_b,
           conv_norm_g, conv_norm_b, w_out, w_router_group, w_router_expert, w_exp_gate, w_exp_up,
           w_exp_down, g_final):
    batch, seq, d = x.shape
    depth = w_ada.shape[0]
    t = batch * seq
    assert d == D_MODEL and seq == GRID_ROWS * GRID_W
    n_rows = t * 2 + N_EXPERTS * MOE_BLOCK
    n_blocks = n_rows // MOE_BLOCK

    mod = _ada_modulation(c, w_ada, b_ada).reshape(depth, batch, 6, d)
    rope_a = _rope_tables(seq, DIFF_DH)
    rope_d = _rope_tables(seq, HEAD_DIM)
    masks = _head_masks()
    gmat = _block_diag_ones(GROUP_WIDTH, HEAD_DIM)

    x2d = x.reshape(t, d)
    for l in range(depth):
        mod_l = mod[l]
        qa, ka, va, qb, kb, vb, pc, qd, kd, vd = _input_projection(
            x2d, mod_l, g_norm1[l], w_in[l].astype(BF16), rope_a, rope_d, seq)
        ya = _diff_attention(qa, ka, va, diff_lambda[l], diff_subln[l], l, batch, seq, masks, gmat)
        yb = _neighborhood_attention(qb, kb, vb, _na_bias_table(na_rpb[l]), batch, seq, masks)
        yc = _conformer_conv(pc, conv_dw[l], conv_b[l], conv_norm_g[l], conv_norm_b[l], batch, seq, gmat)
        yd = _dilated_attention(qd, kd, vd, batch, seq, masks)
        x1, h2, logits = _output_projection(
            (ya, yb, yc, yd), w_out[l].astype(BF16), x2d, mod_l, g_norm2[l],
            _router_weights(w_router_group[l], w_router_expert[l]), seq)
        info, counts = _routing(logits)
        pad_start, blk_e, nvalid = _block_layout(counts, n_blocks)
        ids = info[:, 0:4].astype(jnp.int32)
        dest1 = pad_start[ids[:, 0]] + ids[:, 2]
        dest2 = pad_start[ids[:, 1]] + ids[:, 3]
        xs = _dispatch(h2, dest1, dest2, n_rows)
        ys = _expert_mlp(xs, blk_e, nvalid, w_exp_gate[l].astype(BF16), w_exp_up[l].astype(BF16),
                         w_exp_down[l].astype(BF16))
        x2d = _combine(x1, mod_l, info, ys, dest1, dest2, g_final, seq, final_norm=(l == depth - 1))
    return x2d.reshape(batch, seq, d)
```

```python
import functools
import math

import numpy as np
import jax
import jax.numpy as jnp
from jax import lax
from jax.experimental import pallas as pl
from jax.experimental.pallas import tpu as pltpu

F32 = jnp.float32
BF16 = jnp.bfloat16

D_MODEL = 1024
GROUP_WIDTH = 256
HEAD_DIM = 64
N_HEADS = 4
DIFF_DH = 32
CONV_K = 31
CONV_GROUP_CH = 64
GRID_W = 64
NA_KH = 8
NA_KW = 16
GRID_ROWS = 32
ROPE_THETA = 10000.0
N_GROUPS = 4
EXPERTS_PER_GROUP = 8
N_EXPERTS = 32
D_EXPERT = 512
MOE_BLOCK = 256
EPS = 1e-6
NEG_INF = -1e30
LOG2E = 1.4426950408889634
ROUTER_LANES = 128
DILATED_CFG = ((128, 1), (512, 4), (2048, 16))
DIFF_EXP2_SCALE = (DIFF_DH ** -0.5) * LOG2E

VMEM_LIMIT = 56 * 1024 * 1024


def _cparams(*sem):
    return pltpu.CompilerParams(dimension_semantics=sem, vmem_limit_bytes=VMEM_LIMIT)


def _dot(a, b):
    return jnp.dot(a, b, preferred_element_type=F32)


def _dot_nt(a, b):
    return lax.dot_general(a, b, (((1,), (1,)), ((), ())), preferred_element_type=F32)


def _split(a):
    hi = a.astype(BF16)
    lo = (a - hi.astype(F32)).astype(BF16)
    return hi, lo


def _dot3(a, b):
    ah, al = _split(a)
    bh, bl = _split(b)
    return _dot(ah, bh) + (_dot(ah, bl) + _dot(al, bh))


def _group_mean(v, gmat, width):
    hi, lo = _split(v)
    return (_dot(hi, gmat) + _dot(lo, gmat)) * (1.0 / width)


def _block_diag_ones(n, width):
    idx = np.arange(n) // width
    return jnp.asarray((idx[:, None] == idx[None, :]).astype(np.float32), dtype=BF16)


def _ada_kernel(c_ref, w_ref, b_ref, o_ref):
    c = c_ref[...]
    ca = c * jax.nn.sigmoid(c)
    o_ref[0] = _dot3(ca, w_ref[0]) + b_ref[0]


def _ada_modulation(c, w_ada, b_ada):
    depth, d, n = w_ada.shape
    b = c.shape[0]
    bn = 1024
    return pl.pallas_call(
        _ada_kernel,
        grid=(depth, n // bn),
        in_specs=[
            pl.BlockSpec((b, d), lambda l, j: (0, 0)),
            pl.BlockSpec((1, d, bn), lambda l, j: (l, 0, j)),
            pl.BlockSpec((1, 1, bn), lambda l, j: (l, 0, j)),
        ],
        out_specs=pl.BlockSpec((1, b, bn), lambda l, j: (l, 0, j)),
        out_shape=jax.ShapeDtypeStruct((depth, b, n), F32),
        compiler_params=_cparams("arbitrary", "arbitrary"),
        name="ada_modulation",
    )(c, w_ada, b_ada.reshape(depth, 1, n))


def _rope_tables(seq, dim):
    half = dim // 2
    inv = ROPE_THETA ** (-jnp.arange(0, dim, 2, dtype=F32) / dim)
    ang = jnp.arange(seq, dtype=F32)[:, None] * inv[None, :]
    cos, sin = jnp.cos(ang), jnp.sin(ang)
    reps = 128 // dim
    zeros = jnp.zeros_like(sin)
    cos_t = jnp.tile(jnp.concatenate([cos, cos], axis=1), (1, reps))
    sin_hi = jnp.tile(jnp.concatenate([zeros, sin], axis=1), (1, reps))
    sin_lo = jnp.tile(jnp.concatenate([-sin, zeros], axis=1), (1, reps))
    return cos_t, sin_hi, sin_lo


def _rotary(v, cos_t, sin_hi, sin_lo, half):
    outs = []
    for j in range(v.shape[1] // 128):
        vj = v[:, j * 128:(j + 1) * 128]
        outs.append(vj * cos_t + pltpu.roll(vj, half, 1) * sin_hi + pltpu.roll(vj, 128 - half, 1) * sin_lo)
    return jnp.concatenate(outs, axis=1)


def _inproj_kernel(x_ref, mod_ref, g_ref, w_ref, ca_ref, sha_ref, sla_ref, cd_ref, shd_ref, sld_ref,
                   qa_ref, ka_ref, va_ref, qb_ref, kb_ref, vb_ref, pc_ref, qd_ref, kd_ref, vd_ref):
    x = x_ref[...]
    ms = jnp.mean(x * x, axis=-1, keepdims=True)
    y = x * lax.rsqrt(ms + EPS)
    mod = mod_ref[0]
    h = (y * g_ref[...]) * (1.0 + mod[1:2]) + mod[0:1]
    hb = h.astype(BF16)
    gw = GROUP_WIDTH

    def proj(col):
        return _dot(hb, w_ref[:, col * gw:(col + 1) * gw])

    rot_a = functools.partial(_rotary, cos_t=ca_ref[...], sin_hi=sha_ref[...], sin_lo=sla_ref[...],
                              half=DIFF_DH // 2)
    rot_d = functools.partial(_rotary, cos_t=cd_ref[...], sin_hi=shd_ref[...], sin_lo=sld_ref[...],
                              half=HEAD_DIM // 2)
    na_scale = HEAD_DIM ** -0.5
    qa_ref[...] = (rot_a(proj(0)) * DIFF_EXP2_SCALE).astype(BF16)
    ka_ref[...] = rot_a(proj(1)).astype(BF16)
    va_ref[...] = proj(2).astype(BF16)
    qb_ref[...] = (proj(3) * na_scale).astype(BF16)
    kb_ref[...] = proj(4).astype(BF16)
    vb_ref[...] = proj(5).astype(BF16)
    pc_ref[:, 0:gw] = proj(6)
    pc_ref[:, gw:2 * gw] = proj(7)
    qd_ref[...] = (rot_d(proj(8)) * na_scale).astype(BF16)
    kd_ref[...] = rot_d(proj(9)).astype(BF16)
    vd_ref[...] = proj(10).astype(BF16)


def _input_projection(x2d, mod_l, g1, w_in_bf16, rope_a, rope_d, seq):
    t, d = x2d.shape
    tm = 512
    tiles_per_batch = seq // tm
    p_in = w_in_bf16.shape[1]
    gw = GROUP_WIDTH
    row_spec = lambda width: pl.BlockSpec((tm, width), lambda i: (i, 0))
    tab_spec = pl.BlockSpec((tm, 128), lambda i: (i % tiles_per_batch, 0))
    out_shapes = []
    out_specs = []
    for name in ("qa", "ka", "va", "qb", "kb", "vb", "pc", "qd", "kd", "vd"):
        if name == "pc":
            out_shapes.append(jax.ShapeDtypeStruct((t, 2 * gw), F32))
            out_specs.append(row_spec(2 * gw))
        else:
            out_shapes.append(jax.ShapeDtypeStruct((t, gw), BF16))
            out_specs.append(row_spec(gw))
    return pl.pallas_call(
        _inproj_kernel,
        grid=(t // tm,),
        in_specs=[
            row_spec(d),
            pl.BlockSpec((1, 6, d), lambda i: (i // tiles_per_batch, 0, 0)),
            pl.BlockSpec((1, d), lambda i: (0, 0)),
            pl.BlockSpec((d, p_in), lambda i: (0, 0)),
            tab_spec, tab_spec, tab_spec, tab_spec, tab_spec, tab_spec,
        ],
        out_specs=out_specs,
        out_shape=out_shapes,
        compiler_params=_cparams("arbitrary"),
        name="input_projection",
    )(x2d, mod_l, g1.reshape(1, d), w_in_bf16, *rope_a, *rope_d)


def _head_masks():
    lane = np.arange(GROUP_WIDTH)
    head = np.stack([(lane // HEAD_DIM == h) for h in range(N_HEADS)]).astype(np.float32)
    diff = np.stack([(lane // DIFF_DH == j) for j in range(2 * N_HEADS)]).astype(np.float32)
    return (jnp.asarray(head[:, None, :], dtype=BF16), jnp.asarray(head[:, None, :], dtype=F32),
            jnp.asarray(diff[:, None, :], dtype=BF16))


def _swap_lane_halves(a):
    return jnp.concatenate([pltpu.roll(a[:, j * 128:(j + 1) * 128], 64, 1) for j in range(a.shape[1] // 128)],
                           axis=1)


def _diff_attn_kernel(lam_init, q_ref, k_ref, v_ref, lp_ref, g_ref, dmask_ref, hmask_ref, hmask_f32_ref,
                      gmat_ref, o_ref, acc_ref):
    q = q_ref[...]
    k = k_ref[...]
    v = v_ref[...]
    lp = lp_ref[...]
    lam = (jnp.exp(jnp.sum(lp[0:1] * lp[1:2], axis=-1, keepdims=True))
           - jnp.exp(jnp.sum(lp[2:3] * lp[3:4], axis=-1, keepdims=True)) + lam_init)
    acc_ref[...] = jnp.zeros_like(acc_ref)

    def head(h, carry):
        rhs = v * hmask_ref[h] + hmask_ref[h ^ 1]
        parts = []
        for c in range(2):
            s = _dot_nt(q * dmask_ref[2 * h + c], k)
            m = jnp.max(s, axis=-1, keepdims=True)
            num = _dot(jnp.exp2(s - m).astype(BF16), rhs)
            parts.append(num / _swap_lane_halves(num))
        acc_ref[...] = jnp.where(hmask_f32_ref[h] > 0.5, parts[0] - lam * parts[1], acc_ref[...])
        return carry

    lax.fori_loop(0, N_HEADS, head, 0)
    o = acc_ref[...]
    ms = _group_mean(o * o, gmat_ref[...], HEAD_DIM)
    o_ref[...] = ((o * lax.rsqrt(ms + EPS) * g_ref[...]) * (1.0 - lam_init)).astype(BF16)


def _diff_attention(qa, ka, va, lam_params, subln_g, layer_idx, batch, seq, masks, gmat):
    t, gw = qa.shape
    tq = 256
    nq = seq // tq
    lam_init = 0.8 - 0.6 * math.exp(-0.3 * layer_idx)
    hmask, hmask_f32, dmask = masks
    g_tiled = jnp.tile(subln_g, N_HEADS).reshape(1, gw)
    kv_spec = pl.BlockSpec((seq, gw), lambda b, i: (b, 0))
    full = lambda shape: pl.BlockSpec(shape, lambda b, i: (0,) * len(shape))
    return pl.pallas_call(
        functools.partial(_diff_attn_kernel, lam_init),
        grid=(batch, nq),
        in_specs=[
            pl.BlockSpec((tq, gw), lambda b, i: (b * nq + i, 0)),
            kv_spec, kv_spec,
            full(lam_params.shape), full((1, gw)), full(dmask.shape), full(hmask.shape), full(hmask_f32.shape),
            full(gmat.shape),
        ],
        out_specs=pl.BlockSpec((tq, gw), lambda b, i: (b * nq + i, 0)),
        out_shape=jax.ShapeDtypeStruct((t, gw), BF16),
        scratch_shapes=[pltpu.VMEM((tq, gw), F32)],
        compiler_params=_cparams("arbitrary", "arbitrary"),
        name="diff_attention",
    )(qa, ka, va, lam_params, g_tiled, dmask, hmask, hmask_f32, gmat)


def _na_bias_table(rpb):
    w = GRID_W
    n_heads = rpb.shape[0]
    cq = np.arange(w)[:, None]
    ck = np.arange(w)[None, :]
    dc = np.clip(ck - cq, -(NA_KW - 1), NA_KW - 1) + NA_KW - 1
    col_start = np.clip(cq - NA_KW // 2, 0, w - NA_KW)
    col_ok = (ck >= col_start) & (ck < col_start + NA_KW)
    toep = jnp.where(col_ok[None, None], rpb.astype(F32)[:, :, dc], NEG_INF)
    tabs = jnp.stack([toep[:, NA_KH - 1 - off:2 * NA_KH - 1 - off] for off in range(NA_KH)])
    return jnp.transpose(tabs, (0, 1, 3, 2, 4)).reshape(NA_KH, n_heads * w, NA_KH * w)


def _na_kernel(q_ref, k_ref, v_ref, tab_ref, hmask_ref, hmask_f32_ref, o_ref):
    w = GRID_W
    nk = NA_KH * w

    def grid_row(i, carry):
        row_start = jnp.clip(i - NA_KH // 2, 0, GRID_ROWS - NA_KH)
        kstart = pl.multiple_of(row_start * w, w)
        qstart = pl.multiple_of(i * w, w)
        q = q_ref[pl.ds(qstart, w), :]
        qs = jnp.concatenate([q * hmask_ref[h] for h in range(N_HEADS)], axis=0)
        s = _dot_nt(qs, k_ref[pl.ds(kstart, nk), :]) + tab_ref[i - row_start]
        m = jnp.max(s, axis=-1, keepdims=True)
        e = jnp.exp(s - m)
        p = e * (1.0 / jnp.sum(e, axis=-1, keepdims=True))
        pv = _dot(p.astype(BF16), v_ref[pl.ds(kstart, nk), :])
        o = pv[0:w] * hmask_f32_ref[0]
        for h in range(1, N_HEADS):
            o = o + pv[h * w:(h + 1) * w] * hmask_f32_ref[h]
        o_ref[pl.ds(qstart, w), :] = o.astype(BF16)
        return carry

    lax.fori_loop(0, GRID_ROWS, grid_row, 0)


def _neighborhood_attention(qb, kb, vb, bias_table, batch, seq, masks):
    t, gw = qb.shape
    hmask, hmask_f32, _ = masks
    seq_spec = pl.BlockSpec((seq, gw), lambda b: (b, 0))
    full = lambda shape: pl.BlockSpec(shape, lambda b: (0,) * len(shape))
    return pl.pallas_call(
        _na_kernel,
        grid=(batch,),
        in_specs=[seq_spec, seq_spec, seq_spec, full(bias_table.shape), full(hmask.shape), full(hmask_f32.shape)],
        out_specs=seq_spec,
        out_shape=jax.ShapeDtypeStruct((t, gw), BF16),
        compiler_params=_cparams("arbitrary"),
        name="neighborhood_attention",
    )(qb, kb, vb, bias_table, hmask, hmask_f32)


CONV_PAD = 16
CONV_CHUNK = 128


def _conv_kernel(pc_ref, w_ref, b_ref, gn_ref, bn_ref, gmat_ref, o_ref, zp_ref):
    seq, ch = o_ref.shape
    a = pc_ref[:, 0:ch]
    gate = pc_ref[:, ch:2 * ch]
    zp_ref[0:CONV_PAD, :] = jnp.zeros((CONV_PAD, ch), F32)
    zp_ref[CONV_PAD + seq:2 * CONV_PAD + seq, :] = jnp.zeros((CONV_PAD, ch), F32)
    zp_ref[CONV_PAD:CONV_PAD + seq, :] = a * jax.nn.sigmoid(gate)
    gmat = gmat_ref[...]
    first = CONV_PAD - CONV_K // 2
    for c in range(seq // CONV_CHUNK):
        r0 = c * CONV_CHUNK
        acc = jnp.zeros((CONV_CHUNK, ch), F32)
        for j in range(CONV_K):
            acc = acc + w_ref[j] * zp_ref[r0 + first + j:r0 + first + j + CONV_CHUNK, :]
        z = acc + b_ref[...]
        mu = _group_mean(z, gmat, CONV_GROUP_CH)
        dz = z - mu
        var = _group_mean(dz * dz, gmat, CONV_GROUP_CH)
        zn = dz * lax.rsqrt(var + EPS) * gn_ref[...] + bn_ref[...]
        o_ref[r0:r0 + CONV_CHUNK, :] = (zn * jax.nn.sigmoid(zn)).astype(BF16)


def _conformer_conv(pc, w_dw, b_dw, g_n, b_n, batch, seq, gmat):
    t = pc.shape[0]
    ch = GROUP_WIDTH
    full = lambda shape: pl.BlockSpec(shape, lambda b: (0,) * len(shape))
    return pl.pallas_call(
        _conv_kernel,
        grid=(batch,),
        in_specs=[
            pl.BlockSpec((seq, 2 * ch), lambda b: (b, 0)),
            full((CONV_K, 1, ch)), full((1, ch)), full((1, ch)), full((1, ch)), full(gmat.shape),
        ],
        out_specs=pl.BlockSpec((seq, ch), lambda b: (b, 0)),
        out_shape=jax.ShapeDtypeStruct((t, ch), BF16),
        scratch_shapes=[pltpu.VMEM((seq + 2 * CONV_PAD, ch), F32)],
        compiler_params=_cparams("arbitrary"),
        name="conformer_conv",
    )(pc, w_dw.reshape(CONV_K, 1, ch), b_dw.reshape(1, ch), g_n.reshape(1, ch), b_n.reshape(1, ch), gmat)


def _dilated_kernel(q_ref, k_ref, v_ref, hmask_ref, hmask_f32_ref, o_ref, acc_ref):
    tq = q_ref.shape[0]
    seq = k_ref.shape[0]
    q = q_ref[...]
    k = k_ref[...]
    v = v_ref[...]
    qpos = pl.program_id(1) * tq + lax.broadcasted_iota(jnp.int32, (tq, seq), 0)
    delta = lax.broadcasted_iota(jnp.int32, (tq, seq), 1) - qpos
    dist = jnp.abs(delta)
    mult = jnp.zeros((tq, seq), F32)
    for window, dil in DILATED_CFG:
        inside = (dist <= window // 2) & ((delta & (dil - 1)) == 0)
        mult = mult + jnp.where(inside, 1.0, 0.0)
    acc_ref[...] = jnp.zeros_like(acc_ref)

    def head(h, carry):
        s = jnp.where(mult > 0.0, _dot_nt(q * hmask_ref[h], k), NEG_INF)
        m = jnp.max(s, axis=-1, keepdims=True)
        e = jnp.exp(s - m) * mult
        p = e * (1.0 / jnp.sum(e, axis=-1, keepdims=True))
        acc_ref[...] += _dot(p.astype(BF16), v) * hmask_f32_ref[h]
        return carry

    lax.fori_loop(0, N_HEADS, head, 0)
    o_ref[...] = acc_ref[...].astype(BF16)


def _dilated_attention(qd, kd, vd, batch, seq, masks):
    t, gw = qd.shape
    tq = 256
    nq = seq // tq
    hmask, hmask_f32, _ = masks
    kv_spec = pl.BlockSpec((seq, gw), lambda b, i: (b, 0))
    return pl.pallas_call(
        _dilated_kernel,
        grid=(batch, nq),
        in_specs=[
            pl.BlockSpec((tq, gw), lambda b, i: (b * nq + i, 0)),
            kv_spec, kv_spec,
            pl.BlockSpec(hmask.shape, lambda b, i: (0, 0, 0)),
            pl.BlockSpec(hmask_f32.shape, lambda b, i: (0, 0, 0)),
        ],
        out_specs=pl.BlockSpec((tq, gw), lambda b, i: (b * nq + i, 0)),
        out_shape=jax.ShapeDtypeStruct((t, gw), BF16),
        scratch_shapes=[pltpu.VMEM((tq, gw), F32)],
        compiler_params=_cparams("arbitrary", "arbitrary"),
        name="dilated_attention",
    )(qd, kd, vd, hmask, hmask_f32)


def _outproj_kernel(ya_ref, yb_ref, yc_ref, yd_ref, w_ref, x_ref, mod_ref, g_ref, wr_ref,
                    x1_ref, h2_ref, lg_ref):
    gw = GROUP_WIDTH
    mix = _dot(ya_ref[...], w_ref[0:gw, :])
    mix = mix + _dot(yb_ref[...], w_ref[gw:2 * gw, :])
    mix = mix + _dot(yc_ref[...], w_ref[2 * gw:3 * gw, :])
    mix = mix + _dot(yd_ref[...], w_ref[3 * gw:4 * gw, :])
    mod = mod_ref[0]
    x1 = x_ref[...] + mod[2:3] * mix
    x1_ref[...] = x1
    ms = jnp.mean(x1 * x1, axis=-1, keepdims=True)
    h2 = (x1 * lax.rsqrt(ms + EPS) * g_ref[...]) * (1.0 + mod[4:5]) + mod[3:4]
    h2_ref[...] = h2
    lg_ref[...] = _dot3(h2, wr_ref[...])


def _output_projection(ys, w_out_bf16, x2d, mod_l, g2, w_router, seq):
    t, d = x2d.shape
    tm = 512
    tiles_per_batch = seq // tm
    gw = GROUP_WIDTH
    row_spec = lambda width: pl.BlockSpec((tm, width), lambda i: (i, 0))
    return pl.pallas_call(
        _outproj_kernel,
        grid=(t // tm,),
        in_specs=[
            row_spec(gw), row_spec(gw), row_spec(gw), row_spec(gw),
            pl.BlockSpec((d, d), lambda i: (0, 0)),
            row_spec(d),
            pl.BlockSpec((1, 6, d), lambda i: (i // tiles_per_batch, 0, 0)),
            pl.BlockSpec((1, d), lambda i: (0, 0)),
            pl.BlockSpec((d, ROUTER_LANES), lambda i: (0, 0)),
        ],
        out_specs=[row_spec(d), row_spec(d), row_spec(ROUTER_LANES)],
        out_shape=[jax.ShapeDtypeStruct((t, d), F32), jax.ShapeDtypeStruct((t, d), F32),
                   jax.ShapeDtypeStruct((t, ROUTER_LANES), F32)],
        compiler_params=_cparams("arbitrary"),
        name="output_projection",
    )(*ys, w_out_bf16, x2d, mod_l, g2.reshape(1, d), w_router)


def _routing_kernel(lg_ref, info_ref, cnt_ref, carry_ref):
    tr = lg_ref.shape[0]

    @pl.when(pl.program_id(0) == 0)
    def _():
        carry_ref[...] = jnp.zeros_like(carry_ref)

    lg = lg_ref[...]
    lane = lax.broadcasted_iota(jnp.int32, lg.shape, 1).astype(F32)
    big = float(ROUTER_LANES)
    glog = jnp.where(lane < N_GROUPS, lg, -jnp.inf)
    gmax = jnp.max(glog, axis=-1, keepdims=True)
    p_grp = 1.0 / jnp.sum(jnp.exp(glog - gmax), axis=-1, keepdims=True)
    grp = jnp.min(jnp.where(glog == gmax, lane, big), axis=-1, keepdims=True)
    lo = N_GROUPS + EXPERTS_PER_GROUP * grp
    elog = jnp.where((lane >= lo) & (lane < lo + EXPERTS_PER_GROUP), lg, -jnp.inf)
    v1 = jnp.max(elog, axis=-1, keepdims=True)
    i1 = jnp.min(jnp.where(elog == v1, lane, big), axis=-1, keepdims=True)
    elog2 = jnp.where(lane == i1, -jnp.inf, elog)
    v2 = jnp.max(elog2, axis=-1, keepdims=True)
    i2 = jnp.min(jnp.where(elog2 == v2, lane, big), axis=-1, keepdims=True)
    d = jnp.exp(v2 - v1)
    gate1 = p_grp / (1.0 + d)
    gate2 = p_grp * d / (1.0 + d)
    sel1 = lane == i1
    sel2 = lane == i2
    sel = jnp.where(sel1 | sel2, 1.0, 0.0)
    row = lax.broadcasted_iota(jnp.int32, (tr, tr), 0)
    col = lax.broadcasted_iota(jnp.int32, (tr, tr), 1)
    before = jnp.where(col < row, 1.0, 0.0).astype(BF16)
    rank = _dot(before, sel.astype(BF16)) + carry_ref[...]
    r1 = jnp.sum(jnp.where(sel1, rank, 0.0), axis=-1, keepdims=True)
    r2 = jnp.sum(jnp.where(sel2, rank, 0.0), axis=-1, keepdims=True)
    carry_ref[...] += jnp.sum(sel, axis=0, keepdims=True)
    cnt_ref[...] = carry_ref[...]
    info = jnp.zeros_like(lg)
    for idx, val in enumerate((i1 - N_GROUPS, i2 - N_GROUPS, r1, r2, gate1, gate2)):
        info = jnp.where(lane == idx, val, info)
    info_ref[...] = info


def _routing(logits):
    t = logits.shape[0]
    tr = 512
    return pl.pallas_call(
        _routing_kernel,
        grid=(t // tr,),
        in_specs=[pl.BlockSpec((tr, ROUTER_LANES), lambda i: (i, 0))],
        out_specs=[pl.BlockSpec((tr, ROUTER_LANES), lambda i: (i, 0)),
                   pl.BlockSpec((1, ROUTER_LANES), lambda i: (0, 0))],
        out_shape=[jax.ShapeDtypeStruct((t, ROUTER_LANES), F32),
                   jax.ShapeDtypeStruct((1, ROUTER_LANES), F32)],
        scratch_shapes=[pltpu.VMEM((1, ROUTER_LANES), F32)],
        compiler_params=_cparams("arbitrary"),
        name="routing",
    )(logits)


def _row_copy(src_ref, src_row, dst_ref, dst_row, sem):
    return pltpu.make_async_copy(src_ref.at[pl.ds(src_row, 1)], dst_ref.at[pl.ds(dst_row, 1)], sem)


def _dispatch_kernel(d1_ref, d2_ref, pad_end_ref, count_ref, h_ref, xs_ref, zero_ref, sem, zero_sem):
    td = h_ref.shape[0]
    base = pl.program_id(0) * td

    @pl.when(pl.program_id(0) == 0)
    def _():
        zero_ref[...] = jnp.zeros_like(zero_ref)

        def tail_copy(e):
            start = pl.multiple_of(pad_end_ref[e] - MOE_BLOCK, MOE_BLOCK)
            return pltpu.make_async_copy(zero_ref, xs_ref.at[pl.ds(start, MOE_BLOCK)], zero_sem)

        for e in range(N_EXPERTS):
            @pl.when(count_ref[e] > 0)
            def _():
                tail_copy(e).start()

        for e in range(N_EXPERTS):
            @pl.when(count_ref[e] > 0)
            def _():
                tail_copy(e).wait()

        def unused_copy(j):
            start = pl.multiple_of(j * MOE_BLOCK, MOE_BLOCK)
            return pltpu.make_async_copy(zero_ref, xs_ref.at[pl.ds(start, MOE_BLOCK)], zero_sem)

        first_unused = pad_end_ref[N_EXPERTS - 1] // MOE_BLOCK
        n_blocks = xs_ref.shape[0] // MOE_BLOCK

        def start_unused(j, carry):
            unused_copy(j).start()
            return carry

        def wait_unused(j, carry):
            unused_copy(j).wait()
            return carry

        lax.fori_loop(first_unused, n_blocks, start_unused, 0)
        lax.fori_loop(first_unused, n_blocks, wait_unused, 0)

    def issue(r, carry):
        for dref in (d1_ref, d2_ref):
            _row_copy(h_ref, r, xs_ref, dref[base + r], sem).start()
        return carry

    lax.fori_loop(0, td, issue, 0)

    def drain(r, carry):
        for _ in range(2):
            _row_copy(h_ref, 0, xs_ref, 0, sem).wait()
        return carry

    lax.fori_loop(0, td, drain, 0)


def _dispatch(h2, dest1, dest2, pad_end, counts, n_rows):
    t, d = h2.shape
    td = 256
    grid_spec = pltpu.PrefetchScalarGridSpec(
        num_scalar_prefetch=4,
        grid=(t // td,),
        in_specs=[pl.BlockSpec((td, d), lambda i, *_: (i, 0))],
        out_specs=pl.BlockSpec(memory_space=pl.ANY),
        scratch_shapes=[pltpu.VMEM((MOE_BLOCK, d), h2.dtype), pltpu.SemaphoreType.DMA(()),
                        pltpu.SemaphoreType.DMA(())],
    )
    return pl.pallas_call(
        _dispatch_kernel,
        grid_spec=grid_spec,
        out_shape=jax.ShapeDtypeStruct((n_rows, d), h2.dtype),
        compiler_params=_cparams("arbitrary"),
        name="moe_dispatch",
    )(dest1, dest2, pad_end, counts, h2)


def _expert_kernel(blk_e_ref, nvalid_ref, xs_ref, wg_ref, wu_ref, wd_ref, ys_ref, wg_bf, wu_bf, wd_bf):
    j = pl.program_id(0)

    @pl.when((j == 0) | (blk_e_ref[j] != blk_e_ref[jnp.maximum(j - 1, 0)]))
    def _():
        wg_bf[...] = wg_ref[0, 0].astype(BF16)
        wu_bf[...] = wu_ref[0, 0].astype(BF16)
        wd_bf[...] = wd_ref[0, 0].astype(BF16)

    @pl.when(j < nvalid_ref[0])
    def _():
        xb = xs_ref[...].astype(BF16)
        gate = _dot(xb, wg_bf[...])
        up = _dot(xb, wu_bf[...])
        hdn = (gate * jax.nn.sigmoid(gate)) * up
        ys_ref[...] = _dot(hdn.astype(BF16), wd_bf[...])

    @pl.when(j >= nvalid_ref[0])
    def _():
        ys_ref[...] = jnp.zeros_like(ys_ref)


def _expert_mlp(xs, blk_e, nvalid, w_gate, w_up, w_down, layer):
    n_rows, d = xs.shape
    nblk = n_rows // MOE_BLOCK
    de = w_gate.shape[3]

    def x_map(j, be, nv):
        return (jnp.minimum(j, nv[0] - 1), 0)

    w_map = lambda j, be, nv: (layer, be[j], 0, 0)
    grid_spec = pltpu.PrefetchScalarGridSpec(
        num_scalar_prefetch=2,
        grid=(nblk,),
        in_specs=[
            pl.BlockSpec((MOE_BLOCK, d), x_map),
            pl.BlockSpec((1, 1, d, de), w_map),
            pl.BlockSpec((1, 1, d, de), w_map),
            pl.BlockSpec((1, 1, de, d), w_map),
        ],
        out_specs=pl.BlockSpec((MOE_BLOCK, d), lambda j, be, nv: (j, 0)),
        scratch_shapes=[pltpu.VMEM((d, de), BF16), pltpu.VMEM((d, de), BF16), pltpu.VMEM((de, d), BF16)],
    )
    return pl.pallas_call(
        _expert_kernel,
        grid_spec=grid_spec,
        out_shape=jax.ShapeDtypeStruct((n_rows, d), F32),
        compiler_params=_cparams("arbitrary"),
        name="expert_mlp",
    )(blk_e, nvalid, xs, w_gate, w_up, w_down)


def _combine_kernel(final_norm, d1_ref, d2_ref, x_ref, mod_ref, info_ref, gf_ref, ys_ref, o_ref, buf_ref, sem):
    tc = x_ref.shape[0]
    base = pl.program_id(0) * tc

    def issue(r, carry):
        for slot, dref in enumerate((d1_ref, d2_ref)):
            _row_copy(ys_ref, dref[base + r], buf_ref.at[slot], r, sem).start()
        return carry

    lax.fori_loop(0, tc, issue, 0)

    def drain(r, carry):
        for slot in range(2):
            _row_copy(ys_ref, 0, buf_ref.at[slot], 0, sem).wait()
        return carry

    lax.fori_loop(0, tc, drain, 0)
    info = info_ref[...]
    moe = info[:, 4:5] * buf_ref[0] + info[:, 5:6] * buf_ref[1]
    x2 = x_ref[...] + mod_ref[0][5:6] * moe
    if final_norm:
        ms = jnp.mean(x2 * x2, axis=-1, keepdims=True)
        x2 = x2 * lax.rsqrt(ms + EPS) * gf_ref[...]
    o_ref[...] = x2


def _combine(x1, mod_l, info, ys, dest1, dest2, g_final, seq, final_norm):
    t, d = x1.shape
    tc = 256
    tiles_per_batch = seq // tc
    grid_spec = pltpu.PrefetchScalarGridSpec(
        num_scalar_prefetch=2,
        grid=(t // tc,),
        in_specs=[
            pl.BlockSpec((tc, d), lambda i, d1, d2: (i, 0)),
            pl.BlockSpec((1, 6, d), lambda i, d1, d2: (i // tiles_per_batch, 0, 0)),
            pl.BlockSpec((tc, ROUTER_LANES), lambda i, d1, d2: (i, 0)),
            pl.BlockSpec((1, d), lambda i, d1, d2: (0, 0)),
            pl.BlockSpec(memory_space=pl.ANY),
        ],
        out_specs=pl.BlockSpec((tc, d), lambda i, d1, d2: (i, 0)),
        scratch_shapes=[pltpu.VMEM((2, tc, d), F32), pltpu.SemaphoreType.DMA(())],
    )
    return pl.pallas_call(
        functools.partial(_combine_kernel, final_norm),
        grid_spec=grid_spec,
        out_shape=jax.ShapeDtypeStruct((t, d), F32),
        compiler_params=_cparams("arbitrary"),
        name="moe_combine",
    )(dest1, dest2, x1, mod_l, info, g_final.reshape(1, d), ys)


def _router_weights(w_rg, w_re):
    d = w_rg.shape[0]
    w_experts = jnp.transpose(w_re, (1, 0, 2)).reshape(d, N_EXPERTS)
    pad = jnp.zeros((d, ROUTER_LANES - N_GROUPS - N_EXPERTS), F32)
    return jnp.concatenate([w_rg, w_experts, pad], axis=1)


def _block_layout(counts_row, n_blocks):
    counts = counts_row[0, N_GROUPS:N_GROUPS + N_EXPERTS].astype(jnp.int32)
    padded = (counts + MOE_BLOCK - 1) // MOE_BLOCK * MOE_BLOCK
    pad_end = jnp.cumsum(padded)
    pad_start = pad_end - padded
    starts = jnp.arange(n_blocks, dtype=jnp.int32) * MOE_BLOCK
    blk_e = jnp.minimum(jnp.sum((pad_end[None, :] <= starts[:, None]).astype(jnp.int32), axis=1), N_EXPERTS - 1)
    nvalid = (pad_end[-1:] // MOE_BLOCK).astype(jnp.int32)
    return counts, pad_start, pad_end, blk_e, nvalid


def _destinations(info, pad_start):
    ids = info[:, 0:4].astype(jnp.int32)
    experts = jnp.arange(N_EXPERTS, dtype=jnp.int32)[None, :]

    def segment_start(e):
        return jnp.sum(jnp.where(e[:, None] == experts, pad_start[None, :], 0), axis=1)

    return segment_start(ids[:, 0]) + ids[:, 2], segment_start(ids[:, 1]) + ids[:, 3]


def kernel(x, c, w_ada, b_ada, g_norm1, g_norm2, w_in, diff_lambda, diff_subln, na_rpb, conv_dw, conv_b,
           conv_norm_g, conv_norm_b, w_out, w_router_group, w_router_expert, w_exp_gate, w_exp_up,
           w_exp_down, g_final):
    batch, seq, d = x.shape
    depth = w_ada.shape[0]
    t = batch * seq
    assert d == D_MODEL and seq == GRID_ROWS * GRID_W
    n_rows = t * 2 + N_EXPERTS * MOE_BLOCK
    n_blocks = n_rows // MOE_BLOCK

    mod = _ada_modulation(c, w_ada, b_ada).reshape(depth, batch, 6, d)
    rope_a = _rope_tables(seq, DIFF_DH)
    rope_d = _rope_tables(seq, HEAD_DIM)
    masks = _head_masks()
    gmat = _block_diag_ones(GROUP_WIDTH, HEAD_DIM)

    x2d = x.reshape(t, d)
    for l in range(depth):
        mod_l = mod[l]
        qa, ka, va, qb, kb, vb, pc, qd, kd, vd = _input_projection(
            x2d, mod_l, g_norm1[l], w_in[l].astype(BF16), rope_a, rope_d, seq)
        ya = _diff_attention(qa, ka, va, diff_lambda[l], diff_subln[l], l, batch, seq, masks, gmat)
        yb = _neighborhood_attention(qb, kb, vb, _na_bias_table(na_rpb[l]), batch, seq, masks)
        yc = _conformer_conv(pc, conv_dw[l], conv_b[l], conv_norm_g[l], conv_norm_b[l], batch, seq, gmat)
        yd = _dilated_attention(qd, kd, vd, batch, seq, masks)
        x1, h2, logits = _output_projection(
            (ya, yb, yc, yd), w_out[l].astype(BF16), x2d, mod_l, g_norm2[l],
            _router_weights(w_router_group[l], w_router_expert[l]), seq)
        info, counts = _routing(logits)
        counts, pad_start, pad_end, blk_e, nvalid = _block_layout(counts, n_blocks)
        dest1, dest2 = _destinations(info, pad_start)
        xs = _dispatch(h2, dest1, dest2, pad_end, counts, n_rows)
        ys = _expert_mlp(xs, blk_e, nvalid, w_exp_gate, w_exp_up, w_exp_down, l)
        x2d = _combine(x1, mod_l, info, ys, dest1, dest2, g_final, seq, final_norm=(l == depth - 1))
    return x2d.reshape(batch, seq, d)
```

```python
import functools
import math

import numpy as np
import jax
import jax.numpy as jnp
from jax import lax
from jax.experimental import pallas as pl
from jax.experimental.pallas import tpu as pltpu

F32 = jnp.float32
BF16 = jnp.bfloat16

D_MODEL = 1024
GROUP_WIDTH = 256
HEAD_DIM = 64
N_HEADS = 4
DIFF_DH = 32
CONV_K = 31
CONV_GROUP_CH = 64
GRID_W = 64
NA_KH = 8
NA_KW = 16
GRID_ROWS = 32
ROPE_THETA = 10000.0
N_GROUPS = 4
EXPERTS_PER_GROUP = 8
N_EXPERTS = 32
D_EXPERT = 512
MOE_BLOCK = 256
EPS = 1e-6
NEG_INF = -1e30
LOG2E = 1.4426950408889634
ROUTER_LANES = 128
DILATED_CFG = ((128, 1), (512, 4), (2048, 16))
DIFF_EXP2_SCALE = (DIFF_DH ** -0.5) * LOG2E

VMEM_LIMIT = 56 * 1024 * 1024


def _cparams(*sem):
    return pltpu.CompilerParams(dimension_semantics=sem, vmem_limit_bytes=VMEM_LIMIT)


def _dot(a, b):
    return jnp.dot(a, b, preferred_element_type=F32)


def _dot_nt(a, b):
    return lax.dot_general(a, b, (((1,), (1,)), ((), ())), preferred_element_type=F32)


def _split(a):
    hi = a.astype(BF16)
    lo = (a - hi.astype(F32)).astype(BF16)
    return hi, lo


def _dot3(a, b):
    ah, al = _split(a)
    bh, bl = _split(b)
    return _dot(ah, bh) + (_dot(ah, bl) + _dot(al, bh))


def _group_mean(v, gmat, width):
    hi, lo = _split(v)
    return (_dot(hi, gmat) + _dot(lo, gmat)) * (1.0 / width)


def _block_diag_ones(n, width):
    idx = np.arange(n) // width
    return jnp.asarray((idx[:, None] == idx[None, :]).astype(np.float32), dtype=BF16)


def _ada_kernel(c_ref, w_ref, b_ref, o_ref):
    c = c_ref[...]
    ca = c * jax.nn.sigmoid(c)
    o_ref[0] = _dot3(ca, w_ref[0]) + b_ref[0]


def _ada_modulation(c, w_ada, b_ada):
    depth, d, n = w_ada.shape
    b = c.shape[0]
    bn = 1024
    return pl.pallas_call(
        _ada_kernel,
        grid=(depth, n // bn),
        in_specs=[
            pl.BlockSpec((b, d), lambda l, j: (0, 0)),
            pl.BlockSpec((1, d, bn), lambda l, j: (l, 0, j)),
            pl.BlockSpec((1, 1, bn), lambda l, j: (l, 0, j)),
        ],
        out_specs=pl.BlockSpec((1, b, bn), lambda l, j: (l, 0, j)),
        out_shape=jax.ShapeDtypeStruct((depth, b, n), F32),
        compiler_params=_cparams("arbitrary", "arbitrary"),
        name="ada_modulation",
    )(c, w_ada, b_ada.reshape(depth, 1, n))


def _rope_tables(seq, dim):
    half = dim // 2
    inv = ROPE_THETA ** (-jnp.arange(0, dim, 2, dtype=F32) / dim)
    ang = jnp.arange(seq, dtype=F32)[:, None] * inv[None, :]
    cos, sin = jnp.cos(ang), jnp.sin(ang)
    reps = 128 // dim
    zeros = jnp.zeros_like(sin)
    cos_t = jnp.tile(jnp.concatenate([cos, cos], axis=1), (1, reps))
    sin_hi = jnp.tile(jnp.concatenate([zeros, sin], axis=1), (1, reps))
    sin_lo = jnp.tile(jnp.concatenate([-sin, zeros], axis=1), (1, reps))
    return cos_t, sin_hi, sin_lo


def _rotary(v, cos_t, sin_hi, sin_lo, half):
    outs = []
    for j in range(v.shape[1] // 128):
        vj = v[:, j * 128:(j + 1) * 128]
        outs.append(vj * cos_t + pltpu.roll(vj, half, 1) * sin_hi + pltpu.roll(vj, 128 - half, 1) * sin_lo)
    return jnp.concatenate(outs, axis=1)


def _inproj_kernel(x_ref, mod_ref, g_ref, w_ref, ca_ref, sha_ref, sla_ref, cd_ref, shd_ref, sld_ref,
                   qa_ref, ka_ref, va_ref, qb_ref, kb_ref, vb_ref, pc_ref, qd_ref, kd_ref, vd_ref):
    x = x_ref[...]
    ms = jnp.mean(x * x, axis=-1, keepdims=True)
    y = x * lax.rsqrt(ms + EPS)
    mod = mod_ref[0]
    h = (y * g_ref[...]) * (1.0 + mod[1:2]) + mod[0:1]
    hb = h.astype(BF16)
    gw = GROUP_WIDTH

    def proj(col):
        return _dot(hb, w_ref[:, col * gw:(col + 1) * gw])

    rot_a = functools.partial(_rotary, cos_t=ca_ref[...], sin_hi=sha_ref[...], sin_lo=sla_ref[...],
                              half=DIFF_DH // 2)
    rot_d = functools.partial(_rotary, cos_t=cd_ref[...], sin_hi=shd_ref[...], sin_lo=sld_ref[...],
                              half=HEAD_DIM // 2)
    na_scale = HEAD_DIM ** -0.5
    qa_ref[...] = (rot_a(proj(0)) * DIFF_EXP2_SCALE).astype(BF16)
    ka_ref[...] = rot_a(proj(1)).astype(BF16)
    va_ref[...] = proj(2).astype(BF16)
    qb_ref[...] = (proj(3) * na_scale).astype(BF16)
    kb_ref[...] = proj(4).astype(BF16)
    vb_ref[...] = proj(5).astype(BF16)
    pc_ref[:, 0:gw] = proj(6)
    pc_ref[:, gw:2 * gw] = proj(7)
    for ref, val in ((qd_ref, rot_d(proj(8)) * na_scale), (kd_ref, rot_d(proj(9))), (vd_ref, proj(10))):
        for j in range(gw // 128):
            ref[j] = val[:, j * 128:(j + 1) * 128]


def _input_projection(x2d, mod_l, g1, w_in_bf16, rope_a, rope_d, seq):
    t, d = x2d.shape
    tm = 512
    tiles_per_batch = seq // tm
    p_in = w_in_bf16.shape[1]
    gw = GROUP_WIDTH
    row_spec = lambda width: pl.BlockSpec((tm, width), lambda i: (i, 0))
    tab_spec = pl.BlockSpec((tm, 128), lambda i: (i % tiles_per_batch, 0))
    out_shapes = []
    out_specs = []
    for name in ("qa", "ka", "va", "qb", "kb", "vb", "pc", "qd", "kd", "vd"):
        if name == "pc":
            out_shapes.append(jax.ShapeDtypeStruct((t, 2 * gw), F32))
            out_specs.append(row_spec(2 * gw))
        elif name[1] == "d":
            out_shapes.append(jax.ShapeDtypeStruct((gw // 128, t, 128), F32))
            out_specs.append(pl.BlockSpec((gw // 128, tm, 128), lambda i: (0, i, 0)))
        else:
            out_shapes.append(jax.ShapeDtypeStruct((t, gw), BF16))
            out_specs.append(row_spec(gw))
    return pl.pallas_call(
        _inproj_kernel,
        grid=(t // tm,),
        in_specs=[
            row_spec(d),
            pl.BlockSpec((1, 6, d), lambda i: (i // tiles_per_batch, 0, 0)),
            pl.BlockSpec((1, d), lambda i: (0, 0)),
            pl.BlockSpec((d, p_in), lambda i: (0, 0)),
            tab_spec, tab_spec, tab_spec, tab_spec, tab_spec, tab_spec,
        ],
        out_specs=out_specs,
        out_shape=out_shapes,
        compiler_params=_cparams("arbitrary"),
        name="input_projection",
    )(x2d, mod_l, g1.reshape(1, d), w_in_bf16, *rope_a, *rope_d)


def _head_masks():
    lane = np.arange(GROUP_WIDTH)
    head = np.stack([(lane // HEAD_DIM == h) for h in range(N_HEADS)]).astype(np.float32)
    diff = np.stack([(lane // DIFF_DH == j) for j in range(2 * N_HEADS)]).astype(np.float32)
    return (jnp.asarray(head[:, None, :], dtype=BF16), jnp.asarray(head[:, None, :], dtype=F32),
            jnp.asarray(diff[:, None, :], dtype=BF16))


def _swap_lane_halves(a):
    return jnp.concatenate([pltpu.roll(a[:, j * 128:(j + 1) * 128], 64, 1) for j in range(a.shape[1] // 128)],
                           axis=1)


def _diff_attn_kernel(lam_init, q_ref, k_ref, v_ref, lp_ref, g_ref, dmask_ref, hmask_ref, hmask_f32_ref,
                      gmat_ref, o_ref, acc_ref):
    q = q_ref[...]
    k = k_ref[...]
    v = v_ref[...]
    lp = lp_ref[...]
    lam = (jnp.exp(jnp.sum(lp[0:1] * lp[1:2], axis=-1, keepdims=True))
           - jnp.exp(jnp.sum(lp[2:3] * lp[3:4], axis=-1, keepdims=True)) + lam_init)
    acc_ref[...] = jnp.zeros_like(acc_ref)

    def head(h, carry):
        rhs = v * hmask_ref[h] + hmask_ref[h ^ 1]
        parts = []
        for c in range(2):
            s = _dot_nt(q * dmask_ref[2 * h + c], k)
            m = jnp.max(s, axis=-1, keepdims=True)
            num = _dot(jnp.exp2(s - m).astype(BF16), rhs)
            parts.append(num / _swap_lane_halves(num))
        acc_ref[...] = jnp.where(hmask_f32_ref[h] > 0.5, parts[0] - lam * parts[1], acc_ref[...])
        return carry

    lax.fori_loop(0, N_HEADS, head, 0)
    o = acc_ref[...]
    ms = _group_mean(o * o, gmat_ref[...], HEAD_DIM)
    o_ref[...] = ((o * lax.rsqrt(ms + EPS) * g_ref[...]) * (1.0 - lam_init)).astype(BF16)


def _diff_attention(qa, ka, va, lam_params, subln_g, layer_idx, batch, seq, masks, gmat):
    t, gw = qa.shape
    tq = 256
    nq = seq // tq
    lam_init = 0.8 - 0.6 * math.exp(-0.3 * layer_idx)
    hmask, hmask_f32, dmask = masks
    g_tiled = jnp.tile(subln_g, N_HEADS).reshape(1, gw)
    kv_spec = pl.BlockSpec((seq, gw), lambda b, i: (b, 0))
    full = lambda shape: pl.BlockSpec(shape, lambda b, i: (0,) * len(shape))
    return pl.pallas_call(
        functools.partial(_diff_attn_kernel, lam_init),
        grid=(batch, nq),
        in_specs=[
            pl.BlockSpec((tq, gw), lambda b, i: (b * nq + i, 0)),
            kv_spec, kv_spec,
            full(lam_params.shape), full((1, gw)), full(dmask.shape), full(hmask.shape), full(hmask_f32.shape),
            full(gmat.shape),
        ],
        out_specs=pl.BlockSpec((tq, gw), lambda b, i: (b * nq + i, 0)),
        out_shape=jax.ShapeDtypeStruct((t, gw), BF16),
        scratch_shapes=[pltpu.VMEM((tq, gw), F32)],
        compiler_params=_cparams("arbitrary", "arbitrary"),
        name="diff_attention",
    )(qa, ka, va, lam_params, g_tiled, dmask, hmask, hmask_f32, gmat)


def _na_bias_table(rpb):
    w = GRID_W
    n_heads = rpb.shape[0]
    cq = np.arange(w)[:, None]
    ck = np.arange(w)[None, :]
    dc = np.clip(ck - cq, -(NA_KW - 1), NA_KW - 1) + NA_KW - 1
    col_start = np.clip(cq - NA_KW // 2, 0, w - NA_KW)
    col_ok = (ck >= col_start) & (ck < col_start + NA_KW)
    toep = jnp.where(col_ok[None, None], rpb.astype(F32)[:, :, dc], NEG_INF)
    tabs = jnp.stack([toep[:, NA_KH - 1 - off:2 * NA_KH - 1 - off] for off in range(NA_KH)])
    return jnp.transpose(tabs, (0, 1, 3, 2, 4)).reshape(NA_KH, n_heads * w, NA_KH * w)


def _na_kernel(q_ref, k_ref, v_ref, tab_ref, hmask_ref, hmask_f32_ref, o_ref):
    w = GRID_W
    nk = NA_KH * w

    def grid_row(i, carry):
        row_start = jnp.clip(i - NA_KH // 2, 0, GRID_ROWS - NA_KH)
        kstart = pl.multiple_of(row_start * w, w)
        qstart = pl.multiple_of(i * w, w)
        q = q_ref[pl.ds(qstart, w), :]
        qs = jnp.concatenate([q * hmask_ref[h] for h in range(N_HEADS)], axis=0)
        s = _dot_nt(qs, k_ref[pl.ds(kstart, nk), :]) + tab_ref[i - row_start]
        m = jnp.max(s, axis=-1, keepdims=True)
        e = jnp.exp(s - m)
        p = e * (1.0 / jnp.sum(e, axis=-1, keepdims=True))
        pv = _dot(p.astype(BF16), v_ref[pl.ds(kstart, nk), :])
        o = pv[0:w] * hmask_f32_ref[0]
        for h in range(1, N_HEADS):
            o = o + pv[h * w:(h + 1) * w] * hmask_f32_ref[h]
        o_ref[pl.ds(qstart, w), :] = o.astype(BF16)
        return carry

    lax.fori_loop(0, GRID_ROWS, grid_row, 0)


def _neighborhood_attention(qb, kb, vb, bias_table, batch, seq, masks):
    t, gw = qb.shape
    hmask, hmask_f32, _ = masks
    seq_spec = pl.BlockSpec((seq, gw), lambda b: (b, 0))
    full = lambda shape: pl.BlockSpec(shape, lambda b: (0,) * len(shape))
    return pl.pallas_call(
        _na_kernel,
        grid=(batch,),
        in_specs=[seq_spec, seq_spec, seq_spec, full(bias_table.shape), full(hmask.shape), full(hmask_f32.shape)],
        out_specs=seq_spec,
        out_shape=jax.ShapeDtypeStruct((t, gw), BF16),
        compiler_params=_cparams("arbitrary"),
        name="neighborhood_attention",
    )(qb, kb, vb, bias_table, hmask, hmask_f32)


CONV_PAD = 16
CONV_CHUNK = 128


def _conv_kernel(pc_ref, w_ref, b_ref, gn_ref, bn_ref, gmat_ref, o_ref, zp_ref):
    seq, ch = o_ref.shape
    a = pc_ref[:, 0:ch]
    gate = pc_ref[:, ch:2 * ch]
    zp_ref[0:CONV_PAD, :] = jnp.zeros((CONV_PAD, ch), F32)
    zp_ref[CONV_PAD + seq:2 * CONV_PAD + seq, :] = jnp.zeros((CONV_PAD, ch), F32)
    zp_ref[CONV_PAD:CONV_PAD + seq, :] = a * jax.nn.sigmoid(gate)
    gmat = gmat_ref[...]
    first = CONV_PAD - CONV_K // 2
    for c in range(seq // CONV_CHUNK):
        r0 = c * CONV_CHUNK
        acc = jnp.zeros((CONV_CHUNK, ch), F32)
        for j in range(CONV_K):
            acc = acc + w_ref[j] * zp_ref[r0 + first + j:r0 + first + j + CONV_CHUNK, :]
        z = acc + b_ref[...]
        mu = _group_mean(z, gmat, CONV_GROUP_CH)
        dz = z - mu
        var = _group_mean(dz * dz, gmat, CONV_GROUP_CH)
        zn = dz * lax.rsqrt(var + EPS) * gn_ref[...] + bn_ref[...]
        o_ref[r0:r0 + CONV_CHUNK, :] = (zn * jax.nn.sigmoid(zn)).astype(BF16)


def _conformer_conv(pc, w_dw, b_dw, g_n, b_n, batch, seq, gmat):
    t = pc.shape[0]
    ch = GROUP_WIDTH
    full = lambda shape: pl.BlockSpec(shape, lambda b: (0,) * len(shape))
    return pl.pallas_call(
        _conv_kernel,
        grid=(batch,),
        in_specs=[
            pl.BlockSpec((seq, 2 * ch), lambda b: (b, 0)),
            full((CONV_K, 1, ch)), full((1, ch)), full((1, ch)), full((1, ch)), full(gmat.shape),
        ],
        out_specs=pl.BlockSpec((seq, ch), lambda b: (b, 0)),
        out_shape=jax.ShapeDtypeStruct((t, ch), BF16),
        scratch_shapes=[pltpu.VMEM((seq + 2 * CONV_PAD, ch), F32)],
        compiler_params=_cparams("arbitrary"),
        name="conformer_conv",
    )(pc, w_dw.reshape(CONV_K, 1, ch), b_dw.reshape(1, ch), g_n.reshape(1, ch), b_n.reshape(1, ch), gmat)


DIL_QB = 128
DIL_KB = 256


def _strided_rows(first, count, stride):
    return pl.ds(first, count) if stride == 1 else pl.ds(first, count, stride=stride)


def _load_rows(ref, rows):
    return jnp.concatenate([ref[j, rows, :] for j in range(ref.shape[0])], axis=1)


def _store_rows(ref, rows, val):
    for j in range(ref.shape[0]):
        ref[j, rows, :] = val[:, j * 128:(j + 1) * 128]


def _dilated_kernel(q_ref, k_ref, v_ref, hmask_ref, hmask_f32_ref, o_ref, acc_ref, m_ref, l_ref):
    seq = q_ref.shape[1]
    for branch, (window, dil) in enumerate(DILATED_CFG):
        n_side = window // (2 * dil)
        sub_len = seq // dil
        qb = min(DIL_QB, sub_len)
        kb = min(DIL_KB, sub_len)
        row = lax.broadcasted_iota(jnp.int32, (N_HEADS * qb, kb), 0) & (qb - 1)
        col = lax.broadcasted_iota(jnp.int32, (N_HEADS * qb, kb), 1)
        for r in range(dil):

            def block(c, carry, r=r, dil=dil, n_side=n_side, sub_len=sub_len, qb=qb, kb=kb, row=row, col=col,
                      branch=branch):
                l0 = c * qb
                kl0 = jnp.clip(l0 - n_side, 0, sub_len - kb)
                if dil == 1:
                    l0, kl0 = pl.multiple_of(l0, qb), pl.multiple_of(kl0, n_side)
                q_rows = _strided_rows(r + dil * l0, qb, dil)
                k_rows = _strided_rows(r + dil * kl0, kb, dil)
                q = _load_rows(q_ref, q_rows).astype(BF16)
                qs = jnp.concatenate([q * hmask_ref[h] for h in range(N_HEADS)], axis=0)
                s = _dot_nt(qs, _load_rows(k_ref, k_rows).astype(BF16))
                s = jnp.where(jnp.abs(col + kl0 - row - l0) <= n_side, s, NEG_INF)
                m = jnp.max(s, axis=-1, keepdims=True)
                e = jnp.exp(s - m)
                l = jnp.sum(e, axis=-1, keepdims=True)
                pv = _dot(e.astype(BF16), _load_rows(v_ref, k_rows).astype(BF16))

                def unstack(a):
                    out = a[0:qb] * hmask_f32_ref[0]
                    for h in range(1, N_HEADS):
                        out = out + a[h * qb:(h + 1) * qb] * hmask_f32_ref[h]
                    return out

                acc_new, m_new, l_new = unstack(pv), unstack(m), unstack(l)
                if branch == 0:
                    _store_rows(acc_ref, q_rows, acc_new)
                    _store_rows(m_ref, q_rows, m_new)
                    _store_rows(l_ref, q_rows, l_new)
                else:
                    m_old = _load_rows(m_ref, q_rows)
                    m_max = jnp.maximum(m_old, m_new)
                    w_old = jnp.exp(m_old - m_max)
                    w_new = jnp.exp(m_new - m_max)
                    _store_rows(acc_ref, q_rows, _load_rows(acc_ref, q_rows) * w_old + acc_new * w_new)
                    _store_rows(l_ref, q_rows, _load_rows(l_ref, q_rows) * w_old + l_new * w_new)
                    _store_rows(m_ref, q_rows, m_max)
                return carry

            lax.fori_loop(0, sub_len // qb, block, 0)
    for j in range(acc_ref.shape[0]):
        o_ref[:, j * 128:(j + 1) * 128] = (acc_ref[j] / l_ref[j]).astype(BF16)


def _dilated_attention(qd, kd, vd, batch, seq, masks):
    tiles, t, _ = qd.shape
    gw = tiles * 128
    hmask, hmask_f32, _ = masks
    in_spec = pl.BlockSpec((tiles, seq, 128), lambda b: (0, b, 0))
    stat = pltpu.VMEM((tiles, seq, 128), F32)
    return pl.pallas_call(
        _dilated_kernel,
        grid=(batch,),
        in_specs=[
            in_spec, in_spec, in_spec,
            pl.BlockSpec(hmask.shape, lambda b: (0, 0, 0)),
            pl.BlockSpec(hmask_f32.shape, lambda b: (0, 0, 0)),
        ],
        out_specs=pl.BlockSpec((seq, gw), lambda b: (b, 0)),
        out_shape=jax.ShapeDtypeStruct((t, gw), BF16),
        scratch_shapes=[stat, stat, stat],
        compiler_params=_cparams("arbitrary"),
        name="dilated_attention",
    )(qd, kd, vd, hmask, hmask_f32)


def _outproj_kernel(ya_ref, yb_ref, yc_ref, yd_ref, w_ref, x_ref, mod_ref, g_ref, wr_ref,
                    x1_ref, h2_ref, lg_ref):
    gw = GROUP_WIDTH
    mix = _dot(ya_ref[...], w_ref[0:gw, :])
    mix = mix + _dot(yb_ref[...], w_ref[gw:2 * gw, :])
    mix = mix + _dot(yc_ref[...], w_ref[2 * gw:3 * gw, :])
    mix = mix + _dot(yd_ref[...], w_ref[3 * gw:4 * gw, :])
    mod = mod_ref[0]
    x1 = x_ref[...] + mod[2:3] * mix
    x1_ref[...] = x1
    ms = jnp.mean(x1 * x1, axis=-1, keepdims=True)
    h2 = (x1 * lax.rsqrt(ms + EPS) * g_ref[...]) * (1.0 + mod[4:5]) + mod[3:4]
    h2_ref[...] = h2
    lg_ref[...] = _dot3(h2, wr_ref[...])


def _output_projection(ys, w_out_bf16, x2d, mod_l, g2, w_router, seq):
    t, d = x2d.shape
    tm = 512
    tiles_per_batch = seq // tm
    gw = GROUP_WIDTH
    row_spec = lambda width: pl.BlockSpec((tm, width), lambda i: (i, 0))
    return pl.pallas_call(
        _outproj_kernel,
        grid=(t // tm,),
        in_specs=[
            row_spec(gw), row_spec(gw), row_spec(gw), row_spec(gw),
            pl.BlockSpec((d, d), lambda i: (0, 0)),
            row_spec(d),
            pl.BlockSpec((1, 6, d), lambda i: (i // tiles_per_batch, 0, 0)),
            pl.BlockSpec((1, d), lambda i: (0, 0)),
            pl.BlockSpec((d, ROUTER_LANES), lambda i: (0, 0)),
        ],
        out_specs=[row_spec(d), row_spec(d), row_spec(ROUTER_LANES)],
        out_shape=[jax.ShapeDtypeStruct((t, d), F32), jax.ShapeDtypeStruct((t, d), F32),
                   jax.ShapeDtypeStruct((t, ROUTER_LANES), F32)],
        compiler_params=_cparams("arbitrary"),
        name="output_projection",
    )(*ys, w_out_bf16, x2d, mod_l, g2.reshape(1, d), w_router)


def _routing_kernel(lg_ref, info_ref, cnt_ref, carry_ref):
    tr = lg_ref.shape[0]

    @pl.when(pl.program_id(0) == 0)
    def _():
        carry_ref[...] = jnp.zeros_like(carry_ref)

    lg = lg_ref[...]
    lane = lax.broadcasted_iota(jnp.int32, lg.shape, 1).astype(F32)
    big = float(ROUTER_LANES)
    glog = jnp.where(lane < N_GROUPS, lg, -jnp.inf)
    gmax = jnp.max(glog, axis=-1, keepdims=True)
    p_grp = 1.0 / jnp.sum(jnp.exp(glog - gmax), axis=-1, keepdims=True)
    grp = jnp.min(jnp.where(glog == gmax, lane, big), axis=-1, keepdims=True)
    lo = N_GROUPS + EXPERTS_PER_GROUP * grp
    elog = jnp.where((lane >= lo) & (lane < lo + EXPERTS_PER_GROUP), lg, -jnp.inf)
    v1 = jnp.max(elog, axis=-1, keepdims=True)
    i1 = jnp.min(jnp.where(elog == v1, lane, big), axis=-1, keepdims=True)
    elog2 = jnp.where(lane == i1, -jnp.inf, elog)
    v2 = jnp.max(elog2, axis=-1, keepdims=True)
    i2 = jnp.min(jnp.where(elog2 == v2, lane, big), axis=-1, keepdims=True)
    d = jnp.exp(v2 - v1)
    gate1 = p_grp / (1.0 + d)
    gate2 = p_grp * d / (1.0 + d)
    sel1 = lane == i1
    sel2 = lane == i2
    sel = jnp.where(sel1 | sel2, 1.0, 0.0)
    row = lax.broadcasted_iota(jnp.int32, (tr, tr), 0)
    col = lax.broadcasted_iota(jnp.int32, (tr, tr), 1)
    before = jnp.where(col < row, 1.0, 0.0).astype(BF16)
    rank = _dot(before, sel.astype(BF16)) + carry_ref[...]
    r1 = jnp.sum(jnp.where(sel1, rank, 0.0), axis=-1, keepdims=True)
    r2 = jnp.sum(jnp.where(sel2, rank, 0.0), axis=-1, keepdims=True)
    carry_ref[...] += jnp.sum(sel, axis=0, keepdims=True)
    cnt_ref[...] = carry_ref[...]
    info = jnp.zeros_like(lg)
    for idx, val in enumerate((i1 - N_GROUPS, i2 - N_GROUPS, r1, r2, gate1, gate2)):
        info = jnp.where(lane == idx, val, info)
    info_ref[...] = info


def _routing(logits):
    t = logits.shape[0]
    tr = 512
    return pl.pallas_call(
        _routing_kernel,
        grid=(t // tr,),
        in_specs=[pl.BlockSpec((tr, ROUTER_LANES), lambda i: (i, 0))],
        out_specs=[pl.BlockSpec((tr, ROUTER_LANES), lambda i: (i, 0)),
                   pl.BlockSpec((1, ROUTER_LANES), lambda i: (0, 0))],
        out_shape=[jax.ShapeDtypeStruct((t, ROUTER_LANES), F32),
                   jax.ShapeDtypeStruct((1, ROUTER_LANES), F32)],
        scratch_shapes=[pltpu.VMEM((1, ROUTER_LANES), F32)],
        compiler_params=_cparams("arbitrary"),
        name="routing",
    )(logits)


ROW_DMA_UNROLL = 8


def _row_copy(src_ref, src_row, dst_ref, dst_row, sem):
    return pltpu.make_async_copy(src_ref.at[pl.ds(src_row, 1)], dst_ref.at[pl.ds(dst_row, 1)], sem)


def _dispatch_kernel(d1_ref, d2_ref, pad_end_ref, count_ref, h_ref, xs_ref, zero_ref, sem, zero_sem):
    td = h_ref.shape[0]
    base = pl.program_id(0) * td

    @pl.when(pl.program_id(0) == 0)
    def _():
        zero_ref[...] = jnp.zeros_like(zero_ref)

        def tail_copy(e):
            start = pl.multiple_of(pad_end_ref[e] - MOE_BLOCK, MOE_BLOCK)
            return pltpu.make_async_copy(zero_ref, xs_ref.at[pl.ds(start, MOE_BLOCK)], zero_sem)

        for e in range(N_EXPERTS):
            @pl.when(count_ref[e] > 0)
            def _():
                tail_copy(e).start()

        for e in range(N_EXPERTS):
            @pl.when(count_ref[e] > 0)
            def _():
                tail_copy(e).wait()

        def unused_copy(j):
            start = pl.multiple_of(j * MOE_BLOCK, MOE_BLOCK)
            return pltpu.make_async_copy(zero_ref, xs_ref.at[pl.ds(start, MOE_BLOCK)], zero_sem)

        first_unused = pad_end_ref[N_EXPERTS - 1] // MOE_BLOCK
        n_blocks = xs_ref.shape[0] // MOE_BLOCK

        def start_unused(j, carry):
            unused_copy(j).start()
            return carry

        def wait_unused(j, carry):
            unused_copy(j).wait()
            return carry

        lax.fori_loop(first_unused, n_blocks, start_unused, 0)
        lax.fori_loop(first_unused, n_blocks, wait_unused, 0)

    def issue(r, carry):
        for slot, dref in enumerate((d1_ref, d2_ref)):
            _row_copy(h_ref, r, xs_ref, dref[base + r], sem).start(priority=slot)
        return carry

    lax.fori_loop(0, td, issue, 0, unroll=ROW_DMA_UNROLL)

    def drain(r, carry):
        for _ in range(2):
            _row_copy(h_ref, 0, xs_ref, 0, sem).wait()
        return carry

    lax.fori_loop(0, td, drain, 0, unroll=ROW_DMA_UNROLL)


def _dispatch(h2, dest1, dest2, pad_end, counts, n_rows):
    t, d = h2.shape
    td = 256
    grid_spec = pltpu.PrefetchScalarGridSpec(
        num_scalar_prefetch=4,
        grid=(t // td,),
        in_specs=[pl.BlockSpec((td, d), lambda i, *_: (i, 0))],
        out_specs=pl.BlockSpec(memory_space=pl.ANY),
        scratch_shapes=[pltpu.VMEM((MOE_BLOCK, d), h2.dtype), pltpu.SemaphoreType.DMA(()),
                        pltpu.SemaphoreType.DMA(())],
    )
    return pl.pallas_call(
        _dispatch_kernel,
        grid_spec=grid_spec,
        out_shape=jax.ShapeDtypeStruct((n_rows, d), h2.dtype),
        compiler_params=_cparams("arbitrary"),
        name="moe_dispatch",
    )(dest1, dest2, pad_end, counts, h2)


def _expert_kernel(blk_e_ref, nvalid_ref, xs_ref, wg_ref, wu_ref, wd_ref, ys_ref, wg_bf, wu_bf, wd_bf):
    j = pl.program_id(0)

    @pl.when((j == 0) | (blk_e_ref[j] != blk_e_ref[jnp.maximum(j - 1, 0)]))
    def _():
        wg_bf[...] = wg_ref[0, 0].astype(BF16)
        wu_bf[...] = wu_ref[0, 0].astype(BF16)
        wd_bf[...] = wd_ref[0, 0].astype(BF16)

    @pl.when(j < nvalid_ref[0])
    def _():
        xb = xs_ref[...].astype(BF16)
        gate = _dot(xb, wg_bf[...])
        up = _dot(xb, wu_bf[...])
        hdn = (gate * jax.nn.sigmoid(gate)) * up
        ys_ref[...] = _dot(hdn.astype(BF16), wd_bf[...])

    @pl.when(j >= nvalid_ref[0])
    def _():
        ys_ref[...] = jnp.zeros_like(ys_ref)


def _expert_mlp(xs, blk_e, nvalid, w_gate, w_up, w_down, layer):
    n_rows, d = xs.shape
    nblk = n_rows // MOE_BLOCK
    de = w_gate.shape[3]

    def x_map(j, be, nv):
        return (jnp.minimum(j, nv[0] - 1), 0)

    w_map = lambda j, be, nv: (layer, be[j], 0, 0)
    grid_spec = pltpu.PrefetchScalarGridSpec(
        num_scalar_prefetch=2,
        grid=(nblk,),
        in_specs=[
            pl.BlockSpec((MOE_BLOCK, d), x_map),
            pl.BlockSpec((1, 1, d, de), w_map),
            pl.BlockSpec((1, 1, d, de), w_map),
            pl.BlockSpec((1, 1, de, d), w_map),
        ],
        out_specs=pl.BlockSpec((MOE_BLOCK, d), lambda j, be, nv: (j, 0)),
        scratch_shapes=[pltpu.VMEM((d, de), BF16), pltpu.VMEM((d, de), BF16), pltpu.VMEM((de, d), BF16)],
    )
    return pl.pallas_call(
        _expert_kernel,
        grid_spec=grid_spec,
        out_shape=jax.ShapeDtypeStruct((n_rows, d), F32),
        compiler_params=_cparams("arbitrary"),
        name="expert_mlp",
    )(blk_e, nvalid, xs, w_gate, w_up, w_down)


def _combine_kernel(final_norm, d1_ref, d2_ref, x_ref, mod_ref, info_ref, gf_ref, ys_ref, o_ref, buf_ref, sem):
    tc = x_ref.shape[0]
    base = pl.program_id(0) * tc

    def issue(r, carry):
        for slot, dref in enumerate((d1_ref, d2_ref)):
            _row_copy(ys_ref, dref[base + r], buf_ref.at[slot], r, sem).start(priority=slot)
        return carry

    lax.fori_loop(0, tc, issue, 0, unroll=ROW_DMA_UNROLL)

    def drain(r, carry):
        for slot in range(2):
            _row_copy(ys_ref, 0, buf_ref.at[slot], 0, sem).wait()
        return carry

    lax.fori_loop(0, tc, drain, 0, unroll=ROW_DMA_UNROLL)
    info = info_ref[...]
    moe = info[:, 4:5] * buf_ref[0] + info[:, 5:6] * buf_ref[1]
    x2 = x_ref[...] + mod_ref[0][5:6] * moe
    if final_norm:
        ms = jnp.mean(x2 * x2, axis=-1, keepdims=True)
        x2 = x2 * lax.rsqrt(ms + EPS) * gf_ref[...]
    o_ref[...] = x2


def _combine(x1, mod_l, info, ys, dest1, dest2, g_final, seq, final_norm):
    t, d = x1.shape
    tc = 256
    tiles_per_batch = seq // tc
    grid_spec = pltpu.PrefetchScalarGridSpec(
        num_scalar_prefetch=2,
        grid=(t // tc,),
        in_specs=[
            pl.BlockSpec((tc, d), lambda i, d1, d2: (i, 0)),
            pl.BlockSpec((1, 6, d), lambda i, d1, d2: (i // tiles_per_batch, 0, 0)),
            pl.BlockSpec((tc, ROUTER_LANES), lambda i, d1, d2: (i, 0)),
            pl.BlockSpec((1, d), lambda i, d1, d2: (0, 0)),
            pl.BlockSpec(memory_space=pl.ANY),
        ],
        out_specs=pl.BlockSpec((tc, d), lambda i, d1, d2: (i, 0)),
        scratch_shapes=[pltpu.VMEM((2, tc, d), F32), pltpu.SemaphoreType.DMA(())],
    )
    return pl.pallas_call(
        functools.partial(_combine_kernel, final_norm),
        grid_spec=grid_spec,
        out_shape=jax.ShapeDtypeStruct((t, d), F32),
        compiler_params=_cparams("arbitrary"),
        name="moe_combine",
    )(dest1, dest2, x1, mod_l, info, g_final.reshape(1, d), ys)


def _router_weights(w_rg, w_re):
    d = w_rg.shape[0]
    w_experts = jnp.transpose(w_re, (1, 0, 2)).reshape(d, N_EXPERTS)
    pad = jnp.zeros((d, ROUTER_LANES - N_GROUPS - N_EXPERTS), F32)
    return jnp.concatenate([w_rg, w_experts, pad], axis=1)


def _block_layout(counts_row, n_blocks):
    counts = counts_row[0, N_GROUPS:N_GROUPS + N_EXPERTS].astype(jnp.int32)
    padded = (counts + MOE_BLOCK - 1) // MOE_BLOCK * MOE_BLOCK
    pad_end = jnp.cumsum(padded)
    pad_start = pad_end - padded
    starts = jnp.arange(n_blocks, dtype=jnp.int32) * MOE_BLOCK
    blk_e = jnp.minimum(jnp.sum((pad_end[None, :] <= starts[:, None]).astype(jnp.int32), axis=1), N_EXPERTS - 1)
    nvalid = (pad_end[-1:] // MOE_BLOCK).astype(jnp.int32)
    return counts, pad_start, pad_end, blk_e, nvalid


def _destinations(info, pad_start):
    ids = info[:, 0:4].astype(jnp.int32)
    experts = jnp.arange(N_EXPERTS, dtype=jnp.int32)[None, :]

    def segment_start(e):
        return jnp.sum(jnp.where(e[:, None] == experts, pad_start[None, :], 0), axis=1)

    return segment_start(ids[:, 0]) + ids[:, 2], segment_start(ids[:, 1]) + ids[:, 3]


def kernel(x, c, w_ada, b_ada, g_norm1, g_norm2, w_in, diff_lambda, diff_subln, na_rpb, conv_dw, conv_b,
           conv_norm_g, conv_norm_b, w_out, w_router_group, w_router_expert, w_exp_gate, w_exp_up,
           w_exp_down, g_final):
    batch, seq, d = x.shape
    depth = w_ada.shape[0]
    t = batch * seq
    assert d == D_MODEL and seq == GRID_ROWS * GRID_W
    n_rows = t * 2 + N_EXPERTS * MOE_BLOCK
    n_blocks = n_rows // MOE_BLOCK

    mod = _ada_modulation(c, w_ada, b_ada).reshape(depth, batch, 6, d)
    rope_a = _rope_tables(seq, DIFF_DH)
    rope_d = _rope_tables(seq, HEAD_DIM)
    masks = _head_masks()
    gmat = _block_diag_ones(GROUP_WIDTH, HEAD_DIM)

    x2d = x.reshape(t, d)
    for l in range(depth):
        mod_l = mod[l]
        qa, ka, va, qb, kb, vb, pc, qd, kd, vd = _input_projection(
            x2d, mod_l, g_norm1[l], w_in[l].astype(BF16), rope_a, rope_d, seq)
        ya = _diff_attention(qa, ka, va, diff_lambda[l], diff_subln[l], l, batch, seq, masks, gmat)
        yb = _neighborhood_attention(qb, kb, vb, _na_bias_table(na_rpb[l]), batch, seq, masks)
        yc = _conformer_conv(pc, conv_dw[l], conv_b[l], conv_norm_g[l], conv_norm_b[l], batch, seq, gmat)
        yd = _dilated_attention(qd, kd, vd, batch, seq, masks)
        x1, h2, logits = _output_projection(
            (ya, yb, yc, yd), w_out[l].astype(BF16), x2d, mod_l, g_norm2[l],
            _router_weights(w_router_group[l], w_router_expert[l]), seq)
        info, counts = _routing(logits)
        counts, pad_start, pad_end, blk_e, nvalid = _block_layout(counts, n_blocks)
        dest1, dest2 = _destinations(info, pad_start)
        xs = _dispatch(h2, dest1, dest2, pad_end, counts, n_rows)
        ys = _expert_mlp(xs, blk_e, nvalid, w_exp_gate, w_exp_up, w_exp_down, l)
        x2d = _combine(x1, mod_l, info, ys, dest1, dest2, g_final, seq, final_norm=(l == depth - 1))
    return x2d.reshape(batch, seq, d)
```

```python
import functools
import math

import numpy as np
import jax
import jax.numpy as jnp
from jax import lax
from jax.experimental import pallas as pl
from jax.experimental.pallas import tpu as pltpu

F32 = jnp.float32
BF16 = jnp.bfloat16

D_MODEL = 1024
GROUP_WIDTH = 256
HEAD_DIM = 64
N_HEADS = 4
DIFF_DH = 32
CONV_K = 31
CONV_GROUP_CH = 64
GRID_W = 64
NA_KH = 8
NA_KW = 16
GRID_ROWS = 32
NA_ROWS_PER_ITER = 2
ROPE_THETA = 10000.0
N_GROUPS = 4
EXPERTS_PER_GROUP = 8
N_EXPERTS = 32
D_EXPERT = 512
MOE_BLOCK = 256
EPS = 1e-6
NEG_INF = -1e30
LOG2E = 1.4426950408889634
ROUTER_LANES = 128
DILATED_CFG = ((128, 1), (512, 4), (2048, 16))
DIFF_EXP2_SCALE = (DIFF_DH ** -0.5) * LOG2E

VMEM_LIMIT = 56 * 1024 * 1024


def _cparams(*sem):
    return pltpu.CompilerParams(dimension_semantics=sem, vmem_limit_bytes=VMEM_LIMIT)


def _dot(a, b):
    return jnp.dot(a, b, preferred_element_type=F32)


def _dot_nt(a, b):
    return lax.dot_general(a, b, (((1,), (1,)), ((), ())), preferred_element_type=F32)


def _split(a):
    hi = a.astype(BF16)
    lo = (a - hi.astype(F32)).astype(BF16)
    return hi, lo


def _dot3(a, b):
    ah, al = _split(a)
    bh, bl = _split(b)
    return _dot(ah, bh) + (_dot(ah, bl) + _dot(al, bh))


def _group_mean(v, gmat, width):
    hi, lo = _split(v)
    return (_dot(hi, gmat) + _dot(lo, gmat)) * (1.0 / width)


def _block_diag_ones(n, width):
    idx = np.arange(n) // width
    return jnp.asarray((idx[:, None] == idx[None, :]).astype(np.float32), dtype=BF16)


def _ada_kernel(c_ref, w_ref, b_ref, o_ref):
    c = c_ref[...]
    ca = c * jax.nn.sigmoid(c)
    o_ref[0] = _dot3(ca, w_ref[0]) + b_ref[0]


def _ada_modulation(c, w_ada, b_ada):
    depth, d, n = w_ada.shape
    b = c.shape[0]
    bn = 1024
    return pl.pallas_call(
        _ada_kernel,
        grid=(depth, n // bn),
        in_specs=[
            pl.BlockSpec((b, d), lambda l, j: (0, 0)),
            pl.BlockSpec((1, d, bn), lambda l, j: (l, 0, j)),
            pl.BlockSpec((1, 1, bn), lambda l, j: (l, 0, j)),
        ],
        out_specs=pl.BlockSpec((1, b, bn), lambda l, j: (l, 0, j)),
        out_shape=jax.ShapeDtypeStruct((depth, b, n), F32),
        compiler_params=_cparams("arbitrary", "arbitrary"),
        name="ada_modulation",
    )(c, w_ada, b_ada.reshape(depth, 1, n))


def _rope_tables(seq, dim):
    half = dim // 2
    inv = ROPE_THETA ** (-jnp.arange(0, dim, 2, dtype=F32) / dim)
    ang = jnp.arange(seq, dtype=F32)[:, None] * inv[None, :]
    cos, sin = jnp.cos(ang), jnp.sin(ang)
    reps = 128 // dim
    zeros = jnp.zeros_like(sin)
    cos_t = jnp.tile(jnp.concatenate([cos, cos], axis=1), (1, reps))
    sin_hi = jnp.tile(jnp.concatenate([zeros, sin], axis=1), (1, reps))
    sin_lo = jnp.tile(jnp.concatenate([-sin, zeros], axis=1), (1, reps))
    return cos_t, sin_hi, sin_lo


def _rotary(v, cos_t, sin_hi, sin_lo, half):
    outs = []
    for j in range(v.shape[1] // 128):
        vj = v[:, j * 128:(j + 1) * 128]
        outs.append(vj * cos_t + pltpu.roll(vj, half, 1) * sin_hi + pltpu.roll(vj, 128 - half, 1) * sin_lo)
    return jnp.concatenate(outs, axis=1)


def _inproj_kernel(x_ref, mod_ref, g_ref, w_ref, ca_ref, sha_ref, sla_ref, cd_ref, shd_ref, sld_ref,
                   qa_ref, ka_ref, va_ref, qb_ref, kb_ref, vb_ref, pc_ref, qd_ref, kd_ref, vd_ref):
    x = x_ref[...]
    ms = jnp.mean(x * x, axis=-1, keepdims=True)
    y = x * lax.rsqrt(ms + EPS)
    mod = mod_ref[0]
    h = (y * g_ref[...]) * (1.0 + mod[1:2]) + mod[0:1]
    hb = h.astype(BF16)
    gw = GROUP_WIDTH

    def proj(col):
        return _dot(hb, w_ref[:, col * gw:(col + 1) * gw])

    rot_a = functools.partial(_rotary, cos_t=ca_ref[...], sin_hi=sha_ref[...], sin_lo=sla_ref[...],
                              half=DIFF_DH // 2)
    rot_d = functools.partial(_rotary, cos_t=cd_ref[...], sin_hi=shd_ref[...], sin_lo=sld_ref[...],
                              half=HEAD_DIM // 2)
    na_scale = HEAD_DIM ** -0.5
    qa_ref[...] = (rot_a(proj(0)) * DIFF_EXP2_SCALE).astype(BF16)
    ka_ref[...] = rot_a(proj(1)).astype(BF16)
    va_ref[...] = proj(2).astype(BF16)
    qb_ref[...] = (proj(3) * na_scale).astype(BF16)
    kb_ref[...] = proj(4).astype(BF16)
    vb_ref[...] = proj(5).astype(BF16)
    pc_ref[:, 0:gw] = proj(6)
    pc_ref[:, gw:2 * gw] = proj(7)
    for ref, val in ((qd_ref, rot_d(proj(8)) * na_scale), (kd_ref, rot_d(proj(9))), (vd_ref, proj(10))):
        for j in range(gw // 128):
            ref[j] = val[:, j * 128:(j + 1) * 128]


def _input_projection(x2d, mod_l, g1, w_in_bf16, rope_a, rope_d, seq):
    t, d = x2d.shape
    tm = 512
    tiles_per_batch = seq // tm
    p_in = w_in_bf16.shape[1]
    gw = GROUP_WIDTH
    row_spec = lambda width: pl.BlockSpec((tm, width), lambda i: (i, 0))
    tab_spec = pl.BlockSpec((tm, 128), lambda i: (i % tiles_per_batch, 0))
    out_shapes = []
    out_specs = []
    for name in ("qa", "ka", "va", "qb", "kb", "vb", "pc", "qd", "kd", "vd"):
        if name == "pc":
            out_shapes.append(jax.ShapeDtypeStruct((t, 2 * gw), F32))
            out_specs.append(row_spec(2 * gw))
        elif name[1] == "d":
            out_shapes.append(jax.ShapeDtypeStruct((gw // 128, t, 128), F32))
            out_specs.append(pl.BlockSpec((gw // 128, tm, 128), lambda i: (0, i, 0)))
        else:
            out_shapes.append(jax.ShapeDtypeStruct((t, gw), BF16))
            out_specs.append(row_spec(gw))
    return pl.pallas_call(
        _inproj_kernel,
        grid=(t // tm,),
        in_specs=[
            row_spec(d),
            pl.BlockSpec((1, 6, d), lambda i: (i // tiles_per_batch, 0, 0)),
            pl.BlockSpec((1, d), lambda i: (0, 0)),
            pl.BlockSpec((d, p_in), lambda i: (0, 0)),
            tab_spec, tab_spec, tab_spec, tab_spec, tab_spec, tab_spec,
        ],
        out_specs=out_specs,
        out_shape=out_shapes,
        compiler_params=_cparams("arbitrary"),
        name="input_projection",
    )(x2d, mod_l, g1.reshape(1, d), w_in_bf16, *rope_a, *rope_d)


def _head_masks():
    lane = np.arange(GROUP_WIDTH)
    head = np.stack([(lane // HEAD_DIM == h) for h in range(N_HEADS)]).astype(np.float32)
    diff = np.stack([(lane // DIFF_DH == j) for j in range(2 * N_HEADS)]).astype(np.float32)
    return (jnp.asarray(head[:, None, :], dtype=BF16), jnp.asarray(head[:, None, :], dtype=F32),
            jnp.asarray(diff[:, None, :], dtype=BF16))


DIFF_ONES_ROWS = 16


def _diff_attn_kernel(lam_init, q_ref, k_ref, v_ref, lp_ref, g_ref, dmask_ref, gmat_ref, o_ref, vt_ref, ot_ref):
    seq = k_ref.shape[0]

    @pl.when(pl.program_id(1) == 0)
    def _():
        vt = jnp.transpose(v_ref[...].astype(F32))
        for h in range(N_HEADS):
            vt_ref[h, 0:HEAD_DIM, :] = vt[h * HEAD_DIM:(h + 1) * HEAD_DIM].astype(BF16)
            vt_ref[h, HEAD_DIM:HEAD_DIM + DIFF_ONES_ROWS, :] = jnp.ones((DIFF_ONES_ROWS, seq), BF16)

    q = q_ref[...]
    k = k_ref[...]
    lp = lp_ref[...]
    lam = (jnp.exp(jnp.sum(lp[0:1] * lp[1:2], axis=-1, keepdims=True))
           - jnp.exp(jnp.sum(lp[2:3] * lp[3:4], axis=-1, keepdims=True)) + lam_init)

    def scores(j):
        return _dot_nt(k, q * dmask_ref[j])

    def weights(st):
        return jnp.exp2(st - jnp.max(st, axis=0, keepdims=True)).astype(BF16)

    def attend(j, e):
        num = _dot(vt_ref[j // 2], e)
        return num[0:HEAD_DIM] / num[HEAD_DIM:HEAD_DIM + 1]

    n_pairs = 2 * N_HEADS
    outs = [None] * n_pairs
    st_next = scores(0)
    e_prev = None
    for j in range(n_pairs):
        st = st_next
        if j + 1 < n_pairs:
            st_next = scores(j + 1)
        e = weights(st)
        if e_prev is not None:
            outs[j - 1] = attend(j - 1, e_prev)
        e_prev = e
    outs[n_pairs - 1] = attend(n_pairs - 1, e_prev)
    for h in range(N_HEADS):
        ot_ref[h * HEAD_DIM:(h + 1) * HEAD_DIM, :] = outs[2 * h] - lam * outs[2 * h + 1]
    o = jnp.transpose(ot_ref[...])
    ms = _group_mean(o * o, gmat_ref[...], HEAD_DIM)
    o_ref[...] = ((o * lax.rsqrt(ms + EPS) * g_ref[...]) * (1.0 - lam_init)).astype(BF16)


def _diff_attention(qa, ka, va, lam_params, subln_g, layer_idx, batch, seq, masks, gmat):
    t, gw = qa.shape
    tq = 512
    nq = seq // tq
    lam_init = 0.8 - 0.6 * math.exp(-0.3 * layer_idx)
    _, _, dmask = masks
    g_tiled = jnp.tile(subln_g, N_HEADS).reshape(1, gw)
    kv_spec = pl.BlockSpec((seq, gw), lambda b, i: (b, 0))
    full = lambda shape: pl.BlockSpec(shape, lambda b, i: (0,) * len(shape))
    return pl.pallas_call(
        functools.partial(_diff_attn_kernel, lam_init),
        grid=(batch, nq),
        in_specs=[
            pl.BlockSpec((tq, gw), lambda b, i: (b * nq + i, 0)),
            kv_spec, kv_spec,
            full(lam_params.shape), full((1, gw)), full(dmask.shape), full(gmat.shape),
        ],
        out_specs=pl.BlockSpec((tq, gw), lambda b, i: (b * nq + i, 0)),
        out_shape=jax.ShapeDtypeStruct((t, gw), BF16),
        scratch_shapes=[pltpu.VMEM((N_HEADS, HEAD_DIM + DIFF_ONES_ROWS, seq), BF16), pltpu.VMEM((gw, tq), F32)],
        compiler_params=_cparams("arbitrary", "arbitrary"),
        name="diff_attention",
    )(qa, ka, va, lam_params, g_tiled, dmask, gmat)


def _na_bias_table(rpb):
    w = GRID_W
    n_heads = rpb.shape[0]
    cq = np.arange(w)[:, None]
    ck = np.arange(w)[None, :]
    dc = np.clip(ck - cq, -(NA_KW - 1), NA_KW - 1) + NA_KW - 1
    col_start = np.clip(cq - NA_KW // 2, 0, w - NA_KW)
    col_ok = (ck >= col_start) & (ck < col_start + NA_KW)
    onehot = jnp.asarray(dc[:, :, None] == np.arange(2 * NA_KW - 1), dtype=F32)
    toep = jnp.einsum('qkd,hrd->hrqk', onehot, rpb.astype(F32), precision=lax.Precision.HIGHEST)
    toep = jnp.where(col_ok[None, None], toep, NEG_INF)
    tabs = jnp.stack([toep[:, NA_KH - 1 - off:2 * NA_KH - 1 - off] for off in range(NA_KH)])
    return jnp.transpose(tabs, (0, 1, 3, 2, 4)).reshape(NA_KH, n_heads * w, NA_KH * w)


def _na_kernel(q_ref, k_ref, v_ref, tab_ref, hmask_ref, hmask_f32_ref, o_ref):
    w = GRID_W
    nk = NA_KH * w

    def scores(i):
        row_start = jnp.clip(i - NA_KH // 2, 0, GRID_ROWS - NA_KH)
        kstart = pl.multiple_of(row_start * w, w)
        q = q_ref[pl.ds(pl.multiple_of(i * w, w), w), :]
        qs = jnp.concatenate([q * hmask_ref[h] for h in range(N_HEADS)], axis=0)
        return _dot_nt(qs, k_ref[pl.ds(kstart, nk), :]) + tab_ref[i - row_start], kstart

    def attend(i, s, kstart):
        m = jnp.max(s, axis=-1, keepdims=True)
        e = jnp.exp(s - m)
        p = e * (1.0 / jnp.sum(e, axis=-1, keepdims=True))
        pv = _dot(p.astype(BF16), v_ref[pl.ds(kstart, nk), :])
        o = pv[0:w] * hmask_f32_ref[0]
        for h in range(1, N_HEADS):
            o = o + pv[h * w:(h + 1) * w] * hmask_f32_ref[h]
        o_ref[pl.ds(pl.multiple_of(i * w, w), w), :] = o.astype(BF16)

    def grid_rows(p, carry):
        rows = [NA_ROWS_PER_ITER * p + u for u in range(NA_ROWS_PER_ITER)]
        staged = [scores(i) for i in rows]
        for i, (s, kstart) in zip(rows, staged):
            attend(i, s, kstart)
        return carry

    lax.fori_loop(0, GRID_ROWS // NA_ROWS_PER_ITER, grid_rows, 0)


def _neighborhood_attention(qb, kb, vb, bias_table, batch, seq, masks):
    t, gw = qb.shape
    hmask, hmask_f32, _ = masks
    seq_spec = pl.BlockSpec((seq, gw), lambda b: (b, 0))
    full = lambda shape: pl.BlockSpec(shape, lambda b: (0,) * len(shape))
    return pl.pallas_call(
        _na_kernel,
        grid=(batch,),
        in_specs=[seq_spec, seq_spec, seq_spec, full(bias_table.shape), full(hmask.shape), full(hmask_f32.shape)],
        out_specs=seq_spec,
        out_shape=jax.ShapeDtypeStruct((t, gw), BF16),
        compiler_params=_cparams("arbitrary"),
        name="neighborhood_attention",
    )(qb, kb, vb, bias_table, hmask, hmask_f32)


CONV_PAD = 16
CONV_CHUNK = 128


def _conv_kernel(pc_ref, w_ref, b_ref, gn_ref, bn_ref, gmat_ref, o_ref, zp_ref):
    seq, ch = o_ref.shape
    a = pc_ref[:, 0:ch]
    gate = pc_ref[:, ch:2 * ch]
    zp_ref[0:CONV_PAD, :] = jnp.zeros((CONV_PAD, ch), F32)
    zp_ref[CONV_PAD + seq:2 * CONV_PAD + seq, :] = jnp.zeros((CONV_PAD, ch), F32)
    zp_ref[CONV_PAD:CONV_PAD + seq, :] = a * jax.nn.sigmoid(gate)
    gmat = gmat_ref[...]
    first = CONV_PAD - CONV_K // 2
    for c in range(seq // CONV_CHUNK):
        r0 = c * CONV_CHUNK
        acc = jnp.zeros((CONV_CHUNK, ch), F32)
        for j in range(CONV_K):
            acc = acc + w_ref[j] * zp_ref[r0 + first + j:r0 + first + j + CONV_CHUNK, :]
        z = acc + b_ref[...]
        mu = _group_mean(z, gmat, CONV_GROUP_CH)
        dz = z - mu
        var = _group_mean(dz * dz, gmat, CONV_GROUP_CH)
        zn = dz * lax.rsqrt(var + EPS) * gn_ref[...] + bn_ref[...]
        o_ref[r0:r0 + CONV_CHUNK, :] = (zn * jax.nn.sigmoid(zn)).astype(BF16)


def _conformer_conv(pc, w_dw, b_dw, g_n, b_n, batch, seq, gmat):
    t = pc.shape[0]
    ch = GROUP_WIDTH
    full = lambda shape: pl.BlockSpec(shape, lambda b: (0,) * len(shape))
    return pl.pallas_call(
        _conv_kernel,
        grid=(batch,),
        in_specs=[
            pl.BlockSpec((seq, 2 * ch), lambda b: (b, 0)),
            full((CONV_K, 1, ch)), full((1, ch)), full((1, ch)), full((1, ch)), full(gmat.shape),
        ],
        out_specs=pl.BlockSpec((seq, ch), lambda b: (b, 0)),
        out_shape=jax.ShapeDtypeStruct((t, ch), BF16),
        scratch_shapes=[pltpu.VMEM((seq + 2 * CONV_PAD, ch), F32)],
        compiler_params=_cparams("arbitrary"),
        name="conformer_conv",
    )(pc, w_dw.reshape(CONV_K, 1, ch), b_dw.reshape(1, ch), g_n.reshape(1, ch), b_n.reshape(1, ch), gmat)


DIL_QB = 128
DIL_KB = 256
DIL_UNITS_PER_ITER = 2


def _strided_rows(first, count, stride):
    return pl.ds(first, count) if stride == 1 else pl.ds(first, count, stride=stride)


def _load_rows(ref, rows):
    return jnp.concatenate([ref[j, rows, :] for j in range(ref.shape[0])], axis=1)


def _store_rows(ref, rows, val):
    for j in range(ref.shape[0]):
        ref[j, rows, :] = val[:, j * 128:(j + 1) * 128]


def _dilated_kernel(q_ref, k_ref, v_ref, hmask_ref, hmask_f32_ref, o_ref, acc_ref, m_ref, l_ref):
    seq = q_ref.shape[1]
    for branch, (window, dil) in enumerate(DILATED_CFG):
        n_side = window // (2 * dil)
        sub_len = seq // dil
        qb = min(DIL_QB, sub_len)
        kb = min(DIL_KB, sub_len)
        row = lax.broadcasted_iota(jnp.int32, (N_HEADS * qb, kb), 0) & (qb - 1)
        col = lax.broadcasted_iota(jnp.int32, (N_HEADS * qb, kb), 1)
        def scores(r, c, dil=dil, n_side=n_side, sub_len=sub_len, qb=qb, kb=kb, row=row, col=col):
            l0 = c * qb
            kl0 = jnp.clip(l0 - n_side, 0, sub_len - kb)
            if dil == 1:
                l0, kl0 = pl.multiple_of(l0, qb), pl.multiple_of(kl0, n_side)
            q_rows = _strided_rows(r + dil * l0, qb, dil)
            k_rows = _strided_rows(r + dil * kl0, kb, dil)
            q = _load_rows(q_ref, q_rows).astype(BF16)
            qs = jnp.concatenate([q * hmask_ref[h] for h in range(N_HEADS)], axis=0)
            s = _dot_nt(qs, _load_rows(k_ref, k_rows).astype(BF16))
            return jnp.where(jnp.abs(col + kl0 - row - l0) <= n_side, s, NEG_INF), q_rows, k_rows

        def attend(s, q_rows, k_rows, qb=qb, branch=branch):
            m = jnp.max(s, axis=-1, keepdims=True)
            e = jnp.exp(s - m)
            l = jnp.sum(e, axis=-1, keepdims=True)
            pv = _dot(e.astype(BF16), _load_rows(v_ref, k_rows).astype(BF16))

            def unstack(a):
                out = a[0:qb] * hmask_f32_ref[0]
                for h in range(1, N_HEADS):
                    out = out + a[h * qb:(h + 1) * qb] * hmask_f32_ref[h]
                return out

            acc_new, m_new, l_new = unstack(pv), unstack(m), unstack(l)
            if branch == 0:
                _store_rows(acc_ref, q_rows, acc_new)
                _store_rows(m_ref, q_rows, m_new)
                _store_rows(l_ref, q_rows, l_new)
            else:
                m_old = _load_rows(m_ref, q_rows)
                m_max = jnp.maximum(m_old, m_new)
                w_old = jnp.exp(m_old - m_max)
                w_new = jnp.exp(m_new - m_max)
                _store_rows(acc_ref, q_rows, _load_rows(acc_ref, q_rows) * w_old + acc_new * w_new)
                _store_rows(l_ref, q_rows, _load_rows(l_ref, q_rows) * w_old + l_new * w_new)
                _store_rows(m_ref, q_rows, m_max)

        def run(units, scores=scores, attend=attend):
            staged = [scores(r, c) for r, c in units]
            for item in staged:
                attend(*item)

        n_blocks = sub_len // qb
        if n_blocks >= DIL_UNITS_PER_ITER:
            for r in range(dil):
                def block_group(p, carry, r=r, run=run):
                    run([(r, DIL_UNITS_PER_ITER * p + u) for u in range(DIL_UNITS_PER_ITER)])
                    return carry

                lax.fori_loop(0, n_blocks // DIL_UNITS_PER_ITER, block_group, 0)
        else:
            for r0 in range(0, dil, DIL_UNITS_PER_ITER):
                run([(r0 + u, 0) for u in range(DIL_UNITS_PER_ITER)])
    for j in range(acc_ref.shape[0]):
        o_ref[:, j * 128:(j + 1) * 128] = (acc_ref[j] / l_ref[j]).astype(BF16)


def _dilated_attention(qd, kd, vd, batch, seq, masks):
    tiles, t, _ = qd.shape
    gw = tiles * 128
    hmask, hmask_f32, _ = masks
    in_spec = pl.BlockSpec((tiles, seq, 128), lambda b: (0, b, 0))
    stat = pltpu.VMEM((tiles, seq, 128), F32)
    return pl.pallas_call(
        _dilated_kernel,
        grid=(batch,),
        in_specs=[
            in_spec, in_spec, in_spec,
            pl.BlockSpec(hmask.shape, lambda b: (0, 0, 0)),
            pl.BlockSpec(hmask_f32.shape, lambda b: (0, 0, 0)),
        ],
        out_specs=pl.BlockSpec((seq, gw), lambda b: (b, 0)),
        out_shape=jax.ShapeDtypeStruct((t, gw), BF16),
        scratch_shapes=[stat, stat, stat],
        compiler_params=_cparams("arbitrary"),
        name="dilated_attention",
    )(qd, kd, vd, hmask, hmask_f32)


def _outproj_kernel(ya_ref, yb_ref, yc_ref, yd_ref, w_ref, x_ref, mod_ref, g_ref, wr_ref,
                    x1_ref, h2_ref, lg_ref):
    gw = GROUP_WIDTH
    mix = _dot(ya_ref[...], w_ref[0:gw, :])
    mix = mix + _dot(yb_ref[...], w_ref[gw:2 * gw, :])
    mix = mix + _dot(yc_ref[...], w_ref[2 * gw:3 * gw, :])
    mix = mix + _dot(yd_ref[...], w_ref[3 * gw:4 * gw, :])
    mod = mod_ref[0]
    x1 = x_ref[...] + mod[2:3] * mix
    x1_ref[...] = x1
    ms = jnp.mean(x1 * x1, axis=-1, keepdims=True)
    h2 = (x1 * lax.rsqrt(ms + EPS) * g_ref[...]) * (1.0 + mod[4:5]) + mod[3:4]
    h2_ref[...] = h2
    lg_ref[...] = _dot3(h2, wr_ref[...])


def _output_projection(ys, w_out_bf16, x2d, mod_l, g2, w_router, seq):
    t, d = x2d.shape
    tm = 512
    tiles_per_batch = seq // tm
    gw = GROUP_WIDTH
    row_spec = lambda width: pl.BlockSpec((tm, width), lambda i: (i, 0))
    return pl.pallas_call(
        _outproj_kernel,
        grid=(t // tm,),
        in_specs=[
            row_spec(gw), row_spec(gw), row_spec(gw), row_spec(gw),
            pl.BlockSpec((d, d), lambda i: (0, 0)),
            row_spec(d),
            pl.BlockSpec((1, 6, d), lambda i: (i // tiles_per_batch, 0, 0)),
            pl.BlockSpec((1, d), lambda i: (0, 0)),
            pl.BlockSpec((d, ROUTER_LANES), lambda i: (0, 0)),
        ],
        out_specs=[row_spec(d), row_spec(d), row_spec(ROUTER_LANES)],
        out_shape=[jax.ShapeDtypeStruct((t, d), F32), jax.ShapeDtypeStruct((t, d), F32),
                   jax.ShapeDtypeStruct((t, ROUTER_LANES), F32)],
        compiler_params=_cparams("arbitrary"),
        name="output_projection",
    )(*ys, w_out_bf16, x2d, mod_l, g2.reshape(1, d), w_router)


def _routing_kernel(lg_ref, info_ref, cnt_ref, carry_ref):
    tr = lg_ref.shape[0]

    @pl.when(pl.program_id(0) == 0)
    def _():
        carry_ref[...] = jnp.zeros_like(carry_ref)

    lg = lg_ref[...]
    lane = lax.broadcasted_iota(jnp.int32, lg.shape, 1).astype(F32)
    big = float(ROUTER_LANES)
    glog = jnp.where(lane < N_GROUPS, lg, -jnp.inf)
    gmax = jnp.max(glog, axis=-1, keepdims=True)
    p_grp = 1.0 / jnp.sum(jnp.exp(glog - gmax), axis=-1, keepdims=True)
    grp = jnp.min(jnp.where(glog == gmax, lane, big), axis=-1, keepdims=True)
    lo = N_GROUPS + EXPERTS_PER_GROUP * grp
    elog = jnp.where((lane >= lo) & (lane < lo + EXPERTS_PER_GROUP), lg, -jnp.inf)
    v1 = jnp.max(elog, axis=-1, keepdims=True)
    i1 = jnp.min(jnp.where(elog == v1, lane, big), axis=-1, keepdims=True)
    elog2 = jnp.where(lane == i1, -jnp.inf, elog)
    v2 = jnp.max(elog2, axis=-1, keepdims=True)
    i2 = jnp.min(jnp.where(elog2 == v2, lane, big), axis=-1, keepdims=True)
    d = jnp.exp(v2 - v1)
    gate1 = p_grp / (1.0 + d)
    gate2 = p_grp * d / (1.0 + d)
    sel1 = lane == i1
    sel2 = lane == i2
    sel = jnp.where(sel1 | sel2, 1.0, 0.0)
    row = lax.broadcasted_iota(jnp.int32, (tr, tr), 0)
    col = lax.broadcasted_iota(jnp.int32, (tr, tr), 1)
    before = jnp.where(col < row, 1.0, 0.0).astype(BF16)
    rank = _dot(before, sel.astype(BF16)) + carry_ref[...]
    r1 = jnp.sum(jnp.where(sel1, rank, 0.0), axis=-1, keepdims=True)
    r2 = jnp.sum(jnp.where(sel2, rank, 0.0), axis=-1, keepdims=True)
    carry_ref[...] += jnp.sum(sel, axis=0, keepdims=True)
    cnt_ref[...] = carry_ref[...]
    info = jnp.zeros_like(lg)
    for idx, val in enumerate((i1 - N_GROUPS, i2 - N_GROUPS, r1, r2, gate1, gate2)):
        info = jnp.where(lane == idx, val, info)
    info_ref[...] = info


def _routing(logits):
    t = logits.shape[0]
    tr = 512
    return pl.pallas_call(
        _routing_kernel,
        grid=(t // tr,),
        in_specs=[pl.BlockSpec((tr, ROUTER_LANES), lambda i: (i, 0))],
        out_specs=[pl.BlockSpec((tr, ROUTER_LANES), lambda i: (i, 0)),
                   pl.BlockSpec((1, ROUTER_LANES), lambda i: (0, 0))],
        out_shape=[jax.ShapeDtypeStruct((t, ROUTER_LANES), F32),
                   jax.ShapeDtypeStruct((1, ROUTER_LANES), F32)],
        scratch_shapes=[pltpu.VMEM((1, ROUTER_LANES), F32)],
        compiler_params=_cparams("arbitrary"),
        name="routing",
    )(logits)


ROW_DMA_UNROLL = 8


def _row_copy(src_ref, src_row, dst_ref, dst_row, sem):
    return pltpu.make_async_copy(src_ref.at[pl.ds(src_row, 1)], dst_ref.at[pl.ds(dst_row, 1)], sem)


def _dispatch_kernel(d1_ref, d2_ref, pad_end_ref, count_ref, h_ref, xs_ref, zero_ref, sem, zero_sem):
    td = h_ref.shape[0]
    base = pl.program_id(0) * td

    @pl.when(pl.program_id(0) == 0)
    def _():
        zero_ref[...] = jnp.zeros_like(zero_ref)

        def tail_copy(e):
            start = pl.multiple_of(pad_end_ref[e] - MOE_BLOCK, MOE_BLOCK)
            return pltpu.make_async_copy(zero_ref, xs_ref.at[pl.ds(start, MOE_BLOCK)], zero_sem)

        for e in range(N_EXPERTS):
            @pl.when(count_ref[e] > 0)
            def _():
                tail_copy(e).start()

        for e in range(N_EXPERTS):
            @pl.when(count_ref[e] > 0)
            def _():
                tail_copy(e).wait()

        def unused_copy(j):
            start = pl.multiple_of(j * MOE_BLOCK, MOE_BLOCK)
            return pltpu.make_async_copy(zero_ref, xs_ref.at[pl.ds(start, MOE_BLOCK)], zero_sem)

        first_unused = pad_end_ref[N_EXPERTS - 1] // MOE_BLOCK
        n_blocks = xs_ref.shape[0] // MOE_BLOCK

        def start_unused(j, carry):
            unused_copy(j).start()
            return carry

        def wait_unused(j, carry):
            unused_copy(j).wait()
            return carry

        lax.fori_loop(first_unused, n_blocks, start_unused, 0)
        lax.fori_loop(first_unused, n_blocks, wait_unused, 0)

    def issue(r, carry):
        for slot, dref in enumerate((d1_ref, d2_ref)):
            _row_copy(h_ref, r, xs_ref, dref[base + r], sem).start(priority=slot)
        return carry

    lax.fori_loop(0, td, issue, 0, unroll=ROW_DMA_UNROLL)

    def drain(r, carry):
        for _ in range(2):
            _row_copy(h_ref, 0, xs_ref, 0, sem).wait()
        return carry

    lax.fori_loop(0, td, drain, 0, unroll=ROW_DMA_UNROLL)


def _dispatch(h2, dest1, dest2, pad_end, counts, n_rows):
    t, d = h2.shape
    td = 256
    grid_spec = pltpu.PrefetchScalarGridSpec(
        num_scalar_prefetch=4,
        grid=(t // td,),
        in_specs=[pl.BlockSpec((td, d), lambda i, *_: (i, 0))],
        out_specs=pl.BlockSpec(memory_space=pl.ANY),
        scratch_shapes=[pltpu.VMEM((MOE_BLOCK, d), h2.dtype), pltpu.SemaphoreType.DMA(()),
                        pltpu.SemaphoreType.DMA(())],
    )
    return pl.pallas_call(
        _dispatch_kernel,
        grid_spec=grid_spec,
        out_shape=jax.ShapeDtypeStruct((n_rows, d), h2.dtype),
        compiler_params=_cparams("arbitrary"),
        name="moe_dispatch",
    )(dest1, dest2, pad_end, counts, h2)


def _expert_kernel(blk_e_ref, nvalid_ref, xs_ref, wg_ref, wu_ref, wd_ref, ys_ref, wg_bf, wu_bf, wd_bf):
    j = pl.program_id(0)

    @pl.when((j == 0) | (blk_e_ref[j] != blk_e_ref[jnp.maximum(j - 1, 0)]))
    def _():
        wg_bf[...] = wg_ref[0, 0].astype(BF16)
        wu_bf[...] = wu_ref[0, 0].astype(BF16)
        wd_bf[...] = wd_ref[0, 0].astype(BF16)

    @pl.when(j < nvalid_ref[0])
    def _():
        xb = xs_ref[...].astype(BF16)
        gate = _dot(xb, wg_bf[...])
        up = _dot(xb, wu_bf[...])
        hdn = (gate * jax.nn.sigmoid(gate)) * up
        ys_ref[...] = _dot(hdn.astype(BF16), wd_bf[...])

    @pl.when(j >= nvalid_ref[0])
    def _():
        ys_ref[...] = jnp.zeros_like(ys_ref)


def _expert_mlp(xs, blk_e, nvalid, w_gate, w_up, w_down, layer):
    n_rows, d = xs.shape
    nblk = n_rows // MOE_BLOCK
    de = w_gate.shape[3]

    def x_map(j, be, nv):
        return (jnp.minimum(j, nv[0] - 1), 0)

    w_map = lambda j, be, nv: (layer, be[j], 0, 0)
    grid_spec = pltpu.PrefetchScalarGridSpec(
        num_scalar_prefetch=2,
        grid=(nblk,),
        in_specs=[
            pl.BlockSpec((MOE_BLOCK, d), x_map),
            pl.BlockSpec((1, 1, d, de), w_map),
            pl.BlockSpec((1, 1, d, de), w_map),
            pl.BlockSpec((1, 1, de, d), w_map),
        ],
        out_specs=pl.BlockSpec((MOE_BLOCK, d), lambda j, be, nv: (j, 0)),
        scratch_shapes=[pltpu.VMEM((d, de), BF16), pltpu.VMEM((d, de), BF16), pltpu.VMEM((de, d), BF16)],
    )
    return pl.pallas_call(
        _expert_kernel,
        grid_spec=grid_spec,
        out_shape=jax.ShapeDtypeStruct((n_rows, d), F32),
        compiler_params=_cparams("arbitrary"),
        name="expert_mlp",
    )(blk_e, nvalid, xs, w_gate, w_up, w_down)


def _combine_kernel(final_norm, d1_ref, d2_ref, x_ref, mod_ref, info_ref, gf_ref, ys_ref, o_ref, buf_ref, sem):
    tc = x_ref.shape[0]
    base = pl.program_id(0) * tc

    def issue(r, carry):
        for slot, dref in enumerate((d1_ref, d2_ref)):
            _row_copy(ys_ref, dref[base + r], buf_ref.at[slot], r, sem).start(priority=slot)
        return carry

    lax.fori_loop(0, tc, issue, 0, unroll=ROW_DMA_UNROLL)

    def drain(r, carry):
        for slot in range(2):
            _row_copy(ys_ref, 0, buf_ref.at[slot], 0, sem).wait()
        return carry

    lax.fori_loop(0, tc, drain, 0, unroll=ROW_DMA_UNROLL)
    info = info_ref[...]
    moe = info[:, 4:5] * buf_ref[0] + info[:, 5:6] * buf_ref[1]
    x2 = x_ref[...] + mod_ref[0][5:6] * moe
    if final_norm:
        ms = jnp.mean(x2 * x2, axis=-1, keepdims=True)
        x2 = x2 * lax.rsqrt(ms + EPS) * gf_ref[...]
    o_ref[...] = x2


def _combine(x1, mod_l, info, ys, dest1, dest2, g_final, seq, final_norm):
    t, d = x1.shape
    tc = 256
    tiles_per_batch = seq // tc
    grid_spec = pltpu.PrefetchScalarGridSpec(
        num_scalar_prefetch=2,
        grid=(t // tc,),
        in_specs=[
            pl.BlockSpec((tc, d), lambda i, d1, d2: (i, 0)),
            pl.BlockSpec((1, 6, d), lambda i, d1, d2: (i // tiles_per_batch, 0, 0)),
            pl.BlockSpec((tc, ROUTER_LANES), lambda i, d1, d2: (i, 0)),
            pl.BlockSpec((1, d), lambda i, d1, d2: (0, 0)),
            pl.BlockSpec(memory_space=pl.ANY),
        ],
        out_specs=pl.BlockSpec((tc, d), lambda i, d1, d2: (i, 0)),
        scratch_shapes=[pltpu.VMEM((2, tc, d), F32), pltpu.SemaphoreType.DMA(())],
    )
    return pl.pallas_call(
        functools.partial(_combine_kernel, final_norm),
        grid_spec=grid_spec,
        out_shape=jax.ShapeDtypeStruct((t, d), F32),
        compiler_params=_cparams("arbitrary"),
        name="moe_combine",
    )(dest1, dest2, x1, mod_l, info, g_final.reshape(1, d), ys)


def _router_weights(w_rg, w_re):
    d = w_rg.shape[0]
    w_experts = jnp.transpose(w_re, (1, 0, 2)).reshape(d, N_EXPERTS)
    pad = jnp.zeros((d, ROUTER_LANES - N_GROUPS - N_EXPERTS), F32)
    return jnp.concatenate([w_rg, w_experts, pad], axis=1)


def _block_layout(counts_row, n_blocks):
    counts = counts_row[0, N_GROUPS:N_GROUPS + N_EXPERTS].astype(jnp.int32)
    padded = (counts + MOE_BLOCK - 1) // MOE_BLOCK * MOE_BLOCK
    pad_end = jnp.cumsum(padded)
    pad_start = pad_end - padded
    starts = jnp.arange(n_blocks, dtype=jnp.int32) * MOE_BLOCK
    blk_e = jnp.minimum(jnp.sum((pad_end[None, :] <= starts[:, None]).astype(jnp.int32), axis=1), N_EXPERTS - 1)
    nvalid = (pad_end[-1:] // MOE_BLOCK).astype(jnp.int32)
    return counts, pad_start, pad_end, blk_e, nvalid


def _destinations(info, pad_start):
    ids = info[:, 0:4].astype(jnp.int32)
    experts = jnp.arange(N_EXPERTS, dtype=jnp.int32)[None, :]

    def segment_start(e):
        return jnp.sum(jnp.where(e[:, None] == experts, pad_start[None, :], 0), axis=1)

    return segment_start(ids[:, 0]) + ids[:, 2], segment_start(ids[:, 1]) + ids[:, 3]


def kernel(x, c, w_ada, b_ada, g_norm1, g_norm2, w_in, diff_lambda, diff_subln, na_rpb, conv_dw, conv_b,
           conv_norm_g, conv_norm_b, w_out, w_router_group, w_router_expert, w_exp_gate, w_exp_up,
           w_exp_down, g_final):
    batch, seq, d = x.shape
    depth = w_ada.shape[0]
    t = batch * seq
    assert d == D_MODEL and seq == GRID_ROWS * GRID_W
    n_rows = t * 2 + N_EXPERTS * MOE_BLOCK
    n_blocks = n_rows // MOE_BLOCK

    mod = _ada_modulation(c, w_ada, b_ada).reshape(depth, batch, 6, d)
    rope_a = _rope_tables(seq, DIFF_DH)
    rope_d = _rope_tables(seq, HEAD_DIM)
    masks = _head_masks()
    gmat = _block_diag_ones(GROUP_WIDTH, HEAD_DIM)

    x2d = x.reshape(t, d)
    for l in range(depth):
        mod_l = mod[l]
        qa, ka, va, qb, kb, vb, pc, qd, kd, vd = _input_projection(
            x2d, mod_l, g_norm1[l], w_in[l].astype(BF16), rope_a, rope_d, seq)
        ya = _diff_attention(qa, ka, va, diff_lambda[l], diff_subln[l], l, batch, seq, masks, gmat)
        yb = _neighborhood_attention(qb, kb, vb, _na_bias_table(na_rpb[l]), batch, seq, masks)
        yc = _conformer_conv(pc, conv_dw[l], conv_b[l], conv_norm_g[l], conv_norm_b[l], batch, seq, gmat)
        yd = _dilated_attention(qd, kd, vd, batch, seq, masks)
        x1, h2, logits = _output_projection(
            (ya, yb, yc, yd), w_out[l].astype(BF16), x2d, mod_l, g_norm2[l],
            _router_weights(w_router_group[l], w_router_expert[l]), seq)
        info, counts = _routing(logits)
        counts, pad_start, pad_end, blk_e, nvalid = _block_layout(counts, n_blocks)
        dest1, dest2 = _destinations(info, pad_start)
        xs = _dispatch(h2, dest1, dest2, pad_end, counts, n_rows)
        ys = _expert_mlp(xs, blk_e, nvalid, w_exp_gate, w_exp_up, w_exp_down, l)
        x2d = _combine(x1, mod_l, info, ys, dest1, dest2, g_final, seq, final_norm=(l == depth - 1))
    return x2d.reshape(batch, seq, d)
```

```python
import functools
import math

import numpy as np
import jax
import jax.numpy as jnp
from jax import lax
from jax.experimental import pallas as pl
from jax.experimental.pallas import tpu as pltpu

F32 = jnp.float32
BF16 = jnp.bfloat16

D_MODEL = 1024
GROUP_WIDTH = 256
HEAD_DIM = 64
N_HEADS = 4
DIFF_DH = 32
CONV_K = 31
CONV_GROUP_CH = 64
GRID_W = 64
NA_KH = 8
NA_KW = 16
GRID_ROWS = 32
NA_ROWS_PER_ITER = 2
ROPE_THETA = 10000.0
N_GROUPS = 4
EXPERTS_PER_GROUP = 8
N_EXPERTS = 32
D_EXPERT = 512
MOE_BLOCK = 256
EPS = 1e-6
NEG_INF = -1e30
LOG2E = 1.4426950408889634
ROUTER_LANES = 128
DILATED_CFG = ((128, 1), (512, 4), (2048, 16))
DIFF_EXP2_SCALE = (DIFF_DH ** -0.5) * LOG2E

VMEM_LIMIT = 56 * 1024 * 1024


def _cparams(*sem):
    return pltpu.CompilerParams(dimension_semantics=sem, vmem_limit_bytes=VMEM_LIMIT)


def _dot(a, b):
    return jnp.dot(a, b, preferred_element_type=F32)


def _dot_nt(a, b):
    return lax.dot_general(a, b, (((1,), (1,)), ((), ())), preferred_element_type=F32)


def _split(a):
    hi = a.astype(BF16)
    lo = (a - hi.astype(F32)).astype(BF16)
    return hi, lo


def _dot3(a, b):
    ah, al = _split(a)
    bh, bl = _split(b)
    return _dot(ah, bh) + (_dot(ah, bl) + _dot(al, bh))


def _group_mean(v, gmat, width):
    hi, lo = _split(v)
    return (_dot(hi, gmat) + _dot(lo, gmat)) * (1.0 / width)


def _block_diag_ones(n, width):
    idx = np.arange(n) // width
    return jnp.asarray((idx[:, None] == idx[None, :]).astype(np.float32), dtype=BF16)


def _ada_kernel(c_ref, w_ref, b_ref, o_ref):
    c = c_ref[...]
    ca = c * jax.nn.sigmoid(c)
    o_ref[0] = _dot3(ca, w_ref[0]) + b_ref[0]


def _ada_modulation(c, w_ada, b_ada):
    depth, d, n = w_ada.shape
    b = c.shape[0]
    bn = 1024
    return pl.pallas_call(
        _ada_kernel,
        grid=(depth, n // bn),
        in_specs=[
            pl.BlockSpec((b, d), lambda l, j: (0, 0)),
            pl.BlockSpec((1, d, bn), lambda l, j: (l, 0, j)),
            pl.BlockSpec((1, 1, bn), lambda l, j: (l, 0, j)),
        ],
        out_specs=pl.BlockSpec((1, b, bn), lambda l, j: (l, 0, j)),
        out_shape=jax.ShapeDtypeStruct((depth, b, n), F32),
        compiler_params=_cparams("arbitrary", "arbitrary"),
        name="ada_modulation",
    )(c, w_ada, b_ada.reshape(depth, 1, n))


def _rope_tables(seq, dim):
    half = dim // 2
    inv = ROPE_THETA ** (-jnp.arange(0, dim, 2, dtype=F32) / dim)
    ang = jnp.arange(seq, dtype=F32)[:, None] * inv[None, :]
    cos, sin = jnp.cos(ang), jnp.sin(ang)
    reps = 128 // dim
    zeros = jnp.zeros_like(sin)
    cos_t = jnp.tile(jnp.concatenate([cos, cos], axis=1), (1, reps))
    sin_hi = jnp.tile(jnp.concatenate([zeros, sin], axis=1), (1, reps))
    sin_lo = jnp.tile(jnp.concatenate([-sin, zeros], axis=1), (1, reps))
    return cos_t, sin_hi, sin_lo


def _rotary(v, cos_t, sin_hi, sin_lo, half):
    outs = []
    for j in range(v.shape[1] // 128):
        vj = v[:, j * 128:(j + 1) * 128]
        outs.append(vj * cos_t + pltpu.roll(vj, half, 1) * sin_hi + pltpu.roll(vj, 128 - half, 1) * sin_lo)
    return jnp.concatenate(outs, axis=1)


def _inproj_kernel(x_ref, mod_ref, g_ref, w_ref, ca_ref, sha_ref, sla_ref, cd_ref, shd_ref, sld_ref,
                   qa_ref, ka_ref, va_ref, qb_ref, kb_ref, vb_ref, pc_ref, qd_ref, kd_ref, vd_ref):
    x = x_ref[...]
    ms = jnp.mean(x * x, axis=-1, keepdims=True)
    y = x * lax.rsqrt(ms + EPS)
    mod = mod_ref[0]
    h = (y * g_ref[...]) * (1.0 + mod[1:2]) + mod[0:1]
    hb = h.astype(BF16)
    gw = GROUP_WIDTH

    def proj(col):
        return _dot(hb, w_ref[:, col * gw:(col + 1) * gw])

    rot_a = functools.partial(_rotary, cos_t=ca_ref[...], sin_hi=sha_ref[...], sin_lo=sla_ref[...],
                              half=DIFF_DH // 2)
    rot_d = functools.partial(_rotary, cos_t=cd_ref[...], sin_hi=shd_ref[...], sin_lo=sld_ref[...],
                              half=HEAD_DIM // 2)
    na_scale = HEAD_DIM ** -0.5
    qa_ref[...] = (rot_a(proj(0)) * DIFF_EXP2_SCALE).astype(BF16)
    ka_ref[...] = rot_a(proj(1)).astype(BF16)
    va_ref[...] = proj(2).astype(BF16)
    qb_ref[...] = (proj(3) * na_scale).astype(BF16)
    kb_ref[...] = proj(4).astype(BF16)
    vb_ref[...] = proj(5).astype(BF16)
    pc_ref[:, 0:gw] = proj(6)
    pc_ref[:, gw:2 * gw] = proj(7)
    for ref, val in ((qd_ref, rot_d(proj(8)) * na_scale), (kd_ref, rot_d(proj(9))), (vd_ref, proj(10))):
        for j in range(gw // 128):
            ref[j] = val[:, j * 128:(j + 1) * 128]


def _input_projection(x2d, mod_l, g1, w_in_bf16, rope_a, rope_d, seq):
    t, d = x2d.shape
    tm = 512
    tiles_per_batch = seq // tm
    p_in = w_in_bf16.shape[1]
    gw = GROUP_WIDTH
    row_spec = lambda width: pl.BlockSpec((tm, width), lambda i: (i, 0))
    tab_spec = pl.BlockSpec((tm, 128), lambda i: (i % tiles_per_batch, 0))
    out_shapes = []
    out_specs = []
    for name in ("qa", "ka", "va", "qb", "kb", "vb", "pc", "qd", "kd", "vd"):
        if name == "pc":
            out_shapes.append(jax.ShapeDtypeStruct((t, 2 * gw), F32))
            out_specs.append(row_spec(2 * gw))
        elif name[1] == "d":
            out_shapes.append(jax.ShapeDtypeStruct((gw // 128, t, 128), F32))
            out_specs.append(pl.BlockSpec((gw // 128, tm, 128), lambda i: (0, i, 0)))
        else:
            out_shapes.append(jax.ShapeDtypeStruct((t, gw), BF16))
            out_specs.append(row_spec(gw))
    return pl.pallas_call(
        _inproj_kernel,
        grid=(t // tm,),
        in_specs=[
            row_spec(d),
            pl.BlockSpec((1, 6, d), lambda i: (i // tiles_per_batch, 0, 0)),
            pl.BlockSpec((1, d), lambda i: (0, 0)),
            pl.BlockSpec((d, p_in), lambda i: (0, 0)),
            tab_spec, tab_spec, tab_spec, tab_spec, tab_spec, tab_spec,
        ],
        out_specs=out_specs,
        out_shape=out_shapes,
        compiler_params=_cparams("arbitrary"),
        name="input_projection",
    )(x2d, mod_l, g1.reshape(1, d), w_in_bf16, *rope_a, *rope_d)


def _head_masks():
    lane = np.arange(GROUP_WIDTH)
    head = np.stack([(lane // HEAD_DIM == h) for h in range(N_HEADS)]).astype(np.float32)
    diff = np.stack([(lane // DIFF_DH == j) for j in range(2 * N_HEADS)]).astype(np.float32)
    return (jnp.asarray(head[:, None, :], dtype=BF16), jnp.asarray(head[:, None, :], dtype=F32),
            jnp.asarray(diff[:, None, :], dtype=BF16))


DIFF_ONES_ROWS = 16


def _diff_attn_kernel(lam_init, q_ref, k_ref, v_ref, lp_ref, g_ref, dmask_ref, gmat_ref, o_ref, vt_ref, ot_ref):
    seq = k_ref.shape[0]

    @pl.when(pl.program_id(1) == 0)
    def _():
        vt = jnp.transpose(v_ref[...].astype(F32))
        for h in range(N_HEADS):
            vt_ref[h, 0:HEAD_DIM, :] = vt[h * HEAD_DIM:(h + 1) * HEAD_DIM].astype(BF16)
            vt_ref[h, HEAD_DIM:HEAD_DIM + DIFF_ONES_ROWS, :] = jnp.ones((DIFF_ONES_ROWS, seq), BF16)

    q = q_ref[...]
    k = k_ref[...]
    lp = lp_ref[...]
    lam = (jnp.exp(jnp.sum(lp[0:1] * lp[1:2], axis=-1, keepdims=True))
           - jnp.exp(jnp.sum(lp[2:3] * lp[3:4], axis=-1, keepdims=True)) + lam_init)

    def scores(j):
        return _dot_nt(k, q * dmask_ref[j])

    def weights(st):
        return jnp.exp2(st - jnp.max(st, axis=0, keepdims=True)).astype(BF16)

    def attend(j, e):
        num = _dot(vt_ref[j // 2], e)
        return num[0:HEAD_DIM] / num[HEAD_DIM:HEAD_DIM + 1]

    n_pairs = 2 * N_HEADS
    outs = [None] * n_pairs
    st_next = scores(0)
    e_prev = None
    for j in range(n_pairs):
        st = st_next
        if j + 1 < n_pairs:
            st_next = scores(j + 1)
        e = weights(st)
        if e_prev is not None:
            outs[j - 1] = attend(j - 1, e_prev)
        e_prev = e
    outs[n_pairs - 1] = attend(n_pairs - 1, e_prev)
    for h in range(N_HEADS):
        ot_ref[h * HEAD_DIM:(h + 1) * HEAD_DIM, :] = outs[2 * h] - lam * outs[2 * h + 1]
    o = jnp.transpose(ot_ref[...])
    ms = _group_mean(o * o, gmat_ref[...], HEAD_DIM)
    o_ref[...] = ((o * lax.rsqrt(ms + EPS) * g_ref[...]) * (1.0 - lam_init)).astype(BF16)


def _diff_attention(qa, ka, va, lam_params, subln_g, layer_idx, batch, seq, masks, gmat):
    t, gw = qa.shape
    tq = 512
    nq = seq // tq
    lam_init = 0.8 - 0.6 * math.exp(-0.3 * layer_idx)
    _, _, dmask = masks
    g_tiled = jnp.tile(subln_g, N_HEADS).reshape(1, gw)
    kv_spec = pl.BlockSpec((seq, gw), lambda b, i: (b, 0))
    full = lambda shape: pl.BlockSpec(shape, lambda b, i: (0,) * len(shape))
    return pl.pallas_call(
        functools.partial(_diff_attn_kernel, lam_init),
        grid=(batch, nq),
        in_specs=[
            pl.BlockSpec((tq, gw), lambda b, i: (b * nq + i, 0)),
            kv_spec, kv_spec,
            full(lam_params.shape), full((1, gw)), full(dmask.shape), full(gmat.shape),
        ],
        out_specs=pl.BlockSpec((tq, gw), lambda b, i: (b * nq + i, 0)),
        out_shape=jax.ShapeDtypeStruct((t, gw), BF16),
        scratch_shapes=[pltpu.VMEM((N_HEADS, HEAD_DIM + DIFF_ONES_ROWS, seq), BF16), pltpu.VMEM((gw, tq), F32)],
        compiler_params=_cparams("arbitrary", "arbitrary"),
        name="diff_attention",
    )(qa, ka, va, lam_params, g_tiled, dmask, gmat)


def _na_bias_table(rpb):
    w = GRID_W
    n_heads = rpb.shape[0]
    cq = np.arange(w)[:, None]
    ck = np.arange(w)[None, :]
    dc = np.clip(ck - cq, -(NA_KW - 1), NA_KW - 1) + NA_KW - 1
    col_start = np.clip(cq - NA_KW // 2, 0, w - NA_KW)
    col_ok = (ck >= col_start) & (ck < col_start + NA_KW)
    onehot = jnp.asarray(dc[:, :, None] == np.arange(2 * NA_KW - 1), dtype=F32)
    toep = jnp.einsum('qkd,hrd->hrqk', onehot, rpb.astype(F32), precision=lax.Precision.HIGHEST)
    toep = jnp.where(col_ok[None, None], toep, NEG_INF)
    tabs = jnp.stack([toep[:, NA_KH - 1 - off:2 * NA_KH - 1 - off] for off in range(NA_KH)])
    return jnp.transpose(tabs, (0, 1, 3, 2, 4)).reshape(NA_KH, n_heads * w, NA_KH * w)


def _na_kernel(q_ref, k_ref, v_ref, tab_ref, hmask_ref, hmask_f32_ref, o_ref):
    w = GRID_W
    nk = NA_KH * w

    def scores(i):
        row_start = jnp.clip(i - NA_KH // 2, 0, GRID_ROWS - NA_KH)
        kstart = pl.multiple_of(row_start * w, w)
        q = q_ref[pl.ds(pl.multiple_of(i * w, w), w), :]
        qs = jnp.concatenate([q * hmask_ref[h] for h in range(N_HEADS)], axis=0)
        return _dot_nt(qs, k_ref[pl.ds(kstart, nk), :]) + tab_ref[i - row_start], kstart

    def attend(i, s, kstart):
        m = jnp.max(s, axis=-1, keepdims=True)
        e = jnp.exp(s - m)
        p = e * (1.0 / jnp.sum(e, axis=-1, keepdims=True))
        pv = _dot(p.astype(BF16), v_ref[pl.ds(kstart, nk), :])
        o = pv[0:w] * hmask_f32_ref[0]
        for h in range(1, N_HEADS):
            o = o + pv[h * w:(h + 1) * w] * hmask_f32_ref[h]
        o_ref[pl.ds(pl.multiple_of(i * w, w), w), :] = o.astype(BF16)

    def grid_rows(p, carry):
        rows = [NA_ROWS_PER_ITER * p + u for u in range(NA_ROWS_PER_ITER)]
        staged = [scores(i) for i in rows]
        for i, (s, kstart) in zip(rows, staged):
            attend(i, s, kstart)
        return carry

    lax.fori_loop(0, GRID_ROWS // NA_ROWS_PER_ITER, grid_rows, 0)


def _neighborhood_attention(qb, kb, vb, bias_table, batch, seq, masks):
    t, gw = qb.shape
    hmask, hmask_f32, _ = masks
    seq_spec = pl.BlockSpec((seq, gw), lambda b: (b, 0))
    full = lambda shape: pl.BlockSpec(shape, lambda b: (0,) * len(shape))
    return pl.pallas_call(
        _na_kernel,
        grid=(batch,),
        in_specs=[seq_spec, seq_spec, seq_spec, full(bias_table.shape), full(hmask.shape), full(hmask_f32.shape)],
        out_specs=seq_spec,
        out_shape=jax.ShapeDtypeStruct((t, gw), BF16),
        compiler_params=_cparams("arbitrary"),
        name="neighborhood_attention",
    )(qb, kb, vb, bias_table, hmask, hmask_f32)


SUBLANES = 8
CONV_PAD = 16
CONV_CHUNK = 128


def _conv_kernel(pc_ref, w_ref, b_ref, gn_ref, bn_ref, gmat_ref, o_ref, zp_ref, zs_ref):
    seq, ch = o_ref.shape
    a = pc_ref[:, 0:ch]
    gate = pc_ref[:, ch:2 * ch]
    zp_ref[0:CONV_PAD, :] = jnp.zeros((CONV_PAD, ch), F32)
    zp_ref[CONV_PAD + seq:2 * CONV_PAD + seq, :] = jnp.zeros((CONV_PAD, ch), F32)
    zp_ref[CONV_PAD:CONV_PAD + seq, :] = a * jax.nn.sigmoid(gate)
    span = seq + 2 * CONV_PAD - SUBLANES
    for b in range(1, SUBLANES):
        zs_ref[b - 1, 0:span, :] = zp_ref[b:b + span, :]
    gmat = gmat_ref[...]
    first = CONV_PAD - CONV_K // 2
    for c in range(seq // CONV_CHUNK):
        r0 = c * CONV_CHUNK
        acc = jnp.zeros((CONV_CHUNK, ch), F32)
        for j in range(CONV_K):
            shift, aligned = (first + j) % SUBLANES, r0 + (first + j) // SUBLANES * SUBLANES
            src = zp_ref if shift == 0 else zs_ref.at[shift - 1]
            acc = acc + w_ref[j] * src[aligned:aligned + CONV_CHUNK, :]
        z = acc + b_ref[...]
        mu = _group_mean(z, gmat, CONV_GROUP_CH)
        dz = z - mu
        var = _group_mean(dz * dz, gmat, CONV_GROUP_CH)
        zn = dz * lax.rsqrt(var + EPS) * gn_ref[...] + bn_ref[...]
        o_ref[r0:r0 + CONV_CHUNK, :] = (zn * jax.nn.sigmoid(zn)).astype(BF16)


def _conformer_conv(pc, w_dw, b_dw, g_n, b_n, batch, seq, gmat):
    t = pc.shape[0]
    ch = GROUP_WIDTH
    full = lambda shape: pl.BlockSpec(shape, lambda b: (0,) * len(shape))
    return pl.pallas_call(
        _conv_kernel,
        grid=(batch,),
        in_specs=[
            pl.BlockSpec((seq, 2 * ch), lambda b: (b, 0)),
            full((CONV_K, 1, ch)), full((1, ch)), full((1, ch)), full((1, ch)), full(gmat.shape),
        ],
        out_specs=pl.BlockSpec((seq, ch), lambda b: (b, 0)),
        out_shape=jax.ShapeDtypeStruct((t, ch), BF16),
        scratch_shapes=[pltpu.VMEM((seq + 2 * CONV_PAD, ch), F32),
                        pltpu.VMEM((SUBLANES - 1, seq + 2 * CONV_PAD, ch), F32)],
        compiler_params=_cparams("arbitrary"),
        name="conformer_conv",
    )(pc, w_dw.reshape(CONV_K, 1, ch), b_dw.reshape(1, ch), g_n.reshape(1, ch), b_n.reshape(1, ch), gmat)


DIL_QB = 128
DIL_KB = 256
DIL_UNITS_PER_ITER = 2


def _strided_rows(first, count, stride):
    return pl.ds(first, count) if stride == 1 else pl.ds(first, count, stride=stride)


def _load_rows(ref, rows):
    return jnp.concatenate([ref[j, rows, :] for j in range(ref.shape[0])], axis=1)


def _store_rows(ref, rows, val):
    for j in range(ref.shape[0]):
        ref[j, rows, :] = val[:, j * 128:(j + 1) * 128]


def _dilated_kernel(q_ref, k_ref, v_ref, hmask_ref, hmask_f32_ref, o_ref, acc_ref, m_ref, l_ref):
    seq = q_ref.shape[1]
    for branch, (window, dil) in enumerate(DILATED_CFG):
        n_side = window // (2 * dil)
        sub_len = seq // dil
        qb = min(DIL_QB, sub_len)
        kb = min(DIL_KB, sub_len)
        row = lax.broadcasted_iota(jnp.int32, (N_HEADS * qb, kb), 0) & (qb - 1)
        col = lax.broadcasted_iota(jnp.int32, (N_HEADS * qb, kb), 1)
        def scores(r, c, dil=dil, n_side=n_side, sub_len=sub_len, qb=qb, kb=kb, row=row, col=col):
            l0 = c * qb
            kl0 = jnp.clip(l0 - n_side, 0, sub_len - kb)
            if dil == 1:
                l0, kl0 = pl.multiple_of(l0, qb), pl.multiple_of(kl0, n_side)
            q_rows = _strided_rows(r + dil * l0, qb, dil)
            k_rows = _strided_rows(r + dil * kl0, kb, dil)
            q = _load_rows(q_ref, q_rows).astype(BF16)
            qs = jnp.concatenate([q * hmask_ref[h] for h in range(N_HEADS)], axis=0)
            s = _dot_nt(qs, _load_rows(k_ref, k_rows).astype(BF16))
            return jnp.where(jnp.abs(col + kl0 - row - l0) <= n_side, s, NEG_INF), q_rows, k_rows

        def attend(s, q_rows, k_rows, qb=qb, branch=branch):
            m = jnp.max(s, axis=-1, keepdims=True)
            e = jnp.exp(s - m)
            l = jnp.sum(e, axis=-1, keepdims=True)
            pv = _dot(e.astype(BF16), _load_rows(v_ref, k_rows).astype(BF16))

            def unstack(a):
                out = a[0:qb] * hmask_f32_ref[0]
                for h in range(1, N_HEADS):
                    out = out + a[h * qb:(h + 1) * qb] * hmask_f32_ref[h]
                return out

            acc_new, m_new, l_new = unstack(pv), unstack(m), unstack(l)
            if branch == 0:
                _store_rows(acc_ref, q_rows, acc_new)
                _store_rows(m_ref, q_rows, m_new)
                _store_rows(l_ref, q_rows, l_new)
            else:
                m_old = _load_rows(m_ref, q_rows)
                m_max = jnp.maximum(m_old, m_new)
                w_old = jnp.exp(m_old - m_max)
                w_new = jnp.exp(m_new - m_max)
                _store_rows(acc_ref, q_rows, _load_rows(acc_ref, q_rows) * w_old + acc_new * w_new)
                _store_rows(l_ref, q_rows, _load_rows(l_ref, q_rows) * w_old + l_new * w_new)
                _store_rows(m_ref, q_rows, m_max)

        def run(units, scores=scores, attend=attend):
            staged = [scores(r, c) for r, c in units]
            for item in staged:
                attend(*item)

        n_blocks = sub_len // qb
        if n_blocks >= DIL_UNITS_PER_ITER:
            for r in range(dil):
                def block_group(p, carry, r=r, run=run):
                    run([(r, DIL_UNITS_PER_ITER * p + u) for u in range(DIL_UNITS_PER_ITER)])
                    return carry

                lax.fori_loop(0, n_blocks // DIL_UNITS_PER_ITER, block_group, 0)
        else:
            for r0 in range(0, dil, DIL_UNITS_PER_ITER):
                run([(r0 + u, 0) for u in range(DIL_UNITS_PER_ITER)])
    for j in range(acc_ref.shape[0]):
        o_ref[:, j * 128:(j + 1) * 128] = (acc_ref[j] / l_ref[j]).astype(BF16)


def _dilated_attention(qd, kd, vd, batch, seq, masks):
    tiles, t, _ = qd.shape
    gw = tiles * 128
    hmask, hmask_f32, _ = masks
    in_spec = pl.BlockSpec((tiles, seq, 128), lambda b: (0, b, 0))
    stat = pltpu.VMEM((tiles, seq, 128), F32)
    return pl.pallas_call(
        _dilated_kernel,
        grid=(batch,),
        in_specs=[
            in_spec, in_spec, in_spec,
            pl.BlockSpec(hmask.shape, lambda b: (0, 0, 0)),
            pl.BlockSpec(hmask_f32.shape, lambda b: (0, 0, 0)),
        ],
        out_specs=pl.BlockSpec((seq, gw), lambda b: (b, 0)),
        out_shape=jax.ShapeDtypeStruct((t, gw), BF16),
        scratch_shapes=[stat, stat, stat],
        compiler_params=_cparams("arbitrary"),
        name="dilated_attention",
    )(qd, kd, vd, hmask, hmask_f32)


def _outproj_kernel(ya_ref, yb_ref, yc_ref, yd_ref, w_ref, x_ref, mod_ref, g_ref, wr_ref,
                    x1_ref, h2_ref, lg_ref):
    gw = GROUP_WIDTH
    mix = _dot(ya_ref[...], w_ref[0:gw, :])
    mix = mix + _dot(yb_ref[...], w_ref[gw:2 * gw, :])
    mix = mix + _dot(yc_ref[...], w_ref[2 * gw:3 * gw, :])
    mix = mix + _dot(yd_ref[...], w_ref[3 * gw:4 * gw, :])
    mod = mod_ref[0]
    x1 = x_ref[...] + mod[2:3] * mix
    x1_ref[...] = x1
    ms = jnp.mean(x1 * x1, axis=-1, keepdims=True)
    h2 = (x1 * lax.rsqrt(ms + EPS) * g_ref[...]) * (1.0 + mod[4:5]) + mod[3:4]
    h2_ref[...] = h2
    lg_ref[...] = _dot(h2.astype(BF16), wr_ref[...])


def _output_projection(ys, w_out_bf16, x2d, mod_l, g2, w_router, seq):
    t, d = x2d.shape
    tm = 512
    tiles_per_batch = seq // tm
    gw = GROUP_WIDTH
    row_spec = lambda width: pl.BlockSpec((tm, width), lambda i: (i, 0))
    return pl.pallas_call(
        _outproj_kernel,
        grid=(t // tm,),
        in_specs=[
            row_spec(gw), row_spec(gw), row_spec(gw), row_spec(gw),
            pl.BlockSpec((d, d), lambda i: (0, 0)),
            row_spec(d),
            pl.BlockSpec((1, 6, d), lambda i: (i // tiles_per_batch, 0, 0)),
            pl.BlockSpec((1, d), lambda i: (0, 0)),
            pl.BlockSpec((d, ROUTER_LANES), lambda i: (0, 0)),
        ],
        out_specs=[row_spec(d), row_spec(d), row_spec(ROUTER_LANES)],
        out_shape=[jax.ShapeDtypeStruct((t, d), F32), jax.ShapeDtypeStruct((t, d), F32),
                   jax.ShapeDtypeStruct((t, ROUTER_LANES), F32)],
        compiler_params=_cparams("arbitrary"),
        name="output_projection",
    )(*ys, w_out_bf16, x2d, mod_l, g2.reshape(1, d), w_router)


def _routing_kernel(lg_ref, info_ref, cnt_ref, carry_ref):
    tr = lg_ref.shape[0]

    @pl.when(pl.program_id(0) == 0)
    def _():
        carry_ref[...] = jnp.zeros_like(carry_ref)

    lg = lg_ref[...]
    lane = lax.broadcasted_iota(jnp.int32, lg.shape, 1).astype(F32)
    big = float(ROUTER_LANES)
    glog = jnp.where(lane < N_GROUPS, lg, -jnp.inf)
    gmax = jnp.max(glog, axis=-1, keepdims=True)
    p_grp = 1.0 / jnp.sum(jnp.exp(glog - gmax), axis=-1, keepdims=True)
    grp = jnp.min(jnp.where(glog == gmax, lane, big), axis=-1, keepdims=True)
    lo = N_GROUPS + EXPERTS_PER_GROUP * grp
    elog = jnp.where((lane >= lo) & (lane < lo + EXPERTS_PER_GROUP), lg, -jnp.inf)
    v1 = jnp.max(elog, axis=-1, keepdims=True)
    i1 = jnp.min(jnp.where(elog == v1, lane, big), axis=-1, keepdims=True)
    elog2 = jnp.where(lane == i1, -jnp.inf, elog)
    v2 = jnp.max(elog2, axis=-1, keepdims=True)
    i2 = jnp.min(jnp.where(elog2 == v2, lane, big), axis=-1, keepdims=True)
    d = jnp.exp(v2 - v1)
    gate1 = p_grp / (1.0 + d)
    gate2 = p_grp * d / (1.0 + d)
    sel1 = lane == i1
    sel2 = lane == i2
    sel = jnp.where(sel1 | sel2, 1.0, 0.0)
    row = lax.broadcasted_iota(jnp.int32, (tr, tr), 0)
    col = lax.broadcasted_iota(jnp.int32, (tr, tr), 1)
    before = jnp.where(col < row, 1.0, 0.0).astype(BF16)
    rank = _dot(before, sel.astype(BF16)) + carry_ref[...]
    r1 = jnp.sum(jnp.where(sel1, rank, 0.0), axis=-1, keepdims=True)
    r2 = jnp.sum(jnp.where(sel2, rank, 0.0), axis=-1, keepdims=True)
    carry_ref[...] += jnp.sum(sel, axis=0, keepdims=True)
    cnt_ref[...] = carry_ref[...]
    info = jnp.zeros_like(lg)
    for idx, val in enumerate((i1 - N_GROUPS, i2 - N_GROUPS, r1, r2, gate1, gate2)):
        info = jnp.where(lane == idx, val, info)
    info_ref[...] = info


def _routing(logits):
    t = logits.shape[0]
    tr = 512
    return pl.pallas_call(
        _routing_kernel,
        grid=(t // tr,),
        in_specs=[pl.BlockSpec((tr, ROUTER_LANES), lambda i: (i, 0))],
        out_specs=[pl.BlockSpec((tr, ROUTER_LANES), lambda i: (i, 0)),
                   pl.BlockSpec((1, ROUTER_LANES), lambda i: (0, 0))],
        out_shape=[jax.ShapeDtypeStruct((t, ROUTER_LANES), F32),
                   jax.ShapeDtypeStruct((1, ROUTER_LANES), F32)],
        scratch_shapes=[pltpu.VMEM((1, ROUTER_LANES), F32)],
        compiler_params=_cparams("arbitrary"),
        name="routing",
    )(logits)


def _row_copy(src_ref, src_row, dst_ref, dst_row, sem):
    return pltpu.make_async_copy(src_ref.at[pl.ds(src_row, 1)], dst_ref.at[pl.ds(dst_row, 1)], sem)


def _dispatch_kernel(d1_ref, d2_ref, pad_end_ref, count_ref, h_ref, xs_ref, zero_ref, sem, zero_sem):
    td = h_ref.shape[0]
    base = pl.program_id(0) * td

    @pl.when(pl.program_id(0) == 0)
    def _():
        zero_ref[...] = jnp.zeros_like(zero_ref)

        def tail_copy(e):
            start = pl.multiple_of(pad_end_ref[e] - MOE_BLOCK, MOE_BLOCK)
            return pltpu.make_async_copy(zero_ref, xs_ref.at[pl.ds(start, MOE_BLOCK)], zero_sem)

        for e in range(N_EXPERTS):
            @pl.when(count_ref[e] > 0)
            def _():
                tail_copy(e).start()

        for e in range(N_EXPERTS):
            @pl.when(count_ref[e] > 0)
            def _():
                tail_copy(e).wait()

        def unused_copy(j):
            start = pl.multiple_of(j * MOE_BLOCK, MOE_BLOCK)
            return pltpu.make_async_copy(zero_ref, xs_ref.at[pl.ds(start, MOE_BLOCK)], zero_sem)

        first_unused = pad_end_ref[N_EXPERTS - 1] // MOE_BLOCK
        n_blocks = xs_ref.shape[0] // MOE_BLOCK

        def start_unused(j, carry):
            unused_copy(j).start()
            return carry

        def wait_unused(j, carry):
            unused_copy(j).wait()
            return carry

        lax.fori_loop(first_unused, n_blocks, start_unused, 0)
        lax.fori_loop(first_unused, n_blocks, wait_unused, 0)

    for r in range(td):
        for slot, dref in enumerate((d1_ref, d2_ref)):
            _row_copy(h_ref, r, xs_ref, dref[base + r], sem).start(priority=slot)
    for r in range(td):
        for _ in range(2):
            _row_copy(h_ref, r, xs_ref, 0, sem).wait()


def _dispatch(h2, dest1, dest2, pad_end, counts, n_rows):
    t, d = h2.shape
    td = 256
    grid_spec = pltpu.PrefetchScalarGridSpec(
        num_scalar_prefetch=4,
        grid=(t // td,),
        in_specs=[pl.BlockSpec((td, d), lambda i, *_: (i, 0))],
        out_specs=pl.BlockSpec(memory_space=pl.ANY),
        scratch_shapes=[pltpu.VMEM((MOE_BLOCK, d), h2.dtype), pltpu.SemaphoreType.DMA(()),
                        pltpu.SemaphoreType.DMA(())],
    )
    return pl.pallas_call(
        _dispatch_kernel,
        grid_spec=grid_spec,
        out_shape=jax.ShapeDtypeStruct((n_rows, d), h2.dtype),
        compiler_params=_cparams("arbitrary"),
        name="moe_dispatch",
    )(dest1, dest2, pad_end, counts, h2)


def _expert_kernel(blk_e_ref, nvalid_ref, xs_ref, wg_ref, wu_ref, wd_ref, ys_ref, wg_bf, wu_bf, wd_bf):
    j = pl.program_id(0)

    @pl.when((j == 0) | (blk_e_ref[j] != blk_e_ref[jnp.maximum(j - 1, 0)]))
    def _():
        wg_bf[...] = wg_ref[0, 0].astype(BF16)
        wu_bf[...] = wu_ref[0, 0].astype(BF16)
        wd_bf[...] = wd_ref[0, 0].astype(BF16)

    @pl.when(j < nvalid_ref[0])
    def _():
        xb = xs_ref[...].astype(BF16)
        gate = _dot(xb, wg_bf[...])
        up = _dot(xb, wu_bf[...])
        hdn = (gate * jax.nn.sigmoid(gate)) * up
        ys_ref[...] = _dot(hdn.astype(BF16), wd_bf[...])

    @pl.when(j >= nvalid_ref[0])
    def _():
        ys_ref[...] = jnp.zeros_like(ys_ref)


def _expert_mlp(xs, blk_e, nvalid, w_gate, w_up, w_down, layer):
    n_rows, d = xs.shape
    nblk = n_rows // MOE_BLOCK
    de = w_gate.shape[3]

    def x_map(j, be, nv):
        return (jnp.minimum(j, nv[0] - 1), 0)

    w_map = lambda j, be, nv: (layer, be[j], 0, 0)
    grid_spec = pltpu.PrefetchScalarGridSpec(
        num_scalar_prefetch=2,
        grid=(nblk,),
        in_specs=[
            pl.BlockSpec((MOE_BLOCK, d), x_map),
            pl.BlockSpec((1, 1, d, de), w_map),
            pl.BlockSpec((1, 1, d, de), w_map),
            pl.BlockSpec((1, 1, de, d), w_map),
        ],
        out_specs=pl.BlockSpec((MOE_BLOCK, d), lambda j, be, nv: (j, 0)),
        scratch_shapes=[pltpu.VMEM((d, de), BF16), pltpu.VMEM((d, de), BF16), pltpu.VMEM((de, d), BF16)],
    )
    return pl.pallas_call(
        _expert_kernel,
        grid_spec=grid_spec,
        out_shape=jax.ShapeDtypeStruct((n_rows, d), F32),
        compiler_params=_cparams("arbitrary"),
        name="expert_mlp",
    )(blk_e, nvalid, xs, w_gate, w_up, w_down)


def _combine_kernel(final_norm, d1_ref, d2_ref, x_ref, mod_ref, info_ref, gf_ref, ys_ref, o_ref, buf_ref, sem):
    tc = x_ref.shape[0]
    base = pl.program_id(0) * tc

    for r in range(tc):
        for slot, dref in enumerate((d1_ref, d2_ref)):
            _row_copy(ys_ref, dref[base + r], buf_ref.at[slot], r, sem).start(priority=slot)
    for r in range(tc):
        for slot in range(2):
            _row_copy(ys_ref, 0, buf_ref.at[slot], r, sem).wait()
    info = info_ref[...]
    moe = info[:, 4:5] * buf_ref[0] + info[:, 5:6] * buf_ref[1]
    x2 = x_ref[...] + mod_ref[0][5:6] * moe
    if final_norm:
        ms = jnp.mean(x2 * x2, axis=-1, keepdims=True)
        x2 = x2 * lax.rsqrt(ms + EPS) * gf_ref[...]
    o_ref[...] = x2


def _combine(x1, mod_l, info, ys, dest1, dest2, g_final, seq, final_norm):
    t, d = x1.shape
    tc = 256
    tiles_per_batch = seq // tc
    grid_spec = pltpu.PrefetchScalarGridSpec(
        num_scalar_prefetch=2,
        grid=(t // tc,),
        in_specs=[
            pl.BlockSpec((tc, d), lambda i, d1, d2: (i, 0)),
            pl.BlockSpec((1, 6, d), lambda i, d1, d2: (i // tiles_per_batch, 0, 0)),
            pl.BlockSpec((tc, ROUTER_LANES), lambda i, d1, d2: (i, 0)),
            pl.BlockSpec((1, d), lambda i, d1, d2: (0, 0)),
            pl.BlockSpec(memory_space=pl.ANY),
        ],
        out_specs=pl.BlockSpec((tc, d), lambda i, d1, d2: (i, 0)),
        scratch_shapes=[pltpu.VMEM((2, tc, d), F32), pltpu.SemaphoreType.DMA(())],
    )
    return pl.pallas_call(
        functools.partial(_combine_kernel, final_norm),
        grid_spec=grid_spec,
        out_shape=jax.ShapeDtypeStruct((t, d), F32),
        compiler_params=_cparams("arbitrary"),
        name="moe_combine",
    )(dest1, dest2, x1, mod_l, info, g_final.reshape(1, d), ys)


def _router_weights(w_rg, w_re):
    d = w_rg.shape[0]
    w_experts = jnp.transpose(w_re, (1, 0, 2)).reshape(d, N_EXPERTS)
    pad = jnp.zeros((d, ROUTER_LANES - N_GROUPS - N_EXPERTS), F32)
    return jnp.concatenate([w_rg, w_experts, pad], axis=1)


def _block_layout(counts_row, n_blocks):
    counts = counts_row[0, N_GROUPS:N_GROUPS + N_EXPERTS].astype(jnp.int32)
    padded = (counts + MOE_BLOCK - 1) // MOE_BLOCK * MOE_BLOCK
    pad_end = jnp.cumsum(padded)
    pad_start = pad_end - padded
    starts = jnp.arange(n_blocks, dtype=jnp.int32) * MOE_BLOCK
    blk_e = jnp.minimum(jnp.sum((pad_end[None, :] <= starts[:, None]).astype(jnp.int32), axis=1), N_EXPERTS - 1)
    nvalid = (pad_end[-1:] // MOE_BLOCK).astype(jnp.int32)
    return counts, pad_start, pad_end, blk_e, nvalid


def _destinations(info, pad_start):
    ids = info[:, 0:4].astype(jnp.int32)
    experts = jnp.arange(N_EXPERTS, dtype=jnp.int32)[None, :]

    def segment_start(e):
        return jnp.sum(jnp.where(e[:, None] == experts, pad_start[None, :], 0), axis=1)

    return segment_start(ids[:, 0]) + ids[:, 2], segment_start(ids[:, 1]) + ids[:, 3]


def kernel(x, c, w_ada, b_ada, g_norm1, g_norm2, w_in, diff_lambda, diff_subln, na_rpb, conv_dw, conv_b,
           conv_norm_g, conv_norm_b, w_out, w_router_group, w_router_expert, w_exp_gate, w_exp_up,
           w_exp_down, g_final):
    batch, seq, d = x.shape
    depth = w_ada.shape[0]
    t = batch * seq
    assert d == D_MODEL and seq == GRID_ROWS * GRID_W
    n_rows = t * 2 + N_EXPERTS * MOE_BLOCK
    n_blocks = n_rows // MOE_BLOCK

    mod = _ada_modulation(c, w_ada, b_ada).reshape(depth, batch, 6, d)
    rope_a = _rope_tables(seq, DIFF_DH)
    rope_d = _rope_tables(seq, HEAD_DIM)
    masks = _head_masks()
    gmat = _block_diag_ones(GROUP_WIDTH, HEAD_DIM)

    x2d = x.reshape(t, d)
    for l in range(depth):
        mod_l = mod[l]
        qa, ka, va, qb, kb, vb, pc, qd, kd, vd = _input_projection(
            x2d, mod_l, g_norm1[l], w_in[l].astype(BF16), rope_a, rope_d, seq)
        ya = _diff_attention(qa, ka, va, diff_lambda[l], diff_subln[l], l, batch, seq, masks, gmat)
        yb = _neighborhood_attention(qb, kb, vb, _na_bias_table(na_rpb[l]), batch, seq, masks)
        yc = _conformer_conv(pc, conv_dw[l], conv_b[l], conv_norm_g[l], conv_norm_b[l], batch, seq, gmat)
        yd = _dilated_attention(qd, kd, vd, batch, seq, masks)
        x1, h2, logits = _output_projection(
            (ya, yb, yc, yd), w_out[l].astype(BF16), x2d, mod_l, g_norm2[l],
            _router_weights(w_router_group[l], w_router_expert[l]).astype(BF16), seq)
        info, counts = _routing(logits)
        counts, pad_start, pad_end, blk_e, nvalid = _block_layout(counts, n_blocks)
        dest1, dest2 = _destinations(info, pad_start)
        xs = _dispatch(h2, dest1, dest2, pad_end, counts, n_rows)
        ys = _expert_mlp(xs, blk_e, nvalid, w_exp_gate, w_exp_up, w_exp_down, l)
        x2d = _combine(x1, mod_l, info, ys, dest1, dest2, g_final, seq, final_norm=(l == depth - 1))
    return x2d.reshape(batch, seq, d)
```

```python
import functools
import math

import numpy as np
import jax
import jax.numpy as jnp
from jax import lax
from jax.experimental import pallas as pl
from jax.experimental.pallas import tpu as pltpu

F32 = jnp.float32
BF16 = jnp.bfloat16

D_MODEL = 1024
GROUP_WIDTH = 256
HEAD_DIM = 64
N_HEADS = 4
DIFF_DH = 32
CONV_K = 31
CONV_GROUP_CH = 64
GRID_W = 64
NA_KH = 8
NA_KW = 16
GRID_ROWS = 32
NA_ROWS_PER_ITER = 2
ROPE_THETA = 10000.0
N_GROUPS = 4
EXPERTS_PER_GROUP = 8
N_EXPERTS = 32
D_EXPERT = 512
MOE_BLOCK = 256
EPS = 1e-6
NEG_INF = -1e30
LOG2E = 1.4426950408889634
ROUTER_LANES = 128
DILATED_CFG = ((128, 1), (512, 4), (2048, 16))
DIFF_EXP2_SCALE = (DIFF_DH ** -0.5) * LOG2E

VMEM_LIMIT = 56 * 1024 * 1024


def _cparams(*sem):
    return pltpu.CompilerParams(dimension_semantics=sem, vmem_limit_bytes=VMEM_LIMIT)


def _dot(a, b):
    return jnp.dot(a, b, preferred_element_type=F32)


def _dot_nt(a, b):
    return lax.dot_general(a, b, (((1,), (1,)), ((), ())), preferred_element_type=F32)


def _split(a):
    hi = a.astype(BF16)
    lo = (a - hi.astype(F32)).astype(BF16)
    return hi, lo


def _dot3(a, b):
    ah, al = _split(a)
    bh, bl = _split(b)
    return _dot(ah, bh) + (_dot(ah, bl) + _dot(al, bh))


def _group_mean(v, gmat, width):
    hi, lo = _split(v)
    return (_dot(hi, gmat) + _dot(lo, gmat)) * (1.0 / width)


def _block_diag_ones(n, width):
    idx = np.arange(n) // width
    return jnp.asarray((idx[:, None] == idx[None, :]).astype(np.float32), dtype=BF16)


def _ada_kernel(c_ref, w_ref, b_ref, o_ref):
    c = c_ref[...]
    ca = c * jax.nn.sigmoid(c)
    o_ref[0] = _dot3(ca, w_ref[0]) + b_ref[0]


def _ada_modulation(c, w_ada, b_ada):
    depth, d, n = w_ada.shape
    b = c.shape[0]
    bn = 1024
    return pl.pallas_call(
        _ada_kernel,
        grid=(depth, n // bn),
        in_specs=[
            pl.BlockSpec((b, d), lambda l, j: (0, 0)),
            pl.BlockSpec((1, d, bn), lambda l, j: (l, 0, j)),
            pl.BlockSpec((1, 1, bn), lambda l, j: (l, 0, j)),
        ],
        out_specs=pl.BlockSpec((1, b, bn), lambda l, j: (l, 0, j)),
        out_shape=jax.ShapeDtypeStruct((depth, b, n), F32),
        compiler_params=_cparams("arbitrary", "arbitrary"),
        name="ada_modulation",
    )(c, w_ada, b_ada.reshape(depth, 1, n))


def _rope_tables(seq, dim):
    half = dim // 2
    inv = ROPE_THETA ** (-jnp.arange(0, dim, 2, dtype=F32) / dim)
    ang = jnp.arange(seq, dtype=F32)[:, None] * inv[None, :]
    cos, sin = jnp.cos(ang), jnp.sin(ang)
    reps = 128 // dim
    zeros = jnp.zeros_like(sin)
    cos_t = jnp.tile(jnp.concatenate([cos, cos], axis=1), (1, reps))
    sin_hi = jnp.tile(jnp.concatenate([zeros, sin], axis=1), (1, reps))
    sin_lo = jnp.tile(jnp.concatenate([-sin, zeros], axis=1), (1, reps))
    return cos_t, sin_hi, sin_lo


def _rotary(v, cos_t, sin_hi, sin_lo, half):
    outs = []
    for j in range(v.shape[1] // 128):
        vj = v[:, j * 128:(j + 1) * 128]
        outs.append(vj * cos_t + pltpu.roll(vj, half, 1) * sin_hi + pltpu.roll(vj, 128 - half, 1) * sin_lo)
    return jnp.concatenate(outs, axis=1)


def _inproj_kernel(x_ref, mod_ref, g_ref, w_ref, ca_ref, sha_ref, sla_ref, cd_ref, shd_ref, sld_ref,
                   qa_ref, ka_ref, va_ref, qb_ref, kb_ref, vb_ref, pc_ref, qd_ref, kd_ref, vd_ref):
    x = x_ref[...]
    ms = jnp.mean(x * x, axis=-1, keepdims=True)
    y = x * lax.rsqrt(ms + EPS)
    mod = mod_ref[0]
    h = (y * g_ref[...]) * (1.0 + mod[1:2]) + mod[0:1]
    hb = h.astype(BF16)
    gw = GROUP_WIDTH

    def proj(col):
        return _dot(hb, w_ref[:, col * gw:(col + 1) * gw])

    rot_a = functools.partial(_rotary, cos_t=ca_ref[...], sin_hi=sha_ref[...], sin_lo=sla_ref[...],
                              half=DIFF_DH // 2)
    rot_d = functools.partial(_rotary, cos_t=cd_ref[...], sin_hi=shd_ref[...], sin_lo=sld_ref[...],
                              half=HEAD_DIM // 2)
    na_scale = HEAD_DIM ** -0.5
    qa_ref[...] = (rot_a(proj(0)) * DIFF_EXP2_SCALE).astype(BF16)
    ka_ref[...] = rot_a(proj(1)).astype(BF16)
    va_ref[...] = proj(2).astype(BF16)
    qb_ref[...] = (proj(3) * na_scale).astype(BF16)
    kb_ref[...] = proj(4).astype(BF16)
    vb_ref[...] = proj(5).astype(BF16)
    pc_ref[:, 0:gw] = proj(6)
    pc_ref[:, gw:2 * gw] = proj(7)
    for ref, val in ((qd_ref, rot_d(proj(8)) * na_scale), (kd_ref, rot_d(proj(9))), (vd_ref, proj(10))):
        for j in range(gw // 128):
            ref[j] = val[:, j * 128:(j + 1) * 128]


def _input_projection(x2d, mod_l, g1, w_in_bf16, rope_a, rope_d, seq):
    t, d = x2d.shape
    tm = 512
    tiles_per_batch = seq // tm
    p_in = w_in_bf16.shape[1]
    gw = GROUP_WIDTH
    row_spec = lambda width: pl.BlockSpec((tm, width), lambda i: (i, 0))
    tab_spec = pl.BlockSpec((tm, 128), lambda i: (i % tiles_per_batch, 0))
    out_shapes = []
    out_specs = []
    for name in ("qa", "ka", "va", "qb", "kb", "vb", "pc", "qd", "kd", "vd"):
        if name == "pc":
            out_shapes.append(jax.ShapeDtypeStruct((t, 2 * gw), F32))
            out_specs.append(row_spec(2 * gw))
        elif name[1] == "d":
            out_shapes.append(jax.ShapeDtypeStruct((gw // 128, t, 128), F32))
            out_specs.append(pl.BlockSpec((gw // 128, tm, 128), lambda i: (0, i, 0)))
        else:
            out_shapes.append(jax.ShapeDtypeStruct((t, gw), BF16))
            out_specs.append(row_spec(gw))
    return pl.pallas_call(
        _inproj_kernel,
        grid=(t // tm,),
        in_specs=[
            row_spec(d),
            pl.BlockSpec((1, 6, d), lambda i: (i // tiles_per_batch, 0, 0)),
            pl.BlockSpec((1, d), lambda i: (0, 0)),
            pl.BlockSpec((d, p_in), lambda i: (0, 0)),
            tab_spec, tab_spec, tab_spec, tab_spec, tab_spec, tab_spec,
        ],
        out_specs=out_specs,
        out_shape=out_shapes,
        compiler_params=_cparams("arbitrary"),
        name="input_projection",
    )(x2d, mod_l, g1.reshape(1, d), w_in_bf16, *rope_a, *rope_d)


def _head_masks():
    lane = np.arange(GROUP_WIDTH)
    head = np.stack([(lane // HEAD_DIM == h) for h in range(N_HEADS)]).astype(np.float32)
    diff = np.stack([(lane // DIFF_DH == j) for j in range(2 * N_HEADS)]).astype(np.float32)
    return (jnp.asarray(head[:, None, :], dtype=BF16), jnp.asarray(head[:, None, :], dtype=F32),
            jnp.asarray(diff[:, None, :], dtype=BF16))


DIFF_ONES_ROWS = 16


def _diff_attn_kernel(lam_init, q_ref, k_ref, v_ref, lp_ref, g_ref, dmask_ref, gmat_ref, o_ref, vt_ref, ot_ref):
    seq = k_ref.shape[0]

    @pl.when(pl.program_id(1) == 0)
    def _():
        vt = jnp.transpose(v_ref[...].astype(F32))
        for h in range(N_HEADS):
            vt_ref[h, 0:HEAD_DIM, :] = vt[h * HEAD_DIM:(h + 1) * HEAD_DIM].astype(BF16)
            vt_ref[h, HEAD_DIM:HEAD_DIM + DIFF_ONES_ROWS, :] = jnp.ones((DIFF_ONES_ROWS, seq), BF16)

    q = q_ref[...]
    k = k_ref[...]
    lp = lp_ref[...]
    lam = (jnp.exp(jnp.sum(lp[0:1] * lp[1:2], axis=-1, keepdims=True))
           - jnp.exp(jnp.sum(lp[2:3] * lp[3:4], axis=-1, keepdims=True)) + lam_init)

    def scores(j):
        return _dot_nt(k, q * dmask_ref[j])

    def weights(st):
        return jnp.exp2(st - jnp.max(st, axis=0, keepdims=True)).astype(BF16)

    def attend(j, e):
        num = _dot(vt_ref[j // 2], e)
        return num[0:HEAD_DIM] / num[HEAD_DIM:HEAD_DIM + 1]

    n_pairs = 2 * N_HEADS
    outs = [None] * n_pairs
    st_next = scores(0)
    e_prev = None
    for j in range(n_pairs):
        st = st_next
        if j + 1 < n_pairs:
            st_next = scores(j + 1)
        e = weights(st)
        if e_prev is not None:
            outs[j - 1] = attend(j - 1, e_prev)
        e_prev = e
    outs[n_pairs - 1] = attend(n_pairs - 1, e_prev)
    for h in range(N_HEADS):
        ot_ref[h * HEAD_DIM:(h + 1) * HEAD_DIM, :] = outs[2 * h] - lam * outs[2 * h + 1]
    o = jnp.transpose(ot_ref[...])
    ms = _group_mean(o * o, gmat_ref[...], HEAD_DIM)
    o_ref[...] = ((o * lax.rsqrt(ms + EPS) * g_ref[...]) * (1.0 - lam_init)).astype(BF16)


def _diff_attention(qa, ka, va, lam_params, subln_g, layer_idx, batch, seq, masks, gmat):
    t, gw = qa.shape
    tq = 512
    nq = seq // tq
    lam_init = 0.8 - 0.6 * math.exp(-0.3 * layer_idx)
    _, _, dmask = masks
    g_tiled = jnp.tile(subln_g, N_HEADS).reshape(1, gw)
    kv_spec = pl.BlockSpec((seq, gw), lambda b, i: (b, 0))
    full = lambda shape: pl.BlockSpec(shape, lambda b, i: (0,) * len(shape))
    return pl.pallas_call(
        functools.partial(_diff_attn_kernel, lam_init),
        grid=(batch, nq),
        in_specs=[
            pl.BlockSpec((tq, gw), lambda b, i: (b * nq + i, 0)),
            kv_spec, kv_spec,
            full(lam_params.shape), full((1, gw)), full(dmask.shape), full(gmat.shape),
        ],
        out_specs=pl.BlockSpec((tq, gw), lambda b, i: (b * nq + i, 0)),
        out_shape=jax.ShapeDtypeStruct((t, gw), BF16),
        scratch_shapes=[pltpu.VMEM((N_HEADS, HEAD_DIM + DIFF_ONES_ROWS, seq), BF16), pltpu.VMEM((gw, tq), F32)],
        compiler_params=_cparams("arbitrary", "arbitrary"),
        name="diff_attention",
    )(qa, ka, va, lam_params, g_tiled, dmask, gmat)


def _na_bias_table(rpb):
    w = GRID_W
    n_heads = rpb.shape[0]
    cq = np.arange(w)[:, None]
    ck = np.arange(w)[None, :]
    dc = np.clip(ck - cq, -(NA_KW - 1), NA_KW - 1) + NA_KW - 1
    col_start = np.clip(cq - NA_KW // 2, 0, w - NA_KW)
    col_ok = (ck >= col_start) & (ck < col_start + NA_KW)
    onehot = jnp.asarray(dc[:, :, None] == np.arange(2 * NA_KW - 1), dtype=F32)
    toep = jnp.einsum('qkd,hrd->hrqk', onehot, rpb.astype(F32), precision=lax.Precision.HIGHEST)
    toep = jnp.where(col_ok[None, None], toep, NEG_INF)
    tabs = jnp.stack([toep[:, NA_KH - 1 - off:2 * NA_KH - 1 - off] for off in range(NA_KH)])
    return jnp.transpose(tabs, (0, 1, 3, 2, 4)).reshape(NA_KH, n_heads * w, NA_KH * w)


def _na_kernel(q_ref, k_ref, v_ref, tab_ref, hmask_ref, hmask_f32_ref, o_ref):
    w = GRID_W
    nk = NA_KH * w

    def scores(i):
        row_start = jnp.clip(i - NA_KH // 2, 0, GRID_ROWS - NA_KH)
        kstart = pl.multiple_of(row_start * w, w)
        q = q_ref[pl.ds(pl.multiple_of(i * w, w), w), :]
        qs = jnp.concatenate([q * hmask_ref[h] for h in range(N_HEADS)], axis=0)
        return _dot_nt(qs, k_ref[pl.ds(kstart, nk), :]) + tab_ref[i - row_start], kstart

    def attend(i, s, kstart):
        m = jnp.max(s, axis=-1, keepdims=True)
        e = jnp.exp(s - m)
        p = e * (1.0 / jnp.sum(e, axis=-1, keepdims=True))
        pv = _dot(p.astype(BF16), v_ref[pl.ds(kstart, nk), :])
        o = pv[0:w] * hmask_f32_ref[0]
        for h in range(1, N_HEADS):
            o = o + pv[h * w:(h + 1) * w] * hmask_f32_ref[h]
        o_ref[pl.ds(pl.multiple_of(i * w, w), w), :] = o.astype(BF16)

    def grid_rows(p, carry):
        rows = [NA_ROWS_PER_ITER * p + u for u in range(NA_ROWS_PER_ITER)]
        staged = [scores(i) for i in rows]
        for i, (s, kstart) in zip(rows, staged):
            attend(i, s, kstart)
        return carry

    lax.fori_loop(0, GRID_ROWS // NA_ROWS_PER_ITER, grid_rows, 0)


def _neighborhood_attention(qb, kb, vb, bias_table, batch, seq, masks):
    t, gw = qb.shape
    hmask, hmask_f32, _ = masks
    seq_spec = pl.BlockSpec((seq, gw), lambda b: (b, 0))
    full = lambda shape: pl.BlockSpec(shape, lambda b: (0,) * len(shape))
    return pl.pallas_call(
        _na_kernel,
        grid=(batch,),
        in_specs=[seq_spec, seq_spec, seq_spec, full(bias_table.shape), full(hmask.shape), full(hmask_f32.shape)],
        out_specs=seq_spec,
        out_shape=jax.ShapeDtypeStruct((t, gw), BF16),
        compiler_params=_cparams("arbitrary"),
        name="neighborhood_attention",
    )(qb, kb, vb, bias_table, hmask, hmask_f32)


SUBLANES = 8
CONV_PAD = 16
CONV_CHUNK = 128


def _conv_kernel(pc_ref, w_ref, b_ref, gn_ref, bn_ref, gmat_ref, o_ref, zp_ref, zs_ref):
    seq, ch = o_ref.shape
    a = pc_ref[:, 0:ch]
    gate = pc_ref[:, ch:2 * ch]
    zp_ref[0:CONV_PAD, :] = jnp.zeros((CONV_PAD, ch), F32)
    zp_ref[CONV_PAD + seq:2 * CONV_PAD + seq, :] = jnp.zeros((CONV_PAD, ch), F32)
    zp_ref[CONV_PAD:CONV_PAD + seq, :] = a * jax.nn.sigmoid(gate)
    span = seq + 2 * CONV_PAD - SUBLANES
    for b in range(1, SUBLANES):
        zs_ref[b - 1, 0:span, :] = zp_ref[b:b + span, :]
    gmat = gmat_ref[...]
    first = CONV_PAD - CONV_K // 2
    for c in range(seq // CONV_CHUNK):
        r0 = c * CONV_CHUNK
        acc = jnp.zeros((CONV_CHUNK, ch), F32)
        for j in range(CONV_K):
            shift, aligned = (first + j) % SUBLANES, r0 + (first + j) // SUBLANES * SUBLANES
            src = zp_ref if shift == 0 else zs_ref.at[shift - 1]
            acc = acc + w_ref[j] * src[aligned:aligned + CONV_CHUNK, :]
        z = acc + b_ref[...]
        mu = _group_mean(z, gmat, CONV_GROUP_CH)
        dz = z - mu
        var = _group_mean(dz * dz, gmat, CONV_GROUP_CH)
        zn = dz * lax.rsqrt(var + EPS) * gn_ref[...] + bn_ref[...]
        o_ref[r0:r0 + CONV_CHUNK, :] = (zn * jax.nn.sigmoid(zn)).astype(BF16)


def _conformer_conv(pc, w_dw, b_dw, g_n, b_n, batch, seq, gmat):
    t = pc.shape[0]
    ch = GROUP_WIDTH
    full = lambda shape: pl.BlockSpec(shape, lambda b: (0,) * len(shape))
    return pl.pallas_call(
        _conv_kernel,
        grid=(batch,),
        in_specs=[
            pl.BlockSpec((seq, 2 * ch), lambda b: (b, 0)),
            full((CONV_K, 1, ch)), full((1, ch)), full((1, ch)), full((1, ch)), full(gmat.shape),
        ],
        out_specs=pl.BlockSpec((seq, ch), lambda b: (b, 0)),
        out_shape=jax.ShapeDtypeStruct((t, ch), BF16),
        scratch_shapes=[pltpu.VMEM((seq + 2 * CONV_PAD, ch), F32),
                        pltpu.VMEM((SUBLANES - 1, seq + 2 * CONV_PAD, ch), F32)],
        compiler_params=_cparams("arbitrary"),
        name="conformer_conv",
    )(pc, w_dw.reshape(CONV_K, 1, ch), b_dw.reshape(1, ch), g_n.reshape(1, ch), b_n.reshape(1, ch), gmat)


DIL_QB = 128
DIL_KB = 256
DIL_UNITS_PER_ITER = 2


def _strided_rows(first, count, stride):
    return pl.ds(first, count) if stride == 1 else pl.ds(first, count, stride=stride)


def _load_rows(ref, rows):
    return jnp.concatenate([ref[j, rows, :] for j in range(ref.shape[0])], axis=1)


def _store_rows(ref, rows, val):
    for j in range(ref.shape[0]):
        ref[j, rows, :] = val[:, j * 128:(j + 1) * 128]


def _dilated_kernel(q_ref, k_ref, v_ref, hmask_ref, hmask_f32_ref, o_ref, acc_ref, m_ref, l_ref):
    seq = q_ref.shape[1]
    for branch, (window, dil) in enumerate(DILATED_CFG):
        n_side = window // (2 * dil)
        sub_len = seq // dil
        qb = min(DIL_QB, sub_len)
        kb = min(DIL_KB, sub_len)
        row = lax.broadcasted_iota(jnp.int32, (N_HEADS * qb, kb), 0) & (qb - 1)
        col = lax.broadcasted_iota(jnp.int32, (N_HEADS * qb, kb), 1)
        def scores(r, c, dil=dil, n_side=n_side, sub_len=sub_len, qb=qb, kb=kb, row=row, col=col):
            l0 = c * qb
            kl0 = jnp.clip(l0 - n_side, 0, sub_len - kb)
            if dil == 1:
                l0, kl0 = pl.multiple_of(l0, qb), pl.multiple_of(kl0, n_side)
            q_rows = _strided_rows(r + dil * l0, qb, dil)
            k_rows = _strided_rows(r + dil * kl0, kb, dil)
            q = _load_rows(q_ref, q_rows).astype(BF16)
            qs = jnp.concatenate([q * hmask_ref[h] for h in range(N_HEADS)], axis=0)
            s = _dot_nt(qs, _load_rows(k_ref, k_rows).astype(BF16))
            return jnp.where(jnp.abs(col + kl0 - row - l0) <= n_side, s, NEG_INF), q_rows, k_rows

        def attend(s, q_rows, k_rows, qb=qb, branch=branch):
            m = jnp.max(s, axis=-1, keepdims=True)
            e = jnp.exp(s - m)
            l = jnp.sum(e, axis=-1, keepdims=True)
            pv = _dot(e.astype(BF16), _load_rows(v_ref, k_rows).astype(BF16))

            def unstack(a):
                out = a[0:qb] * hmask_f32_ref[0]
                for h in range(1, N_HEADS):
                    out = out + a[h * qb:(h + 1) * qb] * hmask_f32_ref[h]
                return out

            acc_new, m_new, l_new = unstack(pv), unstack(m), unstack(l)
            if branch == 0:
                _store_rows(acc_ref, q_rows, acc_new)
                _store_rows(m_ref, q_rows, m_new)
                _store_rows(l_ref, q_rows, l_new)
            else:
                m_old = _load_rows(m_ref, q_rows)
                m_max = jnp.maximum(m_old, m_new)
                w_old = jnp.exp(m_old - m_max)
                w_new = jnp.exp(m_new - m_max)
                _store_rows(acc_ref, q_rows, _load_rows(acc_ref, q_rows) * w_old + acc_new * w_new)
                _store_rows(l_ref, q_rows, _load_rows(l_ref, q_rows) * w_old + l_new * w_new)
                _store_rows(m_ref, q_rows, m_max)

        def run(units, scores=scores, attend=attend):
            staged = [scores(r, c) for r, c in units]
            for item in staged:
                attend(*item)

        n_blocks = sub_len // qb
        if n_blocks >= DIL_UNITS_PER_ITER:
            for r in range(dil):
                def block_group(p, carry, r=r, run=run):
                    run([(r, DIL_UNITS_PER_ITER * p + u) for u in range(DIL_UNITS_PER_ITER)])
                    return carry

                lax.fori_loop(0, n_blocks // DIL_UNITS_PER_ITER, block_group, 0)
        else:
            for r0 in range(0, dil, DIL_UNITS_PER_ITER):
                run([(r0 + u, 0) for u in range(DIL_UNITS_PER_ITER)])
    for j in range(acc_ref.shape[0]):
        o_ref[:, j * 128:(j + 1) * 128] = (acc_ref[j] / l_ref[j]).astype(BF16)


def _dilated_attention(qd, kd, vd, batch, seq, masks):
    tiles, t, _ = qd.shape
    gw = tiles * 128
    hmask, hmask_f32, _ = masks
    in_spec = pl.BlockSpec((tiles, seq, 128), lambda b: (0, b, 0))
    stat = pltpu.VMEM((tiles, seq, 128), F32)
    return pl.pallas_call(
        _dilated_kernel,
        grid=(batch,),
        in_specs=[
            in_spec, in_spec, in_spec,
            pl.BlockSpec(hmask.shape, lambda b: (0, 0, 0)),
            pl.BlockSpec(hmask_f32.shape, lambda b: (0, 0, 0)),
        ],
        out_specs=pl.BlockSpec((seq, gw), lambda b: (b, 0)),
        out_shape=jax.ShapeDtypeStruct((t, gw), BF16),
        scratch_shapes=[stat, stat, stat],
        compiler_params=_cparams("arbitrary"),
        name="dilated_attention",
    )(qd, kd, vd, hmask, hmask_f32)


def _store_token_tiles(ref, val):
    rows = val.shape[0]
    for s in range(SUBLANES):
        ref[pl.ds(s, rows, stride=SUBLANES), :] = val[:, s * 128:(s + 1) * 128]


def _load_token_tiles(ref, rows):
    return jnp.concatenate([ref[pl.ds(s, rows, stride=SUBLANES), :] for s in range(SUBLANES)], axis=1)


def _token_tile(ref, token):
    first = token * SUBLANES
    return ref.at[pl.ds(first if isinstance(token, int) else pl.multiple_of(first, SUBLANES), SUBLANES)]


def _outproj_kernel(ya_ref, yb_ref, yc_ref, yd_ref, w_ref, x_ref, mod_ref, g_ref, wr_ref,
                    x1_ref, h2_ref, lg_ref):
    gw = GROUP_WIDTH
    mix = _dot(ya_ref[...], w_ref[0:gw, :])
    mix = mix + _dot(yb_ref[...], w_ref[gw:2 * gw, :])
    mix = mix + _dot(yc_ref[...], w_ref[2 * gw:3 * gw, :])
    mix = mix + _dot(yd_ref[...], w_ref[3 * gw:4 * gw, :])
    mod = mod_ref[0]
    x1 = x_ref[...] + mod[2:3] * mix
    x1_ref[...] = x1
    ms = jnp.mean(x1 * x1, axis=-1, keepdims=True)
    h2 = (x1 * lax.rsqrt(ms + EPS) * g_ref[...]) * (1.0 + mod[4:5]) + mod[3:4]
    _store_token_tiles(h2_ref, h2)
    lg_ref[...] = _dot(h2.astype(BF16), wr_ref[...])


def _output_projection(ys, w_out_bf16, x2d, mod_l, g2, w_router, seq):
    t, d = x2d.shape
    tm = 512
    tiles_per_batch = seq // tm
    gw = GROUP_WIDTH
    row_spec = lambda width: pl.BlockSpec((tm, width), lambda i: (i, 0))
    return pl.pallas_call(
        _outproj_kernel,
        grid=(t // tm,),
        in_specs=[
            row_spec(gw), row_spec(gw), row_spec(gw), row_spec(gw),
            pl.BlockSpec((d, d), lambda i: (0, 0)),
            row_spec(d),
            pl.BlockSpec((1, 6, d), lambda i: (i // tiles_per_batch, 0, 0)),
            pl.BlockSpec((1, d), lambda i: (0, 0)),
            pl.BlockSpec((d, ROUTER_LANES), lambda i: (0, 0)),
        ],
        out_specs=[row_spec(d), pl.BlockSpec((tm * SUBLANES, 128), lambda i: (i, 0)), row_spec(ROUTER_LANES)],
        out_shape=[jax.ShapeDtypeStruct((t, d), F32), jax.ShapeDtypeStruct((t * SUBLANES, 128), F32),
                   jax.ShapeDtypeStruct((t, ROUTER_LANES), F32)],
        compiler_params=_cparams("arbitrary"),
        name="output_projection",
    )(*ys, w_out_bf16, x2d, mod_l, g2.reshape(1, d), w_router)


def _routing_kernel(lg_ref, info_ref, cnt_ref, carry_ref):
    tr = lg_ref.shape[0]

    @pl.when(pl.program_id(0) == 0)
    def _():
        carry_ref[...] = jnp.zeros_like(carry_ref)

    lg = lg_ref[...]
    lane = lax.broadcasted_iota(jnp.int32, lg.shape, 1).astype(F32)
    big = float(ROUTER_LANES)
    glog = jnp.where(lane < N_GROUPS, lg, -jnp.inf)
    gmax = jnp.max(glog, axis=-1, keepdims=True)
    p_grp = 1.0 / jnp.sum(jnp.exp(glog - gmax), axis=-1, keepdims=True)
    grp = jnp.min(jnp.where(glog == gmax, lane, big), axis=-1, keepdims=True)
    lo = N_GROUPS + EXPERTS_PER_GROUP * grp
    elog = jnp.where((lane >= lo) & (lane < lo + EXPERTS_PER_GROUP), lg, -jnp.inf)
    v1 = jnp.max(elog, axis=-1, keepdims=True)
    i1 = jnp.min(jnp.where(elog == v1, lane, big), axis=-1, keepdims=True)
    elog2 = jnp.where(lane == i1, -jnp.inf, elog)
    v2 = jnp.max(elog2, axis=-1, keepdims=True)
    i2 = jnp.min(jnp.where(elog2 == v2, lane, big), axis=-1, keepdims=True)
    d = jnp.exp(v2 - v1)
    gate1 = p_grp / (1.0 + d)
    gate2 = p_grp * d / (1.0 + d)
    sel1 = lane == i1
    sel2 = lane == i2
    sel = jnp.where(sel1 | sel2, 1.0, 0.0)
    row = lax.broadcasted_iota(jnp.int32, (tr, tr), 0)
    col = lax.broadcasted_iota(jnp.int32, (tr, tr), 1)
    before = jnp.where(col < row, 1.0, 0.0).astype(BF16)
    rank = _dot(before, sel.astype(BF16)) + carry_ref[...]
    r1 = jnp.sum(jnp.where(sel1, rank, 0.0), axis=-1, keepdims=True)
    r2 = jnp.sum(jnp.where(sel2, rank, 0.0), axis=-1, keepdims=True)
    carry_ref[...] += jnp.sum(sel, axis=0, keepdims=True)
    cnt_ref[...] = carry_ref[...]
    info = jnp.zeros_like(lg)
    for idx, val in enumerate((i1 - N_GROUPS, i2 - N_GROUPS, r1, r2, gate1, gate2)):
        info = jnp.where(lane == idx, val, info)
    info_ref[...] = info


def _routing(logits):
    t = logits.shape[0]
    tr = 512
    return pl.pallas_call(
        _routing_kernel,
        grid=(t // tr,),
        in_specs=[pl.BlockSpec((tr, ROUTER_LANES), lambda i: (i, 0))],
        out_specs=[pl.BlockSpec((tr, ROUTER_LANES), lambda i: (i, 0)),
                   pl.BlockSpec((1, ROUTER_LANES), lambda i: (0, 0))],
        out_shape=[jax.ShapeDtypeStruct((t, ROUTER_LANES), F32),
                   jax.ShapeDtypeStruct((1, ROUTER_LANES), F32)],
        scratch_shapes=[pltpu.VMEM((1, ROUTER_LANES), F32)],
        compiler_params=_cparams("arbitrary"),
        name="routing",
    )(logits)


BLOCK_TILE_ROWS = MOE_BLOCK * SUBLANES


def _block_rows(ref, block):
    return ref.at[pl.ds(pl.multiple_of(block * BLOCK_TILE_ROWS, BLOCK_TILE_ROWS), BLOCK_TILE_ROWS)]


def _dispatch_kernel(d1_ref, d2_ref, pad_end_ref, count_ref, h_ref, xs_ref, zero_ref, sem, zero_sem):
    td = h_ref.shape[0] // SUBLANES
    base = pl.program_id(0) * td

    @pl.when(pl.program_id(0) == 0)
    def _():
        zero_ref[...] = jnp.zeros_like(zero_ref)

        def tail_copy(e):
            return pltpu.make_async_copy(zero_ref, _block_rows(xs_ref, pad_end_ref[e] // MOE_BLOCK - 1), zero_sem)

        for e in range(N_EXPERTS):
            @pl.when(count_ref[e] > 0)
            def _():
                tail_copy(e).start()

        for e in range(N_EXPERTS):
            @pl.when(count_ref[e] > 0)
            def _():
                tail_copy(e).wait()

        def unused_copy(j):
            return pltpu.make_async_copy(zero_ref, _block_rows(xs_ref, j), zero_sem)

        first_unused = pad_end_ref[N_EXPERTS - 1] // MOE_BLOCK
        n_blocks = xs_ref.shape[0] // BLOCK_TILE_ROWS

        def start_unused(j, carry):
            unused_copy(j).start()
            return carry

        def wait_unused(j, carry):
            unused_copy(j).wait()
            return carry

        lax.fori_loop(first_unused, n_blocks, start_unused, 0)
        lax.fori_loop(first_unused, n_blocks, wait_unused, 0)

    def copy(r, dst):
        return pltpu.make_async_copy(_token_tile(h_ref, r), _token_tile(xs_ref, dst), sem)

    for r in range(td):
        for slot, dref in enumerate((d1_ref, d2_ref)):
            copy(r, dref[base + r]).start(priority=slot)
    for r in range(td):
        for _ in range(2):
            copy(r, 0).wait()


def _dispatch(h2_tiles, dest1, dest2, pad_end, counts, n_rows):
    t = h2_tiles.shape[0] // SUBLANES
    td = 256
    grid_spec = pltpu.PrefetchScalarGridSpec(
        num_scalar_prefetch=4,
        grid=(t // td,),
        in_specs=[pl.BlockSpec((td * SUBLANES, 128), lambda i, *_: (i, 0))],
        out_specs=pl.BlockSpec(memory_space=pl.ANY),
        scratch_shapes=[pltpu.VMEM((BLOCK_TILE_ROWS, 128), F32), pltpu.SemaphoreType.DMA(()),
                        pltpu.SemaphoreType.DMA(())],
    )
    return pl.pallas_call(
        _dispatch_kernel,
        grid_spec=grid_spec,
        out_shape=jax.ShapeDtypeStruct((n_rows * SUBLANES, 128), F32),
        compiler_params=_cparams("arbitrary"),
        name="moe_dispatch",
    )(dest1, dest2, pad_end, counts, h2_tiles)


def _expert_kernel(first_blk_ref, n_blk_ref, wg_ref, wu_ref, wd_ref, xs_ref, ys_ref,
                   x_buf, y_buf, wg_bf, wu_bf, wd_bf, in_sem, out_sem):
    e = pl.program_id(0)
    first = first_blk_ref[e]
    n_blk = n_blk_ref[e]

    def fetch(b, slot):
        return pltpu.make_async_copy(_block_rows(xs_ref, first + b), x_buf.at[slot], in_sem.at[slot])

    def flush(b, slot):
        return pltpu.make_async_copy(y_buf.at[slot], _block_rows(ys_ref, first + b), out_sem.at[slot])

    @pl.when(n_blk > 0)
    def _():
        fetch(0, 0).start()
        wg_bf[...] = wg_ref[0, 0].astype(BF16)
        wu_bf[...] = wu_ref[0, 0].astype(BF16)
        wd_bf[...] = wd_ref[0, 0].astype(BF16)

    def block(b, carry):
        slot = b & 1
        fetch(b, slot).wait()

        @pl.when(b + 1 < n_blk)
        def _():
            fetch(b + 1, 1 - slot).start()

        xb = _load_token_tiles(x_buf.at[slot], MOE_BLOCK).astype(BF16)
        gate = _dot(xb, wg_bf[...])
        up = _dot(xb, wu_bf[...])
        hdn = (gate * jax.nn.sigmoid(gate)) * up
        y = _dot(hdn.astype(BF16), wd_bf[...])

        @pl.when(b >= 2)
        def _():
            flush(b - 2, slot).wait()

        _store_token_tiles(y_buf.at[slot], y)
        flush(b, slot).start()
        return carry

    lax.fori_loop(0, n_blk, block, 0)

    for back in (2, 1):
        @pl.when(n_blk >= back)
        def _():
            b = n_blk - back
            flush(b, b & 1).wait()

    @pl.when(e == pl.num_programs(0) - 1)
    def _():
        y_buf[0] = jnp.zeros(y_buf.shape[1:], F32)
        n_total = ys_ref.shape[0] // BLOCK_TILE_ROWS

        def unused(j):
            return pltpu.make_async_copy(y_buf.at[0], _block_rows(ys_ref, j), out_sem.at[0])

        def start_unused(j, carry):
            unused(j).start()
            return carry

        def wait_unused(j, carry):
            unused(j).wait()
            return carry

        lax.fori_loop(first + n_blk, n_total, start_unused, 0)
        lax.fori_loop(first + n_blk, n_total, wait_unused, 0)


def _expert_mlp(xs_tiles, first_blk, n_blk, w_gate, w_up, w_down, layer):
    n_experts, d, de = w_gate.shape[1:]
    w_map = lambda e, fb, nb: (layer, e, 0, 0)
    block_buf = pltpu.VMEM((2, BLOCK_TILE_ROWS, 128), F32)
    grid_spec = pltpu.PrefetchScalarGridSpec(
        num_scalar_prefetch=2,
        grid=(n_experts,),
        in_specs=[
            pl.BlockSpec((1, 1, d, de), w_map),
            pl.BlockSpec((1, 1, d, de), w_map),
            pl.BlockSpec((1, 1, de, d), w_map),
            pl.BlockSpec(memory_space=pl.ANY),
        ],
        out_specs=pl.BlockSpec(memory_space=pl.ANY),
        scratch_shapes=[block_buf, block_buf, pltpu.VMEM((d, de), BF16), pltpu.VMEM((d, de), BF16),
                        pltpu.VMEM((de, d), BF16), pltpu.SemaphoreType.DMA((2,)), pltpu.SemaphoreType.DMA((2,))],
    )
    return pl.pallas_call(
        _expert_kernel,
        grid_spec=grid_spec,
        out_shape=jax.ShapeDtypeStruct(xs_tiles.shape, F32),
        compiler_params=_cparams("arbitrary"),
        name="expert_mlp",
    )(first_blk, n_blk, w_gate, w_up, w_down, xs_tiles)


def _combine_kernel(final_norm, d1_ref, d2_ref, x_ref, mod_ref, info_ref, gf_ref, ys_ref, o_ref, buf_ref, sem):
    tc = x_ref.shape[0]
    base = pl.program_id(0) * tc

    def copy(src, slot, r):
        return pltpu.make_async_copy(_token_tile(ys_ref, src), _token_tile(buf_ref.at[slot], r), sem)

    for r in range(tc):
        for slot, dref in enumerate((d1_ref, d2_ref)):
            copy(dref[base + r], slot, r).start(priority=slot)
    for r in range(tc):
        for slot in range(2):
            copy(0, slot, r).wait()
    info = info_ref[...]
    moe = (info[:, 4:5] * _load_token_tiles(buf_ref.at[0], tc)
           + info[:, 5:6] * _load_token_tiles(buf_ref.at[1], tc))
    x2 = x_ref[...] + mod_ref[0][5:6] * moe
    if final_norm:
        ms = jnp.mean(x2 * x2, axis=-1, keepdims=True)
        x2 = x2 * lax.rsqrt(ms + EPS) * gf_ref[...]
    o_ref[...] = x2


def _combine(x1, mod_l, info, ys, dest1, dest2, g_final, seq, final_norm):
    t, d = x1.shape
    tc = 256
    tiles_per_batch = seq // tc
    grid_spec = pltpu.PrefetchScalarGridSpec(
        num_scalar_prefetch=2,
        grid=(t // tc,),
        in_specs=[
            pl.BlockSpec((tc, d), lambda i, d1, d2: (i, 0)),
            pl.BlockSpec((1, 6, d), lambda i, d1, d2: (i // tiles_per_batch, 0, 0)),
            pl.BlockSpec((tc, ROUTER_LANES), lambda i, d1, d2: (i, 0)),
            pl.BlockSpec((1, d), lambda i, d1, d2: (0, 0)),
            pl.BlockSpec(memory_space=pl.ANY),
        ],
        out_specs=pl.BlockSpec((tc, d), lambda i, d1, d2: (i, 0)),
        scratch_shapes=[pltpu.VMEM((2, tc * SUBLANES, 128), F32), pltpu.SemaphoreType.DMA(())],
    )
    return pl.pallas_call(
        functools.partial(_combine_kernel, final_norm),
        grid_spec=grid_spec,
        out_shape=jax.ShapeDtypeStruct((t, d), F32),
        compiler_params=_cparams("arbitrary"),
        name="moe_combine",
    )(dest1, dest2, x1, mod_l, info, g_final.reshape(1, d), ys)


def _router_weights(w_rg, w_re):
    d = w_rg.shape[0]
    w_experts = jnp.transpose(w_re, (1, 0, 2)).reshape(d, N_EXPERTS)
    pad = jnp.zeros((d, ROUTER_LANES - N_GROUPS - N_EXPERTS), F32)
    return jnp.concatenate([w_rg, w_experts, pad], axis=1)


def _block_layout(counts_row):
    counts = counts_row[0, N_GROUPS:N_GROUPS + N_EXPERTS].astype(jnp.int32)
    padded = (counts + MOE_BLOCK - 1) // MOE_BLOCK * MOE_BLOCK
    pad_end = jnp.cumsum(padded)
    pad_start = pad_end - padded
    return counts, pad_start, pad_end, pad_start // MOE_BLOCK, padded // MOE_BLOCK


def _destinations(info, pad_start):
    ids = info[:, 0:4].astype(jnp.int32)
    experts = jnp.arange(N_EXPERTS, dtype=jnp.int32)[None, :]

    def segment_start(e):
        return jnp.sum(jnp.where(e[:, None] == experts, pad_start[None, :], 0), axis=1)

    return segment_start(ids[:, 0]) + ids[:, 2], segment_start(ids[:, 1]) + ids[:, 3]


def kernel(x, c, w_ada, b_ada, g_norm1, g_norm2, w_in, diff_lambda, diff_subln, na_rpb, conv_dw, conv_b,
           conv_norm_g, conv_norm_b, w_out, w_router_group, w_router_expert, w_exp_gate, w_exp_up,
           w_exp_down, g_final):
    batch, seq, d = x.shape
    depth = w_ada.shape[0]
    t = batch * seq
    assert d == D_MODEL and seq == GRID_ROWS * GRID_W
    n_rows = t * 2 + N_EXPERTS * MOE_BLOCK

    mod = _ada_modulation(c, w_ada, b_ada).reshape(depth, batch, 6, d)
    rope_a = _rope_tables(seq, DIFF_DH)
    rope_d = _rope_tables(seq, HEAD_DIM)
    masks = _head_masks()
    gmat = _block_diag_ones(GROUP_WIDTH, HEAD_DIM)

    x2d = x.reshape(t, d)
    for l in range(depth):
        mod_l = mod[l]
        qa, ka, va, qb, kb, vb, pc, qd, kd, vd = _input_projection(
            x2d, mod_l, g_norm1[l], w_in[l].astype(BF16), rope_a, rope_d, seq)
        ya = _diff_attention(qa, ka, va, diff_lambda[l], diff_subln[l], l, batch, seq, masks, gmat)
        yb = _neighborhood_attention(qb, kb, vb, _na_bias_table(na_rpb[l]), batch, seq, masks)
        yc = _conformer_conv(pc, conv_dw[l], conv_b[l], conv_norm_g[l], conv_norm_b[l], batch, seq, gmat)
        yd = _dilated_attention(qd, kd, vd, batch, seq, masks)
        x1, h2, logits = _output_projection(
            (ya, yb, yc, yd), w_out[l].astype(BF16), x2d, mod_l, g_norm2[l],
            _router_weights(w_router_group[l], w_router_expert[l]).astype(BF16), seq)
        info, counts = _routing(logits)
        counts, pad_start, pad_end, first_blk, n_blk = _block_layout(counts)
        dest1, dest2 = _destinations(info, pad_start)
        xs = _dispatch(h2, dest1, dest2, pad_end, counts, n_rows)
        ys = _expert_mlp(xs, first_blk, n_blk, w_exp_gate, w_exp_up, w_exp_down, l)
        x2d = _combine(x1, mod_l, info, ys, dest1, dest2, g_final, seq, final_norm=(l == depth - 1))
    return x2d.reshape(batch, seq, d)
```

```python
import functools
import math

import numpy as np
import jax
import jax.numpy as jnp
from jax import lax
from jax.experimental import pallas as pl
from jax.experimental.pallas import tpu as pltpu

F32 = jnp.float32
BF16 = jnp.bfloat16

D_MODEL = 1024
GROUP_WIDTH = 256
HEAD_DIM = 64
N_HEADS = 4
DIFF_DH = 32
CONV_K = 31
CONV_GROUP_CH = 64
GRID_W = 64
NA_KH = 8
NA_KW = 16
GRID_ROWS = 32
NA_ROWS_PER_ITER = 2
ROPE_THETA = 10000.0
N_GROUPS = 4
EXPERTS_PER_GROUP = 8
N_EXPERTS = 32
D_EXPERT = 512
MOE_BLOCK = 256
EPS = 1e-6
NEG_INF = -1e30
LOG2E = 1.4426950408889634
ROUTER_LANES = 128
DILATED_CFG = ((128, 1), (512, 4), (2048, 16))
DIFF_EXP2_SCALE = (DIFF_DH ** -0.5) * LOG2E

VMEM_LIMIT = 56 * 1024 * 1024


def _cparams(*sem):
    return pltpu.CompilerParams(dimension_semantics=sem, vmem_limit_bytes=VMEM_LIMIT)


def _dot(a, b):
    return jnp.dot(a, b, preferred_element_type=F32)


def _dot_nt(a, b):
    return lax.dot_general(a, b, (((1,), (1,)), ((), ())), preferred_element_type=F32)


def _split(a):
    hi = a.astype(BF16)
    lo = (a - hi.astype(F32)).astype(BF16)
    return hi, lo


def _dot3(a, b):
    ah, al = _split(a)
    bh, bl = _split(b)
    return _dot(ah, bh) + (_dot(ah, bl) + _dot(al, bh))


def _group_mean(v, gmat, width):
    hi, lo = _split(v)
    return (_dot(hi, gmat) + _dot(lo, gmat)) * (1.0 / width)


def _block_diag_ones(n, width):
    idx = np.arange(n) // width
    return jnp.asarray((idx[:, None] == idx[None, :]).astype(np.float32), dtype=BF16)


def _ada_kernel(c_ref, w_ref, b_ref, o_ref):
    c = c_ref[...]
    ca = c * jax.nn.sigmoid(c)
    o_ref[0] = _dot3(ca, w_ref[0]) + b_ref[0]


def _ada_modulation(c, w_ada, b_ada):
    depth, d, n = w_ada.shape
    b = c.shape[0]
    bn = 1024
    return pl.pallas_call(
        _ada_kernel,
        grid=(depth, n // bn),
        in_specs=[
            pl.BlockSpec((b, d), lambda l, j: (0, 0)),
            pl.BlockSpec((1, d, bn), lambda l, j: (l, 0, j)),
            pl.BlockSpec((1, 1, bn), lambda l, j: (l, 0, j)),
        ],
        out_specs=pl.BlockSpec((1, b, bn), lambda l, j: (l, 0, j)),
        out_shape=jax.ShapeDtypeStruct((depth, b, n), F32),
        compiler_params=_cparams("arbitrary", "arbitrary"),
        name="ada_modulation",
    )(c, w_ada, b_ada.reshape(depth, 1, n))


def _rope_tables(seq, dim):
    half = dim // 2
    inv = ROPE_THETA ** (-jnp.arange(0, dim, 2, dtype=F32) / dim)
    ang = jnp.arange(seq, dtype=F32)[:, None] * inv[None, :]
    cos, sin = jnp.cos(ang), jnp.sin(ang)
    reps = 128 // dim
    zeros = jnp.zeros_like(sin)
    cos_t = jnp.tile(jnp.concatenate([cos, cos], axis=1), (1, reps))
    sin_hi = jnp.tile(jnp.concatenate([zeros, sin], axis=1), (1, reps))
    sin_lo = jnp.tile(jnp.concatenate([-sin, zeros], axis=1), (1, reps))
    return cos_t, sin_hi, sin_lo


def _rotary(v, cos_t, sin_hi, sin_lo, half):
    outs = []
    for j in range(v.shape[1] // 128):
        vj = v[:, j * 128:(j + 1) * 128]
        outs.append(vj * cos_t + pltpu.roll(vj, half, 1) * sin_hi + pltpu.roll(vj, 128 - half, 1) * sin_lo)
    return jnp.concatenate(outs, axis=1)


def _inproj_kernel(x_ref, mod_ref, g_ref, w_ref, ca_ref, sha_ref, sla_ref, cd_ref, shd_ref, sld_ref,
                   qa_ref, ka_ref, va_ref, qb_ref, kb_ref, vb_ref, pc_ref, qd_ref, kd_ref, vd_ref):
    x = x_ref[...]
    ms = jnp.mean(x * x, axis=-1, keepdims=True)
    y = x * lax.rsqrt(ms + EPS)
    mod = mod_ref[0]
    h = (y * g_ref[...]) * (1.0 + mod[1:2]) + mod[0:1]
    hb = h.astype(BF16)
    gw = GROUP_WIDTH

    def proj(col):
        return _dot(hb, w_ref[:, col * gw:(col + 1) * gw])

    rot_a = functools.partial(_rotary, cos_t=ca_ref[...], sin_hi=sha_ref[...], sin_lo=sla_ref[...],
                              half=DIFF_DH // 2)
    rot_d = functools.partial(_rotary, cos_t=cd_ref[...], sin_hi=shd_ref[...], sin_lo=sld_ref[...],
                              half=HEAD_DIM // 2)
    na_scale = HEAD_DIM ** -0.5
    qa_ref[...] = (rot_a(proj(0)) * DIFF_EXP2_SCALE).astype(BF16)
    ka_ref[...] = rot_a(proj(1)).astype(BF16)
    va_ref[...] = proj(2).astype(BF16)
    qb_ref[...] = (proj(3) * na_scale).astype(BF16)
    kb_ref[...] = proj(4).astype(BF16)
    vb_ref[...] = proj(5).astype(BF16)
    pc_ref[:, 0:gw] = proj(6)
    pc_ref[:, gw:2 * gw] = proj(7)
    for ref, val in ((qd_ref, rot_d(proj(8)) * na_scale), (kd_ref, rot_d(proj(9))), (vd_ref, proj(10))):
        for j in range(gw // 128):
            ref[j] = val[:, j * 128:(j + 1) * 128]


def _input_projection(x2d, mod_l, g1, w_in_bf16, rope_a, rope_d, seq):
    t, d = x2d.shape
    tm = 512
    tiles_per_batch = seq // tm
    p_in = w_in_bf16.shape[1]
    gw = GROUP_WIDTH
    row_spec = lambda width: pl.BlockSpec((tm, width), lambda i: (i, 0))
    tab_spec = pl.BlockSpec((tm, 128), lambda i: (i % tiles_per_batch, 0))
    out_shapes = []
    out_specs = []
    for name in ("qa", "ka", "va", "qb", "kb", "vb", "pc", "qd", "kd", "vd"):
        if name == "pc":
            out_shapes.append(jax.ShapeDtypeStruct((t, 2 * gw), F32))
            out_specs.append(row_spec(2 * gw))
        elif name[1] == "d":
            out_shapes.append(jax.ShapeDtypeStruct((gw // 128, t, 128), F32))
            out_specs.append(pl.BlockSpec((gw // 128, tm, 128), lambda i: (0, i, 0)))
        else:
            out_shapes.append(jax.ShapeDtypeStruct((t, gw), BF16))
            out_specs.append(row_spec(gw))
    return pl.pallas_call(
        _inproj_kernel,
        grid=(t // tm,),
        in_specs=[
            row_spec(d),
            pl.BlockSpec((1, 6, d), lambda i: (i // tiles_per_batch, 0, 0)),
            pl.BlockSpec((1, d), lambda i: (0, 0)),
            pl.BlockSpec((d, p_in), lambda i: (0, 0)),
            tab_spec, tab_spec, tab_spec, tab_spec, tab_spec, tab_spec,
        ],
        out_specs=out_specs,
        out_shape=out_shapes,
        compiler_params=_cparams("arbitrary"),
        name="input_projection",
    )(x2d, mod_l, g1.reshape(1, d), w_in_bf16, *rope_a, *rope_d)


def _head_masks():
    lane = np.arange(GROUP_WIDTH)
    head = np.stack([(lane // HEAD_DIM == h) for h in range(N_HEADS)]).astype(np.float32)
    diff = np.stack([(lane // DIFF_DH == j) for j in range(2 * N_HEADS)]).astype(np.float32)
    return (jnp.asarray(head[:, None, :], dtype=BF16), jnp.asarray(head[:, None, :], dtype=F32),
            jnp.asarray(diff[:, None, :], dtype=BF16))


DIFF_ONES_ROWS = 16


def _diff_attn_kernel(lam_init, q_ref, k_ref, v_ref, lp_ref, g_ref, dmask_ref, gmat_ref, o_ref, vt_ref, ot_ref):
    seq = k_ref.shape[0]

    @pl.when(pl.program_id(1) == 0)
    def _():
        vt = jnp.transpose(v_ref[...].astype(F32))
        for h in range(N_HEADS):
            vt_ref[h, 0:HEAD_DIM, :] = vt[h * HEAD_DIM:(h + 1) * HEAD_DIM].astype(BF16)
            vt_ref[h, HEAD_DIM:HEAD_DIM + DIFF_ONES_ROWS, :] = jnp.ones((DIFF_ONES_ROWS, seq), BF16)

    q = q_ref[...]
    k = k_ref[...]
    lp = lp_ref[...]
    lam = (jnp.exp(jnp.sum(lp[0:1] * lp[1:2], axis=-1, keepdims=True))
           - jnp.exp(jnp.sum(lp[2:3] * lp[3:4], axis=-1, keepdims=True)) + lam_init)

    def scores(j):
        return _dot_nt(k, q * dmask_ref[j])

    def weights(st):
        return jnp.exp2(st - jnp.max(st, axis=0, keepdims=True)).astype(BF16)

    def attend(j, e):
        num = _dot(vt_ref[j // 2], e)
        return num[0:HEAD_DIM] / num[HEAD_DIM:HEAD_DIM + 1]

    n_pairs = 2 * N_HEADS
    outs = [None] * n_pairs
    st_next = scores(0)
    e_prev = None
    for j in range(n_pairs):
        st = st_next
        if j + 1 < n_pairs:
            st_next = scores(j + 1)
        e = weights(st)
        if e_prev is not None:
            outs[j - 1] = attend(j - 1, e_prev)
        e_prev = e
    outs[n_pairs - 1] = attend(n_pairs - 1, e_prev)
    for h in range(N_HEADS):
        ot_ref[h * HEAD_DIM:(h + 1) * HEAD_DIM, :] = outs[2 * h] - lam * outs[2 * h + 1]
    o = jnp.transpose(ot_ref[...])
    ms = _group_mean(o * o, gmat_ref[...], HEAD_DIM)
    o_ref[...] = ((o * lax.rsqrt(ms + EPS) * g_ref[...]) * (1.0 - lam_init)).astype(BF16)


def _diff_attention(qa, ka, va, lam_params, subln_g, layer_idx, batch, seq, masks, gmat):
    t, gw = qa.shape
    tq = 512
    nq = seq // tq
    lam_init = 0.8 - 0.6 * math.exp(-0.3 * layer_idx)
    _, _, dmask = masks
    g_tiled = jnp.tile(subln_g, N_HEADS).reshape(1, gw)
    kv_spec = pl.BlockSpec((seq, gw), lambda b, i: (b, 0))
    full = lambda shape: pl.BlockSpec(shape, lambda b, i: (0,) * len(shape))
    return pl.pallas_call(
        functools.partial(_diff_attn_kernel, lam_init),
        grid=(batch, nq),
        in_specs=[
            pl.BlockSpec((tq, gw), lambda b, i: (b * nq + i, 0)),
            kv_spec, kv_spec,
            full(lam_params.shape), full((1, gw)), full(dmask.shape), full(gmat.shape),
        ],
        out_specs=pl.BlockSpec((tq, gw), lambda b, i: (b * nq + i, 0)),
        out_shape=jax.ShapeDtypeStruct((t, gw), BF16),
        scratch_shapes=[pltpu.VMEM((N_HEADS, HEAD_DIM + DIFF_ONES_ROWS, seq), BF16), pltpu.VMEM((gw, tq), F32)],
        compiler_params=_cparams("arbitrary", "arbitrary"),
        name="diff_attention",
    )(qa, ka, va, lam_params, g_tiled, dmask, gmat)


def _na_bias_table(rpb):
    w = GRID_W
    n_heads = rpb.shape[0]
    cq = np.arange(w)[:, None]
    ck = np.arange(w)[None, :]
    dc = np.clip(ck - cq, -(NA_KW - 1), NA_KW - 1) + NA_KW - 1
    col_start = np.clip(cq - NA_KW // 2, 0, w - NA_KW)
    col_ok = (ck >= col_start) & (ck < col_start + NA_KW)
    onehot = jnp.asarray(dc[:, :, None] == np.arange(2 * NA_KW - 1), dtype=F32)
    toep = jnp.einsum('qkd,hrd->hrqk', onehot, rpb.astype(F32), precision=lax.Precision.HIGHEST)
    toep = jnp.where(col_ok[None, None], toep, NEG_INF)
    tabs = jnp.stack([toep[:, NA_KH - 1 - off:2 * NA_KH - 1 - off] for off in range(NA_KH)])
    return jnp.transpose(tabs, (0, 1, 3, 2, 4)).reshape(NA_KH, n_heads * w, NA_KH * w)


def _na_kernel(q_ref, k_ref, v_ref, tab_ref, hmask_ref, hmask_f32_ref, o_ref):
    w = GRID_W
    nk = NA_KH * w

    def scores(i):
        row_start = jnp.clip(i - NA_KH // 2, 0, GRID_ROWS - NA_KH)
        kstart = pl.multiple_of(row_start * w, w)
        q = q_ref[pl.ds(pl.multiple_of(i * w, w), w), :]
        qs = jnp.concatenate([q * hmask_ref[h] for h in range(N_HEADS)], axis=0)
        return _dot_nt(qs, k_ref[pl.ds(kstart, nk), :]) + tab_ref[i - row_start], kstart

    def attend(i, s, kstart):
        m = jnp.max(s, axis=-1, keepdims=True)
        e = jnp.exp(s - m)
        p = e * (1.0 / jnp.sum(e, axis=-1, keepdims=True))
        pv = _dot(p.astype(BF16), v_ref[pl.ds(kstart, nk), :])
        o = pv[0:w] * hmask_f32_ref[0]
        for h in range(1, N_HEADS):
            o = o + pv[h * w:(h + 1) * w] * hmask_f32_ref[h]
        o_ref[pl.ds(pl.multiple_of(i * w, w), w), :] = o.astype(BF16)

    def grid_rows(p, carry):
        rows = [NA_ROWS_PER_ITER * p + u for u in range(NA_ROWS_PER_ITER)]
        staged = [scores(i) for i in rows]
        for i, (s, kstart) in zip(rows, staged):
            attend(i, s, kstart)
        return carry

    lax.fori_loop(0, GRID_ROWS // NA_ROWS_PER_ITER, grid_rows, 0)


def _neighborhood_attention(qb, kb, vb, bias_table, batch, seq, masks):
    t, gw = qb.shape
    hmask, hmask_f32, _ = masks
    seq_spec = pl.BlockSpec((seq, gw), lambda b: (b, 0))
    full = lambda shape: pl.BlockSpec(shape, lambda b: (0,) * len(shape))
    return pl.pallas_call(
        _na_kernel,
        grid=(batch,),
        in_specs=[seq_spec, seq_spec, seq_spec, full(bias_table.shape), full(hmask.shape), full(hmask_f32.shape)],
        out_specs=seq_spec,
        out_shape=jax.ShapeDtypeStruct((t, gw), BF16),
        compiler_params=_cparams("arbitrary"),
        name="neighborhood_attention",
    )(qb, kb, vb, bias_table, hmask, hmask_f32)


SUBLANES = 8
CONV_PAD = 16
CONV_CHUNK = 128


def _conv_kernel(pc_ref, w_ref, b_ref, gn_ref, bn_ref, gmat_ref, o_ref, zp_ref, zs_ref):
    seq, ch = o_ref.shape
    a = pc_ref[:, 0:ch]
    gate = pc_ref[:, ch:2 * ch]
    zp_ref[0:CONV_PAD, :] = jnp.zeros((CONV_PAD, ch), F32)
    zp_ref[CONV_PAD + seq:2 * CONV_PAD + seq, :] = jnp.zeros((CONV_PAD, ch), F32)
    zp_ref[CONV_PAD:CONV_PAD + seq, :] = a * jax.nn.sigmoid(gate)
    span = seq + 2 * CONV_PAD - SUBLANES
    for b in range(1, SUBLANES):
        zs_ref[b - 1, 0:span, :] = zp_ref[b:b + span, :]
    gmat = gmat_ref[...]
    first = CONV_PAD - CONV_K // 2
    for c in range(seq // CONV_CHUNK):
        r0 = c * CONV_CHUNK
        acc = jnp.zeros((CONV_CHUNK, ch), F32)
        for j in range(CONV_K):
            shift, aligned = (first + j) % SUBLANES, r0 + (first + j) // SUBLANES * SUBLANES
            src = zp_ref if shift == 0 else zs_ref.at[shift - 1]
            acc = acc + w_ref[j] * src[aligned:aligned + CONV_CHUNK, :]
        z = acc + b_ref[...]
        mu = _group_mean(z, gmat, CONV_GROUP_CH)
        dz = z - mu
        var = _group_mean(dz * dz, gmat, CONV_GROUP_CH)
        zn = dz * lax.rsqrt(var + EPS) * gn_ref[...] + bn_ref[...]
        o_ref[r0:r0 + CONV_CHUNK, :] = (zn * jax.nn.sigmoid(zn)).astype(BF16)


def _conformer_conv(pc, w_dw, b_dw, g_n, b_n, batch, seq, gmat):
    t = pc.shape[0]
    ch = GROUP_WIDTH
    full = lambda shape: pl.BlockSpec(shape, lambda b: (0,) * len(shape))
    return pl.pallas_call(
        _conv_kernel,
        grid=(batch,),
        in_specs=[
            pl.BlockSpec((seq, 2 * ch), lambda b: (b, 0)),
            full((CONV_K, 1, ch)), full((1, ch)), full((1, ch)), full((1, ch)), full(gmat.shape),
        ],
        out_specs=pl.BlockSpec((seq, ch), lambda b: (b, 0)),
        out_shape=jax.ShapeDtypeStruct((t, ch), BF16),
        scratch_shapes=[pltpu.VMEM((seq + 2 * CONV_PAD, ch), F32),
                        pltpu.VMEM((SUBLANES - 1, seq + 2 * CONV_PAD, ch), F32)],
        compiler_params=_cparams("arbitrary"),
        name="conformer_conv",
    )(pc, w_dw.reshape(CONV_K, 1, ch), b_dw.reshape(1, ch), g_n.reshape(1, ch), b_n.reshape(1, ch), gmat)


DIL_QB = 128
DIL_KB = 256
DIL_UNITS_PER_ITER = 2


def _strided_rows(first, count, stride):
    return pl.ds(first, count) if stride == 1 else pl.ds(first, count, stride=stride)


def _load_rows(ref, rows):
    return jnp.concatenate([ref[j, rows, :] for j in range(ref.shape[0])], axis=1)


def _store_rows(ref, rows, val):
    for j in range(ref.shape[0]):
        ref[j, rows, :] = val[:, j * 128:(j + 1) * 128]


def _dilated_kernel(q_ref, k_ref, v_ref, hmask_ref, hmask_f32_ref, o_ref, acc_ref, m_ref, l_ref):
    seq = q_ref.shape[1]
    for branch, (window, dil) in enumerate(DILATED_CFG):
        n_side = window // (2 * dil)
        sub_len = seq // dil
        qb = min(DIL_QB, sub_len)
        kb = min(DIL_KB, sub_len)
        row = lax.broadcasted_iota(jnp.int32, (N_HEADS * qb, kb), 0) & (qb - 1)
        col = lax.broadcasted_iota(jnp.int32, (N_HEADS * qb, kb), 1)
        def scores(r, c, dil=dil, n_side=n_side, sub_len=sub_len, qb=qb, kb=kb, row=row, col=col):
            l0 = c * qb
            kl0 = jnp.clip(l0 - n_side, 0, sub_len - kb)
            if dil == 1:
                l0, kl0 = pl.multiple_of(l0, qb), pl.multiple_of(kl0, n_side)
            q_rows = _strided_rows(r + dil * l0, qb, dil)
            k_rows = _strided_rows(r + dil * kl0, kb, dil)
            q = _load_rows(q_ref, q_rows).astype(BF16)
            qs = jnp.concatenate([q * hmask_ref[h] for h in range(N_HEADS)], axis=0)
            s = _dot_nt(qs, _load_rows(k_ref, k_rows).astype(BF16))
            return jnp.where(jnp.abs(col + kl0 - row - l0) <= n_side, s, NEG_INF), q_rows, k_rows

        def attend(s, q_rows, k_rows, qb=qb, branch=branch):
            m = jnp.max(s, axis=-1, keepdims=True)
            e = jnp.exp(s - m)
            l = jnp.sum(e, axis=-1, keepdims=True)
            pv = _dot(e.astype(BF16), _load_rows(v_ref, k_rows).astype(BF16))

            def unstack(a):
                out = a[0:qb] * hmask_f32_ref[0]
                for h in range(1, N_HEADS):
                    out = out + a[h * qb:(h + 1) * qb] * hmask_f32_ref[h]
                return out

            acc_new, m_new, l_new = unstack(pv), unstack(m), unstack(l)
            if branch == 0:
                _store_rows(acc_ref, q_rows, acc_new)
                _store_rows(m_ref, q_rows, m_new)
                _store_rows(l_ref, q_rows, l_new)
            else:
                m_old = _load_rows(m_ref, q_rows)
                m_max = jnp.maximum(m_old, m_new)
                w_old = jnp.exp(m_old - m_max)
                w_new = jnp.exp(m_new - m_max)
                _store_rows(acc_ref, q_rows, _load_rows(acc_ref, q_rows) * w_old + acc_new * w_new)
                _store_rows(l_ref, q_rows, _load_rows(l_ref, q_rows) * w_old + l_new * w_new)
                _store_rows(m_ref, q_rows, m_max)

        def run(units, scores=scores, attend=attend):
            staged = [scores(r, c) for r, c in units]
            for item in staged:
                attend(*item)

        n_blocks = sub_len // qb
        if n_blocks >= DIL_UNITS_PER_ITER:
            for r in range(dil):
                def block_group(p, carry, r=r, run=run):
                    run([(r, DIL_UNITS_PER_ITER * p + u) for u in range(DIL_UNITS_PER_ITER)])
                    return carry

                lax.fori_loop(0, n_blocks // DIL_UNITS_PER_ITER, block_group, 0)
        else:
            for r0 in range(0, dil, DIL_UNITS_PER_ITER):
                run([(r0 + u, 0) for u in range(DIL_UNITS_PER_ITER)])
    for j in range(acc_ref.shape[0]):
        o_ref[:, j * 128:(j + 1) * 128] = (acc_ref[j] / l_ref[j]).astype(BF16)


def _dilated_attention(qd, kd, vd, batch, seq, masks):
    tiles, t, _ = qd.shape
    gw = tiles * 128
    hmask, hmask_f32, _ = masks
    in_spec = pl.BlockSpec((tiles, seq, 128), lambda b: (0, b, 0))
    stat = pltpu.VMEM((tiles, seq, 128), F32)
    return pl.pallas_call(
        _dilated_kernel,
        grid=(batch,),
        in_specs=[
            in_spec, in_spec, in_spec,
            pl.BlockSpec(hmask.shape, lambda b: (0, 0, 0)),
            pl.BlockSpec(hmask_f32.shape, lambda b: (0, 0, 0)),
        ],
        out_specs=pl.BlockSpec((seq, gw), lambda b: (b, 0)),
        out_shape=jax.ShapeDtypeStruct((t, gw), BF16),
        scratch_shapes=[stat, stat, stat],
        compiler_params=_cparams("arbitrary"),
        name="dilated_attention",
    )(qd, kd, vd, hmask, hmask_f32)


def _outproj_kernel(ya_ref, yb_ref, yc_ref, yd_ref, w_ref, x_ref, mod_ref, g_ref, wr_ref,
                    x1_ref, h2_ref, lg_ref):
    gw = GROUP_WIDTH
    mix = _dot(ya_ref[...], w_ref[0:gw, :])
    mix = mix + _dot(yb_ref[...], w_ref[gw:2 * gw, :])
    mix = mix + _dot(yc_ref[...], w_ref[2 * gw:3 * gw, :])
    mix = mix + _dot(yd_ref[...], w_ref[3 * gw:4 * gw, :])
    mod = mod_ref[0]
    x1 = x_ref[...] + mod[2:3] * mix
    x1_ref[...] = x1
    ms = jnp.mean(x1 * x1, axis=-1, keepdims=True)
    h2 = (x1 * lax.rsqrt(ms + EPS) * g_ref[...]) * (1.0 + mod[4:5]) + mod[3:4]
    h2_ref[...] = h2
    lg_ref[...] = _dot(h2.astype(BF16), wr_ref[...])


def _output_projection(ys, w_out_bf16, x2d, mod_l, g2, w_router, seq):
    t, d = x2d.shape
    tm = 512
    tiles_per_batch = seq // tm
    gw = GROUP_WIDTH
    row_spec = lambda width: pl.BlockSpec((tm, width), lambda i: (i, 0))
    return pl.pallas_call(
        _outproj_kernel,
        grid=(t // tm,),
        in_specs=[
            row_spec(gw), row_spec(gw), row_spec(gw), row_spec(gw),
            pl.BlockSpec((d, d), lambda i: (0, 0)),
            row_spec(d),
            pl.BlockSpec((1, 6, d), lambda i: (i // tiles_per_batch, 0, 0)),
            pl.BlockSpec((1, d), lambda i: (0, 0)),
            pl.BlockSpec((d, ROUTER_LANES), lambda i: (0, 0)),
        ],
        out_specs=[row_spec(d), row_spec(d), row_spec(ROUTER_LANES)],
        out_shape=[jax.ShapeDtypeStruct((t, d), F32), jax.ShapeDtypeStruct((t, d), F32),
                   jax.ShapeDtypeStruct((t, ROUTER_LANES), F32)],
        compiler_params=_cparams("arbitrary"),
        name="output_projection",
    )(*ys, w_out_bf16, x2d, mod_l, g2.reshape(1, d), w_router)


def _routing_kernel(lg_ref, info_ref, cnt_ref, carry_ref):
    tr = lg_ref.shape[0]

    @pl.when(pl.program_id(0) == 0)
    def _():
        carry_ref[...] = jnp.zeros_like(carry_ref)

    lg = lg_ref[...]
    lane = lax.broadcasted_iota(jnp.int32, lg.shape, 1).astype(F32)
    big = float(ROUTER_LANES)
    glog = jnp.where(lane < N_GROUPS, lg, -jnp.inf)
    gmax = jnp.max(glog, axis=-1, keepdims=True)
    p_grp = 1.0 / jnp.sum(jnp.exp(glog - gmax), axis=-1, keepdims=True)
    grp = jnp.min(jnp.where(glog == gmax, lane, big), axis=-1, keepdims=True)
    lo = N_GROUPS + EXPERTS_PER_GROUP * grp
    elog = jnp.where((lane >= lo) & (lane < lo + EXPERTS_PER_GROUP), lg, -jnp.inf)
    v1 = jnp.max(elog, axis=-1, keepdims=True)
    i1 = jnp.min(jnp.where(elog == v1, lane, big), axis=-1, keepdims=True)
    elog2 = jnp.where(lane == i1, -jnp.inf, elog)
    v2 = jnp.max(elog2, axis=-1, keepdims=True)
    i2 = jnp.min(jnp.where(elog2 == v2, lane, big), axis=-1, keepdims=True)
    d = jnp.exp(v2 - v1)
    gate1 = p_grp / (1.0 + d)
    gate2 = p_grp * d / (1.0 + d)
    sel1 = lane == i1
    sel2 = lane == i2
    sel = jnp.where(sel1 | sel2, 1.0, 0.0)
    row = lax.broadcasted_iota(jnp.int32, (tr, tr), 0)
    col = lax.broadcasted_iota(jnp.int32, (tr, tr), 1)
    before = jnp.where(col < row, 1.0, 0.0).astype(BF16)
    rank = _dot(before, sel.astype(BF16)) + carry_ref[...]
    r1 = jnp.sum(jnp.where(sel1, rank, 0.0), axis=-1, keepdims=True)
    r2 = jnp.sum(jnp.where(sel2, rank, 0.0), axis=-1, keepdims=True)
    carry_ref[...] += jnp.sum(sel, axis=0, keepdims=True)
    cnt_ref[...] = carry_ref[...]
    info = jnp.zeros_like(lg)
    for idx, val in enumerate((i1 - N_GROUPS, i2 - N_GROUPS, r1, r2, gate1, gate2)):
        info = jnp.where(lane == idx, val, info)
    info_ref[...] = info


def _routing(logits):
    t = logits.shape[0]
    tr = 512
    return pl.pallas_call(
        _routing_kernel,
        grid=(t // tr,),
        in_specs=[pl.BlockSpec((tr, ROUTER_LANES), lambda i: (i, 0))],
        out_specs=[pl.BlockSpec((tr, ROUTER_LANES), lambda i: (i, 0)),
                   pl.BlockSpec((1, ROUTER_LANES), lambda i: (0, 0))],
        out_shape=[jax.ShapeDtypeStruct((t, ROUTER_LANES), F32),
                   jax.ShapeDtypeStruct((1, ROUTER_LANES), F32)],
        scratch_shapes=[pltpu.VMEM((1, ROUTER_LANES), F32)],
        compiler_params=_cparams("arbitrary"),
        name="routing",
    )(logits)


def _row_copy(src_ref, src_row, dst_ref, dst_row, sem):
    return pltpu.make_async_copy(src_ref.at[pl.ds(src_row, 1)], dst_ref.at[pl.ds(dst_row, 1)], sem)


def _block_rows(ref, block):
    return ref.at[pl.ds(pl.multiple_of(block * MOE_BLOCK, MOE_BLOCK), MOE_BLOCK)]


def _dispatch_kernel(d1_ref, d2_ref, pad_end_ref, count_ref, h_ref, xs_ref, zero_ref, sem, zero_sem):
    td = h_ref.shape[0]
    base = pl.program_id(0) * td

    @pl.when(pl.program_id(0) == 0)
    def _():
        zero_ref[...] = jnp.zeros_like(zero_ref)

        def tail_copy(e):
            return pltpu.make_async_copy(zero_ref, _block_rows(xs_ref, pad_end_ref[e] // MOE_BLOCK - 1), zero_sem)

        for e in range(N_EXPERTS):
            @pl.when(count_ref[e] > 0)
            def _():
                tail_copy(e).start()

        for e in range(N_EXPERTS):
            @pl.when(count_ref[e] > 0)
            def _():
                tail_copy(e).wait()

        def unused_copy(j):
            return pltpu.make_async_copy(zero_ref, _block_rows(xs_ref, j), zero_sem)

        first_unused = pad_end_ref[N_EXPERTS - 1] // MOE_BLOCK
        n_blocks = xs_ref.shape[0] // MOE_BLOCK

        def start_unused(j, carry):
            unused_copy(j).start()
            return carry

        def wait_unused(j, carry):
            unused_copy(j).wait()
            return carry

        lax.fori_loop(first_unused, n_blocks, start_unused, 0)
        lax.fori_loop(first_unused, n_blocks, wait_unused, 0)

    for r in range(td):
        for slot, dref in enumerate((d1_ref, d2_ref)):
            _row_copy(h_ref, r, xs_ref, dref[base + r], sem).start(priority=slot)
    for r in range(td):
        for _ in range(2):
            _row_copy(h_ref, r, xs_ref, 0, sem).wait()


def _dispatch(h2, dest1, dest2, pad_end, counts, n_rows):
    t, d = h2.shape
    td = 256
    grid_spec = pltpu.PrefetchScalarGridSpec(
        num_scalar_prefetch=4,
        grid=(t // td,),
        in_specs=[pl.BlockSpec((td, d), lambda i, *_: (i, 0))],
        out_specs=pl.BlockSpec(memory_space=pl.ANY),
        scratch_shapes=[pltpu.VMEM((MOE_BLOCK, d), F32), pltpu.SemaphoreType.DMA(()),
                        pltpu.SemaphoreType.DMA(())],
    )
    return pl.pallas_call(
        _dispatch_kernel,
        grid_spec=grid_spec,
        out_shape=jax.ShapeDtypeStruct((n_rows, d), F32),
        compiler_params=_cparams("arbitrary"),
        name="moe_dispatch",
    )(dest1, dest2, pad_end, counts, h2)


def _expert_kernel(first_blk_ref, n_blk_ref, wg_ref, wu_ref, wd_ref, xs_ref, ys_ref,
                   x_buf, y_buf, wg_bf, wu_bf, wd_bf, in_sem, out_sem):
    e = pl.program_id(0)
    first = first_blk_ref[e]
    n_blk = n_blk_ref[e]

    def fetch(b, slot):
        return pltpu.make_async_copy(_block_rows(xs_ref, first + b), x_buf.at[slot], in_sem.at[slot])

    def flush(b, slot):
        return pltpu.make_async_copy(y_buf.at[slot], _block_rows(ys_ref, first + b), out_sem.at[slot])

    @pl.when(n_blk > 0)
    def _():
        fetch(0, 0).start()
        wg_bf[...] = wg_ref[0, 0].astype(BF16)
        wu_bf[...] = wu_ref[0, 0].astype(BF16)
        wd_bf[...] = wd_ref[0, 0].astype(BF16)

    def block(b, carry):
        slot = b & 1
        fetch(b, slot).wait()

        @pl.when(b + 1 < n_blk)
        def _():
            fetch(b + 1, 1 - slot).start()

        xb = x_buf[slot].astype(BF16)
        gate = _dot(xb, wg_bf[...])
        up = _dot(xb, wu_bf[...])
        hdn = (gate * jax.nn.sigmoid(gate)) * up
        y = _dot(hdn.astype(BF16), wd_bf[...])

        @pl.when(b >= 2)
        def _():
            flush(b - 2, slot).wait()

        y_buf[slot] = y
        flush(b, slot).start()
        return carry

    lax.fori_loop(0, n_blk, block, 0)

    for back in (2, 1):
        @pl.when(n_blk >= back)
        def _():
            b = n_blk - back
            flush(b, b & 1).wait()

    @pl.when(e == pl.num_programs(0) - 1)
    def _():
        y_buf[0] = jnp.zeros(y_buf.shape[1:], F32)
        n_total = ys_ref.shape[0] // MOE_BLOCK

        def unused(j):
            return pltpu.make_async_copy(y_buf.at[0], _block_rows(ys_ref, j), out_sem.at[0])

        def start_unused(j, carry):
            unused(j).start()
            return carry

        def wait_unused(j, carry):
            unused(j).wait()
            return carry

        lax.fori_loop(first + n_blk, n_total, start_unused, 0)
        lax.fori_loop(first + n_blk, n_total, wait_unused, 0)


def _expert_mlp(xs, first_blk, n_blk, w_gate, w_up, w_down, layer):
    n_experts, d, de = w_gate.shape[1:]
    w_map = lambda e, fb, nb: (layer, e, 0, 0)
    block_buf = pltpu.VMEM((2, MOE_BLOCK, d), F32)
    grid_spec = pltpu.PrefetchScalarGridSpec(
        num_scalar_prefetch=2,
        grid=(n_experts,),
        in_specs=[
            pl.BlockSpec((1, 1, d, de), w_map),
            pl.BlockSpec((1, 1, d, de), w_map),
            pl.BlockSpec((1, 1, de, d), w_map),
            pl.BlockSpec(memory_space=pl.ANY),
        ],
        out_specs=pl.BlockSpec(memory_space=pl.ANY),
        scratch_shapes=[block_buf, block_buf, pltpu.VMEM((d, de), BF16), pltpu.VMEM((d, de), BF16),
                        pltpu.VMEM((de, d), BF16), pltpu.SemaphoreType.DMA((2,)), pltpu.SemaphoreType.DMA((2,))],
    )
    return pl.pallas_call(
        _expert_kernel,
        grid_spec=grid_spec,
        out_shape=jax.ShapeDtypeStruct(xs.shape, F32),
        compiler_params=_cparams("arbitrary"),
        name="expert_mlp",
    )(first_blk, n_blk, w_gate, w_up, w_down, xs)


def _combine_kernel(final_norm, d1_ref, d2_ref, x_ref, mod_ref, info_ref, gf_ref, ys_ref, o_ref, buf_ref, sem):
    tc = x_ref.shape[0]
    base = pl.program_id(0) * tc

    for r in range(tc):
        for slot, dref in enumerate((d1_ref, d2_ref)):
            _row_copy(ys_ref, dref[base + r], buf_ref.at[slot], r, sem).start(priority=slot)
    for r in range(tc):
        for slot in range(2):
            _row_copy(ys_ref, 0, buf_ref.at[slot], r, sem).wait()
    info = info_ref[...]
    moe = info[:, 4:5] * buf_ref[0] + info[:, 5:6] * buf_ref[1]
    x2 = x_ref[...] + mod_ref[0][5:6] * moe
    if final_norm:
        ms = jnp.mean(x2 * x2, axis=-1, keepdims=True)
        x2 = x2 * lax.rsqrt(ms + EPS) * gf_ref[...]
    o_ref[...] = x2


def _combine(x1, mod_l, info, ys, dest1, dest2, g_final, seq, final_norm):
    t, d = x1.shape
    tc = 256
    tiles_per_batch = seq // tc
    grid_spec = pltpu.PrefetchScalarGridSpec(
        num_scalar_prefetch=2,
        grid=(t // tc,),
        in_specs=[
            pl.BlockSpec((tc, d), lambda i, d1, d2: (i, 0)),
            pl.BlockSpec((1, 6, d), lambda i, d1, d2: (i // tiles_per_batch, 0, 0)),
            pl.BlockSpec((tc, ROUTER_LANES), lambda i, d1, d2: (i, 0)),
            pl.BlockSpec((1, d), lambda i, d1, d2: (0, 0)),
            pl.BlockSpec(memory_space=pl.ANY),
        ],
        out_specs=pl.BlockSpec((tc, d), lambda i, d1, d2: (i, 0)),
        scratch_shapes=[pltpu.VMEM((2, tc, d), F32), pltpu.SemaphoreType.DMA(())],
    )
    return pl.pallas_call(
        functools.partial(_combine_kernel, final_norm),
        grid_spec=grid_spec,
        out_shape=jax.ShapeDtypeStruct((t, d), F32),
        compiler_params=_cparams("arbitrary"),
        name="moe_combine",
    )(dest1, dest2, x1, mod_l, info, g_final.reshape(1, d), ys)


def _router_weights(w_rg, w_re):
    d = w_rg.shape[0]
    w_experts = jnp.transpose(w_re, (1, 0, 2)).reshape(d, N_EXPERTS)
    pad = jnp.zeros((d, ROUTER_LANES - N_GROUPS - N_EXPERTS), F32)
    return jnp.concatenate([w_rg, w_experts, pad], axis=1)


def _block_layout(counts_row):
    counts = counts_row[0, N_GROUPS:N_GROUPS + N_EXPERTS].astype(jnp.int32)
    padded = (counts + MOE_BLOCK - 1) // MOE_BLOCK * MOE_BLOCK
    pad_end = jnp.cumsum(padded)
    pad_start = pad_end - padded
    return counts, pad_start, pad_end, pad_start // MOE_BLOCK, padded // MOE_BLOCK


def _destinations(info, pad_start):
    ids = info[:, 0:4].astype(jnp.int32)
    experts = jnp.arange(N_EXPERTS, dtype=jnp.int32)[None, :]

    def segment_start(e):
        return jnp.sum(jnp.where(e[:, None] == experts, pad_start[None, :], 0), axis=1)

    return segment_start(ids[:, 0]) + ids[:, 2], segment_start(ids[:, 1]) + ids[:, 3]


def kernel(x, c, w_ada, b_ada, g_norm1, g_norm2, w_in, diff_lambda, diff_subln, na_rpb, conv_dw, conv_b,
           conv_norm_g, conv_norm_b, w_out, w_router_group, w_router_expert, w_exp_gate, w_exp_up,
           w_exp_down, g_final):
    batch, seq, d = x.shape
    depth = w_ada.shape[0]
    t = batch * seq
    assert d == D_MODEL and seq == GRID_ROWS * GRID_W
    n_rows = t * 2 + N_EXPERTS * MOE_BLOCK

    mod = _ada_modulation(c, w_ada, b_ada).reshape(depth, batch, 6, d)
    rope_a = _rope_tables(seq, DIFF_DH)
    rope_d = _rope_tables(seq, HEAD_DIM)
    masks = _head_masks()
    gmat = _block_diag_ones(GROUP_WIDTH, HEAD_DIM)

    x2d = x.reshape(t, d)
    for l in range(depth):
        mod_l = mod[l]
        qa, ka, va, qb, kb, vb, pc, qd, kd, vd = _input_projection(
            x2d, mod_l, g_norm1[l], w_in[l].astype(BF16), rope_a, rope_d, seq)
        ya = _diff_attention(qa, ka, va, diff_lambda[l], diff_subln[l], l, batch, seq, masks, gmat)
        yb = _neighborhood_attention(qb, kb, vb, _na_bias_table(na_rpb[l]), batch, seq, masks)
        yc = _conformer_conv(pc, conv_dw[l], conv_b[l], conv_norm_g[l], conv_norm_b[l], batch, seq, gmat)
        yd = _dilated_attention(qd, kd, vd, batch, seq, masks)
        x1, h2, logits = _output_projection(
            (ya, yb, yc, yd), w_out[l].astype(BF16), x2d, mod_l, g_norm2[l],
            _router_weights(w_router_group[l], w_router_expert[l]).astype(BF16), seq)
        info, counts = _routing(logits)
        counts, pad_start, pad_end, first_blk, n_blk = _block_layout(counts)
        dest1, dest2 = _destinations(info, pad_start)
        xs = _dispatch(h2, dest1, dest2, pad_end, counts, n_rows)
        ys = _expert_mlp(xs, first_blk, n_blk, w_exp_gate, w_exp_up, w_exp_down, l)
        x2d = _combine(x1, mod_l, info, ys, dest1, dest2, g_final, seq, final_norm=(l == depth - 1))
    return x2d.reshape(batch, seq, d)
```

```python
import functools
import math

import numpy as np
import jax
import jax.numpy as jnp
from jax import lax
from jax.experimental import pallas as pl
from jax.experimental.pallas import tpu as pltpu

F32 = jnp.float32
BF16 = jnp.bfloat16

D_MODEL = 1024
GROUP_WIDTH = 256
HEAD_DIM = 64
N_HEADS = 4
DIFF_DH = 32
CONV_K = 31
CONV_GROUP_CH = 64
GRID_W = 64
NA_KH = 8
NA_KW = 16
GRID_ROWS = 32
NA_ROWS_PER_ITER = 2
ROPE_THETA = 10000.0
N_GROUPS = 4
EXPERTS_PER_GROUP = 8
N_EXPERTS = 32
D_EXPERT = 512
MOE_BLOCK = 256
EPS = 1e-6
NEG_INF = -1e30
LOG2E = 1.4426950408889634
ROUTER_LANES = 128
DILATED_CFG = ((128, 1), (512, 4), (2048, 16))
DIFF_EXP2_SCALE = (DIFF_DH ** -0.5) * LOG2E

VMEM_LIMIT = 56 * 1024 * 1024


def _cparams(*sem):
    return pltpu.CompilerParams(dimension_semantics=sem, vmem_limit_bytes=VMEM_LIMIT)


def _dot(a, b):
    return jnp.dot(a, b, preferred_element_type=F32)


def _dot_nt(a, b):
    return lax.dot_general(a, b, (((1,), (1,)), ((), ())), preferred_element_type=F32)


def _split(a):
    hi = a.astype(BF16)
    lo = (a - hi.astype(F32)).astype(BF16)
    return hi, lo


def _dot3(a, b):
    ah, al = _split(a)
    bh, bl = _split(b)
    return _dot(ah, bh) + (_dot(ah, bl) + _dot(al, bh))


def _group_mean(v, gmat, width):
    hi, lo = _split(v)
    return (_dot(hi, gmat) + _dot(lo, gmat)) * (1.0 / width)


HIGH_HALF = 0xFFFF0000


def _pack_bf16_pairs(a):
    n = a.shape[1] // 2
    bits = lax.bitcast_convert_type(a.astype(BF16).astype(F32), jnp.uint32)
    return (bits[:, :n] >> 16) | (bits[:, n:] & jnp.uint32(HIGH_HALF))


def _unpack_bf16_pairs(u):
    lo = lax.bitcast_convert_type(u << 16, F32)
    hi = lax.bitcast_convert_type(u & jnp.uint32(HIGH_HALF), F32)
    return jnp.concatenate([lo, hi], axis=1)


def _block_diag_ones(n, width):
    idx = np.arange(n) // width
    return jnp.asarray((idx[:, None] == idx[None, :]).astype(np.float32), dtype=BF16)


def _ada_kernel(c_ref, w_ref, b_ref, o_ref):
    c = c_ref[...]
    ca = c * jax.nn.sigmoid(c)
    o_ref[0] = _dot3(ca, w_ref[0]) + b_ref[0]


def _ada_modulation(c, w_ada, b_ada):
    depth, d, n = w_ada.shape
    b = c.shape[0]
    bn = 1024
    return pl.pallas_call(
        _ada_kernel,
        grid=(depth, n // bn),
        in_specs=[
            pl.BlockSpec((b, d), lambda l, j: (0, 0)),
            pl.BlockSpec((1, d, bn), lambda l, j: (l, 0, j)),
            pl.BlockSpec((1, 1, bn), lambda l, j: (l, 0, j)),
        ],
        out_specs=pl.BlockSpec((1, b, bn), lambda l, j: (l, 0, j)),
        out_shape=jax.ShapeDtypeStruct((depth, b, n), F32),
        compiler_params=_cparams("arbitrary", "arbitrary"),
        name="ada_modulation",
    )(c, w_ada, b_ada.reshape(depth, 1, n))


def _rope_tables(seq, dim):
    half = dim // 2
    inv = ROPE_THETA ** (-jnp.arange(0, dim, 2, dtype=F32) / dim)
    ang = jnp.arange(seq, dtype=F32)[:, None] * inv[None, :]
    cos, sin = jnp.cos(ang), jnp.sin(ang)
    reps = 128 // dim
    zeros = jnp.zeros_like(sin)
    cos_t = jnp.tile(jnp.concatenate([cos, cos], axis=1), (1, reps))
    sin_hi = jnp.tile(jnp.concatenate([zeros, sin], axis=1), (1, reps))
    sin_lo = jnp.tile(jnp.concatenate([-sin, zeros], axis=1), (1, reps))
    return cos_t, sin_hi, sin_lo


def _rotary(v, cos_t, sin_hi, sin_lo, half):
    outs = []
    for j in range(v.shape[1] // 128):
        vj = v[:, j * 128:(j + 1) * 128]
        outs.append(vj * cos_t + pltpu.roll(vj, half, 1) * sin_hi + pltpu.roll(vj, 128 - half, 1) * sin_lo)
    return jnp.concatenate(outs, axis=1)


def _inproj_kernel(x_ref, mod_ref, g_ref, w_ref, ca_ref, sha_ref, sla_ref, cd_ref, shd_ref, sld_ref,
                   qa_ref, ka_ref, va_ref, qb_ref, kb_ref, vb_ref, pc_ref, qd_ref, kd_ref, vd_ref):
    x = x_ref[...]
    ms = jnp.mean(x * x, axis=-1, keepdims=True)
    y = x * lax.rsqrt(ms + EPS)
    mod = mod_ref[0]
    h = (y * g_ref[...]) * (1.0 + mod[1:2]) + mod[0:1]
    hb = h.astype(BF16)
    gw = GROUP_WIDTH

    def proj(col):
        return _dot(hb, w_ref[:, col * gw:(col + 1) * gw])

    rot_a = functools.partial(_rotary, cos_t=ca_ref[...], sin_hi=sha_ref[...], sin_lo=sla_ref[...],
                              half=DIFF_DH // 2)
    rot_d = functools.partial(_rotary, cos_t=cd_ref[...], sin_hi=shd_ref[...], sin_lo=sld_ref[...],
                              half=HEAD_DIM // 2)
    na_scale = HEAD_DIM ** -0.5
    qa_ref[...] = (rot_a(proj(0)) * DIFF_EXP2_SCALE).astype(BF16)
    ka_ref[...] = rot_a(proj(1)).astype(BF16)
    va_ref[...] = proj(2).astype(BF16)
    qb_ref[...] = (proj(3) * na_scale).astype(BF16)
    kb_ref[...] = proj(4).astype(BF16)
    vb_ref[...] = proj(5).astype(BF16)
    pc_ref[:, 0:gw] = proj(6)
    pc_ref[:, gw:2 * gw] = proj(7)
    for ref, val in ((qd_ref, rot_d(proj(8)) * na_scale), (kd_ref, rot_d(proj(9))), (vd_ref, proj(10))):
        for j in range(gw // 128):
            ref[j] = val[:, j * 128:(j + 1) * 128]


def _input_projection(x2d, mod_l, g1, w_in_bf16, rope_a, rope_d, seq):
    t, d = x2d.shape
    tm = 512
    tiles_per_batch = seq // tm
    p_in = w_in_bf16.shape[1]
    gw = GROUP_WIDTH
    row_spec = lambda width: pl.BlockSpec((tm, width), lambda i: (i, 0))
    tab_spec = pl.BlockSpec((tm, 128), lambda i: (i % tiles_per_batch, 0))
    out_shapes = []
    out_specs = []
    for name in ("qa", "ka", "va", "qb", "kb", "vb", "pc", "qd", "kd", "vd"):
        if name == "pc":
            out_shapes.append(jax.ShapeDtypeStruct((t, 2 * gw), F32))
            out_specs.append(row_spec(2 * gw))
        elif name[1] == "d":
            out_shapes.append(jax.ShapeDtypeStruct((gw // 128, t, 128), F32))
            out_specs.append(pl.BlockSpec((gw // 128, tm, 128), lambda i: (0, i, 0)))
        else:
            out_shapes.append(jax.ShapeDtypeStruct((t, gw), BF16))
            out_specs.append(row_spec(gw))
    return pl.pallas_call(
        _inproj_kernel,
        grid=(t // tm,),
        in_specs=[
            row_spec(d),
            pl.BlockSpec((1, 6, d), lambda i: (i // tiles_per_batch, 0, 0)),
            pl.BlockSpec((1, d), lambda i: (0, 0)),
            pl.BlockSpec((d, p_in), lambda i: (0, 0)),
            tab_spec, tab_spec, tab_spec, tab_spec, tab_spec, tab_spec,
        ],
        out_specs=out_specs,
        out_shape=out_shapes,
        compiler_params=_cparams("arbitrary"),
        name="input_projection",
    )(x2d, mod_l, g1.reshape(1, d), w_in_bf16, *rope_a, *rope_d)


def _head_masks():
    lane = np.arange(GROUP_WIDTH)
    head = np.stack([(lane // HEAD_DIM == h) for h in range(N_HEADS)]).astype(np.float32)
    diff = np.stack([(lane // DIFF_DH == j) for j in range(2 * N_HEADS)]).astype(np.float32)
    return (jnp.asarray(head[:, None, :], dtype=BF16), jnp.asarray(head[:, None, :], dtype=F32),
            jnp.asarray(diff[:, None, :], dtype=BF16))


DIFF_ONES_ROWS = 16


def _diff_attn_kernel(lam_init, q_ref, k_ref, v_ref, lp_ref, g_ref, dmask_ref, gmat_ref, o_ref, vt_ref, ot_ref):
    seq = k_ref.shape[0]

    @pl.when(pl.program_id(1) == 0)
    def _():
        vt = jnp.transpose(v_ref[...].astype(F32))
        for h in range(N_HEADS):
            vt_ref[h, 0:HEAD_DIM, :] = vt[h * HEAD_DIM:(h + 1) * HEAD_DIM].astype(BF16)
            vt_ref[h, HEAD_DIM:HEAD_DIM + DIFF_ONES_ROWS, :] = jnp.ones((DIFF_ONES_ROWS, seq), BF16)

    q = q_ref[...]
    k = k_ref[...]
    lp = lp_ref[...]
    lam = (jnp.exp(jnp.sum(lp[0:1] * lp[1:2], axis=-1, keepdims=True))
           - jnp.exp(jnp.sum(lp[2:3] * lp[3:4], axis=-1, keepdims=True)) + lam_init)

    def scores(j):
        return _dot_nt(k, q * dmask_ref[j])

    def weights(st):
        return jnp.exp2(st - jnp.max(st, axis=0, keepdims=True)).astype(BF16)

    def attend(j, e):
        num = _dot(vt_ref[j // 2], e)
        return num[0:HEAD_DIM] / num[HEAD_DIM:HEAD_DIM + 1]

    n_pairs = 2 * N_HEADS
    outs = [None] * n_pairs
    st_next = scores(0)
    e_prev = None
    for j in range(n_pairs):
        st = st_next
        if j + 1 < n_pairs:
            st_next = scores(j + 1)
        e = weights(st)
        if e_prev is not None:
            outs[j - 1] = attend(j - 1, e_prev)
        e_prev = e
    outs[n_pairs - 1] = attend(n_pairs - 1, e_prev)
    for h in range(N_HEADS):
        ot_ref[h * HEAD_DIM:(h + 1) * HEAD_DIM, :] = outs[2 * h] - lam * outs[2 * h + 1]
    o = jnp.transpose(ot_ref[...])
    ms = _group_mean(o * o, gmat_ref[...], HEAD_DIM)
    o_ref[...] = ((o * lax.rsqrt(ms + EPS) * g_ref[...]) * (1.0 - lam_init)).astype(BF16)


def _diff_attention(qa, ka, va, lam_params, subln_g, layer_idx, batch, seq, masks, gmat):
    t, gw = qa.shape
    tq = 512
    nq = seq // tq
    lam_init = 0.8 - 0.6 * math.exp(-0.3 * layer_idx)
    _, _, dmask = masks
    g_tiled = jnp.tile(subln_g, N_HEADS).reshape(1, gw)
    kv_spec = pl.BlockSpec((seq, gw), lambda b, i: (b, 0))
    full = lambda shape: pl.BlockSpec(shape, lambda b, i: (0,) * len(shape))
    return pl.pallas_call(
        functools.partial(_diff_attn_kernel, lam_init),
        grid=(batch, nq),
        in_specs=[
            pl.BlockSpec((tq, gw), lambda b, i: (b * nq + i, 0)),
            kv_spec, kv_spec,
            full(lam_params.shape), full((1, gw)), full(dmask.shape), full(gmat.shape),
        ],
        out_specs=pl.BlockSpec((tq, gw), lambda b, i: (b * nq + i, 0)),
        out_shape=jax.ShapeDtypeStruct((t, gw), BF16),
        scratch_shapes=[pltpu.VMEM((N_HEADS, HEAD_DIM + DIFF_ONES_ROWS, seq), BF16), pltpu.VMEM((gw, tq), F32)],
        compiler_params=_cparams("arbitrary", "arbitrary"),
        name="diff_attention",
    )(qa, ka, va, lam_params, g_tiled, dmask, gmat)


def _na_bias_table(rpb):
    w = GRID_W
    n_heads = rpb.shape[0]
    cq = np.arange(w)[:, None]
    ck = np.arange(w)[None, :]
    dc = np.clip(ck - cq, -(NA_KW - 1), NA_KW - 1) + NA_KW - 1
    col_start = np.clip(cq - NA_KW // 2, 0, w - NA_KW)
    col_ok = (ck >= col_start) & (ck < col_start + NA_KW)
    onehot = jnp.asarray(dc[:, :, None] == np.arange(2 * NA_KW - 1), dtype=F32)
    toep = jnp.einsum('qkd,hrd->hrqk', onehot, rpb.astype(F32), precision=lax.Precision.HIGHEST)
    toep = jnp.where(col_ok[None, None], toep, NEG_INF)
    tabs = jnp.stack([toep[:, NA_KH - 1 - off:2 * NA_KH - 1 - off] for off in range(NA_KH)])
    return jnp.transpose(tabs, (0, 1, 3, 2, 4)).reshape(NA_KH, n_heads * w, NA_KH * w)


def _na_kernel(q_ref, k_ref, v_ref, tab_ref, hmask_ref, hmask_f32_ref, o_ref):
    w = GRID_W
    nk = NA_KH * w

    def scores(i):
        row_start = jnp.clip(i - NA_KH // 2, 0, GRID_ROWS - NA_KH)
        kstart = pl.multiple_of(row_start * w, w)
        q = q_ref[pl.ds(pl.multiple_of(i * w, w), w), :]
        qs = jnp.concatenate([q * hmask_ref[h] for h in range(N_HEADS)], axis=0)
        return _dot_nt(qs, k_ref[pl.ds(kstart, nk), :]) + tab_ref[i - row_start], kstart

    def attend(i, s, kstart):
        m = jnp.max(s, axis=-1, keepdims=True)
        e = jnp.exp(s - m)
        p = e * (1.0 / jnp.sum(e, axis=-1, keepdims=True))
        pv = _dot(p.astype(BF16), v_ref[pl.ds(kstart, nk), :])
        o = pv[0:w] * hmask_f32_ref[0]
        for h in range(1, N_HEADS):
            o = o + pv[h * w:(h + 1) * w] * hmask_f32_ref[h]
        o_ref[pl.ds(pl.multiple_of(i * w, w), w), :] = o.astype(BF16)

    def grid_rows(p, carry):
        rows = [NA_ROWS_PER_ITER * p + u for u in range(NA_ROWS_PER_ITER)]
        staged = [scores(i) for i in rows]
        for i, (s, kstart) in zip(rows, staged):
            attend(i, s, kstart)
        return carry

    lax.fori_loop(0, GRID_ROWS // NA_ROWS_PER_ITER, grid_rows, 0)


def _neighborhood_attention(qb, kb, vb, bias_table, batch, seq, masks):
    t, gw = qb.shape
    hmask, hmask_f32, _ = masks
    seq_spec = pl.BlockSpec((seq, gw), lambda b: (b, 0))
    full = lambda shape: pl.BlockSpec(shape, lambda b: (0,) * len(shape))
    return pl.pallas_call(
        _na_kernel,
        grid=(batch,),
        in_specs=[seq_spec, seq_spec, seq_spec, full(bias_table.shape), full(hmask.shape), full(hmask_f32.shape)],
        out_specs=seq_spec,
        out_shape=jax.ShapeDtypeStruct((t, gw), BF16),
        compiler_params=_cparams("arbitrary"),
        name="neighborhood_attention",
    )(qb, kb, vb, bias_table, hmask, hmask_f32)


SUBLANES = 8
CONV_PAD = 16
CONV_CHUNK = 128


def _conv_kernel(pc_ref, w_ref, b_ref, gn_ref, bn_ref, gmat_ref, o_ref, zp_ref, zs_ref):
    seq, ch = o_ref.shape
    a = pc_ref[:, 0:ch]
    gate = pc_ref[:, ch:2 * ch]
    zp_ref[0:CONV_PAD, :] = jnp.zeros((CONV_PAD, ch), F32)
    zp_ref[CONV_PAD + seq:2 * CONV_PAD + seq, :] = jnp.zeros((CONV_PAD, ch), F32)
    zp_ref[CONV_PAD:CONV_PAD + seq, :] = a * jax.nn.sigmoid(gate)
    span = seq + 2 * CONV_PAD - SUBLANES
    for b in range(1, SUBLANES):
        zs_ref[b - 1, 0:span, :] = zp_ref[b:b + span, :]
    gmat = gmat_ref[...]
    first = CONV_PAD - CONV_K // 2
    for c in range(seq // CONV_CHUNK):
        r0 = c * CONV_CHUNK
        acc = jnp.zeros((CONV_CHUNK, ch), F32)
        for j in range(CONV_K):
            shift, aligned = (first + j) % SUBLANES, r0 + (first + j) // SUBLANES * SUBLANES
            src = zp_ref if shift == 0 else zs_ref.at[shift - 1]
            acc = acc + w_ref[j] * src[aligned:aligned + CONV_CHUNK, :]
        z = acc + b_ref[...]
        mu = _group_mean(z, gmat, CONV_GROUP_CH)
        dz = z - mu
        var = _group_mean(dz * dz, gmat, CONV_GROUP_CH)
        zn = dz * lax.rsqrt(var + EPS) * gn_ref[...] + bn_ref[...]
        o_ref[r0:r0 + CONV_CHUNK, :] = (zn * jax.nn.sigmoid(zn)).astype(BF16)


def _conformer_conv(pc, w_dw, b_dw, g_n, b_n, batch, seq, gmat):
    t = pc.shape[0]
    ch = GROUP_WIDTH
    full = lambda shape: pl.BlockSpec(shape, lambda b: (0,) * len(shape))
    return pl.pallas_call(
        _conv_kernel,
        grid=(batch,),
        in_specs=[
            pl.BlockSpec((seq, 2 * ch), lambda b: (b, 0)),
            full((CONV_K, 1, ch)), full((1, ch)), full((1, ch)), full((1, ch)), full(gmat.shape),
        ],
        out_specs=pl.BlockSpec((seq, ch), lambda b: (b, 0)),
        out_shape=jax.ShapeDtypeStruct((t, ch), BF16),
        scratch_shapes=[pltpu.VMEM((seq + 2 * CONV_PAD, ch), F32),
                        pltpu.VMEM((SUBLANES - 1, seq + 2 * CONV_PAD, ch), F32)],
        compiler_params=_cparams("arbitrary"),
        name="conformer_conv",
    )(pc, w_dw.reshape(CONV_K, 1, ch), b_dw.reshape(1, ch), g_n.reshape(1, ch), b_n.reshape(1, ch), gmat)


DIL_QB = 128
DIL_KB = 256
DIL_UNITS_PER_ITER = 2


def _strided_rows(first, count, stride):
    return pl.ds(first, count) if stride == 1 else pl.ds(first, count, stride=stride)


def _load_rows(ref, rows):
    return jnp.concatenate([ref[j, rows, :] for j in range(ref.shape[0])], axis=1)


def _store_rows(ref, rows, val):
    for j in range(ref.shape[0]):
        ref[j, rows, :] = val[:, j * 128:(j + 1) * 128]


def _dilated_kernel(q_ref, k_ref, v_ref, hmask_ref, hmask_f32_ref, o_ref, acc_ref, m_ref, l_ref):
    seq = q_ref.shape[1]
    for branch, (window, dil) in enumerate(DILATED_CFG):
        n_side = window // (2 * dil)
        sub_len = seq // dil
        qb = min(DIL_QB, sub_len)
        kb = min(DIL_KB, sub_len)
        row = lax.broadcasted_iota(jnp.int32, (N_HEADS * qb, kb), 0) & (qb - 1)
        col = lax.broadcasted_iota(jnp.int32, (N_HEADS * qb, kb), 1)
        def scores(r, c, dil=dil, n_side=n_side, sub_len=sub_len, qb=qb, kb=kb, row=row, col=col):
            l0 = c * qb
            kl0 = jnp.clip(l0 - n_side, 0, sub_len - kb)
            if dil == 1:
                l0, kl0 = pl.multiple_of(l0, qb), pl.multiple_of(kl0, n_side)
            q_rows = _strided_rows(r + dil * l0, qb, dil)
            k_rows = _strided_rows(r + dil * kl0, kb, dil)
            q = _load_rows(q_ref, q_rows).astype(BF16)
            qs = jnp.concatenate([q * hmask_ref[h] for h in range(N_HEADS)], axis=0)
            s = _dot_nt(qs, _load_rows(k_ref, k_rows).astype(BF16))
            return jnp.where(jnp.abs(col + kl0 - row - l0) <= n_side, s, NEG_INF), q_rows, k_rows

        def attend(s, q_rows, k_rows, qb=qb, branch=branch):
            m = jnp.max(s, axis=-1, keepdims=True)
            e = jnp.exp(s - m)
            l = jnp.sum(e, axis=-1, keepdims=True)
            pv = _dot(e.astype(BF16), _load_rows(v_ref, k_rows).astype(BF16))

            def unstack(a):
                out = a[0:qb] * hmask_f32_ref[0]
                for h in range(1, N_HEADS):
                    out = out + a[h * qb:(h + 1) * qb] * hmask_f32_ref[h]
                return out

            acc_new, m_new, l_new = unstack(pv), unstack(m), unstack(l)
            if branch == 0:
                _store_rows(acc_ref, q_rows, acc_new)
                _store_rows(m_ref, q_rows, m_new)
                _store_rows(l_ref, q_rows, l_new)
            else:
                m_old = _load_rows(m_ref, q_rows)
                m_max = jnp.maximum(m_old, m_new)
                w_old = jnp.exp(m_old - m_max)
                w_new = jnp.exp(m_new - m_max)
                _store_rows(acc_ref, q_rows, _load_rows(acc_ref, q_rows) * w_old + acc_new * w_new)
                _store_rows(l_ref, q_rows, _load_rows(l_ref, q_rows) * w_old + l_new * w_new)
                _store_rows(m_ref, q_rows, m_max)

        def run(units, scores=scores, attend=attend):
            staged = [scores(r, c) for r, c in units]
            for item in staged:
                attend(*item)

        n_blocks = sub_len // qb
        if n_blocks >= DIL_UNITS_PER_ITER:
            for r in range(dil):
                def block_group(p, carry, r=r, run=run):
                    run([(r, DIL_UNITS_PER_ITER * p + u) for u in range(DIL_UNITS_PER_ITER)])
                    return carry

                lax.fori_loop(0, n_blocks // DIL_UNITS_PER_ITER, block_group, 0)
        else:
            for r0 in range(0, dil, DIL_UNITS_PER_ITER):
                run([(r0 + u, 0) for u in range(DIL_UNITS_PER_ITER)])
    for j in range(acc_ref.shape[0]):
        o_ref[:, j * 128:(j + 1) * 128] = (acc_ref[j] / l_ref[j]).astype(BF16)


def _dilated_attention(qd, kd, vd, batch, seq, masks):
    tiles, t, _ = qd.shape
    gw = tiles * 128
    hmask, hmask_f32, _ = masks
    in_spec = pl.BlockSpec((tiles, seq, 128), lambda b: (0, b, 0))
    stat = pltpu.VMEM((tiles, seq, 128), F32)
    return pl.pallas_call(
        _dilated_kernel,
        grid=(batch,),
        in_specs=[
            in_spec, in_spec, in_spec,
            pl.BlockSpec(hmask.shape, lambda b: (0, 0, 0)),
            pl.BlockSpec(hmask_f32.shape, lambda b: (0, 0, 0)),
        ],
        out_specs=pl.BlockSpec((seq, gw), lambda b: (b, 0)),
        out_shape=jax.ShapeDtypeStruct((t, gw), BF16),
        scratch_shapes=[stat, stat, stat],
        compiler_params=_cparams("arbitrary"),
        name="dilated_attention",
    )(qd, kd, vd, hmask, hmask_f32)


def _outproj_kernel(ya_ref, yb_ref, yc_ref, yd_ref, w_ref, x_ref, mod_ref, g_ref, wr_ref,
                    x1_ref, h2_ref, lg_ref):
    gw = GROUP_WIDTH
    mix = _dot(ya_ref[...], w_ref[0:gw, :])
    mix = mix + _dot(yb_ref[...], w_ref[gw:2 * gw, :])
    mix = mix + _dot(yc_ref[...], w_ref[2 * gw:3 * gw, :])
    mix = mix + _dot(yd_ref[...], w_ref[3 * gw:4 * gw, :])
    mod = mod_ref[0]
    x1 = x_ref[...] + mod[2:3] * mix
    x1_ref[...] = x1
    ms = jnp.mean(x1 * x1, axis=-1, keepdims=True)
    h2 = (x1 * lax.rsqrt(ms + EPS) * g_ref[...]) * (1.0 + mod[4:5]) + mod[3:4]
    h2_ref[...] = _pack_bf16_pairs(h2)
    lg_ref[...] = _dot(h2.astype(BF16), wr_ref[...])


def _output_projection(ys, w_out_bf16, x2d, mod_l, g2, w_router, seq):
    t, d = x2d.shape
    tm = 512
    tiles_per_batch = seq // tm
    gw = GROUP_WIDTH
    row_spec = lambda width: pl.BlockSpec((tm, width), lambda i: (i, 0))
    return pl.pallas_call(
        _outproj_kernel,
        grid=(t // tm,),
        in_specs=[
            row_spec(gw), row_spec(gw), row_spec(gw), row_spec(gw),
            pl.BlockSpec((d, d), lambda i: (0, 0)),
            row_spec(d),
            pl.BlockSpec((1, 6, d), lambda i: (i // tiles_per_batch, 0, 0)),
            pl.BlockSpec((1, d), lambda i: (0, 0)),
            pl.BlockSpec((d, ROUTER_LANES), lambda i: (0, 0)),
        ],
        out_specs=[row_spec(d), row_spec(d // 2), row_spec(ROUTER_LANES)],
        out_shape=[jax.ShapeDtypeStruct((t, d), F32), jax.ShapeDtypeStruct((t, d // 2), jnp.uint32),
                   jax.ShapeDtypeStruct((t, ROUTER_LANES), F32)],
        compiler_params=_cparams("arbitrary"),
        name="output_projection",
    )(*ys, w_out_bf16, x2d, mod_l, g2.reshape(1, d), w_router)


def _routing_kernel(lg_ref, info_ref, cnt_ref, carry_ref):
    tr = lg_ref.shape[0]

    @pl.when(pl.program_id(0) == 0)
    def _():
        carry_ref[...] = jnp.zeros_like(carry_ref)

    lg = lg_ref[...]
    lane = lax.broadcasted_iota(jnp.int32, lg.shape, 1).astype(F32)
    big = float(ROUTER_LANES)
    glog = jnp.where(lane < N_GROUPS, lg, -jnp.inf)
    gmax = jnp.max(glog, axis=-1, keepdims=True)
    p_grp = 1.0 / jnp.sum(jnp.exp(glog - gmax), axis=-1, keepdims=True)
    grp = jnp.min(jnp.where(glog == gmax, lane, big), axis=-1, keepdims=True)
    lo = N_GROUPS + EXPERTS_PER_GROUP * grp
    elog = jnp.where((lane >= lo) & (lane < lo + EXPERTS_PER_GROUP), lg, -jnp.inf)
    v1 = jnp.max(elog, axis=-1, keepdims=True)
    i1 = jnp.min(jnp.where(elog == v1, lane, big), axis=-1, keepdims=True)
    elog2 = jnp.where(lane == i1, -jnp.inf, elog)
    v2 = jnp.max(elog2, axis=-1, keepdims=True)
    i2 = jnp.min(jnp.where(elog2 == v2, lane, big), axis=-1, keepdims=True)
    d = jnp.exp(v2 - v1)
    gate1 = p_grp / (1.0 + d)
    gate2 = p_grp * d / (1.0 + d)
    sel1 = lane == i1
    sel2 = lane == i2
    sel = jnp.where(sel1 | sel2, 1.0, 0.0)
    row = lax.broadcasted_iota(jnp.int32, (tr, tr), 0)
    col = lax.broadcasted_iota(jnp.int32, (tr, tr), 1)
    before = jnp.where(col < row, 1.0, 0.0).astype(BF16)
    rank = _dot(before, sel.astype(BF16)) + carry_ref[...]
    r1 = jnp.sum(jnp.where(sel1, rank, 0.0), axis=-1, keepdims=True)
    r2 = jnp.sum(jnp.where(sel2, rank, 0.0), axis=-1, keepdims=True)
    carry_ref[...] += jnp.sum(sel, axis=0, keepdims=True)
    cnt_ref[...] = carry_ref[...]
    info = jnp.zeros_like(lg)
    for idx, val in enumerate((i1 - N_GROUPS, i2 - N_GROUPS, r1, r2, gate1, gate2)):
        info = jnp.where(lane == idx, val, info)
    info_ref[...] = info


def _routing(logits):
    t = logits.shape[0]
    tr = 512
    return pl.pallas_call(
        _routing_kernel,
        grid=(t // tr,),
        in_specs=[pl.BlockSpec((tr, ROUTER_LANES), lambda i: (i, 0))],
        out_specs=[pl.BlockSpec((tr, ROUTER_LANES), lambda i: (i, 0)),
                   pl.BlockSpec((1, ROUTER_LANES), lambda i: (0, 0))],
        out_shape=[jax.ShapeDtypeStruct((t, ROUTER_LANES), F32),
                   jax.ShapeDtypeStruct((1, ROUTER_LANES), F32)],
        scratch_shapes=[pltpu.VMEM((1, ROUTER_LANES), F32)],
        compiler_params=_cparams("arbitrary"),
        name="routing",
    )(logits)


def _row_copy(src_ref, src_row, dst_ref, dst_row, sem):
    return pltpu.make_async_copy(src_ref.at[pl.ds(src_row, 1)], dst_ref.at[pl.ds(dst_row, 1)], sem)


def _block_rows(ref, block):
    return ref.at[pl.ds(pl.multiple_of(block * MOE_BLOCK, MOE_BLOCK), MOE_BLOCK)]


def _dispatch_kernel(d1_ref, d2_ref, pad_end_ref, count_ref, h_ref, xs_ref, zero_ref, sem, zero_sem):
    td = h_ref.shape[0]
    base = pl.program_id(0) * td

    @pl.when(pl.program_id(0) == 0)
    def _():
        zero_ref[...] = jnp.zeros_like(zero_ref)

        def tail_copy(e):
            return pltpu.make_async_copy(zero_ref, _block_rows(xs_ref, pad_end_ref[e] // MOE_BLOCK - 1), zero_sem)

        for e in range(N_EXPERTS):
            @pl.when(count_ref[e] > 0)
            def _():
                tail_copy(e).start()

        for e in range(N_EXPERTS):
            @pl.when(count_ref[e] > 0)
            def _():
                tail_copy(e).wait()

        def unused_copy(j):
            return pltpu.make_async_copy(zero_ref, _block_rows(xs_ref, j), zero_sem)

        first_unused = pad_end_ref[N_EXPERTS - 1] // MOE_BLOCK
        n_blocks = xs_ref.shape[0] // MOE_BLOCK

        def start_unused(j, carry):
            unused_copy(j).start()
            return carry

        def wait_unused(j, carry):
            unused_copy(j).wait()
            return carry

        lax.fori_loop(first_unused, n_blocks, start_unused, 0)
        lax.fori_loop(first_unused, n_blocks, wait_unused, 0)

    for r in range(td):
        for slot, dref in enumerate((d1_ref, d2_ref)):
            _row_copy(h_ref, r, xs_ref, dref[base + r], sem).start(priority=slot)
    for r in range(td):
        for _ in range(2):
            _row_copy(h_ref, r, xs_ref, 0, sem).wait()


def _dispatch(h2, dest1, dest2, pad_end, counts, n_rows):
    t, d = h2.shape
    td = 256
    grid_spec = pltpu.PrefetchScalarGridSpec(
        num_scalar_prefetch=4,
        grid=(t // td,),
        in_specs=[pl.BlockSpec((td, d), lambda i, *_: (i, 0))],
        out_specs=pl.BlockSpec(memory_space=pl.ANY),
        scratch_shapes=[pltpu.VMEM((MOE_BLOCK, d), h2.dtype), pltpu.SemaphoreType.DMA(()),
                        pltpu.SemaphoreType.DMA(())],
    )
    return pl.pallas_call(
        _dispatch_kernel,
        grid_spec=grid_spec,
        out_shape=jax.ShapeDtypeStruct((n_rows, d), h2.dtype),
        compiler_params=_cparams("arbitrary"),
        name="moe_dispatch",
    )(dest1, dest2, pad_end, counts, h2)


def _expert_kernel(blk_e_ref, nvalid_ref, xs_ref, wg_ref, wu_ref, wd_ref, ys_ref, wg_bf, wu_bf, wd_bf):
    j = pl.program_id(0)

    @pl.when((j == 0) | (blk_e_ref[j] != blk_e_ref[jnp.maximum(j - 1, 0)]))
    def _():
        wg_bf[...] = wg_ref[0, 0].astype(BF16)
        wu_bf[...] = wu_ref[0, 0].astype(BF16)
        wd_bf[...] = wd_ref[0, 0].astype(BF16)

    @pl.when(j < nvalid_ref[0])
    def _():
        xb = _unpack_bf16_pairs(xs_ref[...]).astype(BF16)
        gate = _dot(xb, wg_bf[...])
        up = _dot(xb, wu_bf[...])
        hdn = (gate * jax.nn.sigmoid(gate)) * up
        ys_ref[...] = _pack_bf16_pairs(_dot(hdn.astype(BF16), wd_bf[...]))

    @pl.when(j >= nvalid_ref[0])
    def _():
        ys_ref[...] = jnp.zeros_like(ys_ref)


def _expert_mlp(xs, blk_e, nvalid, w_gate, w_up, w_down, layer):
    n_rows, half = xs.shape
    nblk = n_rows // MOE_BLOCK
    d, de = w_gate.shape[2:]

    def x_map(j, be, nv):
        return (jnp.minimum(j, nv[0] - 1), 0)

    w_map = lambda j, be, nv: (layer, be[j], 0, 0)
    grid_spec = pltpu.PrefetchScalarGridSpec(
        num_scalar_prefetch=2,
        grid=(nblk,),
        in_specs=[
            pl.BlockSpec((MOE_BLOCK, half), x_map),
            pl.BlockSpec((1, 1, d, de), w_map),
            pl.BlockSpec((1, 1, d, de), w_map),
            pl.BlockSpec((1, 1, de, d), w_map),
        ],
        out_specs=pl.BlockSpec((MOE_BLOCK, half), lambda j, be, nv: (j, 0)),
        scratch_shapes=[pltpu.VMEM((d, de), BF16), pltpu.VMEM((d, de), BF16), pltpu.VMEM((de, d), BF16)],
    )
    return pl.pallas_call(
        _expert_kernel,
        grid_spec=grid_spec,
        out_shape=jax.ShapeDtypeStruct((n_rows, half), jnp.uint32),
        compiler_params=_cparams("arbitrary"),
        name="expert_mlp",
    )(blk_e, nvalid, xs, w_gate, w_up, w_down)


def _combine_kernel(final_norm, d1_ref, d2_ref, x_ref, mod_ref, info_ref, gf_ref, ys_ref, o_ref, buf_ref, sem):
    tc = x_ref.shape[0]
    base = pl.program_id(0) * tc

    for r in range(tc):
        for slot, dref in enumerate((d1_ref, d2_ref)):
            _row_copy(ys_ref, dref[base + r], buf_ref.at[slot], r, sem).start(priority=slot)
    for r in range(tc):
        for slot in range(2):
            _row_copy(ys_ref, 0, buf_ref.at[slot], r, sem).wait()
    info = info_ref[...]
    moe = info[:, 4:5] * _unpack_bf16_pairs(buf_ref[0]) + info[:, 5:6] * _unpack_bf16_pairs(buf_ref[1])
    x2 = x_ref[...] + mod_ref[0][5:6] * moe
    if final_norm:
        ms = jnp.mean(x2 * x2, axis=-1, keepdims=True)
        x2 = x2 * lax.rsqrt(ms + EPS) * gf_ref[...]
    o_ref[...] = x2


def _combine(x1, mod_l, info, ys, dest1, dest2, g_final, seq, final_norm):
    t, d = x1.shape
    tc = 256
    tiles_per_batch = seq // tc
    grid_spec = pltpu.PrefetchScalarGridSpec(
        num_scalar_prefetch=2,
        grid=(t // tc,),
        in_specs=[
            pl.BlockSpec((tc, d), lambda i, d1, d2: (i, 0)),
            pl.BlockSpec((1, 6, d), lambda i, d1, d2: (i // tiles_per_batch, 0, 0)),
            pl.BlockSpec((tc, ROUTER_LANES), lambda i, d1, d2: (i, 0)),
            pl.BlockSpec((1, d), lambda i, d1, d2: (0, 0)),
            pl.BlockSpec(memory_space=pl.ANY),
        ],
        out_specs=pl.BlockSpec((tc, d), lambda i, d1, d2: (i, 0)),
        scratch_shapes=[pltpu.VMEM((2, tc, d // 2), jnp.uint32), pltpu.SemaphoreType.DMA(())],
    )
    return pl.pallas_call(
        functools.partial(_combine_kernel, final_norm),
        grid_spec=grid_spec,
        out_shape=jax.ShapeDtypeStruct((t, d), F32),
        compiler_params=_cparams("arbitrary"),
        name="moe_combine",
    )(dest1, dest2, x1, mod_l, info, g_final.reshape(1, d), ys)


def _router_weights(w_rg, w_re):
    d = w_rg.shape[0]
    w_experts = jnp.transpose(w_re, (1, 0, 2)).reshape(d, N_EXPERTS)
    pad = jnp.zeros((d, ROUTER_LANES - N_GROUPS - N_EXPERTS), F32)
    return jnp.concatenate([w_rg, w_experts, pad], axis=1)


def _block_layout(counts_row, n_blocks):
    counts = counts_row[0, N_GROUPS:N_GROUPS + N_EXPERTS].astype(jnp.int32)
    padded = (counts + MOE_BLOCK - 1) // MOE_BLOCK * MOE_BLOCK
    pad_end = jnp.cumsum(padded)
    pad_start = pad_end - padded
    starts = jnp.arange(n_blocks, dtype=jnp.int32) * MOE_BLOCK
    blk_e = jnp.minimum(jnp.sum((pad_end[None, :] <= starts[:, None]).astype(jnp.int32), axis=1), N_EXPERTS - 1)
    nvalid = (pad_end[-1:] // MOE_BLOCK).astype(jnp.int32)
    return counts, pad_start, pad_end, blk_e, nvalid


def _destinations(info, pad_start):
    ids = info[:, 0:4].astype(jnp.int32)
    experts = jnp.arange(N_EXPERTS, dtype=jnp.int32)[None, :]

    def segment_start(e):
        return jnp.sum(jnp.where(e[:, None] == experts, pad_start[None, :], 0), axis=1)

    return segment_start(ids[:, 0]) + ids[:, 2], segment_start(ids[:, 1]) + ids[:, 3]


def kernel(x, c, w_ada, b_ada, g_norm1, g_norm2, w_in, diff_lambda, diff_subln, na_rpb, conv_dw, conv_b,
           conv_norm_g, conv_norm_b, w_out, w_router_group, w_router_expert, w_exp_gate, w_exp_up,
           w_exp_down, g_final):
    batch, seq, d = x.shape
    depth = w_ada.shape[0]
    t = batch * seq
    assert d == D_MODEL and seq == GRID_ROWS * GRID_W
    n_rows = t * 2 + N_EXPERTS * MOE_BLOCK

    mod = _ada_modulation(c, w_ada, b_ada).reshape(depth, batch, 6, d)
    rope_a = _rope_tables(seq, DIFF_DH)
    rope_d = _rope_tables(seq, HEAD_DIM)
    masks = _head_masks()
    gmat = _block_diag_ones(GROUP_WIDTH, HEAD_DIM)

    x2d = x.reshape(t, d)
    for l in range(depth):
        mod_l = mod[l]
        qa, ka, va, qb, kb, vb, pc, qd, kd, vd = _input_projection(
            x2d, mod_l, g_norm1[l], w_in[l].astype(BF16), rope_a, rope_d, seq)
        ya = _diff_attention(qa, ka, va, diff_lambda[l], diff_subln[l], l, batch, seq, masks, gmat)
        yb = _neighborhood_attention(qb, kb, vb, _na_bias_table(na_rpb[l]), batch, seq, masks)
        yc = _conformer_conv(pc, conv_dw[l], conv_b[l], conv_norm_g[l], conv_norm_b[l], batch, seq, gmat)
        yd = _dilated_attention(qd, kd, vd, batch, seq, masks)
        x1, h2, logits = _output_projection(
            (ya, yb, yc, yd), w_out[l].astype(BF16), x2d, mod_l, g_norm2[l],
            _router_weights(w_router_group[l], w_router_expert[l]).astype(BF16), seq)
        info, counts = _routing(logits)
        counts, pad_start, pad_end, blk_e, nvalid = _block_layout(counts, n_rows // MOE_BLOCK)
        dest1, dest2 = _destinations(info, pad_start)
        xs = _dispatch(h2, dest1, dest2, pad_end, counts, n_rows)
        ys = _expert_mlp(xs, blk_e, nvalid, w_exp_gate, w_exp_up, w_exp_down, l)
        x2d = _combine(x1, mod_l, info, ys, dest1, dest2, g_final, seq, final_norm=(l == depth - 1))
    return x2d.reshape(batch, seq, d)
```

```python
import functools
import math

import numpy as np
import jax
import jax.numpy as jnp
from jax import lax
from jax.experimental import pallas as pl
from jax.experimental.pallas import tpu as pltpu

F32 = jnp.float32
BF16 = jnp.bfloat16

D_MODEL = 1024
GROUP_WIDTH = 256
HEAD_DIM = 64
N_HEADS = 4
DIFF_DH = 32
CONV_K = 31
CONV_GROUP_CH = 64
GRID_W = 64
NA_KH = 8
NA_KW = 16
GRID_ROWS = 32
NA_ROWS_PER_ITER = 4
ROPE_THETA = 10000.0
N_GROUPS = 4
EXPERTS_PER_GROUP = 8
N_EXPERTS = 32
D_EXPERT = 512
MOE_BLOCK = 256
EPS = 1e-6
NEG_INF = -1e30
LOG2E = 1.4426950408889634
ROUTER_LANES = 128
DILATED_CFG = ((128, 1), (512, 4), (2048, 16))
DIFF_EXP2_SCALE = (DIFF_DH ** -0.5) * LOG2E

VMEM_LIMIT = 56 * 1024 * 1024


def _cparams(*sem):
    return pltpu.CompilerParams(dimension_semantics=sem, vmem_limit_bytes=VMEM_LIMIT)


def _dot(a, b):
    return jnp.dot(a, b, preferred_element_type=F32)


def _dot_nt(a, b):
    return lax.dot_general(a, b, (((1,), (1,)), ((), ())), preferred_element_type=F32)


def _split(a):
    hi = a.astype(BF16)
    lo = (a - hi.astype(F32)).astype(BF16)
    return hi, lo


def _dot3(a, b):
    ah, al = _split(a)
    bh, bl = _split(b)
    return _dot(ah, bh) + (_dot(ah, bl) + _dot(al, bh))


def _group_mean(v, gmat, width):
    hi, lo = _split(v)
    return (_dot(hi, gmat) + _dot(lo, gmat)) * (1.0 / width)


HIGH_HALF = 0xFFFF0000


def _pack_bf16_pairs(a):
    n = a.shape[1] // 2
    bits = lax.bitcast_convert_type(a.astype(BF16).astype(F32), jnp.uint32)
    return (bits[:, :n] >> 16) | (bits[:, n:] & jnp.uint32(HIGH_HALF))


def _unpack_bf16_pairs(u):
    lo = lax.bitcast_convert_type(u << 16, F32)
    hi = lax.bitcast_convert_type(u & jnp.uint32(HIGH_HALF), F32)
    return jnp.concatenate([lo, hi], axis=1)


def _block_diag_ones(n, width):
    idx = np.arange(n) // width
    return jnp.asarray((idx[:, None] == idx[None, :]).astype(np.float32), dtype=BF16)


def _ada_kernel(c_ref, w_ref, b_ref, o_ref):
    c = c_ref[...]
    ca = c * jax.nn.sigmoid(c)
    o_ref[0] = _dot3(ca, w_ref[0]) + b_ref[0]


def _ada_modulation(c, w_ada, b_ada):
    depth, d, n = w_ada.shape
    b = c.shape[0]
    bn = 1024
    return pl.pallas_call(
        _ada_kernel,
        grid=(depth, n // bn),
        in_specs=[
            pl.BlockSpec((b, d), lambda l, j: (0, 0)),
            pl.BlockSpec((1, d, bn), lambda l, j: (l, 0, j)),
            pl.BlockSpec((1, 1, bn), lambda l, j: (l, 0, j)),
        ],
        out_specs=pl.BlockSpec((1, b, bn), lambda l, j: (l, 0, j)),
        out_shape=jax.ShapeDtypeStruct((depth, b, n), F32),
        compiler_params=_cparams("arbitrary", "arbitrary"),
        name="ada_modulation",
    )(c, w_ada, b_ada.reshape(depth, 1, n))


def _rope_tables(seq, dim):
    half = dim // 2
    inv = ROPE_THETA ** (-jnp.arange(0, dim, 2, dtype=F32) / dim)
    ang = jnp.arange(seq, dtype=F32)[:, None] * inv[None, :]
    cos, sin = jnp.cos(ang), jnp.sin(ang)
    reps = 128 // dim
    zeros = jnp.zeros_like(sin)
    cos_t = jnp.tile(jnp.concatenate([cos, cos], axis=1), (1, reps))
    sin_hi = jnp.tile(jnp.concatenate([zeros, sin], axis=1), (1, reps))
    sin_lo = jnp.tile(jnp.concatenate([-sin, zeros], axis=1), (1, reps))
    return cos_t, sin_hi, sin_lo


def _rotary(v, cos_t, sin_hi, sin_lo, half):
    outs = []
    for j in range(v.shape[1] // 128):
        vj = v[:, j * 128:(j + 1) * 128]
        outs.append(vj * cos_t + pltpu.roll(vj, half, 1) * sin_hi + pltpu.roll(vj, 128 - half, 1) * sin_lo)
    return jnp.concatenate(outs, axis=1)


def _inproj_kernel(x_ref, mod_ref, g_ref, w_ref, ca_ref, sha_ref, sla_ref, cd_ref, shd_ref, sld_ref,
                   qa_ref, ka_ref, va_ref, qb_ref, kb_ref, vb_ref, pc_ref, qd_ref, kd_ref, vd_ref):
    x = x_ref[...]
    ms = jnp.mean(x * x, axis=-1, keepdims=True)
    y = x * lax.rsqrt(ms + EPS)
    mod = mod_ref[0]
    h = (y * g_ref[...]) * (1.0 + mod[1:2]) + mod[0:1]
    hb = h.astype(BF16)
    gw = GROUP_WIDTH

    def proj(col):
        return _dot(hb, w_ref[:, col * gw:(col + 1) * gw])

    rot_a = functools.partial(_rotary, cos_t=ca_ref[...], sin_hi=sha_ref[...], sin_lo=sla_ref[...],
                              half=DIFF_DH // 2)
    rot_d = functools.partial(_rotary, cos_t=cd_ref[...], sin_hi=shd_ref[...], sin_lo=sld_ref[...],
                              half=HEAD_DIM // 2)
    na_scale = HEAD_DIM ** -0.5
    qa_ref[...] = (rot_a(proj(0)) * DIFF_EXP2_SCALE).astype(BF16)
    ka_ref[...] = rot_a(proj(1)).astype(BF16)
    va_ref[...] = proj(2).astype(BF16)
    qb_ref[...] = (proj(3) * na_scale).astype(BF16)
    kb_ref[...] = proj(4).astype(BF16)
    vb_ref[...] = proj(5).astype(BF16)
    pc_ref[:, 0:gw] = proj(6)
    pc_ref[:, gw:2 * gw] = proj(7)
    for ref, val in ((qd_ref, rot_d(proj(8)) * na_scale), (kd_ref, rot_d(proj(9))), (vd_ref, proj(10))):
        for j in range(gw // 128):
            ref[j] = val[:, j * 128:(j + 1) * 128]


def _input_projection(x2d, mod_l, g1, w_in_bf16, rope_a, rope_d, seq):
    t, d = x2d.shape
    tm = 512
    tiles_per_batch = seq // tm
    p_in = w_in_bf16.shape[1]
    gw = GROUP_WIDTH
    row_spec = lambda width: pl.BlockSpec((tm, width), lambda i: (i, 0))
    tab_spec = pl.BlockSpec((tm, 128), lambda i: (i % tiles_per_batch, 0))
    out_shapes = []
    out_specs = []
    for name in ("qa", "ka", "va", "qb", "kb", "vb", "pc", "qd", "kd", "vd"):
        if name == "pc":
            out_shapes.append(jax.ShapeDtypeStruct((t, 2 * gw), F32))
            out_specs.append(row_spec(2 * gw))
        elif name[1] == "d":
            out_shapes.append(jax.ShapeDtypeStruct((gw // 128, t, 128), F32))
            out_specs.append(pl.BlockSpec((gw // 128, tm, 128), lambda i: (0, i, 0)))
        else:
            out_shapes.append(jax.ShapeDtypeStruct((t, gw), BF16))
            out_specs.append(row_spec(gw))
    return pl.pallas_call(
        _inproj_kernel,
        grid=(t // tm,),
        in_specs=[
            row_spec(d),
            pl.BlockSpec((1, 6, d), lambda i: (i // tiles_per_batch, 0, 0)),
            pl.BlockSpec((1, d), lambda i: (0, 0)),
            pl.BlockSpec((d, p_in), lambda i: (0, 0)),
            tab_spec, tab_spec, tab_spec, tab_spec, tab_spec, tab_spec,
        ],
        out_specs=out_specs,
        out_shape=out_shapes,
        compiler_params=_cparams("arbitrary"),
        name="input_projection",
    )(x2d, mod_l, g1.reshape(1, d), w_in_bf16, *rope_a, *rope_d)


def _head_masks():
    lane = np.arange(GROUP_WIDTH)
    head = np.stack([(lane // HEAD_DIM == h) for h in range(N_HEADS)]).astype(np.float32)
    diff = np.stack([(lane // DIFF_DH == j) for j in range(2 * N_HEADS)]).astype(np.float32)
    return (jnp.asarray(head[:, None, :], dtype=BF16), jnp.asarray(head[:, None, :], dtype=F32),
            jnp.asarray(diff[:, None, :], dtype=BF16))


DIFF_ONES_ROWS = 16


def _diff_attn_kernel(lam_init, q_ref, k_ref, v_ref, lp_ref, g_ref, dmask_ref, gmat_ref, o_ref, vt_ref, ot_ref):
    seq = k_ref.shape[0]

    @pl.when(pl.program_id(1) == 0)
    def _():
        vt = jnp.transpose(v_ref[...].astype(F32))
        for h in range(N_HEADS):
            vt_ref[h, 0:HEAD_DIM, :] = vt[h * HEAD_DIM:(h + 1) * HEAD_DIM].astype(BF16)
            vt_ref[h, HEAD_DIM:HEAD_DIM + DIFF_ONES_ROWS, :] = jnp.ones((DIFF_ONES_ROWS, seq), BF16)

    q = q_ref[...]
    k = k_ref[...]
    lp = lp_ref[...]
    lam = (jnp.exp(jnp.sum(lp[0:1] * lp[1:2], axis=-1, keepdims=True))
           - jnp.exp(jnp.sum(lp[2:3] * lp[3:4], axis=-1, keepdims=True)) + lam_init)

    def scores(j):
        return _dot_nt(k, q * dmask_ref[j])

    def weights(st):
        return jnp.exp2(st - jnp.max(st, axis=0, keepdims=True)).astype(BF16)

    def attend(j, e):
        num = _dot(vt_ref[j // 2], e)
        return num[0:HEAD_DIM] / num[HEAD_DIM:HEAD_DIM + 1]

    n_pairs = 2 * N_HEADS
    outs = [None] * n_pairs
    st_next = scores(0)
    e_prev = None
    for j in range(n_pairs):
        st = st_next
        if j + 1 < n_pairs:
            st_next = scores(j + 1)
        e = weights(st)
        if e_prev is not None:
            outs[j - 1] = attend(j - 1, e_prev)
        e_prev = e
    outs[n_pairs - 1] = attend(n_pairs - 1, e_prev)
    for h in range(N_HEADS):
        ot_ref[h * HEAD_DIM:(h + 1) * HEAD_DIM, :] = outs[2 * h] - lam * outs[2 * h + 1]
    o = jnp.transpose(ot_ref[...])
    ms = _group_mean(o * o, gmat_ref[...], HEAD_DIM)
    o_ref[...] = ((o * lax.rsqrt(ms + EPS) * g_ref[...]) * (1.0 - lam_init)).astype(BF16)


def _diff_attention(qa, ka, va, lam_params, subln_g, layer_idx, batch, seq, masks, gmat):
    t, gw = qa.shape
    tq = 512
    nq = seq // tq
    lam_init = 0.8 - 0.6 * math.exp(-0.3 * layer_idx)
    _, _, dmask = masks
    g_tiled = jnp.tile(subln_g, N_HEADS).reshape(1, gw)
    kv_spec = pl.BlockSpec((seq, gw), lambda b, i: (b, 0))
    full = lambda shape: pl.BlockSpec(shape, lambda b, i: (0,) * len(shape))
    return pl.pallas_call(
        functools.partial(_diff_attn_kernel, lam_init),
        grid=(batch, nq),
        in_specs=[
            pl.BlockSpec((tq, gw), lambda b, i: (b * nq + i, 0)),
            kv_spec, kv_spec,
            full(lam_params.shape), full((1, gw)), full(dmask.shape), full(gmat.shape),
        ],
        out_specs=pl.BlockSpec((tq, gw), lambda b, i: (b * nq + i, 0)),
        out_shape=jax.ShapeDtypeStruct((t, gw), BF16),
        scratch_shapes=[pltpu.VMEM((N_HEADS, HEAD_DIM + DIFF_ONES_ROWS, seq), BF16), pltpu.VMEM((gw, tq), F32)],
        compiler_params=_cparams("arbitrary", "arbitrary"),
        name="diff_attention",
    )(qa, ka, va, lam_params, g_tiled, dmask, gmat)


def _na_bias_table(rpb):
    w = GRID_W
    n_heads = rpb.shape[0]
    cq = np.arange(w)[:, None]
    ck = np.arange(w)[None, :]
    dc = np.clip(ck - cq, -(NA_KW - 1), NA_KW - 1) + NA_KW - 1
    col_start = np.clip(cq - NA_KW // 2, 0, w - NA_KW)
    col_ok = (ck >= col_start) & (ck < col_start + NA_KW)
    onehot = jnp.asarray(dc[:, :, None] == np.arange(2 * NA_KW - 1), dtype=F32)
    toep = jnp.einsum('qkd,hrd->hrqk', onehot, rpb.astype(F32), precision=lax.Precision.HIGHEST)
    toep = jnp.where(col_ok[None, None], toep, NEG_INF)
    tabs = jnp.stack([toep[:, NA_KH - 1 - off:2 * NA_KH - 1 - off] for off in range(NA_KH)])
    return jnp.transpose(tabs, (0, 1, 3, 2, 4)).reshape(NA_KH, n_heads * w, NA_KH * w)


def _na_kernel(q_ref, k_ref, v_ref, tab_ref, hmask_ref, hmask_f32_ref, o_ref):
    w = GRID_W
    nk = NA_KH * w

    def scores(i):
        row_start = jnp.clip(i - NA_KH // 2, 0, GRID_ROWS - NA_KH)
        kstart = pl.multiple_of(row_start * w, w)
        q = q_ref[pl.ds(pl.multiple_of(i * w, w), w), :]
        qs = jnp.concatenate([q * hmask_ref[h] for h in range(N_HEADS)], axis=0)
        return _dot_nt(qs, k_ref[pl.ds(kstart, nk), :]) + tab_ref[i - row_start], kstart

    def attend(i, s, kstart):
        m = jnp.max(s, axis=-1, keepdims=True)
        e = jnp.exp(s - m)
        p = e * (1.0 / jnp.sum(e, axis=-1, keepdims=True))
        pv = _dot(p.astype(BF16), v_ref[pl.ds(kstart, nk), :])
        o = pv[(N_HEADS - 1) * w:N_HEADS * w]
        for h in range(N_HEADS - 2, -1, -1):
            o = jnp.where(hmask_f32_ref[h] > 0.5, pv[h * w:(h + 1) * w], o)
        o_ref[pl.ds(pl.multiple_of(i * w, w), w), :] = o.astype(BF16)

    def grid_rows(p, carry):
        rows = [NA_ROWS_PER_ITER * p + u for u in range(NA_ROWS_PER_ITER)]
        staged = [scores(i) for i in rows]
        for i, (s, kstart) in zip(rows, staged):
            attend(i, s, kstart)
        return carry

    lax.fori_loop(0, GRID_ROWS // NA_ROWS_PER_ITER, grid_rows, 0)


def _neighborhood_attention(qb, kb, vb, bias_table, batch, seq, masks):
    t, gw = qb.shape
    hmask, hmask_f32, _ = masks
    seq_spec = pl.BlockSpec((seq, gw), lambda b: (b, 0))
    full = lambda shape: pl.BlockSpec(shape, lambda b: (0,) * len(shape))
    return pl.pallas_call(
        _na_kernel,
        grid=(batch,),
        in_specs=[seq_spec, seq_spec, seq_spec, full(bias_table.shape), full(hmask.shape), full(hmask_f32.shape)],
        out_specs=seq_spec,
        out_shape=jax.ShapeDtypeStruct((t, gw), BF16),
        compiler_params=_cparams("arbitrary"),
        name="neighborhood_attention",
    )(qb, kb, vb, bias_table, hmask, hmask_f32)


SUBLANES = 8
CONV_PAD = 16
CONV_CHUNK = 128


def _conv_kernel(pc_ref, w_ref, b_ref, gn_ref, bn_ref, gmat_ref, o_ref, zp_ref, zs_ref):
    seq, ch = o_ref.shape
    a = pc_ref[:, 0:ch]
    gate = pc_ref[:, ch:2 * ch]
    zp_ref[0:CONV_PAD, :] = jnp.zeros((CONV_PAD, ch), F32)
    zp_ref[CONV_PAD + seq:2 * CONV_PAD + seq, :] = jnp.zeros((CONV_PAD, ch), F32)
    zp_ref[CONV_PAD:CONV_PAD + seq, :] = a * jax.nn.sigmoid(gate)
    span = seq + 2 * CONV_PAD - SUBLANES
    for b in range(1, SUBLANES):
        zs_ref[b - 1, 0:span, :] = zp_ref[b:b + span, :]
    gmat = gmat_ref[...]
    first = CONV_PAD - CONV_K // 2
    for c in range(seq // CONV_CHUNK):
        r0 = c * CONV_CHUNK
        acc = jnp.zeros((CONV_CHUNK, ch), F32)
        for j in range(CONV_K):
            shift, aligned = (first + j) % SUBLANES, r0 + (first + j) // SUBLANES * SUBLANES
            src = zp_ref if shift == 0 else zs_ref.at[shift - 1]
            acc = acc + w_ref[j] * src[aligned:aligned + CONV_CHUNK, :]
        z = acc + b_ref[...]
        mu = _group_mean(z, gmat, CONV_GROUP_CH)
        dz = z - mu
        var = _group_mean(dz * dz, gmat, CONV_GROUP_CH)
        zn = dz * lax.rsqrt(var + EPS) * gn_ref[...] + bn_ref[...]
        o_ref[r0:r0 + CONV_CHUNK, :] = (zn * jax.nn.sigmoid(zn)).astype(BF16)


def _conformer_conv(pc, w_dw, b_dw, g_n, b_n, batch, seq, gmat):
    t = pc.shape[0]
    ch = GROUP_WIDTH
    full = lambda shape: pl.BlockSpec(shape, lambda b: (0,) * len(shape))
    return pl.pallas_call(
        _conv_kernel,
        grid=(batch,),
        in_specs=[
            pl.BlockSpec((seq, 2 * ch), lambda b: (b, 0)),
            full((CONV_K, 1, ch)), full((1, ch)), full((1, ch)), full((1, ch)), full(gmat.shape),
        ],
        out_specs=pl.BlockSpec((seq, ch), lambda b: (b, 0)),
        out_shape=jax.ShapeDtypeStruct((t, ch), BF16),
        scratch_shapes=[pltpu.VMEM((seq + 2 * CONV_PAD, ch), F32),
                        pltpu.VMEM((SUBLANES - 1, seq + 2 * CONV_PAD, ch), F32)],
        compiler_params=_cparams("arbitrary"),
        name="conformer_conv",
    )(pc, w_dw.reshape(CONV_K, 1, ch), b_dw.reshape(1, ch), g_n.reshape(1, ch), b_n.reshape(1, ch), gmat)


DIL_QB = 128
DIL_KB = 256
DIL_UNITS_PER_ITER = 2


def _strided_rows(first, count, stride):
    return pl.ds(first, count) if stride == 1 else pl.ds(first, count, stride=stride)


def _load_rows(ref, rows):
    return jnp.concatenate([ref[j, rows, :] for j in range(ref.shape[0])], axis=1)


def _store_rows(ref, rows, val):
    for j in range(ref.shape[0]):
        ref[j, rows, :] = val[:, j * 128:(j + 1) * 128]


def _dilated_kernel(q_ref, k_ref, v_ref, hmask_ref, hmask_f32_ref, o_ref, acc_ref, m_ref, l_ref):
    seq = q_ref.shape[1]
    for branch, (window, dil) in enumerate(DILATED_CFG):
        n_side = window // (2 * dil)
        sub_len = seq // dil
        qb = min(DIL_QB, sub_len)
        kb = min(DIL_KB, sub_len)
        rel = (lax.broadcasted_iota(jnp.int32, (N_HEADS * qb, kb), 1)
               - (lax.broadcasted_iota(jnp.int32, (N_HEADS * qb, kb), 0) & (qb - 1)))

        def scores(r, c, dil=dil, n_side=n_side, sub_len=sub_len, qb=qb, kb=kb, rel=rel):
            l0 = c * qb
            kl0 = jnp.clip(l0 - n_side, 0, sub_len - kb)
            if dil == 1:
                l0, kl0 = pl.multiple_of(l0, qb), pl.multiple_of(kl0, n_side)
            q_rows = _strided_rows(r + dil * l0, qb, dil)
            k_rows = _strided_rows(r + dil * kl0, kb, dil)
            q = _load_rows(q_ref, q_rows).astype(BF16)
            qs = jnp.concatenate([q * hmask_ref[h] for h in range(N_HEADS)], axis=0)
            s = _dot_nt(qs, _load_rows(k_ref, k_rows).astype(BF16))
            return jnp.where(jnp.abs(rel + (kl0 - l0)) <= n_side, s, NEG_INF), q_rows, k_rows

        def attend(s, q_rows, k_rows, qb=qb, branch=branch):
            m = jnp.max(s, axis=-1, keepdims=True)
            e = jnp.exp(s - m)
            l = jnp.sum(e, axis=-1, keepdims=True)
            pv = _dot(e.astype(BF16), _load_rows(v_ref, k_rows).astype(BF16))

            def unstack(a):
                out = a[(N_HEADS - 1) * qb:N_HEADS * qb]
                for h in range(N_HEADS - 2, -1, -1):
                    out = jnp.where(hmask_f32_ref[h] > 0.5, a[h * qb:(h + 1) * qb], out)
                return out

            acc_new, m_new, l_new = unstack(pv), unstack(m), unstack(l)
            if branch == 0:
                _store_rows(acc_ref, q_rows, acc_new)
                _store_rows(m_ref, q_rows, m_new)
                _store_rows(l_ref, q_rows, l_new)
            else:
                m_old = _load_rows(m_ref, q_rows)
                m_max = jnp.maximum(m_old, m_new)
                w_old = jnp.exp(m_old - m_max)
                w_new = jnp.exp(m_new - m_max)
                _store_rows(acc_ref, q_rows, _load_rows(acc_ref, q_rows) * w_old + acc_new * w_new)
                _store_rows(l_ref, q_rows, _load_rows(l_ref, q_rows) * w_old + l_new * w_new)
                _store_rows(m_ref, q_rows, m_max)

        def run(units, scores=scores, attend=attend):
            staged = [scores(r, c) for r, c in units]
            for item in staged:
                attend(*item)

        n_blocks = sub_len // qb
        if n_blocks >= DIL_UNITS_PER_ITER:
            for r in range(dil):
                def block_group(p, carry, r=r, run=run):
                    run([(r, DIL_UNITS_PER_ITER * p + u) for u in range(DIL_UNITS_PER_ITER)])
                    return carry

                lax.fori_loop(0, n_blocks // DIL_UNITS_PER_ITER, block_group, 0)
        else:
            for r0 in range(0, dil, DIL_UNITS_PER_ITER):
                run([(r0 + u, 0) for u in range(DIL_UNITS_PER_ITER)])
    for j in range(acc_ref.shape[0]):
        o_ref[:, j * 128:(j + 1) * 128] = (acc_ref[j] / l_ref[j]).astype(BF16)


def _dilated_attention(qd, kd, vd, batch, seq, masks):
    tiles, t, _ = qd.shape
    gw = tiles * 128
    hmask, hmask_f32, _ = masks
    in_spec = pl.BlockSpec((tiles, seq, 128), lambda b: (0, b, 0))
    stat = pltpu.VMEM((tiles, seq, 128), F32)
    return pl.pallas_call(
        _dilated_kernel,
        grid=(batch,),
        in_specs=[
            in_spec, in_spec, in_spec,
            pl.BlockSpec(hmask.shape, lambda b: (0, 0, 0)),
            pl.BlockSpec(hmask_f32.shape, lambda b: (0, 0, 0)),
        ],
        out_specs=pl.BlockSpec((seq, gw), lambda b: (b, 0)),
        out_shape=jax.ShapeDtypeStruct((t, gw), BF16),
        scratch_shapes=[stat, stat, stat],
        compiler_params=_cparams("arbitrary"),
        name="dilated_attention",
    )(qd, kd, vd, hmask, hmask_f32)


def _outproj_kernel(ya_ref, yb_ref, yc_ref, yd_ref, w_ref, x_ref, mod_ref, g_ref, wr_ref,
                    x1_ref, h2_ref, lg_ref):
    gw = GROUP_WIDTH
    mix = _dot(ya_ref[...], w_ref[0:gw, :])
    mix = mix + _dot(yb_ref[...], w_ref[gw:2 * gw, :])
    mix = mix + _dot(yc_ref[...], w_ref[2 * gw:3 * gw, :])
    mix = mix + _dot(yd_ref[...], w_ref[3 * gw:4 * gw, :])
    mod = mod_ref[0]
    x1 = x_ref[...] + mod[2:3] * mix
    x1_ref[...] = x1
    ms = jnp.mean(x1 * x1, axis=-1, keepdims=True)
    h2 = (x1 * lax.rsqrt(ms + EPS) * g_ref[...]) * (1.0 + mod[4:5]) + mod[3:4]
    h2_ref[...] = _pack_bf16_pairs(h2)
    lg_ref[...] = _dot(h2.astype(BF16), wr_ref[...])


def _output_projection(ys, w_out_bf16, x2d, mod_l, g2, w_router, seq):
    t, d = x2d.shape
    tm = 512
    tiles_per_batch = seq // tm
    gw = GROUP_WIDTH
    row_spec = lambda width: pl.BlockSpec((tm, width), lambda i: (i, 0))
    return pl.pallas_call(
        _outproj_kernel,
        grid=(t // tm,),
        in_specs=[
            row_spec(gw), row_spec(gw), row_spec(gw), row_spec(gw),
            pl.BlockSpec((d, d), lambda i: (0, 0)),
            row_spec(d),
            pl.BlockSpec((1, 6, d), lambda i: (i // tiles_per_batch, 0, 0)),
            pl.BlockSpec((1, d), lambda i: (0, 0)),
            pl.BlockSpec((d, ROUTER_LANES), lambda i: (0, 0)),
        ],
        out_specs=[row_spec(d), row_spec(d // 2), row_spec(ROUTER_LANES)],
        out_shape=[jax.ShapeDtypeStruct((t, d), F32), jax.ShapeDtypeStruct((t, d // 2), jnp.uint32),
                   jax.ShapeDtypeStruct((t, ROUTER_LANES), F32)],
        compiler_params=_cparams("arbitrary"),
        name="output_projection",
    )(*ys, w_out_bf16, x2d, mod_l, g2.reshape(1, d), w_router)


def _routing_kernel(lg_ref, info_ref, cnt_ref, carry_ref):
    tr = lg_ref.shape[0]

    @pl.when(pl.program_id(0) == 0)
    def _():
        carry_ref[...] = jnp.zeros_like(carry_ref)

    lg = lg_ref[...]
    lane = lax.broadcasted_iota(jnp.int32, lg.shape, 1).astype(F32)
    big = float(ROUTER_LANES)
    glog = jnp.where(lane < N_GROUPS, lg, -jnp.inf)
    gmax = jnp.max(glog, axis=-1, keepdims=True)
    p_grp = 1.0 / jnp.sum(jnp.exp(glog - gmax), axis=-1, keepdims=True)
    grp = jnp.min(jnp.where(glog == gmax, lane, big), axis=-1, keepdims=True)
    lo = N_GROUPS + EXPERTS_PER_GROUP * grp
    elog = jnp.where((lane >= lo) & (lane < lo + EXPERTS_PER_GROUP), lg, -jnp.inf)
    v1 = jnp.max(elog, axis=-1, keepdims=True)
    i1 = jnp.min(jnp.where(elog == v1, lane, big), axis=-1, keepdims=True)
    elog2 = jnp.where(lane == i1, -jnp.inf, elog)
    v2 = jnp.max(elog2, axis=-1, keepdims=True)
    i2 = jnp.min(jnp.where(elog2 == v2, lane, big), axis=-1, keepdims=True)
    d = jnp.exp(v2 - v1)
    gate1 = p_grp / (1.0 + d)
    gate2 = p_grp * d / (1.0 + d)
    sel1 = lane == i1
    sel2 = lane == i2
    sel = jnp.where(sel1 | sel2, 1.0, 0.0)
    row = lax.broadcasted_iota(jnp.int32, (tr, tr), 0)
    col = lax.broadcasted_iota(jnp.int32, (tr, tr), 1)
    before = jnp.where(col < row, 1.0, 0.0).astype(BF16)
    rank = _dot(before, sel.astype(BF16)) + carry_ref[...]
    r1 = jnp.sum(jnp.where(sel1, rank, 0.0), axis=-1, keepdims=True)
    r2 = jnp.sum(jnp.where(sel2, rank, 0.0), axis=-1, keepdims=True)
    carry_ref[...] += jnp.sum(sel, axis=0, keepdims=True)
    cnt_ref[...] = carry_ref[...]
    info = jnp.zeros_like(lg)
    for idx, val in enumerate((i1 - N_GROUPS, i2 - N_GROUPS, r1, r2, gate1, gate2)):
        info = jnp.where(lane == idx, val, info)
    info_ref[...] = info


def _routing(logits):
    t = logits.shape[0]
    tr = 512
    return pl.pallas_call(
        _routing_kernel,
        grid=(t // tr,),
        in_specs=[pl.BlockSpec((tr, ROUTER_LANES), lambda i: (i, 0))],
        out_specs=[pl.BlockSpec((tr, ROUTER_LANES), lambda i: (i, 0)),
                   pl.BlockSpec((1, ROUTER_LANES), lambda i: (0, 0))],
        out_shape=[jax.ShapeDtypeStruct((t, ROUTER_LANES), F32),
                   jax.ShapeDtypeStruct((1, ROUTER_LANES), F32)],
        scratch_shapes=[pltpu.VMEM((1, ROUTER_LANES), F32)],
        compiler_params=_cparams("arbitrary"),
        name="routing",
    )(logits)


def _row_copy(src_ref, src_row, dst_ref, dst_row, sem):
    return pltpu.make_async_copy(src_ref.at[pl.ds(src_row, 1)], dst_ref.at[pl.ds(dst_row, 1)], sem)


def _block_rows(ref, block):
    return ref.at[pl.ds(pl.multiple_of(block * MOE_BLOCK, MOE_BLOCK), MOE_BLOCK)]


def _dispatch_kernel(d1_ref, d2_ref, pad_end_ref, count_ref, h_ref, xs_ref, zero_ref, sem, zero_sem):
    td = h_ref.shape[0]
    base = pl.program_id(0) * td

    @pl.when(pl.program_id(0) == 0)
    def _():
        zero_ref[...] = jnp.zeros_like(zero_ref)

        def tail_copy(e):
            return pltpu.make_async_copy(zero_ref, _block_rows(xs_ref, pad_end_ref[e] // MOE_BLOCK - 1), zero_sem)

        for e in range(N_EXPERTS):
            @pl.when(count_ref[e] > 0)
            def _():
                tail_copy(e).start()

        for e in range(N_EXPERTS):
            @pl.when(count_ref[e] > 0)
            def _():
                tail_copy(e).wait()

        def unused_copy(j):
            return pltpu.make_async_copy(zero_ref, _block_rows(xs_ref, j), zero_sem)

        first_unused = pad_end_ref[N_EXPERTS - 1] // MOE_BLOCK
        n_blocks = xs_ref.shape[0] // MOE_BLOCK

        def start_unused(j, carry):
            unused_copy(j).start()
            return carry

        def wait_unused(j, carry):
            unused_copy(j).wait()
            return carry

        lax.fori_loop(first_unused, n_blocks, start_unused, 0)
        lax.fori_loop(first_unused, n_blocks, wait_unused, 0)

    for r in range(td):
        for slot, dref in enumerate((d1_ref, d2_ref)):
            _row_copy(h_ref, r, xs_ref, dref[base + r], sem).start(priority=slot)
    for r in range(td):
        for _ in range(2):
            _row_copy(h_ref, r, xs_ref, 0, sem).wait()


def _dispatch(h2, dest1, dest2, pad_end, counts, n_rows):
    t, d = h2.shape
    td = 256
    grid_spec = pltpu.PrefetchScalarGridSpec(
        num_scalar_prefetch=4,
        grid=(t // td,),
        in_specs=[pl.BlockSpec((td, d), lambda i, *_: (i, 0))],
        out_specs=pl.BlockSpec(memory_space=pl.ANY),
        scratch_shapes=[pltpu.VMEM((MOE_BLOCK, d), h2.dtype), pltpu.SemaphoreType.DMA(()),
                        pltpu.SemaphoreType.DMA(())],
    )
    return pl.pallas_call(
        _dispatch_kernel,
        grid_spec=grid_spec,
        out_shape=jax.ShapeDtypeStruct((n_rows, d), h2.dtype),
        compiler_params=_cparams("arbitrary"),
        name="moe_dispatch",
    )(dest1, dest2, pad_end, counts, h2)


def _expert_kernel(blk_e_ref, nvalid_ref, xs_ref, wg_ref, wu_ref, wd_ref, ys_ref, wg_bf, wu_bf, wd_bf):
    j = pl.program_id(0)

    @pl.when((j == 0) | (blk_e_ref[j] != blk_e_ref[jnp.maximum(j - 1, 0)]))
    def _():
        wg_bf[...] = wg_ref[0, 0].astype(BF16)
        wu_bf[...] = wu_ref[0, 0].astype(BF16)
        wd_bf[...] = wd_ref[0, 0].astype(BF16)

    @pl.when(j < nvalid_ref[0])
    def _():
        xb = _unpack_bf16_pairs(xs_ref[...]).astype(BF16)
        gate = _dot(xb, wg_bf[...])
        up = _dot(xb, wu_bf[...])
        hdn = (gate * jax.nn.sigmoid(gate)) * up
        ys_ref[...] = _pack_bf16_pairs(_dot(hdn.astype(BF16), wd_bf[...]))

    @pl.when(j >= nvalid_ref[0])
    def _():
        ys_ref[...] = jnp.zeros_like(ys_ref)


def _expert_mlp(xs, blk_e, nvalid, w_gate, w_up, w_down, layer):
    n_rows, half = xs.shape
    nblk = n_rows // MOE_BLOCK
    d, de = w_gate.shape[2:]

    def x_map(j, be, nv):
        return (jnp.minimum(j, nv[0] - 1), 0)

    w_map = lambda j, be, nv: (layer, be[j], 0, 0)
    grid_spec = pltpu.PrefetchScalarGridSpec(
        num_scalar_prefetch=2,
        grid=(nblk,),
        in_specs=[
            pl.BlockSpec((MOE_BLOCK, half), x_map),
            pl.BlockSpec((1, 1, d, de), w_map),
            pl.BlockSpec((1, 1, d, de), w_map),
            pl.BlockSpec((1, 1, de, d), w_map),
        ],
        out_specs=pl.BlockSpec((MOE_BLOCK, half), lambda j, be, nv: (j, 0)),
        scratch_shapes=[pltpu.VMEM((d, de), BF16), pltpu.VMEM((d, de), BF16), pltpu.VMEM((de, d), BF16)],
    )
    return pl.pallas_call(
        _expert_kernel,
        grid_spec=grid_spec,
        out_shape=jax.ShapeDtypeStruct((n_rows, half), jnp.uint32),
        compiler_params=_cparams("arbitrary"),
        name="expert_mlp",
    )(blk_e, nvalid, xs, w_gate, w_up, w_down)


def _combine_kernel(final_norm, d1_ref, d2_ref, x_ref, mod_ref, info_ref, gf_ref, ys_ref, o_ref, buf_ref, sem):
    tc = x_ref.shape[0]
    base = pl.program_id(0) * tc

    for r in range(tc):
        for slot, dref in enumerate((d1_ref, d2_ref)):
            _row_copy(ys_ref, dref[base + r], buf_ref.at[slot], r, sem).start(priority=slot)
    for r in range(tc):
        for slot in range(2):
            _row_copy(ys_ref, 0, buf_ref.at[slot], r, sem).wait()
    info = info_ref[...]
    moe = info[:, 4:5] * _unpack_bf16_pairs(buf_ref[0]) + info[:, 5:6] * _unpack_bf16_pairs(buf_ref[1])
    x2 = x_ref[...] + mod_ref[0][5:6] * moe
    if final_norm:
        ms = jnp.mean(x2 * x2, axis=-1, keepdims=True)
        x2 = x2 * lax.rsqrt(ms + EPS) * gf_ref[...]
    o_ref[...] = x2


def _combine(x1, mod_l, info, ys, dest1, dest2, g_final, seq, final_norm):
    t, d = x1.shape
    tc = 256
    tiles_per_batch = seq // tc
    grid_spec = pltpu.PrefetchScalarGridSpec(
        num_scalar_prefetch=2,
        grid=(t // tc,),
        in_specs=[
            pl.BlockSpec((tc, d), lambda i, *_: (i, 0)),
            pl.BlockSpec((1, 6, d), lambda i, *_: (i // tiles_per_batch, 0, 0)),
            pl.BlockSpec((tc, ROUTER_LANES), lambda i, *_: (i, 0)),
            pl.BlockSpec((1, d), lambda i, *_: (0, 0)),
            pl.BlockSpec(memory_space=pl.ANY),
        ],
        out_specs=pl.BlockSpec((tc, d), lambda i, *_: (i, 0)),
        scratch_shapes=[pltpu.VMEM((2, tc, d // 2), jnp.uint32), pltpu.SemaphoreType.DMA(())],
    )
    return pl.pallas_call(
        functools.partial(_combine_kernel, final_norm),
        grid_spec=grid_spec,
        out_shape=jax.ShapeDtypeStruct((t, d), F32),
        compiler_params=_cparams("arbitrary"),
        name="moe_combine",
    )(dest1, dest2, x1, mod_l, info, g_final.reshape(1, d), ys)


def _router_weights(w_rg, w_re):
    d = w_rg.shape[0]
    w_experts = jnp.transpose(w_re, (1, 0, 2)).reshape(d, N_EXPERTS)
    pad = jnp.zeros((d, ROUTER_LANES - N_GROUPS - N_EXPERTS), F32)
    return jnp.concatenate([w_rg, w_experts, pad], axis=1)


def _block_layout(counts_row, n_blocks):
    counts = counts_row[0, N_GROUPS:N_GROUPS + N_EXPERTS].astype(jnp.int32)
    padded = (counts + MOE_BLOCK - 1) // MOE_BLOCK * MOE_BLOCK
    pad_end = jnp.cumsum(padded)
    pad_start = pad_end - padded
    starts = jnp.arange(n_blocks, dtype=jnp.int32) * MOE_BLOCK
    blk_e = jnp.minimum(jnp.sum((pad_end[None, :] <= starts[:, None]).astype(jnp.int32), axis=1), N_EXPERTS - 1)
    nvalid = (pad_end[-1:] // MOE_BLOCK).astype(jnp.int32)
    return counts, pad_start, pad_end, blk_e, nvalid


def _destinations(info, pad_start):
    ids = info[:, 0:4].astype(jnp.int32)
    experts = jnp.arange(N_EXPERTS, dtype=jnp.int32)[None, :]

    def segment_start(e):
        return jnp.sum(jnp.where(e[:, None] == experts, pad_start[None, :], 0), axis=1)

    return segment_start(ids[:, 0]) + ids[:, 2], segment_start(ids[:, 1]) + ids[:, 3]


def kernel(x, c, w_ada, b_ada, g_norm1, g_norm2, w_in, diff_lambda, diff_subln, na_rpb, conv_dw, conv_b,
           conv_norm_g, conv_norm_b, w_out, w_router_group, w_router_expert, w_exp_gate, w_exp_up,
           w_exp_down, g_final):
    batch, seq, d = x.shape
    depth = w_ada.shape[0]
    t = batch * seq
    assert d == D_MODEL and seq == GRID_ROWS * GRID_W
    n_rows = t * 2 + N_EXPERTS * MOE_BLOCK

    mod = _ada_modulation(c, w_ada, b_ada).reshape(depth, batch, 6, d)
    rope_a = _rope_tables(seq, DIFF_DH)
    rope_d = _rope_tables(seq, HEAD_DIM)
    masks = _head_masks()
    gmat = _block_diag_ones(GROUP_WIDTH, HEAD_DIM)

    x2d = x.reshape(t, d)
    for l in range(depth):
        mod_l = mod[l]
        qa, ka, va, qb, kb, vb, pc, qd, kd, vd = _input_projection(
            x2d, mod_l, g_norm1[l], w_in[l].astype(BF16), rope_a, rope_d, seq)
        ya = _diff_attention(qa, ka, va, diff_lambda[l], diff_subln[l], l, batch, seq, masks, gmat)
        yb = _neighborhood_attention(qb, kb, vb, _na_bias_table(na_rpb[l]), batch, seq, masks)
        yc = _conformer_conv(pc, conv_dw[l], conv_b[l], conv_norm_g[l], conv_norm_b[l], batch, seq, gmat)
        yd = _dilated_attention(qd, kd, vd, batch, seq, masks)
        x1, h2, logits = _output_projection(
            (ya, yb, yc, yd), w_out[l].astype(BF16), x2d, mod_l, g_norm2[l],
            _router_weights(w_router_group[l], w_router_expert[l]).astype(BF16), seq)
        info, counts = _routing(logits)
        counts, pad_start, pad_end, blk_e, nvalid = _block_layout(counts, n_rows // MOE_BLOCK)
        dest1, dest2 = _destinations(info, pad_start)
        xs = _dispatch(h2, dest1, dest2, pad_end, counts, n_rows)
        ys = _expert_mlp(xs, blk_e, nvalid, w_exp_gate, w_exp_up, w_exp_down, l)
        x2d = _combine(x1, mod_l, info, ys, dest1, dest2, g_final, seq, final_norm=(l == depth - 1))
    return x2d.reshape(batch, seq, d)
```

```python
import functools
import math

import numpy as np
import jax
import jax.numpy as jnp
from jax import lax
from jax.experimental import pallas as pl
from jax.experimental.pallas import tpu as pltpu

F32 = jnp.float32
BF16 = jnp.bfloat16

D_MODEL = 1024
GROUP_WIDTH = 256
HEAD_DIM = 64
N_HEADS = 4
DIFF_DH = 32
CONV_K = 31
CONV_GROUP_CH = 64
GRID_W = 64
NA_KH = 8
NA_KW = 16
GRID_ROWS = 32
NA_ROWS_PER_ITER = 4
ROPE_THETA = 10000.0
N_GROUPS = 4
EXPERTS_PER_GROUP = 8
N_EXPERTS = 32
D_EXPERT = 512
MOE_BLOCK = 256
EPS = 1e-6
NEG_INF = -1e30
LOG2E = 1.4426950408889634
ROUTER_LANES = 128
DILATED_CFG = ((128, 1), (512, 4), (2048, 16))
DIFF_EXP2_SCALE = (DIFF_DH ** -0.5) * LOG2E

VMEM_LIMIT = 56 * 1024 * 1024


def _cparams(*sem):
    return pltpu.CompilerParams(dimension_semantics=sem, vmem_limit_bytes=VMEM_LIMIT)


def _dot(a, b):
    return jnp.dot(a, b, preferred_element_type=F32)


def _dot_nt(a, b):
    return lax.dot_general(a, b, (((1,), (1,)), ((), ())), preferred_element_type=F32)


def _split(a):
    hi = a.astype(BF16)
    lo = (a - hi.astype(F32)).astype(BF16)
    return hi, lo


def _dot3(a, b):
    ah, al = _split(a)
    bh, bl = _split(b)
    return _dot(ah, bh) + (_dot(ah, bl) + _dot(al, bh))


def _group_mean(v, gmat, width):
    hi, lo = _split(v)
    return (_dot(hi, gmat) + _dot(lo, gmat)) * (1.0 / width)


HIGH_HALF = 0xFFFF0000


def _pack_bf16_pairs(a):
    n = a.shape[1] // 2
    bits = lax.bitcast_convert_type(a.astype(BF16).astype(F32), jnp.uint32)
    return (bits[:, :n] >> 16) | (bits[:, n:] & jnp.uint32(HIGH_HALF))


def _unpack_bf16_pairs(u):
    lo = lax.bitcast_convert_type(u << 16, F32)
    hi = lax.bitcast_convert_type(u & jnp.uint32(HIGH_HALF), F32)
    return jnp.concatenate([lo, hi], axis=1)


def _block_diag_ones(n, width):
    idx = np.arange(n) // width
    return jnp.asarray((idx[:, None] == idx[None, :]).astype(np.float32), dtype=BF16)


def _ada_kernel(c_ref, w_ref, b_ref, o_ref):
    c = c_ref[...]
    ca = c * jax.nn.sigmoid(c)
    o_ref[0] = _dot3(ca, w_ref[0]) + b_ref[0]


def _ada_modulation(c, w_ada, b_ada):
    depth, d, n = w_ada.shape
    b = c.shape[0]
    bn = 1024
    return pl.pallas_call(
        _ada_kernel,
        grid=(depth, n // bn),
        in_specs=[
            pl.BlockSpec((b, d), lambda l, j: (0, 0)),
            pl.BlockSpec((1, d, bn), lambda l, j: (l, 0, j)),
            pl.BlockSpec((1, 1, bn), lambda l, j: (l, 0, j)),
        ],
        out_specs=pl.BlockSpec((1, b, bn), lambda l, j: (l, 0, j)),
        out_shape=jax.ShapeDtypeStruct((depth, b, n), F32),
        compiler_params=_cparams("arbitrary", "arbitrary"),
        name="ada_modulation",
    )(c, w_ada, b_ada.reshape(depth, 1, n))


def _rope_tables(seq, dim):
    half = dim // 2
    inv = ROPE_THETA ** (-jnp.arange(0, dim, 2, dtype=F32) / dim)
    ang = jnp.arange(seq, dtype=F32)[:, None] * inv[None, :]
    cos, sin = jnp.cos(ang), jnp.sin(ang)
    reps = 128 // dim
    zeros = jnp.zeros_like(sin)
    cos_t = jnp.tile(jnp.concatenate([cos, cos], axis=1), (1, reps))
    sin_hi = jnp.tile(jnp.concatenate([zeros, sin], axis=1), (1, reps))
    sin_lo = jnp.tile(jnp.concatenate([-sin, zeros], axis=1), (1, reps))
    return cos_t, sin_hi, sin_lo


def _rotary(v, cos_t, sin_hi, sin_lo, half):
    outs = []
    for j in range(v.shape[1] // 128):
        vj = v[:, j * 128:(j + 1) * 128]
        outs.append(vj * cos_t + pltpu.roll(vj, half, 1) * sin_hi + pltpu.roll(vj, 128 - half, 1) * sin_lo)
    return jnp.concatenate(outs, axis=1)


def _inproj_kernel(x_ref, *rest):
    _project_tokens(x_ref[...], *rest)


def _combine_inproj_kernel(d1_ref, d2_ref, x1_ref, modp_ref, info_ref, ys_ref, *rest):
    proj_refs, x2_ref, out_refs, buf_ref, sem = rest[:9], rest[9], rest[10:20], rest[20], rest[21]
    tm = x1_ref.shape[0]
    i = pl.program_id(0)
    last = pl.num_programs(0) - 1
    slot = i & 1

    def gather(src_row, slot_, r, k):
        return pltpu.make_async_copy(ys_ref.at[pl.ds(src_row, 1)], buf_ref.at[slot_, k, pl.ds(r, 1)],
                                     sem.at[slot_])

    def start_rows(tile, slot_, rows):
        for r in rows:
            for k, dref in enumerate((d1_ref, d2_ref)):
                gather(dref[tile * tm + r], slot_, r, k).start(priority=k)

    def wait_tile(slot_):
        for r in range(tm):
            for k in range(2):
                gather(0, slot_, r, k).wait()

    @pl.when(i == 0)
    def _():
        start_rows(0, 0, range(tm))

    wait_tile(slot)
    info = info_ref[...]
    moe = (info[:, 4:5] * _unpack_bf16_pairs(buf_ref[slot, 0])
           + info[:, 5:6] * _unpack_bf16_pairs(buf_ref[slot, 1]))
    x = x1_ref[...] + modp_ref[0][5:6] * moe
    x2_ref[...] = x
    nxt = jnp.minimum(i + 1, last)
    per_group = -(-tm // N_PROJ_GROUPS)

    def between(group):
        start_rows(nxt, 1 - slot, range(group * per_group, min((group + 1) * per_group, tm)))

    _project_tokens(x, *proj_refs, *out_refs, between=between)

    @pl.when(i == last)
    def _():
        wait_tile(1 - slot)


N_PROJ_GROUPS = 11


def _project_tokens(x, mod_ref, g_ref, w_ref, ca_ref, sha_ref, sla_ref, cd_ref, shd_ref, sld_ref,
                    qa_ref, ka_ref, va_ref, qb_ref, kb_ref, vb_ref, pc_ref, qd_ref, kd_ref, vd_ref,
                    between=lambda group: None):
    ms = jnp.mean(x * x, axis=-1, keepdims=True)
    y = x * lax.rsqrt(ms + EPS)
    mod = mod_ref[0]
    h = (y * g_ref[...]) * (1.0 + mod[1:2]) + mod[0:1]
    hb = h.astype(BF16)
    gw = GROUP_WIDTH

    def proj(col):
        return _dot(hb, w_ref[:, col * gw:(col + 1) * gw])

    rot_a = functools.partial(_rotary, cos_t=ca_ref[...], sin_hi=sha_ref[...], sin_lo=sla_ref[...],
                              half=DIFF_DH // 2)
    rot_d = functools.partial(_rotary, cos_t=cd_ref[...], sin_hi=shd_ref[...], sin_lo=sld_ref[...],
                              half=HEAD_DIM // 2)
    na_scale = HEAD_DIM ** -0.5
    qa_ref[...] = (rot_a(proj(0)) * DIFF_EXP2_SCALE).astype(BF16)
    between(0)
    ka_ref[...] = rot_a(proj(1)).astype(BF16)
    between(1)
    va_ref[...] = proj(2).astype(BF16)
    between(2)
    qb_ref[...] = (proj(3) * na_scale).astype(BF16)
    between(3)
    kb_ref[...] = proj(4).astype(BF16)
    between(4)
    vb_ref[...] = proj(5).astype(BF16)
    between(5)
    pc_ref[:, 0:gw] = proj(6)
    between(6)
    pc_ref[:, gw:2 * gw] = proj(7)
    between(7)
    for group, (ref, val) in enumerate(((qd_ref, lambda: rot_d(proj(8)) * na_scale), (kd_ref, lambda: rot_d(proj(9))),
                                        (vd_ref, lambda: proj(10))), start=8):
        val = val()
        for j in range(gw // 128):
            ref[j] = val[:, j * 128:(j + 1) * 128]
        between(group)


def _input_projection(x2d, mod_l, g1, w_in_bf16, rope_a, rope_d, seq, pending=None):
    t, d = x2d.shape if pending is None else pending[0].shape
    tm = 512
    tiles_per_batch = seq // tm
    p_in = w_in_bf16.shape[1]
    gw = GROUP_WIDTH
    row_spec = lambda width: pl.BlockSpec((tm, width), lambda i, *_: (i, 0))
    mod_spec = pl.BlockSpec((1, 6, d), lambda i, *_: (i // tiles_per_batch, 0, 0))
    tab_spec = pl.BlockSpec((tm, 128), lambda i, *_: (i % tiles_per_batch, 0))
    out_shapes = []
    out_specs = []
    for name in ("qa", "ka", "va", "qb", "kb", "vb", "pc", "qd", "kd", "vd"):
        if name == "pc":
            out_shapes.append(jax.ShapeDtypeStruct((t, 2 * gw), F32))
            out_specs.append(row_spec(2 * gw))
        elif name[1] == "d":
            out_shapes.append(jax.ShapeDtypeStruct((gw // 128, t, 128), F32))
            out_specs.append(pl.BlockSpec((gw // 128, tm, 128), lambda i, *_: (0, i, 0)))
        else:
            out_shapes.append(jax.ShapeDtypeStruct((t, gw), BF16))
            out_specs.append(row_spec(gw))
    proj_specs = [
        mod_spec,
        pl.BlockSpec((1, d), lambda i, *_: (0, 0)),
        pl.BlockSpec((d, p_in), lambda i, *_: (0, 0)),
        tab_spec, tab_spec, tab_spec, tab_spec, tab_spec, tab_spec,
    ]
    proj_args = (mod_l, g1.reshape(1, d), w_in_bf16, *rope_a, *rope_d)
    if pending is None:
        return pl.pallas_call(
            _inproj_kernel,
            grid=(t // tm,),
            in_specs=[row_spec(d)] + proj_specs,
            out_specs=out_specs,
            out_shape=out_shapes,
            compiler_params=_cparams("arbitrary"),
            name="input_projection",
        )(x2d, *proj_args)
    x1, mod_prev, info, ys, dest1, dest2 = pending
    grid_spec = pltpu.PrefetchScalarGridSpec(
        num_scalar_prefetch=2,
        grid=(t // tm,),
        in_specs=[row_spec(d), mod_spec, row_spec(ROUTER_LANES), pl.BlockSpec(memory_space=pl.ANY)] + proj_specs,
        out_specs=[row_spec(d)] + out_specs,
        scratch_shapes=[pltpu.VMEM((2, 2, tm, ys.shape[1]), ys.dtype), pltpu.SemaphoreType.DMA((2,))],
    )
    return pl.pallas_call(
        _combine_inproj_kernel,
        grid_spec=grid_spec,
        out_shape=[jax.ShapeDtypeStruct((t, d), F32)] + out_shapes,
        compiler_params=_cparams("arbitrary"),
        name="combine_input_projection",
    )(dest1, dest2, x1, mod_prev, info, ys, *proj_args)


def _head_masks():
    lane = np.arange(GROUP_WIDTH)
    head = np.stack([(lane // HEAD_DIM == h) for h in range(N_HEADS)]).astype(np.float32)
    diff = np.stack([(lane // DIFF_DH == j) for j in range(2 * N_HEADS)]).astype(np.float32)
    return (jnp.asarray(head[:, None, :], dtype=BF16), jnp.asarray(head[:, None, :], dtype=F32),
            jnp.asarray(diff[:, None, :], dtype=BF16))


DIFF_ONES_ROWS = 16


def _diff_attn_kernel(lam_init, q_ref, k_ref, v_ref, lp_ref, g_ref, dmask_ref, gmat_ref, o_ref, vt_ref, ot_ref):
    seq = k_ref.shape[0]

    @pl.when(pl.program_id(1) == 0)
    def _():
        vt = jnp.transpose(v_ref[...].astype(F32))
        for h in range(N_HEADS):
            vt_ref[h, 0:HEAD_DIM, :] = vt[h * HEAD_DIM:(h + 1) * HEAD_DIM].astype(BF16)
            vt_ref[h, HEAD_DIM:HEAD_DIM + DIFF_ONES_ROWS, :] = jnp.ones((DIFF_ONES_ROWS, seq), BF16)

    q = q_ref[...]
    k = k_ref[...]
    lp = lp_ref[...]
    lam = (jnp.exp(jnp.sum(lp[0:1] * lp[1:2], axis=-1, keepdims=True))
           - jnp.exp(jnp.sum(lp[2:3] * lp[3:4], axis=-1, keepdims=True)) + lam_init)

    def scores(j):
        return _dot_nt(k, q * dmask_ref[j])

    def weights(st):
        return jnp.exp2(st - jnp.max(st, axis=0, keepdims=True)).astype(BF16)

    def attend(j, e):
        num = _dot(vt_ref[j // 2], e)
        return num[0:HEAD_DIM] / num[HEAD_DIM:HEAD_DIM + 1]

    n_pairs = 2 * N_HEADS
    outs = [None] * n_pairs
    st_next = scores(0)
    e_prev = None
    for j in range(n_pairs):
        st = st_next
        if j + 1 < n_pairs:
            st_next = scores(j + 1)
        e = weights(st)
        if e_prev is not None:
            outs[j - 1] = attend(j - 1, e_prev)
        e_prev = e
    outs[n_pairs - 1] = attend(n_pairs - 1, e_prev)
    for h in range(N_HEADS):
        ot_ref[h * HEAD_DIM:(h + 1) * HEAD_DIM, :] = outs[2 * h] - lam * outs[2 * h + 1]
    o = jnp.transpose(ot_ref[...])
    ms = _group_mean(o * o, gmat_ref[...], HEAD_DIM)
    o_ref[...] = ((o * lax.rsqrt(ms + EPS) * g_ref[...]) * (1.0 - lam_init)).astype(BF16)


def _diff_attention(qa, ka, va, lam_params, subln_g, layer_idx, batch, seq, masks, gmat):
    t, gw = qa.shape
    tq = 512
    nq = seq // tq
    lam_init = 0.8 - 0.6 * math.exp(-0.3 * layer_idx)
    _, _, dmask = masks
    g_tiled = jnp.tile(subln_g, N_HEADS).reshape(1, gw)
    kv_spec = pl.BlockSpec((seq, gw), lambda b, i: (b, 0))
    full = lambda shape: pl.BlockSpec(shape, lambda b, i: (0,) * len(shape))
    return pl.pallas_call(
        functools.partial(_diff_attn_kernel, lam_init),
        grid=(batch, nq),
        in_specs=[
            pl.BlockSpec((tq, gw), lambda b, i: (b * nq + i, 0)),
            kv_spec, kv_spec,
            full(lam_params.shape), full((1, gw)), full(dmask.shape), full(gmat.shape),
        ],
        out_specs=pl.BlockSpec((tq, gw), lambda b, i: (b * nq + i, 0)),
        out_shape=jax.ShapeDtypeStruct((t, gw), BF16),
        scratch_shapes=[pltpu.VMEM((N_HEADS, HEAD_DIM + DIFF_ONES_ROWS, seq), BF16), pltpu.VMEM((gw, tq), F32)],
        compiler_params=_cparams("arbitrary", "arbitrary"),
        name="diff_attention",
    )(qa, ka, va, lam_params, g_tiled, dmask, gmat)


def _na_bias_table(rpb):
    w = GRID_W
    n_heads = rpb.shape[0]
    cq = np.arange(w)[:, None]
    ck = np.arange(w)[None, :]
    dc = np.clip(ck - cq, -(NA_KW - 1), NA_KW - 1) + NA_KW - 1
    col_start = np.clip(cq - NA_KW // 2, 0, w - NA_KW)
    col_ok = (ck >= col_start) & (ck < col_start + NA_KW)
    onehot = jnp.asarray(dc[:, :, None] == np.arange(2 * NA_KW - 1), dtype=F32)
    toep = jnp.einsum('qkd,hrd->hrqk', onehot, rpb.astype(F32), precision=lax.Precision.HIGHEST)
    toep = jnp.where(col_ok[None, None], toep, NEG_INF)
    tabs = jnp.stack([toep[:, NA_KH - 1 - off:2 * NA_KH - 1 - off] for off in range(NA_KH)])
    return jnp.transpose(tabs, (0, 1, 3, 2, 4)).reshape(NA_KH, n_heads * w, NA_KH * w)


def _na_kernel(q_ref, k_ref, v_ref, tab_ref, hmask_ref, hmask_f32_ref, o_ref):
    w = GRID_W
    nk = NA_KH * w

    def scores(i):
        row_start = jnp.clip(i - NA_KH // 2, 0, GRID_ROWS - NA_KH)
        kstart = pl.multiple_of(row_start * w, w)
        q = q_ref[pl.ds(pl.multiple_of(i * w, w), w), :]
        qs = jnp.concatenate([q * hmask_ref[h] for h in range(N_HEADS)], axis=0)
        return _dot_nt(qs, k_ref[pl.ds(kstart, nk), :]) + tab_ref[i - row_start], kstart

    def attend(i, s, kstart):
        m = jnp.max(s, axis=-1, keepdims=True)
        e = jnp.exp(s - m)
        p = e * (1.0 / jnp.sum(e, axis=-1, keepdims=True))
        pv = _dot(p.astype(BF16), v_ref[pl.ds(kstart, nk), :])
        o = pv[(N_HEADS - 1) * w:N_HEADS * w]
        for h in range(N_HEADS - 2, -1, -1):
            o = jnp.where(hmask_f32_ref[h] > 0.5, pv[h * w:(h + 1) * w], o)
        o_ref[pl.ds(pl.multiple_of(i * w, w), w), :] = o.astype(BF16)

    def grid_rows(p, carry):
        rows = [NA_ROWS_PER_ITER * p + u for u in range(NA_ROWS_PER_ITER)]
        staged = [scores(i) for i in rows]
        for i, (s, kstart) in zip(rows, staged):
            attend(i, s, kstart)
        return carry

    lax.fori_loop(0, GRID_ROWS // NA_ROWS_PER_ITER, grid_rows, 0)


def _neighborhood_attention(qb, kb, vb, bias_table, batch, seq, masks):
    t, gw = qb.shape
    hmask, hmask_f32, _ = masks
    seq_spec = pl.BlockSpec((seq, gw), lambda b: (b, 0))
    full = lambda shape: pl.BlockSpec(shape, lambda b: (0,) * len(shape))
    return pl.pallas_call(
        _na_kernel,
        grid=(batch,),
        in_specs=[seq_spec, seq_spec, seq_spec, full(bias_table.shape), full(hmask.shape), full(hmask_f32.shape)],
        out_specs=seq_spec,
        out_shape=jax.ShapeDtypeStruct((t, gw), BF16),
        compiler_params=_cparams("arbitrary"),
        name="neighborhood_attention",
    )(qb, kb, vb, bias_table, hmask, hmask_f32)


SUBLANES = 8
CONV_PAD = 16
CONV_CHUNK = 128


def _conv_kernel(pc_ref, w_ref, b_ref, gn_ref, bn_ref, gmat_ref, o_ref, zp_ref, zs_ref):
    seq, ch = o_ref.shape
    a = pc_ref[:, 0:ch]
    gate = pc_ref[:, ch:2 * ch]
    zp_ref[0:CONV_PAD, :] = jnp.zeros((CONV_PAD, ch), F32)
    zp_ref[CONV_PAD + seq:2 * CONV_PAD + seq, :] = jnp.zeros((CONV_PAD, ch), F32)
    zp_ref[CONV_PAD:CONV_PAD + seq, :] = a * jax.nn.sigmoid(gate)
    span = seq + 2 * CONV_PAD - SUBLANES
    for b in range(1, SUBLANES):
        zs_ref[b - 1, 0:span, :] = zp_ref[b:b + span, :]
    gmat = gmat_ref[...]
    first = CONV_PAD - CONV_K // 2
    for c in range(seq // CONV_CHUNK):
        r0 = c * CONV_CHUNK
        acc = jnp.zeros((CONV_CHUNK, ch), F32)
        for j in range(CONV_K):
            shift, aligned = (first + j) % SUBLANES, r0 + (first + j) // SUBLANES * SUBLANES
            src = zp_ref if shift == 0 else zs_ref.at[shift - 1]
            acc = acc + w_ref[j] * src[aligned:aligned + CONV_CHUNK, :]
        z = acc + b_ref[...]
        mu = _group_mean(z, gmat, CONV_GROUP_CH)
        dz = z - mu
        var = _group_mean(dz * dz, gmat, CONV_GROUP_CH)
        zn = dz * lax.rsqrt(var + EPS) * gn_ref[...] + bn_ref[...]
        o_ref[r0:r0 + CONV_CHUNK, :] = (zn * jax.nn.sigmoid(zn)).astype(BF16)


def _conformer_conv(pc, w_dw, b_dw, g_n, b_n, batch, seq, gmat):
    t = pc.shape[0]
    ch = GROUP_WIDTH
    full = lambda shape: pl.BlockSpec(shape, lambda b: (0,) * len(shape))
    return pl.pallas_call(
        _conv_kernel,
        grid=(batch,),
        in_specs=[
            pl.BlockSpec((seq, 2 * ch), lambda b: (b, 0)),
            full((CONV_K, 1, ch)), full((1, ch)), full((1, ch)), full((1, ch)), full(gmat.shape),
        ],
        out_specs=pl.BlockSpec((seq, ch), lambda b: (b, 0)),
        out_shape=jax.ShapeDtypeStruct((t, ch), BF16),
        scratch_shapes=[pltpu.VMEM((seq + 2 * CONV_PAD, ch), F32),
                        pltpu.VMEM((SUBLANES - 1, seq + 2 * CONV_PAD, ch), F32)],
        compiler_params=_cparams("arbitrary"),
        name="conformer_conv",
    )(pc, w_dw.reshape(CONV_K, 1, ch), b_dw.reshape(1, ch), g_n.reshape(1, ch), b_n.reshape(1, ch), gmat)


DIL_QB = 128
DIL_KB = 256
DIL_UNITS_PER_ITER = 2


def _strided_rows(first, count, stride):
    return pl.ds(first, count) if stride == 1 else pl.ds(first, count, stride=stride)


def _load_rows(ref, rows):
    return jnp.concatenate([ref[j, rows, :] for j in range(ref.shape[0])], axis=1)


def _store_rows(ref, rows, val):
    for j in range(ref.shape[0]):
        ref[j, rows, :] = val[:, j * 128:(j + 1) * 128]


def _dilated_kernel(q_ref, k_ref, v_ref, hmask_ref, hmask_f32_ref, o_ref, acc_ref, m_ref, l_ref):
    seq = q_ref.shape[1]
    for branch, (window, dil) in enumerate(DILATED_CFG):
        n_side = window // (2 * dil)
        sub_len = seq // dil
        qb = min(DIL_QB, sub_len)
        kb = min(DIL_KB, sub_len)
        rel = (lax.broadcasted_iota(jnp.int32, (N_HEADS * qb, kb), 1)
               - (lax.broadcasted_iota(jnp.int32, (N_HEADS * qb, kb), 0) & (qb - 1)))

        def scores(r, c, dil=dil, n_side=n_side, sub_len=sub_len, qb=qb, kb=kb, rel=rel):
            l0 = c * qb
            kl0 = jnp.clip(l0 - n_side, 0, sub_len - kb)
            if dil == 1:
                l0, kl0 = pl.multiple_of(l0, qb), pl.multiple_of(kl0, n_side)
            q_rows = _strided_rows(r + dil * l0, qb, dil)
            k_rows = _strided_rows(r + dil * kl0, kb, dil)
            q = _load_rows(q_ref, q_rows).astype(BF16)
            qs = jnp.concatenate([q * hmask_ref[h] for h in range(N_HEADS)], axis=0)
            s = _dot_nt(qs, _load_rows(k_ref, k_rows).astype(BF16))
            return jnp.where(jnp.abs(rel + (kl0 - l0)) <= n_side, s, NEG_INF), q_rows, k_rows

        def attend(s, q_rows, k_rows, qb=qb, branch=branch):
            m = jnp.max(s, axis=-1, keepdims=True)
            e = jnp.exp(s - m)
            l = jnp.sum(e, axis=-1, keepdims=True)
            pv = _dot(e.astype(BF16), _load_rows(v_ref, k_rows).astype(BF16))

            def unstack(a):
                out = a[(N_HEADS - 1) * qb:N_HEADS * qb]
                for h in range(N_HEADS - 2, -1, -1):
                    out = jnp.where(hmask_f32_ref[h] > 0.5, a[h * qb:(h + 1) * qb], out)
                return out

            acc_new, m_new, l_new = unstack(pv), unstack(m), unstack(l)
            if branch == 0:
                _store_rows(acc_ref, q_rows, acc_new)
                _store_rows(m_ref, q_rows, m_new)
                _store_rows(l_ref, q_rows, l_new)
            else:
                m_old = _load_rows(m_ref, q_rows)
                m_max = jnp.maximum(m_old, m_new)
                w_old = jnp.exp(m_old - m_max)
                w_new = jnp.exp(m_new - m_max)
                _store_rows(acc_ref, q_rows, _load_rows(acc_ref, q_rows) * w_old + acc_new * w_new)
                _store_rows(l_ref, q_rows, _load_rows(l_ref, q_rows) * w_old + l_new * w_new)
                _store_rows(m_ref, q_rows, m_max)

        def run(units, scores=scores, attend=attend):
            staged = [scores(r, c) for r, c in units]
            for item in staged:
                attend(*item)

        n_blocks = sub_len // qb
        if n_blocks >= DIL_UNITS_PER_ITER:
            for r in range(dil):
                def block_group(p, carry, r=r, run=run):
                    run([(r, DIL_UNITS_PER_ITER * p + u) for u in range(DIL_UNITS_PER_ITER)])
                    return carry

                lax.fori_loop(0, n_blocks // DIL_UNITS_PER_ITER, block_group, 0)
        else:
            for r0 in range(0, dil, DIL_UNITS_PER_ITER):
                run([(r0 + u, 0) for u in range(DIL_UNITS_PER_ITER)])
    for j in range(acc_ref.shape[0]):
        o_ref[:, j * 128:(j + 1) * 128] = (acc_ref[j] / l_ref[j]).astype(BF16)


def _dilated_attention(qd, kd, vd, batch, seq, masks):
    tiles, t, _ = qd.shape
    gw = tiles * 128
    hmask, hmask_f32, _ = masks
    in_spec = pl.BlockSpec((tiles, seq, 128), lambda b: (0, b, 0))
    stat = pltpu.VMEM((tiles, seq, 128), F32)
    return pl.pallas_call(
        _dilated_kernel,
        grid=(batch,),
        in_specs=[
            in_spec, in_spec, in_spec,
            pl.BlockSpec(hmask.shape, lambda b: (0, 0, 0)),
            pl.BlockSpec(hmask_f32.shape, lambda b: (0, 0, 0)),
        ],
        out_specs=pl.BlockSpec((seq, gw), lambda b: (b, 0)),
        out_shape=jax.ShapeDtypeStruct((t, gw), BF16),
        scratch_shapes=[stat, stat, stat],
        compiler_params=_cparams("arbitrary"),
        name="dilated_attention",
    )(qd, kd, vd, hmask, hmask_f32)


def _outproj_kernel(ya_ref, yb_ref, yc_ref, yd_ref, w_ref, x_ref, mod_ref, g_ref, wr_ref,
                    x1_ref, h2_ref, lg_ref):
    gw = GROUP_WIDTH
    mix = _dot(ya_ref[...], w_ref[0:gw, :])
    mix = mix + _dot(yb_ref[...], w_ref[gw:2 * gw, :])
    mix = mix + _dot(yc_ref[...], w_ref[2 * gw:3 * gw, :])
    mix = mix + _dot(yd_ref[...], w_ref[3 * gw:4 * gw, :])
    mod = mod_ref[0]
    x1 = x_ref[...] + mod[2:3] * mix
    x1_ref[...] = x1
    ms = jnp.mean(x1 * x1, axis=-1, keepdims=True)
    h2 = (x1 * lax.rsqrt(ms + EPS) * g_ref[...]) * (1.0 + mod[4:5]) + mod[3:4]
    h2_ref[...] = _pack_bf16_pairs(h2)
    lg_ref[...] = _dot(h2.astype(BF16), wr_ref[...])


def _output_projection(ys, w_out_bf16, x2d, mod_l, g2, w_router, seq):
    t, d = x2d.shape
    tm = 512
    tiles_per_batch = seq // tm
    gw = GROUP_WIDTH
    row_spec = lambda width: pl.BlockSpec((tm, width), lambda i: (i, 0))
    return pl.pallas_call(
        _outproj_kernel,
        grid=(t // tm,),
        in_specs=[
            row_spec(gw), row_spec(gw), row_spec(gw), row_spec(gw),
            pl.BlockSpec((d, d), lambda i: (0, 0)),
            row_spec(d),
            pl.BlockSpec((1, 6, d), lambda i: (i // tiles_per_batch, 0, 0)),
            pl.BlockSpec((1, d), lambda i: (0, 0)),
            pl.BlockSpec((d, ROUTER_LANES), lambda i: (0, 0)),
        ],
        out_specs=[row_spec(d), row_spec(d // 2), row_spec(ROUTER_LANES)],
        out_shape=[jax.ShapeDtypeStruct((t, d), F32), jax.ShapeDtypeStruct((t, d // 2), jnp.uint32),
                   jax.ShapeDtypeStruct((t, ROUTER_LANES), F32)],
        compiler_params=_cparams("arbitrary"),
        name="output_projection",
    )(*ys, w_out_bf16, x2d, mod_l, g2.reshape(1, d), w_router)


def _routing_kernel(lg_ref, info_ref, cnt_ref, carry_ref):
    tr = lg_ref.shape[0]

    @pl.when(pl.program_id(0) == 0)
    def _():
        carry_ref[...] = jnp.zeros_like(carry_ref)

    lg = lg_ref[...]
    lane = lax.broadcasted_iota(jnp.int32, lg.shape, 1).astype(F32)
    big = float(ROUTER_LANES)
    glog = jnp.where(lane < N_GROUPS, lg, -jnp.inf)
    gmax = jnp.max(glog, axis=-1, keepdims=True)
    p_grp = 1.0 / jnp.sum(jnp.exp(glog - gmax), axis=-1, keepdims=True)
    grp = jnp.min(jnp.where(glog == gmax, lane, big), axis=-1, keepdims=True)
    lo = N_GROUPS + EXPERTS_PER_GROUP * grp
    elog = jnp.where((lane >= lo) & (lane < lo + EXPERTS_PER_GROUP), lg, -jnp.inf)
    v1 = jnp.max(elog, axis=-1, keepdims=True)
    i1 = jnp.min(jnp.where(elog == v1, lane, big), axis=-1, keepdims=True)
    elog2 = jnp.where(lane == i1, -jnp.inf, elog)
    v2 = jnp.max(elog2, axis=-1, keepdims=True)
    i2 = jnp.min(jnp.where(elog2 == v2, lane, big), axis=-1, keepdims=True)
    d = jnp.exp(v2 - v1)
    gate1 = p_grp / (1.0 + d)
    gate2 = p_grp * d / (1.0 + d)
    sel1 = lane == i1
    sel2 = lane == i2
    sel = jnp.where(sel1 | sel2, 1.0, 0.0)
    row = lax.broadcasted_iota(jnp.int32, (tr, tr), 0)
    col = lax.broadcasted_iota(jnp.int32, (tr, tr), 1)
    before = jnp.where(col < row, 1.0, 0.0).astype(BF16)
    rank = _dot(before, sel.astype(BF16)) + carry_ref[...]
    r1 = jnp.sum(jnp.where(sel1, rank, 0.0), axis=-1, keepdims=True)
    r2 = jnp.sum(jnp.where(sel2, rank, 0.0), axis=-1, keepdims=True)
    carry_ref[...] += jnp.sum(sel, axis=0, keepdims=True)
    cnt_ref[...] = carry_ref[...]
    info = jnp.zeros_like(lg)
    for idx, val in enumerate((i1 - N_GROUPS, i2 - N_GROUPS, r1, r2, gate1, gate2)):
        info = jnp.where(lane == idx, val, info)
    info_ref[...] = info


def _routing(logits):
    t = logits.shape[0]
    tr = 512
    return pl.pallas_call(
        _routing_kernel,
        grid=(t // tr,),
        in_specs=[pl.BlockSpec((tr, ROUTER_LANES), lambda i: (i, 0))],
        out_specs=[pl.BlockSpec((tr, ROUTER_LANES), lambda i: (i, 0)),
                   pl.BlockSpec((1, ROUTER_LANES), lambda i: (0, 0))],
        out_shape=[jax.ShapeDtypeStruct((t, ROUTER_LANES), F32),
                   jax.ShapeDtypeStruct((1, ROUTER_LANES), F32)],
        scratch_shapes=[pltpu.VMEM((1, ROUTER_LANES), F32)],
        compiler_params=_cparams("arbitrary"),
        name="routing",
    )(logits)


def _row_copy(src_ref, src_row, dst_ref, dst_row, sem):
    return pltpu.make_async_copy(src_ref.at[pl.ds(src_row, 1)], dst_ref.at[pl.ds(dst_row, 1)], sem)


def _block_rows(ref, block):
    return ref.at[pl.ds(pl.multiple_of(block * MOE_BLOCK, MOE_BLOCK), MOE_BLOCK)]


def _dispatch_kernel(d1_ref, d2_ref, pad_end_ref, count_ref, h_ref, xs_ref, zero_ref, sem, zero_sem):
    td = h_ref.shape[0]
    base = pl.program_id(0) * td

    @pl.when(pl.program_id(0) == 0)
    def _():
        zero_ref[...] = jnp.zeros_like(zero_ref)

        def tail_copy(e):
            return pltpu.make_async_copy(zero_ref, _block_rows(xs_ref, pad_end_ref[e] // MOE_BLOCK - 1), zero_sem)

        for e in range(N_EXPERTS):
            @pl.when(count_ref[e] > 0)
            def _():
                tail_copy(e).start()

        for e in range(N_EXPERTS):
            @pl.when(count_ref[e] > 0)
            def _():
                tail_copy(e).wait()

        def unused_copy(j):
            return pltpu.make_async_copy(zero_ref, _block_rows(xs_ref, j), zero_sem)

        first_unused = pad_end_ref[N_EXPERTS - 1] // MOE_BLOCK
        n_blocks = xs_ref.shape[0] // MOE_BLOCK

        def start_unused(j, carry):
            unused_copy(j).start()
            return carry

        def wait_unused(j, carry):
            unused_copy(j).wait()
            return carry

        lax.fori_loop(first_unused, n_blocks, start_unused, 0)
        lax.fori_loop(first_unused, n_blocks, wait_unused, 0)

    for r in range(td):
        for slot, dref in enumerate((d1_ref, d2_ref)):
            _row_copy(h_ref, r, xs_ref, dref[base + r], sem).start(priority=slot)
    for r in range(td):
        for _ in range(2):
            _row_copy(h_ref, r, xs_ref, 0, sem).wait()


def _dispatch(h2, dest1, dest2, pad_end, counts, n_rows):
    t, d = h2.shape
    td = 256
    grid_spec = pltpu.PrefetchScalarGridSpec(
        num_scalar_prefetch=4,
        grid=(t // td,),
        in_specs=[pl.BlockSpec((td, d), lambda i, *_: (i, 0))],
        out_specs=pl.BlockSpec(memory_space=pl.ANY),
        scratch_shapes=[pltpu.VMEM((MOE_BLOCK, d), h2.dtype), pltpu.SemaphoreType.DMA(()),
                        pltpu.SemaphoreType.DMA(())],
    )
    return pl.pallas_call(
        _dispatch_kernel,
        grid_spec=grid_spec,
        out_shape=jax.ShapeDtypeStruct((n_rows, d), h2.dtype),
        compiler_params=_cparams("arbitrary"),
        name="moe_dispatch",
    )(dest1, dest2, pad_end, counts, h2)


def _expert_kernel(blk_e_ref, nvalid_ref, xs_ref, wg_ref, wu_ref, wd_ref, ys_ref, wg_bf, wu_bf, wd_bf):
    j = pl.program_id(0)

    @pl.when((j == 0) | (blk_e_ref[j] != blk_e_ref[jnp.maximum(j - 1, 0)]))
    def _():
        wg_bf[...] = wg_ref[0, 0].astype(BF16)
        wu_bf[...] = wu_ref[0, 0].astype(BF16)
        wd_bf[...] = wd_ref[0, 0].astype(BF16)

    @pl.when(j < nvalid_ref[0])
    def _():
        xb = _unpack_bf16_pairs(xs_ref[...]).astype(BF16)
        gate = _dot(xb, wg_bf[...])
        up = _dot(xb, wu_bf[...])
        hdn = (gate * jax.nn.sigmoid(gate)) * up
        ys_ref[...] = _pack_bf16_pairs(_dot(hdn.astype(BF16), wd_bf[...]))

    @pl.when(j >= nvalid_ref[0])
    def _():
        ys_ref[...] = jnp.zeros_like(ys_ref)


def _expert_mlp(xs, blk_e, nvalid, w_gate, w_up, w_down, layer):
    n_rows, half = xs.shape
    nblk = n_rows // MOE_BLOCK
    d, de = w_gate.shape[2:]

    def x_map(j, be, nv):
        return (jnp.minimum(j, nv[0] - 1), 0)

    w_map = lambda j, be, nv: (layer, be[j], 0, 0)
    grid_spec = pltpu.PrefetchScalarGridSpec(
        num_scalar_prefetch=2,
        grid=(nblk,),
        in_specs=[
            pl.BlockSpec((MOE_BLOCK, half), x_map),
            pl.BlockSpec((1, 1, d, de), w_map),
            pl.BlockSpec((1, 1, d, de), w_map),
            pl.BlockSpec((1, 1, de, d), w_map),
        ],
        out_specs=pl.BlockSpec((MOE_BLOCK, half), lambda j, be, nv: (j, 0)),
        scratch_shapes=[pltpu.VMEM((d, de), BF16), pltpu.VMEM((d, de), BF16), pltpu.VMEM((de, d), BF16)],
    )
    return pl.pallas_call(
        _expert_kernel,
        grid_spec=grid_spec,
        out_shape=jax.ShapeDtypeStruct((n_rows, half), jnp.uint32),
        compiler_params=_cparams("arbitrary"),
        name="expert_mlp",
    )(blk_e, nvalid, xs, w_gate, w_up, w_down)


def _combine_kernel(d1_ref, d2_ref, x_ref, mod_ref, info_ref, gf_ref, ys_ref, o_ref, buf_ref, sem):
    tc = x_ref.shape[0]
    base = pl.program_id(0) * tc

    for r in range(tc):
        for slot, dref in enumerate((d1_ref, d2_ref)):
            _row_copy(ys_ref, dref[base + r], buf_ref.at[slot], r, sem).start(priority=slot)
    for r in range(tc):
        for slot in range(2):
            _row_copy(ys_ref, 0, buf_ref.at[slot], r, sem).wait()
    info = info_ref[...]
    moe = info[:, 4:5] * _unpack_bf16_pairs(buf_ref[0]) + info[:, 5:6] * _unpack_bf16_pairs(buf_ref[1])
    x2 = x_ref[...] + mod_ref[0][5:6] * moe
    ms = jnp.mean(x2 * x2, axis=-1, keepdims=True)
    o_ref[...] = x2 * lax.rsqrt(ms + EPS) * gf_ref[...]


def _combine(x1, mod_l, info, ys, dest1, dest2, g_final, seq):
    t, d = x1.shape
    tc = 256
    tiles_per_batch = seq // tc
    grid_spec = pltpu.PrefetchScalarGridSpec(
        num_scalar_prefetch=2,
        grid=(t // tc,),
        in_specs=[
            pl.BlockSpec((tc, d), lambda i, *_: (i, 0)),
            pl.BlockSpec((1, 6, d), lambda i, *_: (i // tiles_per_batch, 0, 0)),
            pl.BlockSpec((tc, ROUTER_LANES), lambda i, *_: (i, 0)),
            pl.BlockSpec((1, d), lambda i, *_: (0, 0)),
            pl.BlockSpec(memory_space=pl.ANY),
        ],
        out_specs=pl.BlockSpec((tc, d), lambda i, *_: (i, 0)),
        scratch_shapes=[pltpu.VMEM((2, tc, d // 2), jnp.uint32), pltpu.SemaphoreType.DMA(())],
    )
    return pl.pallas_call(
        _combine_kernel,
        grid_spec=grid_spec,
        out_shape=jax.ShapeDtypeStruct((t, d), F32),
        compiler_params=_cparams("arbitrary"),
        name="moe_combine",
    )(dest1, dest2, x1, mod_l, info, g_final.reshape(1, d), ys)


def _router_weights(w_rg, w_re):
    d = w_rg.shape[0]
    w_experts = jnp.transpose(w_re, (1, 0, 2)).reshape(d, N_EXPERTS)
    pad = jnp.zeros((d, ROUTER_LANES - N_GROUPS - N_EXPERTS), F32)
    return jnp.concatenate([w_rg, w_experts, pad], axis=1)


def _block_layout(counts_row, n_blocks):
    counts = counts_row[0, N_GROUPS:N_GROUPS + N_EXPERTS].astype(jnp.int32)
    padded = (counts + MOE_BLOCK - 1) // MOE_BLOCK * MOE_BLOCK
    pad_end = jnp.cumsum(padded)
    pad_start = pad_end - padded
    starts = jnp.arange(n_blocks, dtype=jnp.int32) * MOE_BLOCK
    blk_e = jnp.minimum(jnp.sum((pad_end[None, :] <= starts[:, None]).astype(jnp.int32), axis=1), N_EXPERTS - 1)
    nvalid = (pad_end[-1:] // MOE_BLOCK).astype(jnp.int32)
    return counts, pad_start, pad_end, blk_e, nvalid


def _destinations(info, pad_start):
    ids = info[:, 0:4].astype(jnp.int32)
    experts = jnp.arange(N_EXPERTS, dtype=jnp.int32)[None, :]

    def segment_start(e):
        return jnp.sum(jnp.where(e[:, None] == experts, pad_start[None, :], 0), axis=1)

    return segment_start(ids[:, 0]) + ids[:, 2], segment_start(ids[:, 1]) + ids[:, 3]


def kernel(x, c, w_ada, b_ada, g_norm1, g_norm2, w_in, diff_lambda, diff_subln, na_rpb, conv_dw, conv_b,
           conv_norm_g, conv_norm_b, w_out, w_router_group, w_router_expert, w_exp_gate, w_exp_up,
           w_exp_down, g_final):
    batch, seq, d = x.shape
    depth = w_ada.shape[0]
    t = batch * seq
    assert d == D_MODEL and seq == GRID_ROWS * GRID_W
    n_rows = t * 2 + N_EXPERTS * MOE_BLOCK

    mod = _ada_modulation(c, w_ada, b_ada).reshape(depth, batch, 6, d)
    rope_a = _rope_tables(seq, DIFF_DH)
    rope_d = _rope_tables(seq, HEAD_DIM)
    masks = _head_masks()
    gmat = _block_diag_ones(GROUP_WIDTH, HEAD_DIM)

    x2d = x.reshape(t, d)
    pending = None
    for l in range(depth):
        mod_l = mod[l]
        projected = _input_projection(x2d, mod_l, g_norm1[l], w_in[l].astype(BF16), rope_a, rope_d, seq, pending)
        if pending is not None:
            x2d, projected = projected[0], projected[1:]
        qa, ka, va, qb, kb, vb, pc, qd, kd, vd = projected
        ya = _diff_attention(qa, ka, va, diff_lambda[l], diff_subln[l], l, batch, seq, masks, gmat)
        yb = _neighborhood_attention(qb, kb, vb, _na_bias_table(na_rpb[l]), batch, seq, masks)
        yc = _conformer_conv(pc, conv_dw[l], conv_b[l], conv_norm_g[l], conv_norm_b[l], batch, seq, gmat)
        yd = _dilated_attention(qd, kd, vd, batch, seq, masks)
        x1, h2, logits = _output_projection(
            (ya, yb, yc, yd), w_out[l].astype(BF16), x2d, mod_l, g_norm2[l],
            _router_weights(w_router_group[l], w_router_expert[l]).astype(BF16), seq)
        info, counts = _routing(logits)
        counts, pad_start, pad_end, blk_e, nvalid = _block_layout(counts, n_rows // MOE_BLOCK)
        dest1, dest2 = _destinations(info, pad_start)
        xs = _dispatch(h2, dest1, dest2, pad_end, counts, n_rows)
        ys = _expert_mlp(xs, blk_e, nvalid, w_exp_gate, w_exp_up, w_exp_down, l)
        pending = (x1, mod_l, info, ys, dest1, dest2)
    x1, mod_l, info, ys, dest1, dest2 = pending
    out = _combine(x1, mod_l, info, ys, dest1, dest2, g_final, seq)
    return out.reshape(batch, seq, d)
```

```python
import functools
import math

import numpy as np
import jax
import jax.numpy as jnp
from jax import lax
from jax.experimental import pallas as pl
from jax.experimental.pallas import tpu as pltpu
from jax.experimental.pallas import tpu_sc as plsc

F32 = jnp.float32
BF16 = jnp.bfloat16

D_MODEL = 1024
GROUP_WIDTH = 256
HEAD_DIM = 64
N_HEADS = 4
DIFF_DH = 32
CONV_K = 31
CONV_GROUP_CH = 64
GRID_W = 64
NA_KH = 8
NA_KW = 16
GRID_ROWS = 32
NA_ROWS_PER_ITER = 4
ROPE_THETA = 10000.0
N_GROUPS = 4
EXPERTS_PER_GROUP = 8
N_EXPERTS = 32
D_EXPERT = 512
MOE_BLOCK = 256
EPS = 1e-6
NEG_INF = -1e30
LOG2E = 1.4426950408889634
ROUTER_LANES = 128
DILATED_CFG = ((128, 1), (512, 4), (2048, 16))
DIFF_EXP2_SCALE = (DIFF_DH ** -0.5) * LOG2E

VMEM_LIMIT = 56 * 1024 * 1024


def _cparams(*sem):
    return pltpu.CompilerParams(dimension_semantics=sem, vmem_limit_bytes=VMEM_LIMIT)


def _dot(a, b):
    return jnp.dot(a, b, preferred_element_type=F32)


def _dot_nt(a, b):
    return lax.dot_general(a, b, (((1,), (1,)), ((), ())), preferred_element_type=F32)


def _split(a):
    hi = a.astype(BF16)
    lo = (a - hi.astype(F32)).astype(BF16)
    return hi, lo


def _dot3(a, b):
    ah, al = _split(a)
    bh, bl = _split(b)
    return _dot(ah, bh) + (_dot(ah, bl) + _dot(al, bh))


def _group_mean(v, gmat, width):
    hi, lo = _split(v)
    return (_dot(hi, gmat) + _dot(lo, gmat)) * (1.0 / width)


HIGH_HALF = 0xFFFF0000


def _pack_bf16_pairs(a):
    n = a.shape[1] // 2
    bits = lax.bitcast_convert_type(a.astype(BF16).astype(F32), jnp.uint32)
    return (bits[:, :n] >> 16) | (bits[:, n:] & jnp.uint32(HIGH_HALF))


def _unpack_bf16_pairs(u):
    lo = lax.bitcast_convert_type(u << 16, F32)
    hi = lax.bitcast_convert_type(u & jnp.uint32(HIGH_HALF), F32)
    return jnp.concatenate([lo, hi], axis=1)


def _block_diag_ones(n, width):
    idx = np.arange(n) // width
    return jnp.asarray((idx[:, None] == idx[None, :]).astype(np.float32), dtype=BF16)


def _ada_kernel(c_ref, w_ref, b_ref, o_ref):
    c = c_ref[...]
    ca = c * jax.nn.sigmoid(c)
    o_ref[0] = _dot3(ca, w_ref[0]) + b_ref[0]


def _ada_modulation(c, w_ada, b_ada):
    depth, d, n = w_ada.shape
    b = c.shape[0]
    bn = 1024
    return pl.pallas_call(
        _ada_kernel,
        grid=(depth, n // bn),
        in_specs=[
            pl.BlockSpec((b, d), lambda l, j: (0, 0)),
            pl.BlockSpec((1, d, bn), lambda l, j: (l, 0, j)),
            pl.BlockSpec((1, 1, bn), lambda l, j: (l, 0, j)),
        ],
        out_specs=pl.BlockSpec((1, b, bn), lambda l, j: (l, 0, j)),
        out_shape=jax.ShapeDtypeStruct((depth, b, n), F32),
        compiler_params=_cparams("arbitrary", "arbitrary"),
        name="ada_modulation",
    )(c, w_ada, b_ada.reshape(depth, 1, n))


def _rope_tables(seq, dim):
    half = dim // 2
    inv = ROPE_THETA ** (-jnp.arange(0, dim, 2, dtype=F32) / dim)
    ang = jnp.arange(seq, dtype=F32)[:, None] * inv[None, :]
    cos, sin = jnp.cos(ang), jnp.sin(ang)
    reps = 128 // dim
    zeros = jnp.zeros_like(sin)
    cos_t = jnp.tile(jnp.concatenate([cos, cos], axis=1), (1, reps))
    sin_hi = jnp.tile(jnp.concatenate([zeros, sin], axis=1), (1, reps))
    sin_lo = jnp.tile(jnp.concatenate([-sin, zeros], axis=1), (1, reps))
    return cos_t, sin_hi, sin_lo


def _rotary(v, cos_t, sin_hi, sin_lo, half):
    outs = []
    for j in range(v.shape[1] // 128):
        vj = v[:, j * 128:(j + 1) * 128]
        outs.append(vj * cos_t + pltpu.roll(vj, half, 1) * sin_hi + pltpu.roll(vj, 128 - half, 1) * sin_lo)
    return jnp.concatenate(outs, axis=1)


def _inproj_kernel(x_ref, mod_ref, g_ref, w_ref, ca_ref, sha_ref, sla_ref, cd_ref, shd_ref, sld_ref,
                   qa_ref, ka_ref, va_ref, qb_ref, kb_ref, vb_ref, pc_ref, qd_ref, kd_ref, vd_ref):
    x = x_ref[...]
    ms = jnp.mean(x * x, axis=-1, keepdims=True)
    y = x * lax.rsqrt(ms + EPS)
    mod = mod_ref[0]
    h = (y * g_ref[...]) * (1.0 + mod[1:2]) + mod[0:1]
    hb = h.astype(BF16)
    gw = GROUP_WIDTH

    def proj(col):
        return _dot(hb, w_ref[:, col * gw:(col + 1) * gw])

    rot_a = functools.partial(_rotary, cos_t=ca_ref[...], sin_hi=sha_ref[...], sin_lo=sla_ref[...],
                              half=DIFF_DH // 2)
    rot_d = functools.partial(_rotary, cos_t=cd_ref[...], sin_hi=shd_ref[...], sin_lo=sld_ref[...],
                              half=HEAD_DIM // 2)
    na_scale = HEAD_DIM ** -0.5
    qa_ref[...] = (rot_a(proj(0)) * DIFF_EXP2_SCALE).astype(BF16)
    ka_ref[...] = rot_a(proj(1)).astype(BF16)
    va_ref[...] = proj(2).astype(BF16)
    qb_ref[...] = (proj(3) * na_scale).astype(BF16)
    kb_ref[...] = proj(4).astype(BF16)
    vb_ref[...] = proj(5).astype(BF16)
    pc_ref[:, 0:gw] = proj(6)
    pc_ref[:, gw:2 * gw] = proj(7)
    for ref, val in ((qd_ref, rot_d(proj(8)) * na_scale), (kd_ref, rot_d(proj(9))), (vd_ref, proj(10))):
        for j in range(gw // 128):
            ref[j] = val[:, j * 128:(j + 1) * 128]


def _input_projection(x2d, mod_l, g1, w_in_bf16, rope_a, rope_d, seq):
    t, d = x2d.shape
    tm = 512
    tiles_per_batch = seq // tm
    p_in = w_in_bf16.shape[1]
    gw = GROUP_WIDTH
    row_spec = lambda width: pl.BlockSpec((tm, width), lambda i: (i, 0))
    tab_spec = pl.BlockSpec((tm, 128), lambda i: (i % tiles_per_batch, 0))
    out_shapes = []
    out_specs = []
    for name in ("qa", "ka", "va", "qb", "kb", "vb", "pc", "qd", "kd", "vd"):
        if name == "pc":
            out_shapes.append(jax.ShapeDtypeStruct((t, 2 * gw), F32))
            out_specs.append(row_spec(2 * gw))
        elif name[1] == "d":
            out_shapes.append(jax.ShapeDtypeStruct((gw // 128, t, 128), F32))
            out_specs.append(pl.BlockSpec((gw // 128, tm, 128), lambda i: (0, i, 0)))
        else:
            out_shapes.append(jax.ShapeDtypeStruct((t, gw), BF16))
            out_specs.append(row_spec(gw))
    return pl.pallas_call(
        _inproj_kernel,
        grid=(t // tm,),
        in_specs=[
            row_spec(d),
            pl.BlockSpec((1, 6, d), lambda i: (i // tiles_per_batch, 0, 0)),
            pl.BlockSpec((1, d), lambda i: (0, 0)),
            pl.BlockSpec((d, p_in), lambda i: (0, 0)),
            tab_spec, tab_spec, tab_spec, tab_spec, tab_spec, tab_spec,
        ],
        out_specs=out_specs,
        out_shape=out_shapes,
        compiler_params=_cparams("arbitrary"),
        name="input_projection",
    )(x2d, mod_l, g1.reshape(1, d), w_in_bf16, *rope_a, *rope_d)


def _head_masks():
    lane = np.arange(GROUP_WIDTH)
    head = np.stack([(lane // HEAD_DIM == h) for h in range(N_HEADS)]).astype(np.float32)
    diff = np.stack([(lane // DIFF_DH == j) for j in range(2 * N_HEADS)]).astype(np.float32)
    return (jnp.asarray(head[:, None, :], dtype=BF16), jnp.asarray(head[:, None, :], dtype=F32),
            jnp.asarray(diff[:, None, :], dtype=BF16))


DIFF_ONES_ROWS = 16


def _diff_attn_kernel(lam_init, q_ref, k_ref, v_ref, lp_ref, g_ref, dmask_ref, gmat_ref, o_ref, vt_ref, ot_ref):
    seq = k_ref.shape[0]

    @pl.when(pl.program_id(1) == 0)
    def _():
        vt = jnp.transpose(v_ref[...].astype(F32))
        for h in range(N_HEADS):
            vt_ref[h, 0:HEAD_DIM, :] = vt[h * HEAD_DIM:(h + 1) * HEAD_DIM].astype(BF16)
            vt_ref[h, HEAD_DIM:HEAD_DIM + DIFF_ONES_ROWS, :] = jnp.ones((DIFF_ONES_ROWS, seq), BF16)

    q = q_ref[...]
    k = k_ref[...]
    lp = lp_ref[...]
    lam = (jnp.exp(jnp.sum(lp[0:1] * lp[1:2], axis=-1, keepdims=True))
           - jnp.exp(jnp.sum(lp[2:3] * lp[3:4], axis=-1, keepdims=True)) + lam_init)

    def scores(j):
        return _dot_nt(k, q * dmask_ref[j])

    def weights(st):
        return jnp.exp2(st - jnp.max(st, axis=0, keepdims=True)).astype(BF16)

    def attend(j, e):
        num = _dot(vt_ref[j // 2], e)
        return num[0:HEAD_DIM] / num[HEAD_DIM:HEAD_DIM + 1]

    n_pairs = 2 * N_HEADS
    outs = [None] * n_pairs
    st_next = scores(0)
    e_prev = None
    for j in range(n_pairs):
        st = st_next
        if j + 1 < n_pairs:
            st_next = scores(j + 1)
        e = weights(st)
        if e_prev is not None:
            outs[j - 1] = attend(j - 1, e_prev)
        e_prev = e
    outs[n_pairs - 1] = attend(n_pairs - 1, e_prev)
    for h in range(N_HEADS):
        ot_ref[h * HEAD_DIM:(h + 1) * HEAD_DIM, :] = outs[2 * h] - lam * outs[2 * h + 1]
    o = jnp.transpose(ot_ref[...])
    ms = _group_mean(o * o, gmat_ref[...], HEAD_DIM)
    o_ref[...] = ((o * lax.rsqrt(ms + EPS) * g_ref[...]) * (1.0 - lam_init)).astype(BF16)


def _diff_attention(qa, ka, va, lam_params, subln_g, layer_idx, batch, seq, masks, gmat):
    t, gw = qa.shape
    tq = 512
    nq = seq // tq
    lam_init = 0.8 - 0.6 * math.exp(-0.3 * layer_idx)
    _, _, dmask = masks
    g_tiled = jnp.tile(subln_g, N_HEADS).reshape(1, gw)
    kv_spec = pl.BlockSpec((seq, gw), lambda b, i: (b, 0))
    full = lambda shape: pl.BlockSpec(shape, lambda b, i: (0,) * len(shape))
    return pl.pallas_call(
        functools.partial(_diff_attn_kernel, lam_init),
        grid=(batch, nq),
        in_specs=[
            pl.BlockSpec((tq, gw), lambda b, i: (b * nq + i, 0)),
            kv_spec, kv_spec,
            full(lam_params.shape), full((1, gw)), full(dmask.shape), full(gmat.shape),
        ],
        out_specs=pl.BlockSpec((tq, gw), lambda b, i: (b * nq + i, 0)),
        out_shape=jax.ShapeDtypeStruct((t, gw), BF16),
        scratch_shapes=[pltpu.VMEM((N_HEADS, HEAD_DIM + DIFF_ONES_ROWS, seq), BF16), pltpu.VMEM((gw, tq), F32)],
        compiler_params=_cparams("arbitrary", "arbitrary"),
        name="diff_attention",
    )(qa, ka, va, lam_params, g_tiled, dmask, gmat)


def _na_bias_table(rpb):
    w = GRID_W
    n_heads = rpb.shape[0]
    cq = np.arange(w)[:, None]
    ck = np.arange(w)[None, :]
    dc = np.clip(ck - cq, -(NA_KW - 1), NA_KW - 1) + NA_KW - 1
    col_start = np.clip(cq - NA_KW // 2, 0, w - NA_KW)
    col_ok = (ck >= col_start) & (ck < col_start + NA_KW)
    onehot = jnp.asarray(dc[:, :, None] == np.arange(2 * NA_KW - 1), dtype=F32)
    toep = jnp.einsum('qkd,hrd->hrqk', onehot, rpb.astype(F32), precision=lax.Precision.HIGHEST)
    toep = jnp.where(col_ok[None, None], toep, NEG_INF)
    tabs = jnp.stack([toep[:, NA_KH - 1 - off:2 * NA_KH - 1 - off] for off in range(NA_KH)])
    return jnp.transpose(tabs, (0, 1, 3, 2, 4)).reshape(NA_KH, n_heads * w, NA_KH * w)


def _na_kernel(q_ref, k_ref, v_ref, tab_ref, hmask_ref, hmask_f32_ref, o_ref):
    w = GRID_W
    nk = NA_KH * w

    def scores(i):
        row_start = jnp.clip(i - NA_KH // 2, 0, GRID_ROWS - NA_KH)
        kstart = pl.multiple_of(row_start * w, w)
        q = q_ref[pl.ds(pl.multiple_of(i * w, w), w), :]
        qs = jnp.concatenate([q * hmask_ref[h] for h in range(N_HEADS)], axis=0)
        return _dot_nt(qs, k_ref[pl.ds(kstart, nk), :]) + tab_ref[i - row_start], kstart

    def attend(i, s, kstart):
        m = jnp.max(s, axis=-1, keepdims=True)
        e = jnp.exp(s - m)
        p = e * (1.0 / jnp.sum(e, axis=-1, keepdims=True))
        pv = _dot(p.astype(BF16), v_ref[pl.ds(kstart, nk), :])
        o = pv[(N_HEADS - 1) * w:N_HEADS * w]
        for h in range(N_HEADS - 2, -1, -1):
            o = jnp.where(hmask_f32_ref[h] > 0.5, pv[h * w:(h + 1) * w], o)
        o_ref[pl.ds(pl.multiple_of(i * w, w), w), :] = o.astype(BF16)

    def grid_rows(p, carry):
        rows = [NA_ROWS_PER_ITER * p + u for u in range(NA_ROWS_PER_ITER)]
        staged = [scores(i) for i in rows]
        for i, (s, kstart) in zip(rows, staged):
            attend(i, s, kstart)
        return carry

    lax.fori_loop(0, GRID_ROWS // NA_ROWS_PER_ITER, grid_rows, 0)


def _neighborhood_attention(qb, kb, vb, bias_table, batch, seq, masks):
    t, gw = qb.shape
    hmask, hmask_f32, _ = masks
    seq_spec = pl.BlockSpec((seq, gw), lambda b: (b, 0))
    full = lambda shape: pl.BlockSpec(shape, lambda b: (0,) * len(shape))
    return pl.pallas_call(
        _na_kernel,
        grid=(batch,),
        in_specs=[seq_spec, seq_spec, seq_spec, full(bias_table.shape), full(hmask.shape), full(hmask_f32.shape)],
        out_specs=seq_spec,
        out_shape=jax.ShapeDtypeStruct((t, gw), BF16),
        compiler_params=_cparams("arbitrary"),
        name="neighborhood_attention",
    )(qb, kb, vb, bias_table, hmask, hmask_f32)


SUBLANES = 8
CONV_PAD = 16
CONV_CHUNK = 128


def _conv_kernel(pc_ref, w_ref, b_ref, gn_ref, bn_ref, gmat_ref, o_ref, zp_ref, zs_ref):
    seq, ch = o_ref.shape
    a = pc_ref[:, 0:ch]
    gate = pc_ref[:, ch:2 * ch]
    zp_ref[0:CONV_PAD, :] = jnp.zeros((CONV_PAD, ch), F32)
    zp_ref[CONV_PAD + seq:2 * CONV_PAD + seq, :] = jnp.zeros((CONV_PAD, ch), F32)
    zp_ref[CONV_PAD:CONV_PAD + seq, :] = a * jax.nn.sigmoid(gate)
    span = seq + 2 * CONV_PAD - SUBLANES
    for b in range(1, SUBLANES):
        zs_ref[b - 1, 0:span, :] = zp_ref[b:b + span, :]
    gmat = gmat_ref[...]
    first = CONV_PAD - CONV_K // 2
    for c in range(seq // CONV_CHUNK):
        r0 = c * CONV_CHUNK
        acc = jnp.zeros((CONV_CHUNK, ch), F32)
        for j in range(CONV_K):
            shift, aligned = (first + j) % SUBLANES, r0 + (first + j) // SUBLANES * SUBLANES
            src = zp_ref if shift == 0 else zs_ref.at[shift - 1]
            acc = acc + w_ref[j] * src[aligned:aligned + CONV_CHUNK, :]
        z = acc + b_ref[...]
        mu = _group_mean(z, gmat, CONV_GROUP_CH)
        dz = z - mu
        var = _group_mean(dz * dz, gmat, CONV_GROUP_CH)
        zn = dz * lax.rsqrt(var + EPS) * gn_ref[...] + bn_ref[...]
        o_ref[r0:r0 + CONV_CHUNK, :] = (zn * jax.nn.sigmoid(zn)).astype(BF16)


def _conformer_conv(pc, w_dw, b_dw, g_n, b_n, batch, seq, gmat):
    t = pc.shape[0]
    ch = GROUP_WIDTH
    full = lambda shape: pl.BlockSpec(shape, lambda b: (0,) * len(shape))
    return pl.pallas_call(
        _conv_kernel,
        grid=(batch,),
        in_specs=[
            pl.BlockSpec((seq, 2 * ch), lambda b: (b, 0)),
            full((CONV_K, 1, ch)), full((1, ch)), full((1, ch)), full((1, ch)), full(gmat.shape),
        ],
        out_specs=pl.BlockSpec((seq, ch), lambda b: (b, 0)),
        out_shape=jax.ShapeDtypeStruct((t, ch), BF16),
        scratch_shapes=[pltpu.VMEM((seq + 2 * CONV_PAD, ch), F32),
                        pltpu.VMEM((SUBLANES - 1, seq + 2 * CONV_PAD, ch), F32)],
        compiler_params=_cparams("arbitrary"),
        name="conformer_conv",
    )(pc, w_dw.reshape(CONV_K, 1, ch), b_dw.reshape(1, ch), g_n.reshape(1, ch), b_n.reshape(1, ch), gmat)


DIL_QB = 128
DIL_KB = 256
DIL_UNITS_PER_ITER = 2


def _strided_rows(first, count, stride):
    return pl.ds(first, count) if stride == 1 else pl.ds(first, count, stride=stride)


def _load_rows(ref, rows):
    return jnp.concatenate([ref[j, rows, :] for j in range(ref.shape[0])], axis=1)


def _store_rows(ref, rows, val):
    for j in range(ref.shape[0]):
        ref[j, rows, :] = val[:, j * 128:(j + 1) * 128]


def _dilated_kernel(q_ref, k_ref, v_ref, hmask_ref, hmask_f32_ref, o_ref, acc_ref, m_ref, l_ref):
    seq = q_ref.shape[1]
    for branch, (window, dil) in enumerate(DILATED_CFG):
        n_side = window // (2 * dil)
        sub_len = seq // dil
        qb = min(DIL_QB, sub_len)
        kb = min(DIL_KB, sub_len)
        rel = (lax.broadcasted_iota(jnp.int32, (N_HEADS * qb, kb), 1)
               - (lax.broadcasted_iota(jnp.int32, (N_HEADS * qb, kb), 0) & (qb - 1)))

        def scores(r, c, dil=dil, n_side=n_side, sub_len=sub_len, qb=qb, kb=kb, rel=rel):
            l0 = c * qb
            kl0 = jnp.clip(l0 - n_side, 0, sub_len - kb)
            if dil == 1:
                l0, kl0 = pl.multiple_of(l0, qb), pl.multiple_of(kl0, n_side)
            q_rows = _strided_rows(r + dil * l0, qb, dil)
            k_rows = _strided_rows(r + dil * kl0, kb, dil)
            q = _load_rows(q_ref, q_rows).astype(BF16)
            qs = jnp.concatenate([q * hmask_ref[h] for h in range(N_HEADS)], axis=0)
            s = _dot_nt(qs, _load_rows(k_ref, k_rows).astype(BF16))
            return jnp.where(jnp.abs(rel + (kl0 - l0)) <= n_side, s, NEG_INF), q_rows, k_rows

        def attend(s, q_rows, k_rows, qb=qb, branch=branch):
            m = jnp.max(s, axis=-1, keepdims=True)
            e = jnp.exp(s - m)
            l = jnp.sum(e, axis=-1, keepdims=True)
            pv = _dot(e.astype(BF16), _load_rows(v_ref, k_rows).astype(BF16))

            def unstack(a):
                out = a[(N_HEADS - 1) * qb:N_HEADS * qb]
                for h in range(N_HEADS - 2, -1, -1):
                    out = jnp.where(hmask_f32_ref[h] > 0.5, a[h * qb:(h + 1) * qb], out)
                return out

            acc_new, m_new, l_new = unstack(pv), unstack(m), unstack(l)
            if branch == 0:
                _store_rows(acc_ref, q_rows, acc_new)
                _store_rows(m_ref, q_rows, m_new)
                _store_rows(l_ref, q_rows, l_new)
            else:
                m_old = _load_rows(m_ref, q_rows)
                m_max = jnp.maximum(m_old, m_new)
                w_old = jnp.exp(m_old - m_max)
                w_new = jnp.exp(m_new - m_max)
                _store_rows(acc_ref, q_rows, _load_rows(acc_ref, q_rows) * w_old + acc_new * w_new)
                _store_rows(l_ref, q_rows, _load_rows(l_ref, q_rows) * w_old + l_new * w_new)
                _store_rows(m_ref, q_rows, m_max)

        def run(units, scores=scores, attend=attend):
            staged = [scores(r, c) for r, c in units]
            for item in staged:
                attend(*item)

        n_blocks = sub_len // qb
        if n_blocks >= DIL_UNITS_PER_ITER:
            for r in range(dil):
                def block_group(p, carry, r=r, run=run):
                    run([(r, DIL_UNITS_PER_ITER * p + u) for u in range(DIL_UNITS_PER_ITER)])
                    return carry

                lax.fori_loop(0, n_blocks // DIL_UNITS_PER_ITER, block_group, 0)
        else:
            for r0 in range(0, dil, DIL_UNITS_PER_ITER):
                run([(r0 + u, 0) for u in range(DIL_UNITS_PER_ITER)])
    for j in range(acc_ref.shape[0]):
        o_ref[:, j * 128:(j + 1) * 128] = (acc_ref[j] / l_ref[j]).astype(BF16)


def _dilated_attention(qd, kd, vd, batch, seq, masks):
    tiles, t, _ = qd.shape
    gw = tiles * 128
    hmask, hmask_f32, _ = masks
    in_spec = pl.BlockSpec((tiles, seq, 128), lambda b: (0, b, 0))
    stat = pltpu.VMEM((tiles, seq, 128), F32)
    return pl.pallas_call(
        _dilated_kernel,
        grid=(batch,),
        in_specs=[
            in_spec, in_spec, in_spec,
            pl.BlockSpec(hmask.shape, lambda b: (0, 0, 0)),
            pl.BlockSpec(hmask_f32.shape, lambda b: (0, 0, 0)),
        ],
        out_specs=pl.BlockSpec((seq, gw), lambda b: (b, 0)),
        out_shape=jax.ShapeDtypeStruct((t, gw), BF16),
        scratch_shapes=[stat, stat, stat],
        compiler_params=_cparams("arbitrary"),
        name="dilated_attention",
    )(qd, kd, vd, hmask, hmask_f32)


def _outproj_kernel(ya_ref, yb_ref, yc_ref, yd_ref, w_ref, x_ref, mod_ref, g_ref, wr_ref,
                    x1_ref, h2_ref, lg_ref):
    gw = GROUP_WIDTH
    mix = _dot(ya_ref[...], w_ref[0:gw, :])
    mix = mix + _dot(yb_ref[...], w_ref[gw:2 * gw, :])
    mix = mix + _dot(yc_ref[...], w_ref[2 * gw:3 * gw, :])
    mix = mix + _dot(yd_ref[...], w_ref[3 * gw:4 * gw, :])
    mod = mod_ref[0]
    x1 = x_ref[...] + mod[2:3] * mix
    x1_ref[...] = x1
    ms = jnp.mean(x1 * x1, axis=-1, keepdims=True)
    h2 = (x1 * lax.rsqrt(ms + EPS) * g_ref[...]) * (1.0 + mod[4:5]) + mod[3:4]
    h2_ref[...] = _pack_bf16_pairs(h2)
    lg_ref[...] = _dot(h2.astype(BF16), wr_ref[...])


def _output_projection(ys, w_out_bf16, x2d, mod_l, g2, w_router, seq):
    t, d = x2d.shape
    tm = 512
    tiles_per_batch = seq // tm
    gw = GROUP_WIDTH
    row_spec = lambda width: pl.BlockSpec((tm, width), lambda i: (i, 0))
    return pl.pallas_call(
        _outproj_kernel,
        grid=(t // tm,),
        in_specs=[
            row_spec(gw), row_spec(gw), row_spec(gw), row_spec(gw),
            pl.BlockSpec((d, d), lambda i: (0, 0)),
            row_spec(d),
            pl.BlockSpec((1, 6, d), lambda i: (i // tiles_per_batch, 0, 0)),
            pl.BlockSpec((1, d), lambda i: (0, 0)),
            pl.BlockSpec((d, ROUTER_LANES), lambda i: (0, 0)),
        ],
        out_specs=[row_spec(d), row_spec(d // 2), row_spec(ROUTER_LANES)],
        out_shape=[jax.ShapeDtypeStruct((t, d), F32), jax.ShapeDtypeStruct((t, d // 2), jnp.uint32),
                   jax.ShapeDtypeStruct((t, ROUTER_LANES), F32)],
        compiler_params=_cparams("arbitrary"),
        name="output_projection",
    )(*ys, w_out_bf16, x2d, mod_l, g2.reshape(1, d), w_router)


def _routing_kernel(lg_ref, info_ref, cnt_ref, carry_ref):
    tr = lg_ref.shape[0]

    @pl.when(pl.program_id(0) == 0)
    def _():
        carry_ref[...] = jnp.zeros_like(carry_ref)

    lg = lg_ref[...]
    lane = lax.broadcasted_iota(jnp.int32, lg.shape, 1).astype(F32)
    big = float(ROUTER_LANES)
    glog = jnp.where(lane < N_GROUPS, lg, -jnp.inf)
    gmax = jnp.max(glog, axis=-1, keepdims=True)
    p_grp = 1.0 / jnp.sum(jnp.exp(glog - gmax), axis=-1, keepdims=True)
    grp = jnp.min(jnp.where(glog == gmax, lane, big), axis=-1, keepdims=True)
    lo = N_GROUPS + EXPERTS_PER_GROUP * grp
    elog = jnp.where((lane >= lo) & (lane < lo + EXPERTS_PER_GROUP), lg, -jnp.inf)
    v1 = jnp.max(elog, axis=-1, keepdims=True)
    i1 = jnp.min(jnp.where(elog == v1, lane, big), axis=-1, keepdims=True)
    elog2 = jnp.where(lane == i1, -jnp.inf, elog)
    v2 = jnp.max(elog2, axis=-1, keepdims=True)
    i2 = jnp.min(jnp.where(elog2 == v2, lane, big), axis=-1, keepdims=True)
    d = jnp.exp(v2 - v1)
    gate1 = p_grp / (1.0 + d)
    gate2 = p_grp * d / (1.0 + d)
    sel1 = lane == i1
    sel2 = lane == i2
    sel = jnp.where(sel1 | sel2, 1.0, 0.0)
    row = lax.broadcasted_iota(jnp.int32, (tr, tr), 0)
    col = lax.broadcasted_iota(jnp.int32, (tr, tr), 1)
    before = jnp.where(col < row, 1.0, 0.0).astype(BF16)
    rank = _dot(before, sel.astype(BF16)) + carry_ref[...]
    r1 = jnp.sum(jnp.where(sel1, rank, 0.0), axis=-1, keepdims=True)
    r2 = jnp.sum(jnp.where(sel2, rank, 0.0), axis=-1, keepdims=True)
    carry_ref[...] += jnp.sum(sel, axis=0, keepdims=True)
    cnt_ref[...] = carry_ref[...]
    info = jnp.zeros_like(lg)
    for idx, val in enumerate((i1 - N_GROUPS, i2 - N_GROUPS, r1, r2, gate1, gate2)):
        info = jnp.where(lane == idx, val, info)
    info_ref[...] = info


def _routing(logits):
    t = logits.shape[0]
    tr = 512
    return pl.pallas_call(
        _routing_kernel,
        grid=(t // tr,),
        in_specs=[pl.BlockSpec((tr, ROUTER_LANES), lambda i: (i, 0))],
        out_specs=[pl.BlockSpec((tr, ROUTER_LANES), lambda i: (i, 0)),
                   pl.BlockSpec((1, ROUTER_LANES), lambda i: (0, 0))],
        out_shape=[jax.ShapeDtypeStruct((t, ROUTER_LANES), F32),
                   jax.ShapeDtypeStruct((1, ROUTER_LANES), F32)],
        scratch_shapes=[pltpu.VMEM((1, ROUTER_LANES), F32)],
        compiler_params=_cparams("arbitrary"),
        name="routing",
    )(logits)


def _row_copy(src_ref, src_row, dst_ref, dst_row, sem):
    return pltpu.make_async_copy(src_ref.at[pl.ds(src_row, 1)], dst_ref.at[pl.ds(dst_row, 1)], sem)


def _block_rows(ref, block):
    return ref.at[pl.ds(pl.multiple_of(block * MOE_BLOCK, MOE_BLOCK), MOE_BLOCK)]


def _dispatch_kernel(d1_ref, d2_ref, pad_end_ref, count_ref, h_ref, xs_ref, zero_ref, sem, zero_sem):
    td = h_ref.shape[0]
    base = pl.program_id(0) * td

    @pl.when(pl.program_id(0) == 0)
    def _():
        zero_ref[...] = jnp.zeros_like(zero_ref)

        def tail_copy(e):
            return pltpu.make_async_copy(zero_ref, _block_rows(xs_ref, pad_end_ref[e] // MOE_BLOCK - 1), zero_sem)

        for e in range(N_EXPERTS):
            @pl.when(count_ref[e] > 0)
            def _():
                tail_copy(e).start()

        for e in range(N_EXPERTS):
            @pl.when(count_ref[e] > 0)
            def _():
                tail_copy(e).wait()

        def unused_copy(j):
            return pltpu.make_async_copy(zero_ref, _block_rows(xs_ref, j), zero_sem)

        first_unused = pad_end_ref[N_EXPERTS - 1] // MOE_BLOCK
        n_blocks = xs_ref.shape[0] // MOE_BLOCK

        def start_unused(j, carry):
            unused_copy(j).start()
            return carry

        def wait_unused(j, carry):
            unused_copy(j).wait()
            return carry

        lax.fori_loop(first_unused, n_blocks, start_unused, 0)
        lax.fori_loop(first_unused, n_blocks, wait_unused, 0)

    for r in range(td):
        for slot, dref in enumerate((d1_ref, d2_ref)):
            _row_copy(h_ref, r, xs_ref, dref[base + r], sem).start(priority=slot)
    for r in range(td):
        for _ in range(2):
            _row_copy(h_ref, r, xs_ref, 0, sem).wait()


def _dispatch(h2, dest1, dest2, pad_end, counts, n_rows):
    t, d = h2.shape
    td = 256
    grid_spec = pltpu.PrefetchScalarGridSpec(
        num_scalar_prefetch=4,
        grid=(t // td,),
        in_specs=[pl.BlockSpec((td, d), lambda i, *_: (i, 0))],
        out_specs=pl.BlockSpec(memory_space=pl.ANY),
        scratch_shapes=[pltpu.VMEM((MOE_BLOCK, d), h2.dtype), pltpu.SemaphoreType.DMA(()),
                        pltpu.SemaphoreType.DMA(())],
    )
    return pl.pallas_call(
        _dispatch_kernel,
        grid_spec=grid_spec,
        out_shape=jax.ShapeDtypeStruct((n_rows, d), h2.dtype),
        compiler_params=_cparams("arbitrary"),
        name="moe_dispatch",
    )(dest1, dest2, pad_end, counts, h2)


def _expert_kernel(blk_e_ref, nvalid_ref, xs_ref, wg_ref, wu_ref, wd_ref, ys_lo_ref, ys_hi_ref, wg_bf, wu_bf, wd_bf):
    j = pl.program_id(0)
    half = ys_lo_ref.shape[1]

    @pl.when((j == 0) | (blk_e_ref[j] != blk_e_ref[jnp.maximum(j - 1, 0)]))
    def _():
        wg_bf[...] = wg_ref[0, 0].astype(BF16)
        wu_bf[...] = wu_ref[0, 0].astype(BF16)
        wd_bf[...] = wd_ref[0, 0].astype(BF16)

    @pl.when(j < nvalid_ref[0])
    def _():
        xb = _unpack_bf16_pairs(xs_ref[...]).astype(BF16)
        gate = _dot(xb, wg_bf[...])
        up = _dot(xb, wu_bf[...])
        hdn = (gate * jax.nn.sigmoid(gate)) * up
        packed = _pack_bf16_pairs(_dot(hdn.astype(BF16), wd_bf[...]))
        ys_lo_ref[...] = packed[:, :half]
        ys_hi_ref[...] = packed[:, half:]

    @pl.when(j >= nvalid_ref[0])
    def _():
        ys_lo_ref[...] = jnp.zeros_like(ys_lo_ref)
        ys_hi_ref[...] = jnp.zeros_like(ys_hi_ref)


def _expert_mlp(xs, blk_e, nvalid, w_gate, w_up, w_down, layer):
    n_rows, half = xs.shape
    nblk = n_rows // MOE_BLOCK
    d, de = w_gate.shape[2:]

    def x_map(j, be, nv):
        return (jnp.minimum(j, nv[0] - 1), 0)

    w_map = lambda j, be, nv: (layer, be[j], 0, 0)
    grid_spec = pltpu.PrefetchScalarGridSpec(
        num_scalar_prefetch=2,
        grid=(nblk,),
        in_specs=[
            pl.BlockSpec((MOE_BLOCK, half), x_map),
            pl.BlockSpec((1, 1, d, de), w_map),
            pl.BlockSpec((1, 1, d, de), w_map),
            pl.BlockSpec((1, 1, de, d), w_map),
        ],
        out_specs=[pl.BlockSpec((MOE_BLOCK, half // 2), lambda j, be, nv: (j, 0))] * 2,
        scratch_shapes=[pltpu.VMEM((d, de), BF16), pltpu.VMEM((d, de), BF16), pltpu.VMEM((de, d), BF16)],
    )
    return pl.pallas_call(
        _expert_kernel,
        grid_spec=grid_spec,
        out_shape=[jax.ShapeDtypeStruct((n_rows, half // 2), jnp.uint32)] * 2,
        compiler_params=_cparams("arbitrary"),
        name="expert_mlp",
    )(blk_e, nvalid, xs, w_gate, w_up, w_down)


SC_GATHER_WINDOW = 128


def _gather_rows(table, indices):
    n = indices.shape[0]
    width = table.shape[1]
    mesh = plsc.VectorSubcoreMesh(core_axis_name="core", subcore_axis_name="subcore")

    @pl.kernel(out_type=jax.ShapeDtypeStruct((n, width), table.dtype), mesh=mesh, scratch_types=[],
               name="moe_row_gather")
    def gather_kernel(table_hbm, idx_hbm, out_hbm):
        def body(idx_vmem, out_vmem):
            pltpu.sync_copy(table_hbm.at[idx_vmem.at[0]], out_vmem)

        pltpu.emit_pipeline(
            body,
            grid=(n // SC_GATHER_WINDOW,),
            in_specs=[pl.BlockSpec((1, SC_GATHER_WINDOW), index_map=lambda i: (0, i))],
            out_specs=[pl.BlockSpec((SC_GATHER_WINDOW, width), index_map=lambda i: (i, 0))],
            core_axis_name=("core", "subcore"),
            dimension_semantics=(pltpu.PARALLEL,),
        )(idx_hbm, out_hbm)

    return gather_kernel(table, indices.reshape(1, n))


def _combine_kernel(final_norm, x_ref, mod_ref, info_ref, gf_ref, y1_lo_ref, y1_hi_ref, y2_lo_ref, y2_hi_ref,
                    o_ref):
    info = info_ref[...]
    y1 = _unpack_bf16_pairs(jnp.concatenate([y1_lo_ref[...], y1_hi_ref[...]], axis=1))
    y2 = _unpack_bf16_pairs(jnp.concatenate([y2_lo_ref[...], y2_hi_ref[...]], axis=1))
    moe = info[:, 4:5] * y1 + info[:, 5:6] * y2
    x2 = x_ref[...] + mod_ref[0][5:6] * moe
    if final_norm:
        ms = jnp.mean(x2 * x2, axis=-1, keepdims=True)
        x2 = x2 * lax.rsqrt(ms + EPS) * gf_ref[...]
    o_ref[...] = x2


def _combine(x1, mod_l, info, ys, dest1, dest2, g_final, seq, final_norm):
    t, d = x1.shape
    tc = 512
    tiles_per_batch = seq // tc
    n_tiles = t // tc
    dest = jnp.concatenate([dest1, dest2])
    g_lo, g_hi = _gather_rows(ys[0], dest), _gather_rows(ys[1], dest)
    quarter = ys[0].shape[1]
    row_spec = lambda width: pl.BlockSpec((tc, width), lambda i: (i, 0))
    slot2_spec = pl.BlockSpec((tc, quarter), lambda i: (i + n_tiles, 0))
    return pl.pallas_call(
        functools.partial(_combine_kernel, final_norm),
        grid=(n_tiles,),
        in_specs=[
            row_spec(d),
            pl.BlockSpec((1, 6, d), lambda i: (i // tiles_per_batch, 0, 0)),
            row_spec(ROUTER_LANES),
            pl.BlockSpec((1, d), lambda i: (0, 0)),
            row_spec(quarter), row_spec(quarter), slot2_spec, slot2_spec,
        ],
        out_specs=row_spec(d),
        out_shape=jax.ShapeDtypeStruct((t, d), F32),
        compiler_params=_cparams("arbitrary"),
        name="moe_combine",
    )(x1, mod_l, info, g_final.reshape(1, d), g_lo, g_hi, g_lo, g_hi)


def _router_weights(w_rg, w_re):
    d = w_rg.shape[0]
    w_experts = jnp.transpose(w_re, (1, 0, 2)).reshape(d, N_EXPERTS)
    pad = jnp.zeros((d, ROUTER_LANES - N_GROUPS - N_EXPERTS), F32)
    return jnp.concatenate([w_rg, w_experts, pad], axis=1)


def _block_layout(counts_row, n_blocks):
    counts = counts_row[0, N_GROUPS:N_GROUPS + N_EXPERTS].astype(jnp.int32)
    padded = (counts + MOE_BLOCK - 1) // MOE_BLOCK * MOE_BLOCK
    pad_end = jnp.cumsum(padded)
    pad_start = pad_end - padded
    starts = jnp.arange(n_blocks, dtype=jnp.int32) * MOE_BLOCK
    blk_e = jnp.minimum(jnp.sum((pad_end[None, :] <= starts[:, None]).astype(jnp.int32), axis=1), N_EXPERTS - 1)
    nvalid = (pad_end[-1:] // MOE_BLOCK).astype(jnp.int32)
    return counts, pad_start, pad_end, blk_e, nvalid


def _destinations(info, pad_start):
    ids = info[:, 0:4].astype(jnp.int32)
    experts = jnp.arange(N_EXPERTS, dtype=jnp.int32)[None, :]

    def segment_start(e):
        return jnp.sum(jnp.where(e[:, None] == experts, pad_start[None, :], 0), axis=1)

    return segment_start(ids[:, 0]) + ids[:, 2], segment_start(ids[:, 1]) + ids[:, 3]


def kernel(x, c, w_ada, b_ada, g_norm1, g_norm2, w_in, diff_lambda, diff_subln, na_rpb, conv_dw, conv_b,
           conv_norm_g, conv_norm_b, w_out, w_router_group, w_router_expert, w_exp_gate, w_exp_up,
           w_exp_down, g_final):
    batch, seq, d = x.shape
    depth = w_ada.shape[0]
    t = batch * seq
    assert d == D_MODEL and seq == GRID_ROWS * GRID_W
    n_rows = t * 2 + N_EXPERTS * MOE_BLOCK

    mod = _ada_modulation(c, w_ada, b_ada).reshape(depth, batch, 6, d)
    rope_a = _rope_tables(seq, DIFF_DH)
    rope_d = _rope_tables(seq, HEAD_DIM)
    masks = _head_masks()
    gmat = _block_diag_ones(GROUP_WIDTH, HEAD_DIM)

    x2d = x.reshape(t, d)
    for l in range(depth):
        mod_l = mod[l]
        qa, ka, va, qb, kb, vb, pc, qd, kd, vd = _input_projection(
            x2d, mod_l, g_norm1[l], w_in[l].astype(BF16), rope_a, rope_d, seq)
        ya = _diff_attention(qa, ka, va, diff_lambda[l], diff_subln[l], l, batch, seq, masks, gmat)
        yb = _neighborhood_attention(qb, kb, vb, _na_bias_table(na_rpb[l]), batch, seq, masks)
        yc = _conformer_conv(pc, conv_dw[l], conv_b[l], conv_norm_g[l], conv_norm_b[l], batch, seq, gmat)
        yd = _dilated_attention(qd, kd, vd, batch, seq, masks)
        x1, h2, logits = _output_projection(
            (ya, yb, yc, yd), w_out[l].astype(BF16), x2d, mod_l, g_norm2[l],
            _router_weights(w_router_group[l], w_router_expert[l]).astype(BF16), seq)
        info, counts = _routing(logits)
        counts, pad_start, pad_end, blk_e, nvalid = _block_layout(counts, n_rows // MOE_BLOCK)
        dest1, dest2 = _destinations(info, pad_start)
        xs = _dispatch(h2, dest1, dest2, pad_end, counts, n_rows)
        ys = _expert_mlp(xs, blk_e, nvalid, w_exp_gate, w_exp_up, w_exp_down, l)
        x2d = _combine(x1, mod_l, info, ys, dest1, dest2, g_final, seq, final_norm=(l == depth - 1))
    return x2d.reshape(batch, seq, d)
```

```python
import functools
import math

import numpy as np
import jax
import jax.numpy as jnp
from jax import lax
from jax.experimental import pallas as pl
from jax.experimental.pallas import tpu as pltpu
from jax.experimental.pallas import tpu_sc as plsc

F32 = jnp.float32
BF16 = jnp.bfloat16

D_MODEL = 1024
GROUP_WIDTH = 256
HEAD_DIM = 64
N_HEADS = 4
DIFF_DH = 32
CONV_K = 31
CONV_GROUP_CH = 64
GRID_W = 64
NA_KH = 8
NA_KW = 16
GRID_ROWS = 32
NA_ROWS_PER_ITER = 4
ROPE_THETA = 10000.0
N_GROUPS = 4
EXPERTS_PER_GROUP = 8
N_EXPERTS = 32
D_EXPERT = 512
MOE_BLOCK = 256
EPS = 1e-6
NEG_INF = -1e30
LOG2E = 1.4426950408889634
ROUTER_LANES = 128
DILATED_CFG = ((128, 1), (512, 4), (2048, 16))
DIFF_EXP2_SCALE = (DIFF_DH ** -0.5) * LOG2E

VMEM_LIMIT = 56 * 1024 * 1024


def _cparams(*sem):
    return pltpu.CompilerParams(dimension_semantics=sem, vmem_limit_bytes=VMEM_LIMIT)


def _dot(a, b):
    return jnp.dot(a, b, preferred_element_type=F32)


def _dot_nt(a, b):
    return lax.dot_general(a, b, (((1,), (1,)), ((), ())), preferred_element_type=F32)


def _split(a):
    hi = a.astype(BF16)
    lo = (a - hi.astype(F32)).astype(BF16)
    return hi, lo


def _dot3(a, b):
    ah, al = _split(a)
    bh, bl = _split(b)
    return _dot(ah, bh) + (_dot(ah, bl) + _dot(al, bh))


def _group_mean(v, gmat, width):
    hi, lo = _split(v)
    return (_dot(hi, gmat) + _dot(lo, gmat)) * (1.0 / width)


HIGH_HALF = 0xFFFF0000


def _pack_bf16_pairs(a):
    n = a.shape[1] // 2
    bits = lax.bitcast_convert_type(a.astype(BF16).astype(F32), jnp.uint32)
    return (bits[:, :n] >> 16) | (bits[:, n:] & jnp.uint32(HIGH_HALF))


def _unpack_bf16_pairs(u):
    lo = lax.bitcast_convert_type(u << 16, F32)
    hi = lax.bitcast_convert_type(u & jnp.uint32(HIGH_HALF), F32)
    return jnp.concatenate([lo, hi], axis=1)


def _block_diag_ones(n, width):
    idx = np.arange(n) // width
    return jnp.asarray((idx[:, None] == idx[None, :]).astype(np.float32), dtype=BF16)


def _ada_kernel(c_ref, w_ref, b_ref, o_ref):
    c = c_ref[...]
    ca = c * jax.nn.sigmoid(c)
    o_ref[0] = _dot3(ca, w_ref[0]) + b_ref[0]


def _ada_modulation(c, w_ada, b_ada):
    depth, d, n = w_ada.shape
    b = c.shape[0]
    bn = 1024
    return pl.pallas_call(
        _ada_kernel,
        grid=(depth, n // bn),
        in_specs=[
            pl.BlockSpec((b, d), lambda l, j: (0, 0)),
            pl.BlockSpec((1, d, bn), lambda l, j: (l, 0, j)),
            pl.BlockSpec((1, 1, bn), lambda l, j: (l, 0, j)),
        ],
        out_specs=pl.BlockSpec((1, b, bn), lambda l, j: (l, 0, j)),
        out_shape=jax.ShapeDtypeStruct((depth, b, n), F32),
        compiler_params=_cparams("arbitrary", "arbitrary"),
        name="ada_modulation",
    )(c, w_ada, b_ada.reshape(depth, 1, n))


def _rope_tables(seq, dim):
    half = dim // 2
    inv = ROPE_THETA ** (-jnp.arange(0, dim, 2, dtype=F32) / dim)
    ang = jnp.arange(seq, dtype=F32)[:, None] * inv[None, :]
    cos, sin = jnp.cos(ang), jnp.sin(ang)
    reps = 128 // dim
    zeros = jnp.zeros_like(sin)
    cos_t = jnp.tile(jnp.concatenate([cos, cos], axis=1), (1, reps))
    sin_hi = jnp.tile(jnp.concatenate([zeros, sin], axis=1), (1, reps))
    sin_lo = jnp.tile(jnp.concatenate([-sin, zeros], axis=1), (1, reps))
    return cos_t, sin_hi, sin_lo


def _rotary(v, cos_t, sin_hi, sin_lo, half):
    outs = []
    for j in range(v.shape[1] // 128):
        vj = v[:, j * 128:(j + 1) * 128]
        outs.append(vj * cos_t + pltpu.roll(vj, half, 1) * sin_hi + pltpu.roll(vj, 128 - half, 1) * sin_lo)
    return jnp.concatenate(outs, axis=1)


def _inproj_kernel(x_ref, mod_ref, g_ref, w_ref, ca_ref, sha_ref, sla_ref, cd_ref, shd_ref, sld_ref,
                   qa_ref, ka_ref, va_ref, qb_ref, kb_ref, vb_ref, pc_ref, qd_ref, kd_ref, vd_ref):
    x = x_ref[...]
    ms = jnp.mean(x * x, axis=-1, keepdims=True)
    y = x * lax.rsqrt(ms + EPS)
    mod = mod_ref[0]
    h = (y * g_ref[...]) * (1.0 + mod[1:2]) + mod[0:1]
    hb = h.astype(BF16)
    gw = GROUP_WIDTH

    def proj(col):
        return _dot(hb, w_ref[:, col * gw:(col + 1) * gw])

    rot_a = functools.partial(_rotary, cos_t=ca_ref[...], sin_hi=sha_ref[...], sin_lo=sla_ref[...],
                              half=DIFF_DH // 2)
    rot_d = functools.partial(_rotary, cos_t=cd_ref[...], sin_hi=shd_ref[...], sin_lo=sld_ref[...],
                              half=HEAD_DIM // 2)
    na_scale = HEAD_DIM ** -0.5
    qa_ref[...] = (rot_a(proj(0)) * DIFF_EXP2_SCALE).astype(BF16)
    ka_ref[...] = rot_a(proj(1)).astype(BF16)
    va_ref[...] = proj(2).astype(BF16)
    qb_ref[...] = (proj(3) * na_scale).astype(BF16)
    kb_ref[...] = proj(4).astype(BF16)
    vb_ref[...] = proj(5).astype(BF16)
    pc_ref[:, 0:gw] = proj(6)
    pc_ref[:, gw:2 * gw] = proj(7)
    for ref, val in ((qd_ref, rot_d(proj(8)) * na_scale), (kd_ref, rot_d(proj(9))), (vd_ref, proj(10))):
        for j in range(gw // 128):
            ref[j] = val[:, j * 128:(j + 1) * 128]


def _input_projection(x2d, mod_l, g1, w_in_bf16, rope_a, rope_d, seq):
    t, d = x2d.shape
    tm = 512
    tiles_per_batch = seq // tm
    p_in = w_in_bf16.shape[1]
    gw = GROUP_WIDTH
    row_spec = lambda width: pl.BlockSpec((tm, width), lambda i: (i, 0))
    tab_spec = pl.BlockSpec((tm, 128), lambda i: (i % tiles_per_batch, 0))
    out_shapes = []
    out_specs = []
    for name in ("qa", "ka", "va", "qb", "kb", "vb", "pc", "qd", "kd", "vd"):
        if name == "pc":
            out_shapes.append(jax.ShapeDtypeStruct((t, 2 * gw), F32))
            out_specs.append(row_spec(2 * gw))
        elif name[1] == "d":
            out_shapes.append(jax.ShapeDtypeStruct((gw // 128, t, 128), F32))
            out_specs.append(pl.BlockSpec((gw // 128, tm, 128), lambda i: (0, i, 0)))
        else:
            out_shapes.append(jax.ShapeDtypeStruct((t, gw), BF16))
            out_specs.append(row_spec(gw))
    return pl.pallas_call(
        _inproj_kernel,
        grid=(t // tm,),
        in_specs=[
            row_spec(d),
            pl.BlockSpec((1, 6, d), lambda i: (i // tiles_per_batch, 0, 0)),
            pl.BlockSpec((1, d), lambda i: (0, 0)),
            pl.BlockSpec((d, p_in), lambda i: (0, 0)),
            tab_spec, tab_spec, tab_spec, tab_spec, tab_spec, tab_spec,
        ],
        out_specs=out_specs,
        out_shape=out_shapes,
        compiler_params=_cparams("arbitrary"),
        name="input_projection",
    )(x2d, mod_l, g1.reshape(1, d), w_in_bf16, *rope_a, *rope_d)


def _head_masks():
    lane = np.arange(GROUP_WIDTH)
    head = np.stack([(lane // HEAD_DIM == h) for h in range(N_HEADS)]).astype(np.float32)
    diff = np.stack([(lane // DIFF_DH == j) for j in range(2 * N_HEADS)]).astype(np.float32)
    return (jnp.asarray(head[:, None, :], dtype=BF16), jnp.asarray(head[:, None, :], dtype=F32),
            jnp.asarray(diff[:, None, :], dtype=BF16))


DIFF_ONES_ROWS = 16


def _diff_attn_kernel(lam_init, q_ref, k_ref, v_ref, lp_ref, g_ref, dmask_ref, gmat_ref, o_ref, vt_ref, ot_ref):
    seq = k_ref.shape[0]

    @pl.when(pl.program_id(1) == 0)
    def _():
        vt = jnp.transpose(v_ref[...].astype(F32))
        for h in range(N_HEADS):
            vt_ref[h, 0:HEAD_DIM, :] = vt[h * HEAD_DIM:(h + 1) * HEAD_DIM].astype(BF16)
            vt_ref[h, HEAD_DIM:HEAD_DIM + DIFF_ONES_ROWS, :] = jnp.ones((DIFF_ONES_ROWS, seq), BF16)

    q = q_ref[...]
    k = k_ref[...]
    lp = lp_ref[...]
    lam = (jnp.exp(jnp.sum(lp[0:1] * lp[1:2], axis=-1, keepdims=True))
           - jnp.exp(jnp.sum(lp[2:3] * lp[3:4], axis=-1, keepdims=True)) + lam_init)

    def scores(j):
        return _dot_nt(k, q * dmask_ref[j])

    def weights(st):
        return jnp.exp2(st - jnp.max(st, axis=0, keepdims=True)).astype(BF16)

    def attend(j, e):
        num = _dot(vt_ref[j // 2], e)
        return num[0:HEAD_DIM] / num[HEAD_DIM:HEAD_DIM + 1]

    n_pairs = 2 * N_HEADS
    outs = [None] * n_pairs
    st_next = scores(0)
    e_prev = None
    for j in range(n_pairs):
        st = st_next
        if j + 1 < n_pairs:
            st_next = scores(j + 1)
        e = weights(st)
        if e_prev is not None:
            outs[j - 1] = attend(j - 1, e_prev)
        e_prev = e
    outs[n_pairs - 1] = attend(n_pairs - 1, e_prev)
    for h in range(N_HEADS):
        ot_ref[h * HEAD_DIM:(h + 1) * HEAD_DIM, :] = outs[2 * h] - lam * outs[2 * h + 1]
    o = jnp.transpose(ot_ref[...])
    ms = _group_mean(o * o, gmat_ref[...], HEAD_DIM)
    o_ref[...] = ((o * lax.rsqrt(ms + EPS) * g_ref[...]) * (1.0 - lam_init)).astype(BF16)


def _diff_attention(qa, ka, va, lam_params, subln_g, layer_idx, batch, seq, masks, gmat):
    t, gw = qa.shape
    tq = 512
    nq = seq // tq
    lam_init = 0.8 - 0.6 * math.exp(-0.3 * layer_idx)
    _, _, dmask = masks
    g_tiled = jnp.tile(subln_g, N_HEADS).reshape(1, gw)
    kv_spec = pl.BlockSpec((seq, gw), lambda b, i: (b, 0))
    full = lambda shape: pl.BlockSpec(shape, lambda b, i: (0,) * len(shape))
    return pl.pallas_call(
        functools.partial(_diff_attn_kernel, lam_init),
        grid=(batch, nq),
        in_specs=[
            pl.BlockSpec((tq, gw), lambda b, i: (b * nq + i, 0)),
            kv_spec, kv_spec,
            full(lam_params.shape), full((1, gw)), full(dmask.shape), full(gmat.shape),
        ],
        out_specs=pl.BlockSpec((tq, gw), lambda b, i: (b * nq + i, 0)),
        out_shape=jax.ShapeDtypeStruct((t, gw), BF16),
        scratch_shapes=[pltpu.VMEM((N_HEADS, HEAD_DIM + DIFF_ONES_ROWS, seq), BF16), pltpu.VMEM((gw, tq), F32)],
        compiler_params=_cparams("arbitrary", "arbitrary"),
        name="diff_attention",
    )(qa, ka, va, lam_params, g_tiled, dmask, gmat)


def _na_bias_table(rpb):
    w = GRID_W
    n_heads = rpb.shape[0]
    cq = np.arange(w)[:, None]
    ck = np.arange(w)[None, :]
    dc = np.clip(ck - cq, -(NA_KW - 1), NA_KW - 1) + NA_KW - 1
    col_start = np.clip(cq - NA_KW // 2, 0, w - NA_KW)
    col_ok = (ck >= col_start) & (ck < col_start + NA_KW)
    onehot = jnp.asarray(dc[:, :, None] == np.arange(2 * NA_KW - 1), dtype=F32)
    toep = jnp.einsum('qkd,hrd->hrqk', onehot, rpb.astype(F32), precision=lax.Precision.HIGHEST)
    toep = jnp.where(col_ok[None, None], toep, NEG_INF)
    tabs = jnp.stack([toep[:, NA_KH - 1 - off:2 * NA_KH - 1 - off] for off in range(NA_KH)])
    return jnp.transpose(tabs, (0, 1, 3, 2, 4)).reshape(NA_KH, n_heads * w, NA_KH * w)


def _na_kernel(q_ref, k_ref, v_ref, tab_ref, hmask_ref, hmask_f32_ref, o_ref):
    w = GRID_W
    nk = NA_KH * w

    def scores(i):
        row_start = jnp.clip(i - NA_KH // 2, 0, GRID_ROWS - NA_KH)
        kstart = pl.multiple_of(row_start * w, w)
        q = q_ref[pl.ds(pl.multiple_of(i * w, w), w), :]
        qs = jnp.concatenate([q * hmask_ref[h] for h in range(N_HEADS)], axis=0)
        return _dot_nt(qs, k_ref[pl.ds(kstart, nk), :]) + tab_ref[i - row_start], kstart

    def attend(i, s, kstart):
        m = jnp.max(s, axis=-1, keepdims=True)
        e = jnp.exp(s - m)
        p = e * (1.0 / jnp.sum(e, axis=-1, keepdims=True))
        pv = _dot(p.astype(BF16), v_ref[pl.ds(kstart, nk), :])
        o = pv[(N_HEADS - 1) * w:N_HEADS * w]
        for h in range(N_HEADS - 2, -1, -1):
            o = jnp.where(hmask_f32_ref[h] > 0.5, pv[h * w:(h + 1) * w], o)
        o_ref[pl.ds(pl.multiple_of(i * w, w), w), :] = o.astype(BF16)

    def grid_rows(p, carry):
        rows = [NA_ROWS_PER_ITER * p + u for u in range(NA_ROWS_PER_ITER)]
        staged = [scores(i) for i in rows]
        for i, (s, kstart) in zip(rows, staged):
            attend(i, s, kstart)
        return carry

    lax.fori_loop(0, GRID_ROWS // NA_ROWS_PER_ITER, grid_rows, 0)


def _neighborhood_attention(qb, kb, vb, bias_table, batch, seq, masks):
    t, gw = qb.shape
    hmask, hmask_f32, _ = masks
    seq_spec = pl.BlockSpec((seq, gw), lambda b: (b, 0))
    full = lambda shape: pl.BlockSpec(shape, lambda b: (0,) * len(shape))
    return pl.pallas_call(
        _na_kernel,
        grid=(batch,),
        in_specs=[seq_spec, seq_spec, seq_spec, full(bias_table.shape), full(hmask.shape), full(hmask_f32.shape)],
        out_specs=seq_spec,
        out_shape=jax.ShapeDtypeStruct((t, gw), BF16),
        compiler_params=_cparams("arbitrary"),
        name="neighborhood_attention",
    )(qb, kb, vb, bias_table, hmask, hmask_f32)


SUBLANES = 8
CONV_PAD = 16
CONV_CHUNK = 128


def _conv_kernel(pc_ref, w_ref, b_ref, gn_ref, bn_ref, gmat_ref, o_ref, zp_ref, zs_ref):
    seq, ch = o_ref.shape
    a = pc_ref[:, 0:ch]
    gate = pc_ref[:, ch:2 * ch]
    zp_ref[0:CONV_PAD, :] = jnp.zeros((CONV_PAD, ch), F32)
    zp_ref[CONV_PAD + seq:2 * CONV_PAD + seq, :] = jnp.zeros((CONV_PAD, ch), F32)
    zp_ref[CONV_PAD:CONV_PAD + seq, :] = a * jax.nn.sigmoid(gate)
    span = seq + 2 * CONV_PAD - SUBLANES
    for b in range(1, SUBLANES):
        zs_ref[b - 1, 0:span, :] = zp_ref[b:b + span, :]
    gmat = gmat_ref[...]
    first = CONV_PAD - CONV_K // 2
    for c in range(seq // CONV_CHUNK):
        r0 = c * CONV_CHUNK
        acc = jnp.zeros((CONV_CHUNK, ch), F32)
        for j in range(CONV_K):
            shift, aligned = (first + j) % SUBLANES, r0 + (first + j) // SUBLANES * SUBLANES
            src = zp_ref if shift == 0 else zs_ref.at[shift - 1]
            acc = acc + w_ref[j] * src[aligned:aligned + CONV_CHUNK, :]
        z = acc + b_ref[...]
        mu = _group_mean(z, gmat, CONV_GROUP_CH)
        dz = z - mu
        var = _group_mean(dz * dz, gmat, CONV_GROUP_CH)
        zn = dz * lax.rsqrt(var + EPS) * gn_ref[...] + bn_ref[...]
        o_ref[r0:r0 + CONV_CHUNK, :] = (zn * jax.nn.sigmoid(zn)).astype(BF16)


def _conformer_conv(pc, w_dw, b_dw, g_n, b_n, batch, seq, gmat):
    t = pc.shape[0]
    ch = GROUP_WIDTH
    full = lambda shape: pl.BlockSpec(shape, lambda b: (0,) * len(shape))
    return pl.pallas_call(
        _conv_kernel,
        grid=(batch,),
        in_specs=[
            pl.BlockSpec((seq, 2 * ch), lambda b: (b, 0)),
            full((CONV_K, 1, ch)), full((1, ch)), full((1, ch)), full((1, ch)), full(gmat.shape),
        ],
        out_specs=pl.BlockSpec((seq, ch), lambda b: (b, 0)),
        out_shape=jax.ShapeDtypeStruct((t, ch), BF16),
        scratch_shapes=[pltpu.VMEM((seq + 2 * CONV_PAD, ch), F32),
                        pltpu.VMEM((SUBLANES - 1, seq + 2 * CONV_PAD, ch), F32)],
        compiler_params=_cparams("arbitrary"),
        name="conformer_conv",
    )(pc, w_dw.reshape(CONV_K, 1, ch), b_dw.reshape(1, ch), g_n.reshape(1, ch), b_n.reshape(1, ch), gmat)


DIL_QB = 128
DIL_KB = 256
DIL_UNITS_PER_ITER = 2


def _strided_rows(first, count, stride):
    return pl.ds(first, count) if stride == 1 else pl.ds(first, count, stride=stride)


def _load_rows(ref, rows):
    return jnp.concatenate([ref[j, rows, :] for j in range(ref.shape[0])], axis=1)


def _store_rows(ref, rows, val):
    for j in range(ref.shape[0]):
        ref[j, rows, :] = val[:, j * 128:(j + 1) * 128]


def _dilated_kernel(q_ref, k_ref, v_ref, hmask_ref, hmask_f32_ref, o_ref, acc_ref, m_ref, l_ref):
    seq = q_ref.shape[1]
    for branch, (window, dil) in enumerate(DILATED_CFG):
        n_side = window // (2 * dil)
        sub_len = seq // dil
        qb = min(DIL_QB, sub_len)
        kb = min(DIL_KB, sub_len)
        rel = (lax.broadcasted_iota(jnp.int32, (N_HEADS * qb, kb), 1)
               - (lax.broadcasted_iota(jnp.int32, (N_HEADS * qb, kb), 0) & (qb - 1)))

        def scores(r, c, dil=dil, n_side=n_side, sub_len=sub_len, qb=qb, kb=kb, rel=rel):
            l0 = c * qb
            kl0 = jnp.clip(l0 - n_side, 0, sub_len - kb)
            if dil == 1:
                l0, kl0 = pl.multiple_of(l0, qb), pl.multiple_of(kl0, n_side)
            q_rows = _strided_rows(r + dil * l0, qb, dil)
            k_rows = _strided_rows(r + dil * kl0, kb, dil)
            q = _load_rows(q_ref, q_rows).astype(BF16)
            qs = jnp.concatenate([q * hmask_ref[h] for h in range(N_HEADS)], axis=0)
            s = _dot_nt(qs, _load_rows(k_ref, k_rows).astype(BF16))
            return jnp.where(jnp.abs(rel + (kl0 - l0)) <= n_side, s, NEG_INF), q_rows, k_rows

        def attend(s, q_rows, k_rows, qb=qb, branch=branch):
            m = jnp.max(s, axis=-1, keepdims=True)
            e = jnp.exp(s - m)
            l = jnp.sum(e, axis=-1, keepdims=True)
            pv = _dot(e.astype(BF16), _load_rows(v_ref, k_rows).astype(BF16))

            def unstack(a):
                out = a[(N_HEADS - 1) * qb:N_HEADS * qb]
                for h in range(N_HEADS - 2, -1, -1):
                    out = jnp.where(hmask_f32_ref[h] > 0.5, a[h * qb:(h + 1) * qb], out)
                return out

            acc_new, m_new, l_new = unstack(pv), unstack(m), unstack(l)
            if branch == 0:
                _store_rows(acc_ref, q_rows, acc_new)
                _store_rows(m_ref, q_rows, m_new)
                _store_rows(l_ref, q_rows, l_new)
            else:
                m_old = _load_rows(m_ref, q_rows)
                m_max = jnp.maximum(m_old, m_new)
                w_old = jnp.exp(m_old - m_max)
                w_new = jnp.exp(m_new - m_max)
                _store_rows(acc_ref, q_rows, _load_rows(acc_ref, q_rows) * w_old + acc_new * w_new)
                _store_rows(l_ref, q_rows, _load_rows(l_ref, q_rows) * w_old + l_new * w_new)
                _store_rows(m_ref, q_rows, m_max)

        def run(units, scores=scores, attend=attend):
            staged = [scores(r, c) for r, c in units]
            for item in staged:
                attend(*item)

        n_blocks = sub_len // qb
        if n_blocks >= DIL_UNITS_PER_ITER:
            for r in range(dil):
                def block_group(p, carry, r=r, run=run):
                    run([(r, DIL_UNITS_PER_ITER * p + u) for u in range(DIL_UNITS_PER_ITER)])
                    return carry

                lax.fori_loop(0, n_blocks // DIL_UNITS_PER_ITER, block_group, 0)
        else:
            for r0 in range(0, dil, DIL_UNITS_PER_ITER):
                run([(r0 + u, 0) for u in range(DIL_UNITS_PER_ITER)])
    for j in range(acc_ref.shape[0]):
        o_ref[:, j * 128:(j + 1) * 128] = (acc_ref[j] / l_ref[j]).astype(BF16)


def _dilated_attention(qd, kd, vd, batch, seq, masks):
    tiles, t, _ = qd.shape
    gw = tiles * 128
    hmask, hmask_f32, _ = masks
    in_spec = pl.BlockSpec((tiles, seq, 128), lambda b: (0, b, 0))
    stat = pltpu.VMEM((tiles, seq, 128), F32)
    return pl.pallas_call(
        _dilated_kernel,
        grid=(batch,),
        in_specs=[
            in_spec, in_spec, in_spec,
            pl.BlockSpec(hmask.shape, lambda b: (0, 0, 0)),
            pl.BlockSpec(hmask_f32.shape, lambda b: (0, 0, 0)),
        ],
        out_specs=pl.BlockSpec((seq, gw), lambda b: (b, 0)),
        out_shape=jax.ShapeDtypeStruct((t, gw), BF16),
        scratch_shapes=[stat, stat, stat],
        compiler_params=_cparams("arbitrary"),
        name="dilated_attention",
    )(qd, kd, vd, hmask, hmask_f32)


def _outproj_kernel(ya_ref, yb_ref, yc_ref, yd_ref, w_ref, x_ref, mod_ref, g_ref, wr_ref,
                    x1_ref, h2_lo_ref, h2_hi_ref, lg_ref):
    gw = GROUP_WIDTH
    mix = _dot(ya_ref[...], w_ref[0:gw, :])
    mix = mix + _dot(yb_ref[...], w_ref[gw:2 * gw, :])
    mix = mix + _dot(yc_ref[...], w_ref[2 * gw:3 * gw, :])
    mix = mix + _dot(yd_ref[...], w_ref[3 * gw:4 * gw, :])
    mod = mod_ref[0]
    x1 = x_ref[...] + mod[2:3] * mix
    x1_ref[...] = x1
    ms = jnp.mean(x1 * x1, axis=-1, keepdims=True)
    h2 = (x1 * lax.rsqrt(ms + EPS) * g_ref[...]) * (1.0 + mod[4:5]) + mod[3:4]
    packed = _pack_bf16_pairs(h2)
    quarter = h2_lo_ref.shape[1]
    h2_lo_ref[...] = packed[:, :quarter]
    h2_hi_ref[...] = packed[:, quarter:]
    lg_ref[...] = _dot(h2.astype(BF16), wr_ref[...])


def _output_projection(ys, w_out_bf16, x2d, mod_l, g2, w_router, seq):
    t, d = x2d.shape
    tm = 512
    tiles_per_batch = seq // tm
    gw = GROUP_WIDTH
    row_spec = lambda width: pl.BlockSpec((tm, width), lambda i: (i, 0))
    return pl.pallas_call(
        _outproj_kernel,
        grid=(t // tm,),
        in_specs=[
            row_spec(gw), row_spec(gw), row_spec(gw), row_spec(gw),
            pl.BlockSpec((d, d), lambda i: (0, 0)),
            row_spec(d),
            pl.BlockSpec((1, 6, d), lambda i: (i // tiles_per_batch, 0, 0)),
            pl.BlockSpec((1, d), lambda i: (0, 0)),
            pl.BlockSpec((d, ROUTER_LANES), lambda i: (0, 0)),
        ],
        out_specs=[row_spec(d), row_spec(d // 4), row_spec(d // 4), row_spec(ROUTER_LANES)],
        out_shape=[jax.ShapeDtypeStruct((t, d), F32), jax.ShapeDtypeStruct((t, d // 4), jnp.uint32),
                   jax.ShapeDtypeStruct((t, d // 4), jnp.uint32), jax.ShapeDtypeStruct((t, ROUTER_LANES), F32)],
        compiler_params=_cparams("arbitrary"),
        name="output_projection",
    )(*ys, w_out_bf16, x2d, mod_l, g2.reshape(1, d), w_router)


def _routing_kernel(lg_ref, info_ref, cnt_ref, carry_ref):
    tr = lg_ref.shape[0]

    @pl.when(pl.program_id(0) == 0)
    def _():
        carry_ref[...] = jnp.zeros_like(carry_ref)

    lg = lg_ref[...]
    lane = lax.broadcasted_iota(jnp.int32, lg.shape, 1).astype(F32)
    big = float(ROUTER_LANES)
    glog = jnp.where(lane < N_GROUPS, lg, -jnp.inf)
    gmax = jnp.max(glog, axis=-1, keepdims=True)
    p_grp = 1.0 / jnp.sum(jnp.exp(glog - gmax), axis=-1, keepdims=True)
    grp = jnp.min(jnp.where(glog == gmax, lane, big), axis=-1, keepdims=True)
    lo = N_GROUPS + EXPERTS_PER_GROUP * grp
    elog = jnp.where((lane >= lo) & (lane < lo + EXPERTS_PER_GROUP), lg, -jnp.inf)
    v1 = jnp.max(elog, axis=-1, keepdims=True)
    i1 = jnp.min(jnp.where(elog == v1, lane, big), axis=-1, keepdims=True)
    elog2 = jnp.where(lane == i1, -jnp.inf, elog)
    v2 = jnp.max(elog2, axis=-1, keepdims=True)
    i2 = jnp.min(jnp.where(elog2 == v2, lane, big), axis=-1, keepdims=True)
    d = jnp.exp(v2 - v1)
    gate1 = p_grp / (1.0 + d)
    gate2 = p_grp * d / (1.0 + d)
    sel1 = lane == i1
    sel2 = lane == i2
    sel = jnp.where(sel1 | sel2, 1.0, 0.0)
    row = lax.broadcasted_iota(jnp.int32, (tr, tr), 0)
    col = lax.broadcasted_iota(jnp.int32, (tr, tr), 1)
    before = jnp.where(col < row, 1.0, 0.0).astype(BF16)
    rank = _dot(before, sel.astype(BF16)) + carry_ref[...]
    r1 = jnp.sum(jnp.where(sel1, rank, 0.0), axis=-1, keepdims=True)
    r2 = jnp.sum(jnp.where(sel2, rank, 0.0), axis=-1, keepdims=True)
    carry_ref[...] += jnp.sum(sel, axis=0, keepdims=True)
    cnt_ref[...] = carry_ref[...]
    info = jnp.zeros_like(lg)
    for idx, val in enumerate((i1 - N_GROUPS, i2 - N_GROUPS, r1, r2, gate1, gate2)):
        info = jnp.where(lane == idx, val, info)
    info_ref[...] = info


def _routing(logits):
    t = logits.shape[0]
    tr = 512
    return pl.pallas_call(
        _routing_kernel,
        grid=(t // tr,),
        in_specs=[pl.BlockSpec((tr, ROUTER_LANES), lambda i: (i, 0))],
        out_specs=[pl.BlockSpec((tr, ROUTER_LANES), lambda i: (i, 0)),
                   pl.BlockSpec((1, ROUTER_LANES), lambda i: (0, 0))],
        out_shape=[jax.ShapeDtypeStruct((t, ROUTER_LANES), F32),
                   jax.ShapeDtypeStruct((1, ROUTER_LANES), F32)],
        scratch_shapes=[pltpu.VMEM((1, ROUTER_LANES), F32)],
        compiler_params=_cparams("arbitrary"),
        name="routing",
    )(logits)


SC_GATHER_WINDOW = 128


def _gather_rows(table, indices):
    n = indices.shape[0]
    width = table.shape[1]
    mesh = plsc.VectorSubcoreMesh(core_axis_name="core", subcore_axis_name="subcore")

    @pl.kernel(out_type=jax.ShapeDtypeStruct((n, width), table.dtype), mesh=mesh, scratch_types=[],
               name="moe_row_gather")
    def gather_kernel(table_hbm, idx_hbm, out_hbm):
        def body(idx_vmem, out_vmem):
            pltpu.sync_copy(table_hbm.at[idx_vmem.at[0]], out_vmem)

        pltpu.emit_pipeline(
            body,
            grid=(n // SC_GATHER_WINDOW,),
            in_specs=[pl.BlockSpec((1, SC_GATHER_WINDOW), index_map=lambda i: (0, i))],
            out_specs=[pl.BlockSpec((SC_GATHER_WINDOW, width), index_map=lambda i: (i, 0))],
            core_axis_name=("core", "subcore"),
            dimension_semantics=(pltpu.PARALLEL,),
        )(idx_hbm, out_hbm)

    return gather_kernel(table, indices.reshape(1, n))


def _sorted_sources(dest1, dest2, n_rows):
    t = dest1.shape[0]
    tokens = jnp.arange(t, dtype=jnp.int32)
    return jnp.zeros((n_rows,), jnp.int32).at[jnp.concatenate([dest1, dest2])].set(
        jnp.concatenate([tokens, tokens]), unique_indices=True)


def _expert_kernel(blk_e_ref, nvalid_ref, xs_lo_ref, xs_hi_ref, wg_ref, wu_ref, wd_ref, ys_lo_ref, ys_hi_ref,
                   wg_bf, wu_bf, wd_bf):
    j = pl.program_id(0)
    half = ys_lo_ref.shape[1]

    @pl.when((j == 0) | (blk_e_ref[j] != blk_e_ref[jnp.maximum(j - 1, 0)]))
    def _():
        wg_bf[...] = wg_ref[0, 0].astype(BF16)
        wu_bf[...] = wu_ref[0, 0].astype(BF16)
        wd_bf[...] = wd_ref[0, 0].astype(BF16)

    @pl.when(j < nvalid_ref[0])
    def _():
        xb = _unpack_bf16_pairs(jnp.concatenate([xs_lo_ref[...], xs_hi_ref[...]], axis=1)).astype(BF16)
        gate = _dot(xb, wg_bf[...])
        up = _dot(xb, wu_bf[...])
        hdn = (gate * jax.nn.sigmoid(gate)) * up
        packed = _pack_bf16_pairs(_dot(hdn.astype(BF16), wd_bf[...]))
        ys_lo_ref[...] = packed[:, :half]
        ys_hi_ref[...] = packed[:, half:]

    @pl.when(j >= nvalid_ref[0])
    def _():
        ys_lo_ref[...] = jnp.zeros_like(ys_lo_ref)
        ys_hi_ref[...] = jnp.zeros_like(ys_hi_ref)


def _expert_mlp(xs_lo, xs_hi, blk_e, nvalid, w_gate, w_up, w_down, layer):
    n_rows, quarter = xs_lo.shape
    nblk = n_rows // MOE_BLOCK
    d, de = w_gate.shape[2:]

    def x_map(j, be, nv):
        return (jnp.minimum(j, nv[0] - 1), 0)

    w_map = lambda j, be, nv: (layer, be[j], 0, 0)
    grid_spec = pltpu.PrefetchScalarGridSpec(
        num_scalar_prefetch=2,
        grid=(nblk,),
        in_specs=[
            pl.BlockSpec((MOE_BLOCK, quarter), x_map),
            pl.BlockSpec((MOE_BLOCK, quarter), x_map),
            pl.BlockSpec((1, 1, d, de), w_map),
            pl.BlockSpec((1, 1, d, de), w_map),
            pl.BlockSpec((1, 1, de, d), w_map),
        ],
        out_specs=[pl.BlockSpec((MOE_BLOCK, quarter), lambda j, be, nv: (j, 0))] * 2,
        scratch_shapes=[pltpu.VMEM((d, de), BF16), pltpu.VMEM((d, de), BF16), pltpu.VMEM((de, d), BF16)],
    )
    return pl.pallas_call(
        _expert_kernel,
        grid_spec=grid_spec,
        out_shape=[jax.ShapeDtypeStruct((n_rows, quarter), jnp.uint32)] * 2,
        compiler_params=_cparams("arbitrary"),
        name="expert_mlp",
    )(blk_e, nvalid, xs_lo, xs_hi, w_gate, w_up, w_down)


def _combine_kernel(final_norm, x_ref, mod_ref, info_ref, gf_ref, y1_lo_ref, y1_hi_ref, y2_lo_ref, y2_hi_ref,
                    o_ref):
    info = info_ref[...]
    y1 = _unpack_bf16_pairs(jnp.concatenate([y1_lo_ref[...], y1_hi_ref[...]], axis=1))
    y2 = _unpack_bf16_pairs(jnp.concatenate([y2_lo_ref[...], y2_hi_ref[...]], axis=1))
    moe = info[:, 4:5] * y1 + info[:, 5:6] * y2
    x2 = x_ref[...] + mod_ref[0][5:6] * moe
    if final_norm:
        ms = jnp.mean(x2 * x2, axis=-1, keepdims=True)
        x2 = x2 * lax.rsqrt(ms + EPS) * gf_ref[...]
    o_ref[...] = x2


def _combine(x1, mod_l, info, ys, dest1, dest2, g_final, seq, final_norm):
    t, d = x1.shape
    tc = 512
    tiles_per_batch = seq // tc
    n_tiles = t // tc
    dest = jnp.concatenate([dest1, dest2])
    g_lo, g_hi = _gather_rows(ys[0], dest), _gather_rows(ys[1], dest)
    quarter = ys[0].shape[1]
    row_spec = lambda width: pl.BlockSpec((tc, width), lambda i: (i, 0))
    slot2_spec = pl.BlockSpec((tc, quarter), lambda i: (i + n_tiles, 0))
    return pl.pallas_call(
        functools.partial(_combine_kernel, final_norm),
        grid=(n_tiles,),
        in_specs=[
            row_spec(d),
            pl.BlockSpec((1, 6, d), lambda i: (i // tiles_per_batch, 0, 0)),
            row_spec(ROUTER_LANES),
            pl.BlockSpec((1, d), lambda i: (0, 0)),
            row_spec(quarter), row_spec(quarter), slot2_spec, slot2_spec,
        ],
        out_specs=row_spec(d),
        out_shape=jax.ShapeDtypeStruct((t, d), F32),
        compiler_params=_cparams("arbitrary"),
        name="moe_combine",
    )(x1, mod_l, info, g_final.reshape(1, d), g_lo, g_hi, g_lo, g_hi)


def _router_weights(w_rg, w_re):
    d = w_rg.shape[0]
    w_experts = jnp.transpose(w_re, (1, 0, 2)).reshape(d, N_EXPERTS)
    pad = jnp.zeros((d, ROUTER_LANES - N_GROUPS - N_EXPERTS), F32)
    return jnp.concatenate([w_rg, w_experts, pad], axis=1)


def _block_layout(counts_row, n_blocks):
    counts = counts_row[0, N_GROUPS:N_GROUPS + N_EXPERTS].astype(jnp.int32)
    padded = (counts + MOE_BLOCK - 1) // MOE_BLOCK * MOE_BLOCK
    pad_end = jnp.cumsum(padded)
    pad_start = pad_end - padded
    starts = jnp.arange(n_blocks, dtype=jnp.int32) * MOE_BLOCK
    blk_e = jnp.minimum(jnp.sum((pad_end[None, :] <= starts[:, None]).astype(jnp.int32), axis=1), N_EXPERTS - 1)
    nvalid = (pad_end[-1:] // MOE_BLOCK).astype(jnp.int32)
    return counts, pad_start, pad_end, blk_e, nvalid


def _destinations(info, pad_start):
    ids = info[:, 0:4].astype(jnp.int32)
    experts = jnp.arange(N_EXPERTS, dtype=jnp.int32)[None, :]

    def segment_start(e):
        return jnp.sum(jnp.where(e[:, None] == experts, pad_start[None, :], 0), axis=1)

    return segment_start(ids[:, 0]) + ids[:, 2], segment_start(ids[:, 1]) + ids[:, 3]


def kernel(x, c, w_ada, b_ada, g_norm1, g_norm2, w_in, diff_lambda, diff_subln, na_rpb, conv_dw, conv_b,
           conv_norm_g, conv_norm_b, w_out, w_router_group, w_router_expert, w_exp_gate, w_exp_up,
           w_exp_down, g_final):
    batch, seq, d = x.shape
    depth = w_ada.shape[0]
    t = batch * seq
    assert d == D_MODEL and seq == GRID_ROWS * GRID_W
    n_rows = t * 2 + N_EXPERTS * MOE_BLOCK

    mod = _ada_modulation(c, w_ada, b_ada).reshape(depth, batch, 6, d)
    rope_a = _rope_tables(seq, DIFF_DH)
    rope_d = _rope_tables(seq, HEAD_DIM)
    masks = _head_masks()
    gmat = _block_diag_ones(GROUP_WIDTH, HEAD_DIM)

    x2d = x.reshape(t, d)
    for l in range(depth):
        mod_l = mod[l]
        qa, ka, va, qb, kb, vb, pc, qd, kd, vd = _input_projection(
            x2d, mod_l, g_norm1[l], w_in[l].astype(BF16), rope_a, rope_d, seq)
        ya = _diff_attention(qa, ka, va, diff_lambda[l], diff_subln[l], l, batch, seq, masks, gmat)
        yb = _neighborhood_attention(qb, kb, vb, _na_bias_table(na_rpb[l]), batch, seq, masks)
        yc = _conformer_conv(pc, conv_dw[l], conv_b[l], conv_norm_g[l], conv_norm_b[l], batch, seq, gmat)
        yd = _dilated_attention(qd, kd, vd, batch, seq, masks)
        x1, h2_lo, h2_hi, logits = _output_projection(
            (ya, yb, yc, yd), w_out[l].astype(BF16), x2d, mod_l, g_norm2[l],
            _router_weights(w_router_group[l], w_router_expert[l]).astype(BF16), seq)
        info, counts = _routing(logits)
        counts, pad_start, pad_end, blk_e, nvalid = _block_layout(counts, n_rows // MOE_BLOCK)
        dest1, dest2 = _destinations(info, pad_start)
        sources = _sorted_sources(dest1, dest2, n_rows)
        ys = _expert_mlp(_gather_rows(h2_lo, sources), _gather_rows(h2_hi, sources), blk_e, nvalid,
                         w_exp_gate, w_exp_up, w_exp_down, l)
        x2d = _combine(x1, mod_l, info, ys, dest1, dest2, g_final, seq, final_norm=(l == depth - 1))
    return x2d.reshape(batch, seq, d)
```

```python
import functools
import math

import numpy as np
import jax
import jax.numpy as jnp
from jax import lax
from jax.experimental import pallas as pl
from jax.experimental.pallas import tpu as pltpu
from jax.experimental.pallas import tpu_sc as plsc

F32 = jnp.float32
BF16 = jnp.bfloat16

D_MODEL = 1024
GROUP_WIDTH = 256
HEAD_DIM = 64
N_HEADS = 4
DIFF_DH = 32
CONV_K = 31
CONV_GROUP_CH = 64
GRID_W = 64
NA_KH = 8
NA_KW = 16
GRID_ROWS = 32
NA_ROWS_PER_ITER = 4
ROPE_THETA = 10000.0
N_GROUPS = 4
EXPERTS_PER_GROUP = 8
N_EXPERTS = 32
D_EXPERT = 512
MOE_BLOCK = 256
EPS = 1e-6
NEG_INF = -1e30
LOG2E = 1.4426950408889634
ROUTER_LANES = 128
DILATED_CFG = ((128, 1), (512, 4), (2048, 16))
DIFF_EXP2_SCALE = (DIFF_DH ** -0.5) * LOG2E

VMEM_LIMIT = 56 * 1024 * 1024


def _cparams(*sem):
    return pltpu.CompilerParams(dimension_semantics=sem, vmem_limit_bytes=VMEM_LIMIT)


def _dot(a, b):
    return jnp.dot(a, b, preferred_element_type=F32)


def _dot_nt(a, b):
    return lax.dot_general(a, b, (((1,), (1,)), ((), ())), preferred_element_type=F32)


def _split(a):
    hi = a.astype(BF16)
    lo = (a - hi.astype(F32)).astype(BF16)
    return hi, lo


def _dot3(a, b):
    ah, al = _split(a)
    bh, bl = _split(b)
    return _dot(ah, bh) + (_dot(ah, bl) + _dot(al, bh))


def _group_mean(v, gmat, width):
    hi, lo = _split(v)
    return (_dot(hi, gmat) + _dot(lo, gmat)) * (1.0 / width)


HIGH_HALF = 0xFFFF0000


def _pack_bf16_pairs(a):
    n = a.shape[1] // 2
    bits = lax.bitcast_convert_type(a.astype(BF16).astype(F32), jnp.uint32)
    return (bits[:, :n] >> 16) | (bits[:, n:] & jnp.uint32(HIGH_HALF))


def _unpack_bf16_pairs(u):
    lo = lax.bitcast_convert_type(u << 16, F32)
    hi = lax.bitcast_convert_type(u & jnp.uint32(HIGH_HALF), F32)
    return jnp.concatenate([lo, hi], axis=1)


def _block_diag_ones(n, width):
    idx = np.arange(n) // width
    return jnp.asarray((idx[:, None] == idx[None, :]).astype(np.float32), dtype=BF16)


def _ada_kernel(c_ref, w_ref, b_ref, o_ref):
    c = c_ref[...]
    ca = c * jax.nn.sigmoid(c)
    o_ref[0] = _dot3(ca, w_ref[0]) + b_ref[0]


def _ada_modulation(c, w_ada, b_ada):
    depth, d, n = w_ada.shape
    b = c.shape[0]
    bn = 1024
    return pl.pallas_call(
        _ada_kernel,
        grid=(depth, n // bn),
        in_specs=[
            pl.BlockSpec((b, d), lambda l, j: (0, 0)),
            pl.BlockSpec((1, d, bn), lambda l, j: (l, 0, j)),
            pl.BlockSpec((1, 1, bn), lambda l, j: (l, 0, j)),
        ],
        out_specs=pl.BlockSpec((1, b, bn), lambda l, j: (l, 0, j)),
        out_shape=jax.ShapeDtypeStruct((depth, b, n), F32),
        compiler_params=_cparams("arbitrary", "arbitrary"),
        name="ada_modulation",
    )(c, w_ada, b_ada.reshape(depth, 1, n))


def _rope_tables(seq, dim):
    half = dim // 2
    inv = ROPE_THETA ** (-jnp.arange(0, dim, 2, dtype=F32) / dim)
    ang = jnp.arange(seq, dtype=F32)[:, None] * inv[None, :]
    cos, sin = jnp.cos(ang), jnp.sin(ang)
    reps = 128 // dim
    zeros = jnp.zeros_like(sin)
    cos_t = jnp.tile(jnp.concatenate([cos, cos], axis=1), (1, reps))
    sin_hi = jnp.tile(jnp.concatenate([zeros, sin], axis=1), (1, reps))
    sin_lo = jnp.tile(jnp.concatenate([-sin, zeros], axis=1), (1, reps))
    return cos_t, sin_hi, sin_lo


def _rotary(v, cos_t, sin_hi, sin_lo, half):
    outs = []
    for j in range(v.shape[1] // 128):
        vj = v[:, j * 128:(j + 1) * 128]
        outs.append(vj * cos_t + pltpu.roll(vj, half, 1) * sin_hi + pltpu.roll(vj, 128 - half, 1) * sin_lo)
    return jnp.concatenate(outs, axis=1)


def _inproj_kernel(x_ref, mod_ref, g_ref, w_ref, ca_ref, sha_ref, sla_ref, cd_ref, shd_ref, sld_ref,
                   qa_ref, ka_ref, va_ref, qb_ref, kb_ref, vb_ref, pc_ref, qd_ref, kd_ref, vd_ref):
    x = x_ref[...]
    ms = jnp.mean(x * x, axis=-1, keepdims=True)
    y = x * lax.rsqrt(ms + EPS)
    mod = mod_ref[0]
    h = (y * g_ref[...]) * (1.0 + mod[1:2]) + mod[0:1]
    hb = h.astype(BF16)
    gw = GROUP_WIDTH

    def proj(col):
        return _dot(hb, w_ref[:, col * gw:(col + 1) * gw])

    rot_a = functools.partial(_rotary, cos_t=ca_ref[...], sin_hi=sha_ref[...], sin_lo=sla_ref[...],
                              half=DIFF_DH // 2)
    rot_d = functools.partial(_rotary, cos_t=cd_ref[...], sin_hi=shd_ref[...], sin_lo=sld_ref[...],
                              half=HEAD_DIM // 2)
    na_scale = HEAD_DIM ** -0.5
    qa_ref[...] = (rot_a(proj(0)) * DIFF_EXP2_SCALE).astype(BF16)
    ka_ref[...] = rot_a(proj(1)).astype(BF16)
    va_ref[...] = proj(2).astype(BF16)
    qb_ref[...] = (proj(3) * na_scale).astype(BF16)
    kb_ref[...] = proj(4).astype(BF16)
    vb_ref[...] = proj(5).astype(BF16)
    pc_ref[:, 0:gw] = proj(6)
    pc_ref[:, gw:2 * gw] = proj(7)
    for ref, val in ((qd_ref, rot_d(proj(8)) * na_scale), (kd_ref, rot_d(proj(9))), (vd_ref, proj(10))):
        for j in range(gw // 128):
            ref[j] = val[:, j * 128:(j + 1) * 128]


def _input_projection(x2d, mod_l, g1, w_in_bf16, rope_a, rope_d, seq):
    t, d = x2d.shape
    tm = 512
    tiles_per_batch = seq // tm
    p_in = w_in_bf16.shape[1]
    gw = GROUP_WIDTH
    row_spec = lambda width: pl.BlockSpec((tm, width), lambda i: (i, 0))
    tab_spec = pl.BlockSpec((tm, 128), lambda i: (i % tiles_per_batch, 0))
    out_shapes = []
    out_specs = []
    for name in ("qa", "ka", "va", "qb", "kb", "vb", "pc", "qd", "kd", "vd"):
        if name == "pc":
            out_shapes.append(jax.ShapeDtypeStruct((t, 2 * gw), F32))
            out_specs.append(row_spec(2 * gw))
        elif name[1] == "d":
            out_shapes.append(jax.ShapeDtypeStruct((gw // 128, t, 128), F32))
            out_specs.append(pl.BlockSpec((gw // 128, tm, 128), lambda i: (0, i, 0)))
        else:
            out_shapes.append(jax.ShapeDtypeStruct((t, gw), BF16))
            out_specs.append(row_spec(gw))
    return pl.pallas_call(
        _inproj_kernel,
        grid=(t // tm,),
        in_specs=[
            row_spec(d),
            pl.BlockSpec((1, 6, d), lambda i: (i // tiles_per_batch, 0, 0)),
            pl.BlockSpec((1, d), lambda i: (0, 0)),
            pl.BlockSpec((d, p_in), lambda i: (0, 0)),
            tab_spec, tab_spec, tab_spec, tab_spec, tab_spec, tab_spec,
        ],
        out_specs=out_specs,
        out_shape=out_shapes,
        compiler_params=_cparams("arbitrary"),
        name="input_projection",
    )(x2d, mod_l, g1.reshape(1, d), w_in_bf16, *rope_a, *rope_d)


def _head_masks():
    lane = np.arange(GROUP_WIDTH)
    head = np.stack([(lane // HEAD_DIM == h) for h in range(N_HEADS)]).astype(np.float32)
    diff = np.stack([(lane // DIFF_DH == j) for j in range(2 * N_HEADS)]).astype(np.float32)
    return (jnp.asarray(head[:, None, :], dtype=BF16), jnp.asarray(head[:, None, :], dtype=F32),
            jnp.asarray(diff[:, None, :], dtype=BF16))


DIFF_ONES_ROWS = 16


def _diff_attn_kernel(lam_init, q_ref, k_ref, v_ref, lp_ref, g_ref, dmask_ref, gmat_ref, o_ref, vt_ref, ot_ref):
    seq = k_ref.shape[0]

    @pl.when(pl.program_id(1) == 0)
    def _():
        vt = jnp.transpose(v_ref[...].astype(F32))
        for h in range(N_HEADS):
            vt_ref[h, 0:HEAD_DIM, :] = vt[h * HEAD_DIM:(h + 1) * HEAD_DIM].astype(BF16)
            vt_ref[h, HEAD_DIM:HEAD_DIM + DIFF_ONES_ROWS, :] = jnp.ones((DIFF_ONES_ROWS, seq), BF16)

    q = q_ref[...]
    k = k_ref[...]
    lp = lp_ref[...]
    lam = (jnp.exp(jnp.sum(lp[0:1] * lp[1:2], axis=-1, keepdims=True))
           - jnp.exp(jnp.sum(lp[2:3] * lp[3:4], axis=-1, keepdims=True)) + lam_init)

    def scores(j):
        return _dot_nt(k, q * dmask_ref[j])

    def weights(st):
        return jnp.exp2(st - jnp.max(st, axis=0, keepdims=True)).astype(BF16)

    def attend(j, e):
        num = _dot(vt_ref[j // 2], e)
        return num[0:HEAD_DIM] / num[HEAD_DIM:HEAD_DIM + 1]

    n_pairs = 2 * N_HEADS
    outs = [None] * n_pairs
    st_next = scores(0)
    e_prev = None
    for j in range(n_pairs):
        st = st_next
        if j + 1 < n_pairs:
            st_next = scores(j + 1)
        e = weights(st)
        if e_prev is not None:
            outs[j - 1] = attend(j - 1, e_prev)
        e_prev = e
    outs[n_pairs - 1] = attend(n_pairs - 1, e_prev)
    for h in range(N_HEADS):
        ot_ref[h * HEAD_DIM:(h + 1) * HEAD_DIM, :] = outs[2 * h] - lam * outs[2 * h + 1]
    o = jnp.transpose(ot_ref[...])
    ms = _group_mean(o * o, gmat_ref[...], HEAD_DIM)
    o_ref[...] = ((o * lax.rsqrt(ms + EPS) * g_ref[...]) * (1.0 - lam_init)).astype(BF16)


def _diff_attention(qa, ka, va, lam_params, subln_g, layer_idx, batch, seq, masks, gmat):
    t, gw = qa.shape
    tq = 512
    nq = seq // tq
    lam_init = 0.8 - 0.6 * math.exp(-0.3 * layer_idx)
    _, _, dmask = masks
    g_tiled = jnp.tile(subln_g, N_HEADS).reshape(1, gw)
    kv_spec = pl.BlockSpec((seq, gw), lambda b, i: (b, 0))
    full = lambda shape: pl.BlockSpec(shape, lambda b, i: (0,) * len(shape))
    return pl.pallas_call(
        functools.partial(_diff_attn_kernel, lam_init),
        grid=(batch, nq),
        in_specs=[
            pl.BlockSpec((tq, gw), lambda b, i: (b * nq + i, 0)),
            kv_spec, kv_spec,
            full(lam_params.shape), full((1, gw)), full(dmask.shape), full(gmat.shape),
        ],
        out_specs=pl.BlockSpec((tq, gw), lambda b, i: (b * nq + i, 0)),
        out_shape=jax.ShapeDtypeStruct((t, gw), BF16),
        scratch_shapes=[pltpu.VMEM((N_HEADS, HEAD_DIM + DIFF_ONES_ROWS, seq), BF16), pltpu.VMEM((gw, tq), F32)],
        compiler_params=_cparams("arbitrary", "arbitrary"),
        name="diff_attention",
    )(qa, ka, va, lam_params, g_tiled, dmask, gmat)


def _na_bias_table(rpb):
    w = GRID_W
    n_heads = rpb.shape[0]
    cq = np.arange(w)[:, None]
    ck = np.arange(w)[None, :]
    dc = np.clip(ck - cq, -(NA_KW - 1), NA_KW - 1) + NA_KW - 1
    col_start = np.clip(cq - NA_KW // 2, 0, w - NA_KW)
    col_ok = (ck >= col_start) & (ck < col_start + NA_KW)
    onehot = jnp.asarray(dc[:, :, None] == np.arange(2 * NA_KW - 1), dtype=F32)
    toep = jnp.einsum('qkd,hrd->hrqk', onehot, rpb.astype(F32), precision=lax.Precision.HIGHEST)
    toep = jnp.where(col_ok[None, None], toep, NEG_INF)
    tabs = jnp.stack([toep[:, NA_KH - 1 - off:2 * NA_KH - 1 - off] for off in range(NA_KH)])
    return jnp.transpose(tabs, (0, 1, 3, 2, 4)).reshape(NA_KH, n_heads * w, NA_KH * w)


def _na_kernel(q_ref, k_ref, v_ref, tab_ref, hmask_ref, hmask_f32_ref, o_ref):
    w = GRID_W
    nk = NA_KH * w

    def scores(i):
        row_start = jnp.clip(i - NA_KH // 2, 0, GRID_ROWS - NA_KH)
        kstart = pl.multiple_of(row_start * w, w)
        q = q_ref[pl.ds(pl.multiple_of(i * w, w), w), :]
        qs = jnp.concatenate([q * hmask_ref[h] for h in range(N_HEADS)], axis=0)
        return _dot_nt(qs, k_ref[pl.ds(kstart, nk), :]) + tab_ref[i - row_start], kstart

    def attend(i, s, kstart):
        m = jnp.max(s, axis=-1, keepdims=True)
        e = jnp.exp(s - m)
        p = e * (1.0 / jnp.sum(e, axis=-1, keepdims=True))
        pv = _dot(p.astype(BF16), v_ref[pl.ds(kstart, nk), :])
        o = pv[(N_HEADS - 1) * w:N_HEADS * w]
        for h in range(N_HEADS - 2, -1, -1):
            o = jnp.where(hmask_f32_ref[h] > 0.5, pv[h * w:(h + 1) * w], o)
        o_ref[pl.ds(pl.multiple_of(i * w, w), w), :] = o.astype(BF16)

    def grid_rows(p, carry):
        rows = [NA_ROWS_PER_ITER * p + u for u in range(NA_ROWS_PER_ITER)]
        staged = [scores(i) for i in rows]
        for i, (s, kstart) in zip(rows, staged):
            attend(i, s, kstart)
        return carry

    lax.fori_loop(0, GRID_ROWS // NA_ROWS_PER_ITER, grid_rows, 0)


def _neighborhood_attention(qb, kb, vb, bias_table, batch, seq, masks):
    t, gw = qb.shape
    hmask, hmask_f32, _ = masks
    seq_spec = pl.BlockSpec((seq, gw), lambda b: (b, 0))
    full = lambda shape: pl.BlockSpec(shape, lambda b: (0,) * len(shape))
    return pl.pallas_call(
        _na_kernel,
        grid=(batch,),
        in_specs=[seq_spec, seq_spec, seq_spec, full(bias_table.shape), full(hmask.shape), full(hmask_f32.shape)],
        out_specs=seq_spec,
        out_shape=jax.ShapeDtypeStruct((t, gw), BF16),
        compiler_params=_cparams("arbitrary"),
        name="neighborhood_attention",
    )(qb, kb, vb, bias_table, hmask, hmask_f32)


SUBLANES = 8
CONV_PAD = 16
CONV_CHUNK = 128


def _conv_kernel(pc_ref, w_ref, b_ref, gn_ref, bn_ref, gmat_ref, o_ref, zp_ref, zs_ref):
    seq, ch = o_ref.shape
    a = pc_ref[:, 0:ch]
    gate = pc_ref[:, ch:2 * ch]
    zp_ref[0:CONV_PAD, :] = jnp.zeros((CONV_PAD, ch), F32)
    zp_ref[CONV_PAD + seq:2 * CONV_PAD + seq, :] = jnp.zeros((CONV_PAD, ch), F32)
    zp_ref[CONV_PAD:CONV_PAD + seq, :] = a * jax.nn.sigmoid(gate)
    span = seq + 2 * CONV_PAD - SUBLANES
    for b in range(1, SUBLANES):
        zs_ref[b - 1, 0:span, :] = zp_ref[b:b + span, :]
    gmat = gmat_ref[...]
    first = CONV_PAD - CONV_K // 2
    for c in range(seq // CONV_CHUNK):
        r0 = c * CONV_CHUNK
        acc = jnp.zeros((CONV_CHUNK, ch), F32)
        for j in range(CONV_K):
            shift, aligned = (first + j) % SUBLANES, r0 + (first + j) // SUBLANES * SUBLANES
            src = zp_ref if shift == 0 else zs_ref.at[shift - 1]
            acc = acc + w_ref[j] * src[aligned:aligned + CONV_CHUNK, :]
        z = acc + b_ref[...]
        mu = _group_mean(z, gmat, CONV_GROUP_CH)
        dz = z - mu
        var = _group_mean(dz * dz, gmat, CONV_GROUP_CH)
        zn = dz * lax.rsqrt(var + EPS) * gn_ref[...] + bn_ref[...]
        o_ref[r0:r0 + CONV_CHUNK, :] = (zn * jax.nn.sigmoid(zn)).astype(BF16)


def _conformer_conv(pc, w_dw, b_dw, g_n, b_n, batch, seq, gmat):
    t = pc.shape[0]
    ch = GROUP_WIDTH
    full = lambda shape: pl.BlockSpec(shape, lambda b: (0,) * len(shape))
    return pl.pallas_call(
        _conv_kernel,
        grid=(batch,),
        in_specs=[
            pl.BlockSpec((seq, 2 * ch), lambda b: (b, 0)),
            full((CONV_K, 1, ch)), full((1, ch)), full((1, ch)), full((1, ch)), full(gmat.shape),
        ],
        out_specs=pl.BlockSpec((seq, ch), lambda b: (b, 0)),
        out_shape=jax.ShapeDtypeStruct((t, ch), BF16),
        scratch_shapes=[pltpu.VMEM((seq + 2 * CONV_PAD, ch), F32),
                        pltpu.VMEM((SUBLANES - 1, seq + 2 * CONV_PAD, ch), F32)],
        compiler_params=_cparams("arbitrary"),
        name="conformer_conv",
    )(pc, w_dw.reshape(CONV_K, 1, ch), b_dw.reshape(1, ch), g_n.reshape(1, ch), b_n.reshape(1, ch), gmat)


DIL_QB = 128
DIL_KB = 256
DIL_UNITS_PER_ITER = 2


def _strided_rows(first, count, stride):
    return pl.ds(first, count) if stride == 1 else pl.ds(first, count, stride=stride)


def _load_rows(ref, rows):
    return jnp.concatenate([ref[j, rows, :] for j in range(ref.shape[0])], axis=1)


def _store_rows(ref, rows, val):
    for j in range(ref.shape[0]):
        ref[j, rows, :] = val[:, j * 128:(j + 1) * 128]


def _dilated_kernel(q_ref, k_ref, v_ref, hmask_ref, hmask_f32_ref, o_ref, acc_ref, m_ref, l_ref):
    seq = q_ref.shape[1]
    for branch, (window, dil) in enumerate(DILATED_CFG):
        n_side = window // (2 * dil)
        sub_len = seq // dil
        qb = min(DIL_QB, sub_len)
        kb = min(DIL_KB, sub_len)
        rel = (lax.broadcasted_iota(jnp.int32, (N_HEADS * qb, kb), 1)
               - (lax.broadcasted_iota(jnp.int32, (N_HEADS * qb, kb), 0) & (qb - 1)))

        def scores(r, c, dil=dil, n_side=n_side, sub_len=sub_len, qb=qb, kb=kb, rel=rel):
            l0 = c * qb
            kl0 = jnp.clip(l0 - n_side, 0, sub_len - kb)
            if dil == 1:
                l0, kl0 = pl.multiple_of(l0, qb), pl.multiple_of(kl0, n_side)
            q_rows = _strided_rows(r + dil * l0, qb, dil)
            k_rows = _strided_rows(r + dil * kl0, kb, dil)
            q = _load_rows(q_ref, q_rows).astype(BF16)
            qs = jnp.concatenate([q * hmask_ref[h] for h in range(N_HEADS)], axis=0)
            s = _dot_nt(qs, _load_rows(k_ref, k_rows).astype(BF16))
            return jnp.where(jnp.abs(rel + (kl0 - l0)) <= n_side, s, NEG_INF), q_rows, k_rows

        def attend(s, q_rows, k_rows, qb=qb, branch=branch):
            m = jnp.max(s, axis=-1, keepdims=True)
            e = jnp.exp(s - m)
            l = jnp.sum(e, axis=-1, keepdims=True)
            pv = _dot(e.astype(BF16), _load_rows(v_ref, k_rows).astype(BF16))

            def unstack(a):
                out = a[(N_HEADS - 1) * qb:N_HEADS * qb]
                for h in range(N_HEADS - 2, -1, -1):
                    out = jnp.where(hmask_f32_ref[h] > 0.5, a[h * qb:(h + 1) * qb], out)
                return out

            acc_new, m_new, l_new = unstack(pv), unstack(m), unstack(l)
            if branch == 0:
                _store_rows(acc_ref, q_rows, acc_new)
                _store_rows(m_ref, q_rows, m_new)
                _store_rows(l_ref, q_rows, l_new)
            else:
                m_old = _load_rows(m_ref, q_rows)
                m_max = jnp.maximum(m_old, m_new)
                w_old = jnp.exp(m_old - m_max)
                w_new = jnp.exp(m_new - m_max)
                _store_rows(acc_ref, q_rows, _load_rows(acc_ref, q_rows) * w_old + acc_new * w_new)
                _store_rows(l_ref, q_rows, _load_rows(l_ref, q_rows) * w_old + l_new * w_new)
                _store_rows(m_ref, q_rows, m_max)

        def run(units, scores=scores, attend=attend):
            staged = [scores(r, c) for r, c in units]
            for item in staged:
                attend(*item)

        n_blocks = sub_len // qb
        if n_blocks >= DIL_UNITS_PER_ITER:
            for r in range(dil):
                def block_group(p, carry, r=r, run=run):
                    run([(r, DIL_UNITS_PER_ITER * p + u) for u in range(DIL_UNITS_PER_ITER)])
                    return carry

                lax.fori_loop(0, n_blocks // DIL_UNITS_PER_ITER, block_group, 0)
        else:
            for r0 in range(0, dil, DIL_UNITS_PER_ITER):
                run([(r0 + u, 0) for u in range(DIL_UNITS_PER_ITER)])
    for j in range(acc_ref.shape[0]):
        o_ref[:, j * 128:(j + 1) * 128] = (acc_ref[j] / l_ref[j]).astype(BF16)


def _dilated_attention(qd, kd, vd, batch, seq, masks):
    tiles, t, _ = qd.shape
    gw = tiles * 128
    hmask, hmask_f32, _ = masks
    in_spec = pl.BlockSpec((tiles, seq, 128), lambda b: (0, b, 0))
    stat = pltpu.VMEM((tiles, seq, 128), F32)
    return pl.pallas_call(
        _dilated_kernel,
        grid=(batch,),
        in_specs=[
            in_spec, in_spec, in_spec,
            pl.BlockSpec(hmask.shape, lambda b: (0, 0, 0)),
            pl.BlockSpec(hmask_f32.shape, lambda b: (0, 0, 0)),
        ],
        out_specs=pl.BlockSpec((seq, gw), lambda b: (b, 0)),
        out_shape=jax.ShapeDtypeStruct((t, gw), BF16),
        scratch_shapes=[stat, stat, stat],
        compiler_params=_cparams("arbitrary"),
        name="dilated_attention",
    )(qd, kd, vd, hmask, hmask_f32)


def _outproj_kernel(ya_ref, yb_ref, yc_ref, yd_ref, w_ref, x_ref, mod_ref, g_ref, wr_ref,
                    x1_ref, h2_lo_ref, h2_hi_ref, lg_ref):
    gw = GROUP_WIDTH
    mix = _dot(ya_ref[...], w_ref[0:gw, :])
    mix = mix + _dot(yb_ref[...], w_ref[gw:2 * gw, :])
    mix = mix + _dot(yc_ref[...], w_ref[2 * gw:3 * gw, :])
    mix = mix + _dot(yd_ref[...], w_ref[3 * gw:4 * gw, :])
    mod = mod_ref[0]
    x1 = x_ref[...] + mod[2:3] * mix
    x1_ref[...] = x1
    ms = jnp.mean(x1 * x1, axis=-1, keepdims=True)
    h2 = (x1 * lax.rsqrt(ms + EPS) * g_ref[...]) * (1.0 + mod[4:5]) + mod[3:4]
    packed = _pack_bf16_pairs(h2)
    quarter = h2_lo_ref.shape[1]
    h2_lo_ref[...] = packed[:, :quarter]
    h2_hi_ref[...] = packed[:, quarter:]
    lg_ref[...] = _dot(h2.astype(BF16), wr_ref[...])


def _output_projection(ys, w_out_bf16, x2d, mod_l, g2, w_router, seq):
    t, d = x2d.shape
    tm = 512
    tiles_per_batch = seq // tm
    gw = GROUP_WIDTH
    row_spec = lambda width: pl.BlockSpec((tm, width), lambda i: (i, 0))
    return pl.pallas_call(
        _outproj_kernel,
        grid=(t // tm,),
        in_specs=[
            row_spec(gw), row_spec(gw), row_spec(gw), row_spec(gw),
            pl.BlockSpec((d, d), lambda i: (0, 0)),
            row_spec(d),
            pl.BlockSpec((1, 6, d), lambda i: (i // tiles_per_batch, 0, 0)),
            pl.BlockSpec((1, d), lambda i: (0, 0)),
            pl.BlockSpec((d, ROUTER_LANES), lambda i: (0, 0)),
        ],
        out_specs=[row_spec(d), row_spec(d // 4), row_spec(d // 4), row_spec(ROUTER_LANES)],
        out_shape=[jax.ShapeDtypeStruct((t, d), F32), jax.ShapeDtypeStruct((t, d // 4), jnp.uint32),
                   jax.ShapeDtypeStruct((t, d // 4), jnp.uint32), jax.ShapeDtypeStruct((t, ROUTER_LANES), F32)],
        compiler_params=_cparams("arbitrary"),
        name="output_projection",
    )(*ys, w_out_bf16, x2d, mod_l, g2.reshape(1, d), w_router)


def _routing_kernel(lg_ref, info_ref, cnt_ref, carry_ref):
    tr = lg_ref.shape[0]

    @pl.when(pl.program_id(0) == 0)
    def _():
        carry_ref[...] = jnp.zeros_like(carry_ref)

    lg = lg_ref[...]
    lane = lax.broadcasted_iota(jnp.int32, lg.shape, 1).astype(F32)
    big = float(ROUTER_LANES)
    glog = jnp.where(lane < N_GROUPS, lg, -jnp.inf)
    gmax = jnp.max(glog, axis=-1, keepdims=True)
    p_grp = 1.0 / jnp.sum(jnp.exp(glog - gmax), axis=-1, keepdims=True)
    grp = jnp.min(jnp.where(glog == gmax, lane, big), axis=-1, keepdims=True)
    lo = N_GROUPS + EXPERTS_PER_GROUP * grp
    elog = jnp.where((lane >= lo) & (lane < lo + EXPERTS_PER_GROUP), lg, -jnp.inf)
    v1 = jnp.max(elog, axis=-1, keepdims=True)
    i1 = jnp.min(jnp.where(elog == v1, lane, big), axis=-1, keepdims=True)
    elog2 = jnp.where(lane == i1, -jnp.inf, elog)
    v2 = jnp.max(elog2, axis=-1, keepdims=True)
    i2 = jnp.min(jnp.where(elog2 == v2, lane, big), axis=-1, keepdims=True)
    d = jnp.exp(v2 - v1)
    gate1 = p_grp / (1.0 + d)
    gate2 = p_grp * d / (1.0 + d)
    sel1 = lane == i1
    sel2 = lane == i2
    sel = jnp.where(sel1 | sel2, 1.0, 0.0)
    row = lax.broadcasted_iota(jnp.int32, (tr, tr), 0)
    col = lax.broadcasted_iota(jnp.int32, (tr, tr), 1)
    before = jnp.where(col < row, 1.0, 0.0).astype(BF16)
    rank = _dot(before, sel.astype(BF16)) + carry_ref[...]
    r1 = jnp.sum(jnp.where(sel1, rank, 0.0), axis=-1, keepdims=True)
    r2 = jnp.sum(jnp.where(sel2, rank, 0.0), axis=-1, keepdims=True)
    carry_ref[...] += jnp.sum(sel, axis=0, keepdims=True)
    cnt_ref[...] = carry_ref[...]
    info = jnp.zeros_like(lg)
    for idx, val in enumerate((i1 - N_GROUPS, i2 - N_GROUPS, r1, r2, gate1, gate2)):
        info = jnp.where(lane == idx, val, info)
    info_ref[...] = info


def _routing(logits):
    t = logits.shape[0]
    tr = 512
    return pl.pallas_call(
        _routing_kernel,
        grid=(t // tr,),
        in_specs=[pl.BlockSpec((tr, ROUTER_LANES), lambda i: (i, 0))],
        out_specs=[pl.BlockSpec((tr, ROUTER_LANES), lambda i: (i, 0)),
                   pl.BlockSpec((1, ROUTER_LANES), lambda i: (0, 0))],
        out_shape=[jax.ShapeDtypeStruct((t, ROUTER_LANES), F32),
                   jax.ShapeDtypeStruct((1, ROUTER_LANES), F32)],
        scratch_shapes=[pltpu.VMEM((1, ROUTER_LANES), F32)],
        compiler_params=_cparams("arbitrary"),
        name="routing",
    )(logits)


SC_GATHER_WINDOW = 128


def _gather_rows(table, indices):
    n = indices.shape[0]
    width = table.shape[1]
    mesh = plsc.VectorSubcoreMesh(core_axis_name="core", subcore_axis_name="subcore")

    @pl.kernel(out_type=jax.ShapeDtypeStruct((n, width), table.dtype), mesh=mesh, scratch_types=[],
               name="moe_row_gather")
    def gather_kernel(table_hbm, idx_hbm, out_hbm):
        def body(idx_vmem, out_vmem):
            pltpu.sync_copy(table_hbm.at[idx_vmem.at[0]], out_vmem)

        pltpu.emit_pipeline(
            body,
            grid=(n // SC_GATHER_WINDOW,),
            in_specs=[pl.BlockSpec((1, SC_GATHER_WINDOW), index_map=lambda i: (0, i))],
            out_specs=[pl.BlockSpec((SC_GATHER_WINDOW, width), index_map=lambda i: (i, 0))],
            core_axis_name=("core", "subcore"),
            dimension_semantics=(pltpu.PARALLEL,),
        )(idx_hbm, out_hbm)

    return gather_kernel(table, indices.reshape(1, n))


def _scatter_rows(table, indices):
    n = indices.shape[0]
    rows, width = table.shape
    mesh = plsc.VectorSubcoreMesh(core_axis_name="core", subcore_axis_name="subcore")
    table_windows = rows // SC_GATHER_WINDOW

    @pl.kernel(out_type=jax.ShapeDtypeStruct((n, width), table.dtype), mesh=mesh, scratch_types=[],
               name="moe_row_scatter")
    def scatter_kernel(table_hbm, idx_hbm, out_hbm):
        def body(rows_vmem, idx_vmem):
            pltpu.sync_copy(rows_vmem, out_hbm.at[idx_vmem.at[0]])

        pltpu.emit_pipeline(
            body,
            grid=(n // SC_GATHER_WINDOW,),
            in_specs=[pl.BlockSpec((SC_GATHER_WINDOW, width), index_map=lambda i: (i % table_windows, 0)),
                      pl.BlockSpec((1, SC_GATHER_WINDOW), index_map=lambda i: (0, i))],
            out_specs=[],
            core_axis_name=("core", "subcore"),
            dimension_semantics=(pltpu.PARALLEL,),
        )(table_hbm, idx_hbm)

    return scatter_kernel(table, indices.reshape(1, n))


def _dispatch_targets(dest1, dest2, counts, pad_start, pad_end, n_rows):
    n_pad = n_rows - dest1.shape[0] - dest2.shape[0]
    pad_first = jnp.concatenate([pad_start + counts, pad_end[-1:]])
    pad_count = jnp.concatenate([pad_end - pad_start - counts, n_rows - pad_end[-1:]])
    cum = jnp.cumsum(pad_count)
    k = jnp.arange(n_pad, dtype=jnp.int32)
    seg = jnp.sum((cum[None, :] <= k[:, None]).astype(jnp.int32), axis=1)
    onehot = seg[:, None] == jnp.arange(cum.shape[0], dtype=jnp.int32)[None, :]
    offset = k - jnp.sum(jnp.where(onehot, (cum - pad_count)[None, :], 0), axis=1)
    pad_rows = jnp.sum(jnp.where(onehot, pad_first[None, :], 0), axis=1) + offset
    return jnp.concatenate([dest1, dest2, pad_rows])


def _expert_kernel(blk_e_ref, nvalid_ref, xs_lo_ref, xs_hi_ref, wg_ref, wu_ref, wd_ref, ys_lo_ref, ys_hi_ref,
                   wg_bf, wu_bf, wd_bf):
    j = pl.program_id(0)
    half = ys_lo_ref.shape[1]

    @pl.when((j == 0) | (blk_e_ref[j] != blk_e_ref[jnp.maximum(j - 1, 0)]))
    def _():
        wg_bf[...] = wg_ref[0, 0].astype(BF16)
        wu_bf[...] = wu_ref[0, 0].astype(BF16)
        wd_bf[...] = wd_ref[0, 0].astype(BF16)

    @pl.when(j < nvalid_ref[0])
    def _():
        xb = _unpack_bf16_pairs(jnp.concatenate([xs_lo_ref[...], xs_hi_ref[...]], axis=1)).astype(BF16)
        gate = _dot(xb, wg_bf[...])
        up = _dot(xb, wu_bf[...])
        hdn = (gate * jax.nn.sigmoid(gate)) * up
        packed = _pack_bf16_pairs(_dot(hdn.astype(BF16), wd_bf[...]))
        ys_lo_ref[...] = packed[:, :half]
        ys_hi_ref[...] = packed[:, half:]

    @pl.when(j >= nvalid_ref[0])
    def _():
        ys_lo_ref[...] = jnp.zeros_like(ys_lo_ref)
        ys_hi_ref[...] = jnp.zeros_like(ys_hi_ref)


def _expert_mlp(xs_lo, xs_hi, blk_e, nvalid, w_gate, w_up, w_down, layer):
    n_rows, quarter = xs_lo.shape
    nblk = n_rows // MOE_BLOCK
    d, de = w_gate.shape[2:]

    def x_map(j, be, nv):
        return (jnp.minimum(j, nv[0] - 1), 0)

    w_map = lambda j, be, nv: (layer, be[j], 0, 0)
    grid_spec = pltpu.PrefetchScalarGridSpec(
        num_scalar_prefetch=2,
        grid=(nblk,),
        in_specs=[
            pl.BlockSpec((MOE_BLOCK, quarter), x_map),
            pl.BlockSpec((MOE_BLOCK, quarter), x_map),
            pl.BlockSpec((1, 1, d, de), w_map),
            pl.BlockSpec((1, 1, d, de), w_map),
            pl.BlockSpec((1, 1, de, d), w_map),
        ],
        out_specs=[pl.BlockSpec((MOE_BLOCK, quarter), lambda j, be, nv: (j, 0))] * 2,
        scratch_shapes=[pltpu.VMEM((d, de), BF16), pltpu.VMEM((d, de), BF16), pltpu.VMEM((de, d), BF16)],
    )
    return pl.pallas_call(
        _expert_kernel,
        grid_spec=grid_spec,
        out_shape=[jax.ShapeDtypeStruct((n_rows, quarter), jnp.uint32)] * 2,
        compiler_params=_cparams("arbitrary"),
        name="expert_mlp",
    )(blk_e, nvalid, xs_lo, xs_hi, w_gate, w_up, w_down)


def _combine_kernel(final_norm, x_ref, mod_ref, info_ref, gf_ref, y1_lo_ref, y1_hi_ref, y2_lo_ref, y2_hi_ref,
                    o_ref):
    info = info_ref[...]
    y1 = _unpack_bf16_pairs(jnp.concatenate([y1_lo_ref[...], y1_hi_ref[...]], axis=1))
    y2 = _unpack_bf16_pairs(jnp.concatenate([y2_lo_ref[...], y2_hi_ref[...]], axis=1))
    moe = info[:, 4:5] * y1 + info[:, 5:6] * y2
    x2 = x_ref[...] + mod_ref[0][5:6] * moe
    if final_norm:
        ms = jnp.mean(x2 * x2, axis=-1, keepdims=True)
        x2 = x2 * lax.rsqrt(ms + EPS) * gf_ref[...]
    o_ref[...] = x2


def _combine(x1, mod_l, info, ys, dest1, dest2, g_final, seq, final_norm):
    t, d = x1.shape
    tc = 512
    tiles_per_batch = seq // tc
    n_tiles = t // tc
    dest = jnp.concatenate([dest1, dest2])
    g_lo, g_hi = _gather_rows(ys[0], dest), _gather_rows(ys[1], dest)
    quarter = ys[0].shape[1]
    row_spec = lambda width: pl.BlockSpec((tc, width), lambda i: (i, 0))
    slot2_spec = pl.BlockSpec((tc, quarter), lambda i: (i + n_tiles, 0))
    return pl.pallas_call(
        functools.partial(_combine_kernel, final_norm),
        grid=(n_tiles,),
        in_specs=[
            row_spec(d),
            pl.BlockSpec((1, 6, d), lambda i: (i // tiles_per_batch, 0, 0)),
            row_spec(ROUTER_LANES),
            pl.BlockSpec((1, d), lambda i: (0, 0)),
            row_spec(quarter), row_spec(quarter), slot2_spec, slot2_spec,
        ],
        out_specs=row_spec(d),
        out_shape=jax.ShapeDtypeStruct((t, d), F32),
        compiler_params=_cparams("arbitrary"),
        name="moe_combine",
    )(x1, mod_l, info, g_final.reshape(1, d), g_lo, g_hi, g_lo, g_hi)


def _router_weights(w_rg, w_re):
    d = w_rg.shape[0]
    w_experts = jnp.transpose(w_re, (1, 0, 2)).reshape(d, N_EXPERTS)
    pad = jnp.zeros((d, ROUTER_LANES - N_GROUPS - N_EXPERTS), F32)
    return jnp.concatenate([w_rg, w_experts, pad], axis=1)


def _block_layout(counts_row, n_blocks):
    counts = counts_row[0, N_GROUPS:N_GROUPS + N_EXPERTS].astype(jnp.int32)
    padded = (counts + MOE_BLOCK - 1) // MOE_BLOCK * MOE_BLOCK
    pad_end = jnp.cumsum(padded)
    pad_start = pad_end - padded
    starts = jnp.arange(n_blocks, dtype=jnp.int32) * MOE_BLOCK
    blk_e = jnp.minimum(jnp.sum((pad_end[None, :] <= starts[:, None]).astype(jnp.int32), axis=1), N_EXPERTS - 1)
    nvalid = (pad_end[-1:] // MOE_BLOCK).astype(jnp.int32)
    return counts, pad_start, pad_end, blk_e, nvalid


def _destinations(info, pad_start):
    ids = info[:, 0:4].astype(jnp.int32)
    experts = jnp.arange(N_EXPERTS, dtype=jnp.int32)[None, :]

    def segment_start(e):
        return jnp.sum(jnp.where(e[:, None] == experts, pad_start[None, :], 0), axis=1)

    return segment_start(ids[:, 0]) + ids[:, 2], segment_start(ids[:, 1]) + ids[:, 3]


def kernel(x, c, w_ada, b_ada, g_norm1, g_norm2, w_in, diff_lambda, diff_subln, na_rpb, conv_dw, conv_b,
           conv_norm_g, conv_norm_b, w_out, w_router_group, w_router_expert, w_exp_gate, w_exp_up,
           w_exp_down, g_final):
    batch, seq, d = x.shape
    depth = w_ada.shape[0]
    t = batch * seq
    assert d == D_MODEL and seq == GRID_ROWS * GRID_W
    n_rows = t * 2 + N_EXPERTS * MOE_BLOCK

    mod = _ada_modulation(c, w_ada, b_ada).reshape(depth, batch, 6, d)
    rope_a = _rope_tables(seq, DIFF_DH)
    rope_d = _rope_tables(seq, HEAD_DIM)
    masks = _head_masks()
    gmat = _block_diag_ones(GROUP_WIDTH, HEAD_DIM)

    x2d = x.reshape(t, d)
    for l in range(depth):
        mod_l = mod[l]
        qa, ka, va, qb, kb, vb, pc, qd, kd, vd = _input_projection(
            x2d, mod_l, g_norm1[l], w_in[l].astype(BF16), rope_a, rope_d, seq)
        ya = _diff_attention(qa, ka, va, diff_lambda[l], diff_subln[l], l, batch, seq, masks, gmat)
        yb = _neighborhood_attention(qb, kb, vb, _na_bias_table(na_rpb[l]), batch, seq, masks)
        yc = _conformer_conv(pc, conv_dw[l], conv_b[l], conv_norm_g[l], conv_norm_b[l], batch, seq, gmat)
        yd = _dilated_attention(qd, kd, vd, batch, seq, masks)
        x1, h2_lo, h2_hi, logits = _output_projection(
            (ya, yb, yc, yd), w_out[l].astype(BF16), x2d, mod_l, g_norm2[l],
            _router_weights(w_router_group[l], w_router_expert[l]).astype(BF16), seq)
        info, counts = _routing(logits)
        counts, pad_start, pad_end, blk_e, nvalid = _block_layout(counts, n_rows // MOE_BLOCK)
        dest1, dest2 = _destinations(info, pad_start)
        targets = _dispatch_targets(dest1, dest2, counts, pad_start, pad_end, n_rows)
        ys = _expert_mlp(_scatter_rows(h2_lo, targets), _scatter_rows(h2_hi, targets), blk_e, nvalid,
                         w_exp_gate, w_exp_up, w_exp_down, l)
        x2d = _combine(x1, mod_l, info, ys, dest1, dest2, g_final, seq, final_norm=(l == depth - 1))
    return x2d.reshape(batch, seq, d)
```

```python
import functools
import math

import numpy as np
import jax
import jax.numpy as jnp
from jax import lax
from jax.experimental import pallas as pl
from jax.experimental.pallas import tpu as pltpu
from jax.experimental.pallas import tpu_sc as plsc

F32 = jnp.float32
BF16 = jnp.bfloat16

D_MODEL = 1024
GROUP_WIDTH = 256
HEAD_DIM = 64
N_HEADS = 4
DIFF_DH = 32
CONV_K = 31
CONV_GROUP_CH = 64
GRID_W = 64
NA_KH = 8
NA_KW = 16
GRID_ROWS = 32
NA_ROWS_PER_ITER = 4
ROPE_THETA = 10000.0
N_GROUPS = 4
EXPERTS_PER_GROUP = 8
N_EXPERTS = 32
D_EXPERT = 512
MOE_BLOCK = 256
EPS = 1e-6
NEG_INF = -1e30
LOG2E = 1.4426950408889634
ROUTER_LANES = 128
DILATED_CFG = ((128, 1), (512, 4), (2048, 16))
DIFF_EXP2_SCALE = (DIFF_DH ** -0.5) * LOG2E

VMEM_LIMIT = 56 * 1024 * 1024


def _cparams(*sem):
    return pltpu.CompilerParams(dimension_semantics=sem, vmem_limit_bytes=VMEM_LIMIT)


def _dot(a, b):
    return jnp.dot(a, b, preferred_element_type=F32)


def _dot_nt(a, b):
    return lax.dot_general(a, b, (((1,), (1,)), ((), ())), preferred_element_type=F32)


def _split(a):
    hi = a.astype(BF16)
    lo = (a - hi.astype(F32)).astype(BF16)
    return hi, lo


def _dot3(a, b):
    ah, al = _split(a)
    bh, bl = _split(b)
    return _dot(ah, bh) + (_dot(ah, bl) + _dot(al, bh))


def _group_mean(v, gmat, width):
    hi, lo = _split(v)
    return (_dot(hi, gmat) + _dot(lo, gmat)) * (1.0 / width)


HIGH_HALF = 0xFFFF0000


def _pack_bf16_pairs(a):
    n = a.shape[1] // 2
    bits = lax.bitcast_convert_type(a.astype(BF16).astype(F32), jnp.uint32)
    return (bits[:, :n] >> 16) | (bits[:, n:] & jnp.uint32(HIGH_HALF))


def _unpack_bf16_pairs(u):
    lo = lax.bitcast_convert_type(u << 16, F32)
    hi = lax.bitcast_convert_type(u & jnp.uint32(HIGH_HALF), F32)
    return jnp.concatenate([lo, hi], axis=1)


def _block_diag_ones(n, width):
    idx = np.arange(n) // width
    return jnp.asarray((idx[:, None] == idx[None, :]).astype(np.float32), dtype=BF16)


def _ada_kernel(c_ref, w_ref, b_ref, o_ref):
    c = c_ref[...]
    ca = c * jax.nn.sigmoid(c)
    o_ref[0] = _dot3(ca, w_ref[0]) + b_ref[0]


def _ada_modulation(c, w_ada, b_ada):
    depth, d, n = w_ada.shape
    b = c.shape[0]
    bn = 1024
    return pl.pallas_call(
        _ada_kernel,
        grid=(depth, n // bn),
        in_specs=[
            pl.BlockSpec((b, d), lambda l, j: (0, 0)),
            pl.BlockSpec((1, d, bn), lambda l, j: (l, 0, j)),
            pl.BlockSpec((1, 1, bn), lambda l, j: (l, 0, j)),
        ],
        out_specs=pl.BlockSpec((1, b, bn), lambda l, j: (l, 0, j)),
        out_shape=jax.ShapeDtypeStruct((depth, b, n), F32),
        compiler_params=_cparams("arbitrary", "arbitrary"),
        name="ada_modulation",
    )(c, w_ada, b_ada.reshape(depth, 1, n))


def _rope_tables(seq, dim):
    half = dim // 2
    inv = ROPE_THETA ** (-jnp.arange(0, dim, 2, dtype=F32) / dim)
    ang = jnp.arange(seq, dtype=F32)[:, None] * inv[None, :]
    cos, sin = jnp.cos(ang), jnp.sin(ang)
    reps = 128 // dim
    zeros = jnp.zeros_like(sin)
    cos_t = jnp.tile(jnp.concatenate([cos, cos], axis=1), (1, reps))
    sin_hi = jnp.tile(jnp.concatenate([zeros, sin], axis=1), (1, reps))
    sin_lo = jnp.tile(jnp.concatenate([-sin, zeros], axis=1), (1, reps))
    return cos_t, sin_hi, sin_lo


def _rotary(v, cos_t, sin_hi, sin_lo, half):
    outs = []
    for j in range(v.shape[1] // 128):
        vj = v[:, j * 128:(j + 1) * 128]
        outs.append(vj * cos_t + pltpu.roll(vj, half, 1) * sin_hi + pltpu.roll(vj, 128 - half, 1) * sin_lo)
    return jnp.concatenate(outs, axis=1)


def _inproj_kernel(x_ref, *rest):
    _project_tokens(x_ref[...], *rest)


def _combine_inproj_kernel(x1_ref, modp_ref, info_ref, y1_lo_ref, y1_hi_ref, y2_lo_ref, y2_hi_ref, *rest):
    proj_refs, x2_ref, out_refs = rest[:9], rest[9], rest[10:]
    x = _moe_residual(x1_ref, modp_ref, info_ref, y1_lo_ref, y1_hi_ref, y2_lo_ref, y2_hi_ref)
    x2_ref[...] = x
    _project_tokens(x, *proj_refs, *out_refs)


def _moe_residual(x_ref, mod_ref, info_ref, y1_lo_ref, y1_hi_ref, y2_lo_ref, y2_hi_ref):
    info = info_ref[...]
    y1 = _unpack_bf16_pairs(jnp.concatenate([y1_lo_ref[...], y1_hi_ref[...]], axis=1))
    y2 = _unpack_bf16_pairs(jnp.concatenate([y2_lo_ref[...], y2_hi_ref[...]], axis=1))
    return x_ref[...] + mod_ref[0][5:6] * (info[:, 4:5] * y1 + info[:, 5:6] * y2)


def _project_tokens(x, mod_ref, g_ref, w_ref, ca_ref, sha_ref, sla_ref, cd_ref, shd_ref, sld_ref,
                    qa_ref, ka_ref, va_ref, qb_ref, kb_ref, vb_ref, pc_ref, qd_ref, kd_ref, vd_ref):
    ms = jnp.mean(x * x, axis=-1, keepdims=True)
    y = x * lax.rsqrt(ms + EPS)
    mod = mod_ref[0]
    h = (y * g_ref[...]) * (1.0 + mod[1:2]) + mod[0:1]
    hb = h.astype(BF16)
    gw = GROUP_WIDTH

    def proj(col):
        return _dot(hb, w_ref[:, col * gw:(col + 1) * gw])

    rot_a = functools.partial(_rotary, cos_t=ca_ref[...], sin_hi=sha_ref[...], sin_lo=sla_ref[...],
                              half=DIFF_DH // 2)
    rot_d = functools.partial(_rotary, cos_t=cd_ref[...], sin_hi=shd_ref[...], sin_lo=sld_ref[...],
                              half=HEAD_DIM // 2)
    na_scale = HEAD_DIM ** -0.5
    qa_ref[...] = (rot_a(proj(0)) * DIFF_EXP2_SCALE).astype(BF16)
    ka_ref[...] = rot_a(proj(1)).astype(BF16)
    va_ref[...] = proj(2).astype(BF16)
    qb_ref[...] = (proj(3) * na_scale).astype(BF16)
    kb_ref[...] = proj(4).astype(BF16)
    vb_ref[...] = proj(5).astype(BF16)
    pc_ref[:, 0:gw] = proj(6)
    pc_ref[:, gw:2 * gw] = proj(7)
    for ref, val in ((qd_ref, rot_d(proj(8)) * na_scale), (kd_ref, rot_d(proj(9))), (vd_ref, proj(10))):
        for j in range(gw // 128):
            ref[j] = val[:, j * 128:(j + 1) * 128]


def _input_projection(x2d, mod_l, g1, w_in_bf16, rope_a, rope_d, seq, pending=None):
    t, d = x2d.shape if pending is None else pending[0].shape
    tm = 512
    tiles_per_batch = seq // tm
    p_in = w_in_bf16.shape[1]
    gw = GROUP_WIDTH
    row_spec = lambda width: pl.BlockSpec((tm, width), lambda i: (i, 0))
    tab_spec = pl.BlockSpec((tm, 128), lambda i: (i % tiles_per_batch, 0))
    out_shapes = []
    out_specs = []
    for name in ("qa", "ka", "va", "qb", "kb", "vb", "pc", "qd", "kd", "vd"):
        if name == "pc":
            out_shapes.append(jax.ShapeDtypeStruct((t, 2 * gw), F32))
            out_specs.append(row_spec(2 * gw))
        elif name[1] == "d":
            out_shapes.append(jax.ShapeDtypeStruct((gw // 128, t, 128), F32))
            out_specs.append(pl.BlockSpec((gw // 128, tm, 128), lambda i: (0, i, 0)))
        else:
            out_shapes.append(jax.ShapeDtypeStruct((t, gw), BF16))
            out_specs.append(row_spec(gw))
    mod_spec = pl.BlockSpec((1, 6, d), lambda i: (i // tiles_per_batch, 0, 0))
    proj_specs = [
        mod_spec,
        pl.BlockSpec((1, d), lambda i: (0, 0)),
        pl.BlockSpec((d, p_in), lambda i: (0, 0)),
        tab_spec, tab_spec, tab_spec, tab_spec, tab_spec, tab_spec,
    ]
    proj_args = (mod_l, g1.reshape(1, d), w_in_bf16, *rope_a, *rope_d)
    if pending is None:
        return pl.pallas_call(
            _inproj_kernel,
            grid=(t // tm,),
            in_specs=[row_spec(d)] + proj_specs,
            out_specs=out_specs,
            out_shape=out_shapes,
            compiler_params=_cparams("arbitrary"),
            name="input_projection",
        )(x2d, *proj_args)
    x1, mod_prev, info, g_lo, g_hi = pending
    quarter = g_lo.shape[1]
    n_tiles = t // tm
    slot2_spec = pl.BlockSpec((tm, quarter), lambda i: (i + n_tiles, 0))
    return pl.pallas_call(
        _combine_inproj_kernel,
        grid=(n_tiles,),
        in_specs=[row_spec(d), mod_spec, row_spec(ROUTER_LANES), row_spec(quarter), row_spec(quarter),
                  slot2_spec, slot2_spec] + proj_specs,
        out_specs=[row_spec(d)] + out_specs,
        out_shape=[jax.ShapeDtypeStruct((t, d), F32)] + out_shapes,
        compiler_params=_cparams("arbitrary"),
        name="combine_input_projection",
    )(x1, mod_prev, info, g_lo, g_hi, g_lo, g_hi, *proj_args)


def _head_masks():
    lane = np.arange(GROUP_WIDTH)
    head = np.stack([(lane // HEAD_DIM == h) for h in range(N_HEADS)]).astype(np.float32)
    diff = np.stack([(lane // DIFF_DH == j) for j in range(2 * N_HEADS)]).astype(np.float32)
    return (jnp.asarray(head[:, None, :], dtype=BF16), jnp.asarray(head[:, None, :], dtype=F32),
            jnp.asarray(diff[:, None, :], dtype=BF16))


DIFF_ONES_ROWS = 16


def _diff_attn_kernel(lam_init, q_ref, k_ref, v_ref, lp_ref, g_ref, dmask_ref, gmat_ref, o_ref, vt_ref, ot_ref):
    seq = k_ref.shape[0]

    @pl.when(pl.program_id(1) == 0)
    def _():
        vt = jnp.transpose(v_ref[...].astype(F32))
        for h in range(N_HEADS):
            vt_ref[h, 0:HEAD_DIM, :] = vt[h * HEAD_DIM:(h + 1) * HEAD_DIM].astype(BF16)
            vt_ref[h, HEAD_DIM:HEAD_DIM + DIFF_ONES_ROWS, :] = jnp.ones((DIFF_ONES_ROWS, seq), BF16)

    q = q_ref[...]
    k = k_ref[...]
    lp = lp_ref[...]
    lam = (jnp.exp(jnp.sum(lp[0:1] * lp[1:2], axis=-1, keepdims=True))
           - jnp.exp(jnp.sum(lp[2:3] * lp[3:4], axis=-1, keepdims=True)) + lam_init)

    def scores(j):
        return _dot_nt(k, q * dmask_ref[j])

    def weights(st):
        return jnp.exp2(st - jnp.max(st, axis=0, keepdims=True)).astype(BF16)

    def attend(j, e):
        num = _dot(vt_ref[j // 2], e)
        return num[0:HEAD_DIM] / num[HEAD_DIM:HEAD_DIM + 1]

    n_pairs = 2 * N_HEADS
    outs = [None] * n_pairs
    st_next = scores(0)
    e_prev = None
    for j in range(n_pairs):
        st = st_next
        if j + 1 < n_pairs:
            st_next = scores(j + 1)
        e = weights(st)
        if e_prev is not None:
            outs[j - 1] = attend(j - 1, e_prev)
        e_prev = e
    outs[n_pairs - 1] = attend(n_pairs - 1, e_prev)
    for h in range(N_HEADS):
        ot_ref[h * HEAD_DIM:(h + 1) * HEAD_DIM, :] = outs[2 * h] - lam * outs[2 * h + 1]
    o = jnp.transpose(ot_ref[...])
    ms = _group_mean(o * o, gmat_ref[...], HEAD_DIM)
    o_ref[...] = ((o * lax.rsqrt(ms + EPS) * g_ref[...]) * (1.0 - lam_init)).astype(BF16)


def _diff_attention(qa, ka, va, lam_params, subln_g, layer_idx, batch, seq, masks, gmat):
    t, gw = qa.shape
    tq = 512
    nq = seq // tq
    lam_init = 0.8 - 0.6 * math.exp(-0.3 * layer_idx)
    _, _, dmask = masks
    g_tiled = jnp.tile(subln_g, N_HEADS).reshape(1, gw)
    kv_spec = pl.BlockSpec((seq, gw), lambda b, i: (b, 0))
    full = lambda shape: pl.BlockSpec(shape, lambda b, i: (0,) * len(shape))
    return pl.pallas_call(
        functools.partial(_diff_attn_kernel, lam_init),
        grid=(batch, nq),
        in_specs=[
            pl.BlockSpec((tq, gw), lambda b, i: (b * nq + i, 0)),
            kv_spec, kv_spec,
            full(lam_params.shape), full((1, gw)), full(dmask.shape), full(gmat.shape),
        ],
        out_specs=pl.BlockSpec((tq, gw), lambda b, i: (b * nq + i, 0)),
        out_shape=jax.ShapeDtypeStruct((t, gw), BF16),
        scratch_shapes=[pltpu.VMEM((N_HEADS, HEAD_DIM + DIFF_ONES_ROWS, seq), BF16), pltpu.VMEM((gw, tq), F32)],
        compiler_params=_cparams("arbitrary", "arbitrary"),
        name="diff_attention",
    )(qa, ka, va, lam_params, g_tiled, dmask, gmat)


def _na_bias_table(rpb):
    w = GRID_W
    n_heads = rpb.shape[0]
    cq = np.arange(w)[:, None]
    ck = np.arange(w)[None, :]
    dc = np.clip(ck - cq, -(NA_KW - 1), NA_KW - 1) + NA_KW - 1
    col_start = np.clip(cq - NA_KW // 2, 0, w - NA_KW)
    col_ok = (ck >= col_start) & (ck < col_start + NA_KW)
    onehot = jnp.asarray(dc[:, :, None] == np.arange(2 * NA_KW - 1), dtype=F32)
    toep = jnp.einsum('qkd,hrd->hrqk', onehot, rpb.astype(F32), precision=lax.Precision.HIGHEST)
    toep = jnp.where(col_ok[None, None], toep, NEG_INF)
    tabs = jnp.stack([toep[:, NA_KH - 1 - off:2 * NA_KH - 1 - off] for off in range(NA_KH)])
    return jnp.transpose(tabs, (0, 1, 3, 2, 4)).reshape(NA_KH, n_heads * w, NA_KH * w)


def _na_kernel(q_ref, k_ref, v_ref, tab_ref, hmask_ref, hmask_f32_ref, o_ref):
    w = GRID_W
    nk = NA_KH * w

    def scores(i):
        row_start = jnp.clip(i - NA_KH // 2, 0, GRID_ROWS - NA_KH)
        kstart = pl.multiple_of(row_start * w, w)
        q = q_ref[pl.ds(pl.multiple_of(i * w, w), w), :]
        qs = jnp.concatenate([q * hmask_ref[h] for h in range(N_HEADS)], axis=0)
        return _dot_nt(qs, k_ref[pl.ds(kstart, nk), :]) + tab_ref[i - row_start], kstart

    def attend(i, s, kstart):
        m = jnp.max(s, axis=-1, keepdims=True)
        e = jnp.exp(s - m)
        p = e * (1.0 / jnp.sum(e, axis=-1, keepdims=True))
        pv = _dot(p.astype(BF16), v_ref[pl.ds(kstart, nk), :])
        o = pv[(N_HEADS - 1) * w:N_HEADS * w]
        for h in range(N_HEADS - 2, -1, -1):
            o = jnp.where(hmask_f32_ref[h] > 0.5, pv[h * w:(h + 1) * w], o)
        o_ref[pl.ds(pl.multiple_of(i * w, w), w), :] = o.astype(BF16)

    def grid_rows(p, carry):
        rows = [NA_ROWS_PER_ITER * p + u for u in range(NA_ROWS_PER_ITER)]
        staged = [scores(i) for i in rows]
        for i, (s, kstart) in zip(rows, staged):
            attend(i, s, kstart)
        return carry

    lax.fori_loop(0, GRID_ROWS // NA_ROWS_PER_ITER, grid_rows, 0)


def _neighborhood_attention(qb, kb, vb, bias_table, batch, seq, masks):
    t, gw = qb.shape
    hmask, hmask_f32, _ = masks
    seq_spec = pl.BlockSpec((seq, gw), lambda b: (b, 0))
    full = lambda shape: pl.BlockSpec(shape, lambda b: (0,) * len(shape))
    return pl.pallas_call(
        _na_kernel,
        grid=(batch,),
        in_specs=[seq_spec, seq_spec, seq_spec, full(bias_table.shape), full(hmask.shape), full(hmask_f32.shape)],
        out_specs=seq_spec,
        out_shape=jax.ShapeDtypeStruct((t, gw), BF16),
        compiler_params=_cparams("arbitrary"),
        name="neighborhood_attention",
    )(qb, kb, vb, bias_table, hmask, hmask_f32)


SUBLANES = 8
CONV_PAD = 16
CONV_CHUNK = 128


def _conv_kernel(pc_ref, w_ref, b_ref, gn_ref, bn_ref, gmat_ref, o_ref, zp_ref, zs_ref):
    seq, ch = o_ref.shape
    a = pc_ref[:, 0:ch]
    gate = pc_ref[:, ch:2 * ch]
    zp_ref[0:CONV_PAD, :] = jnp.zeros((CONV_PAD, ch), F32)
    zp_ref[CONV_PAD + seq:2 * CONV_PAD + seq, :] = jnp.zeros((CONV_PAD, ch), F32)
    zp_ref[CONV_PAD:CONV_PAD + seq, :] = a * jax.nn.sigmoid(gate)
    span = seq + 2 * CONV_PAD - SUBLANES
    for b in range(1, SUBLANES):
        zs_ref[b - 1, 0:span, :] = zp_ref[b:b + span, :]
    gmat = gmat_ref[...]
    first = CONV_PAD - CONV_K // 2
    for c in range(seq // CONV_CHUNK):
        r0 = c * CONV_CHUNK
        acc = jnp.zeros((CONV_CHUNK, ch), F32)
        for j in range(CONV_K):
            shift, aligned = (first + j) % SUBLANES, r0 + (first + j) // SUBLANES * SUBLANES
            src = zp_ref if shift == 0 else zs_ref.at[shift - 1]
            acc = acc + w_ref[j] * src[aligned:aligned + CONV_CHUNK, :]
        z = acc + b_ref[...]
        mu = _group_mean(z, gmat, CONV_GROUP_CH)
        dz = z - mu
        var = _group_mean(dz * dz, gmat, CONV_GROUP_CH)
        zn = dz * lax.rsqrt(var + EPS) * gn_ref[...] + bn_ref[...]
        o_ref[r0:r0 + CONV_CHUNK, :] = (zn * jax.nn.sigmoid(zn)).astype(BF16)


def _conformer_conv(pc, w_dw, b_dw, g_n, b_n, batch, seq, gmat):
    t = pc.shape[0]
    ch = GROUP_WIDTH
    full = lambda shape: pl.BlockSpec(shape, lambda b: (0,) * len(shape))
    return pl.pallas_call(
        _conv_kernel,
        grid=(batch,),
        in_specs=[
            pl.BlockSpec((seq, 2 * ch), lambda b: (b, 0)),
            full((CONV_K, 1, ch)), full((1, ch)), full((1, ch)), full((1, ch)), full(gmat.shape),
        ],
        out_specs=pl.BlockSpec((seq, ch), lambda b: (b, 0)),
        out_shape=jax.ShapeDtypeStruct((t, ch), BF16),
        scratch_shapes=[pltpu.VMEM((seq + 2 * CONV_PAD, ch), F32),
                        pltpu.VMEM((SUBLANES - 1, seq + 2 * CONV_PAD, ch), F32)],
        compiler_params=_cparams("arbitrary"),
        name="conformer_conv",
    )(pc, w_dw.reshape(CONV_K, 1, ch), b_dw.reshape(1, ch), g_n.reshape(1, ch), b_n.reshape(1, ch), gmat)


DIL_QB = 128
DIL_KB = 256
DIL_UNITS_PER_ITER = 2


def _strided_rows(first, count, stride):
    return pl.ds(first, count) if stride == 1 else pl.ds(first, count, stride=stride)


def _load_rows(ref, rows):
    return jnp.concatenate([ref[j, rows, :] for j in range(ref.shape[0])], axis=1)


def _store_rows(ref, rows, val):
    for j in range(ref.shape[0]):
        ref[j, rows, :] = val[:, j * 128:(j + 1) * 128]


def _dilated_kernel(q_ref, k_ref, v_ref, hmask_ref, hmask_f32_ref, o_ref, acc_ref, m_ref, l_ref):
    seq = q_ref.shape[1]
    for branch, (window, dil) in enumerate(DILATED_CFG):
        n_side = window // (2 * dil)
        sub_len = seq // dil
        qb = min(DIL_QB, sub_len)
        kb = min(DIL_KB, sub_len)
        rel = (lax.broadcasted_iota(jnp.int32, (N_HEADS * qb, kb), 1)
               - (lax.broadcasted_iota(jnp.int32, (N_HEADS * qb, kb), 0) & (qb - 1)))

        def scores(r, c, dil=dil, n_side=n_side, sub_len=sub_len, qb=qb, kb=kb, rel=rel):
            l0 = c * qb
            kl0 = jnp.clip(l0 - n_side, 0, sub_len - kb)
            if dil == 1:
                l0, kl0 = pl.multiple_of(l0, qb), pl.multiple_of(kl0, n_side)
            q_rows = _strided_rows(r + dil * l0, qb, dil)
            k_rows = _strided_rows(r + dil * kl0, kb, dil)
            q = _load_rows(q_ref, q_rows).astype(BF16)
            qs = jnp.concatenate([q * hmask_ref[h] for h in range(N_HEADS)], axis=0)
            s = _dot_nt(qs, _load_rows(k_ref, k_rows).astype(BF16))
            return jnp.where(jnp.abs(rel + (kl0 - l0)) <= n_side, s, NEG_INF), q_rows, k_rows

        def attend(s, q_rows, k_rows, qb=qb, branch=branch):
            m = jnp.max(s, axis=-1, keepdims=True)
            e = jnp.exp(s - m)
            l = jnp.sum(e, axis=-1, keepdims=True)
            pv = _dot(e.astype(BF16), _load_rows(v_ref, k_rows).astype(BF16))

            def unstack(a):
                out = a[(N_HEADS - 1) * qb:N_HEADS * qb]
                for h in range(N_HEADS - 2, -1, -1):
                    out = jnp.where(hmask_f32_ref[h] > 0.5, a[h * qb:(h + 1) * qb], out)
                return out

            acc_new, m_new, l_new = unstack(pv), unstack(m), unstack(l)
            if branch == 0:
                _store_rows(acc_ref, q_rows, acc_new)
                _store_rows(m_ref, q_rows, m_new)
                _store_rows(l_ref, q_rows, l_new)
            else:
                m_old = _load_rows(m_ref, q_rows)
                m_max = jnp.maximum(m_old, m_new)
                w_old = jnp.exp(m_old - m_max)
                w_new = jnp.exp(m_new - m_max)
                _store_rows(acc_ref, q_rows, _load_rows(acc_ref, q_rows) * w_old + acc_new * w_new)
                _store_rows(l_ref, q_rows, _load_rows(l_ref, q_rows) * w_old + l_new * w_new)
                _store_rows(m_ref, q_rows, m_max)

        def run(units, scores=scores, attend=attend):
            staged = [scores(r, c) for r, c in units]
            for item in staged:
                attend(*item)

        n_blocks = sub_len // qb
        if n_blocks >= DIL_UNITS_PER_ITER:
            for r in range(dil):
                def block_group(p, carry, r=r, run=run):
                    run([(r, DIL_UNITS_PER_ITER * p + u) for u in range(DIL_UNITS_PER_ITER)])
                    return carry

                lax.fori_loop(0, n_blocks // DIL_UNITS_PER_ITER, block_group, 0)
        else:
            for r0 in range(0, dil, DIL_UNITS_PER_ITER):
                run([(r0 + u, 0) for u in range(DIL_UNITS_PER_ITER)])
    for j in range(acc_ref.shape[0]):
        o_ref[:, j * 128:(j + 1) * 128] = (acc_ref[j] / l_ref[j]).astype(BF16)


def _dilated_attention(qd, kd, vd, batch, seq, masks):
    tiles, t, _ = qd.shape
    gw = tiles * 128
    hmask, hmask_f32, _ = masks
    in_spec = pl.BlockSpec((tiles, seq, 128), lambda b: (0, b, 0))
    stat = pltpu.VMEM((tiles, seq, 128), F32)
    return pl.pallas_call(
        _dilated_kernel,
        grid=(batch,),
        in_specs=[
            in_spec, in_spec, in_spec,
            pl.BlockSpec(hmask.shape, lambda b: (0, 0, 0)),
            pl.BlockSpec(hmask_f32.shape, lambda b: (0, 0, 0)),
        ],
        out_specs=pl.BlockSpec((seq, gw), lambda b: (b, 0)),
        out_shape=jax.ShapeDtypeStruct((t, gw), BF16),
        scratch_shapes=[stat, stat, stat],
        compiler_params=_cparams("arbitrary"),
        name="dilated_attention",
    )(qd, kd, vd, hmask, hmask_f32)


def _outproj_kernel(ya_ref, yb_ref, yc_ref, yd_ref, w_ref, x_ref, mod_ref, g_ref, wr_ref,
                    x1_ref, h2_lo_ref, h2_hi_ref, lg_ref):
    gw = GROUP_WIDTH
    mix = _dot(ya_ref[...], w_ref[0:gw, :])
    mix = mix + _dot(yb_ref[...], w_ref[gw:2 * gw, :])
    mix = mix + _dot(yc_ref[...], w_ref[2 * gw:3 * gw, :])
    mix = mix + _dot(yd_ref[...], w_ref[3 * gw:4 * gw, :])
    mod = mod_ref[0]
    x1 = x_ref[...] + mod[2:3] * mix
    x1_ref[...] = x1
    ms = jnp.mean(x1 * x1, axis=-1, keepdims=True)
    h2 = (x1 * lax.rsqrt(ms + EPS) * g_ref[...]) * (1.0 + mod[4:5]) + mod[3:4]
    packed = _pack_bf16_pairs(h2)
    quarter = h2_lo_ref.shape[1]
    h2_lo_ref[...] = packed[:, :quarter]
    h2_hi_ref[...] = packed[:, quarter:]
    lg_ref[...] = _dot(h2.astype(BF16), wr_ref[...])


def _output_projection(ys, w_out_bf16, x2d, mod_l, g2, w_router, seq):
    t, d = x2d.shape
    tm = 512
    tiles_per_batch = seq // tm
    gw = GROUP_WIDTH
    row_spec = lambda width: pl.BlockSpec((tm, width), lambda i: (i, 0))
    return pl.pallas_call(
        _outproj_kernel,
        grid=(t // tm,),
        in_specs=[
            row_spec(gw), row_spec(gw), row_spec(gw), row_spec(gw),
            pl.BlockSpec((d, d), lambda i: (0, 0)),
            row_spec(d),
            pl.BlockSpec((1, 6, d), lambda i: (i // tiles_per_batch, 0, 0)),
            pl.BlockSpec((1, d), lambda i: (0, 0)),
            pl.BlockSpec((d, ROUTER_LANES), lambda i: (0, 0)),
        ],
        out_specs=[row_spec(d), row_spec(d // 4), row_spec(d // 4), row_spec(ROUTER_LANES)],
        out_shape=[jax.ShapeDtypeStruct((t, d), F32), jax.ShapeDtypeStruct((t, d // 4), jnp.uint32),
                   jax.ShapeDtypeStruct((t, d // 4), jnp.uint32), jax.ShapeDtypeStruct((t, ROUTER_LANES), F32)],
        compiler_params=_cparams("arbitrary"),
        name="output_projection",
    )(*ys, w_out_bf16, x2d, mod_l, g2.reshape(1, d), w_router)


def _routing_kernel(lg_ref, info_ref, cnt_ref, carry_ref):
    tr = lg_ref.shape[0]

    @pl.when(pl.program_id(0) == 0)
    def _():
        carry_ref[...] = jnp.zeros_like(carry_ref)

    lg = lg_ref[...]
    lane = lax.broadcasted_iota(jnp.int32, lg.shape, 1).astype(F32)
    big = float(ROUTER_LANES)
    glog = jnp.where(lane < N_GROUPS, lg, -jnp.inf)
    gmax = jnp.max(glog, axis=-1, keepdims=True)
    p_grp = 1.0 / jnp.sum(jnp.exp(glog - gmax), axis=-1, keepdims=True)
    grp = jnp.min(jnp.where(glog == gmax, lane, big), axis=-1, keepdims=True)
    lo = N_GROUPS + EXPERTS_PER_GROUP * grp
    elog = jnp.where((lane >= lo) & (lane < lo + EXPERTS_PER_GROUP), lg, -jnp.inf)
    v1 = jnp.max(elog, axis=-1, keepdims=True)
    i1 = jnp.min(jnp.where(elog == v1, lane, big), axis=-1, keepdims=True)
    elog2 = jnp.where(lane == i1, -jnp.inf, elog)
    v2 = jnp.max(elog2, axis=-1, keepdims=True)
    i2 = jnp.min(jnp.where(elog2 == v2, lane, big), axis=-1, keepdims=True)
    d = jnp.exp(v2 - v1)
    gate1 = p_grp / (1.0 + d)
    gate2 = p_grp * d / (1.0 + d)
    sel1 = lane == i1
    sel2 = lane == i2
    sel = jnp.where(sel1 | sel2, 1.0, 0.0)
    row = lax.broadcasted_iota(jnp.int32, (tr, tr), 0)
    col = lax.broadcasted_iota(jnp.int32, (tr, tr), 1)
    before = jnp.where(col < row, 1.0, 0.0).astype(BF16)
    rank = _dot(before, sel.astype(BF16)) + carry_ref[...]
    r1 = jnp.sum(jnp.where(sel1, rank, 0.0), axis=-1, keepdims=True)
    r2 = jnp.sum(jnp.where(sel2, rank, 0.0), axis=-1, keepdims=True)
    carry_ref[...] += jnp.sum(sel, axis=0, keepdims=True)
    cnt_ref[...] = carry_ref[...]
    info = jnp.zeros_like(lg)
    for idx, val in enumerate((i1 - N_GROUPS, i2 - N_GROUPS, r1, r2, gate1, gate2)):
        info = jnp.where(lane == idx, val, info)
    info_ref[...] = info


def _routing(logits):
    t = logits.shape[0]
    tr = 512
    return pl.pallas_call(
        _routing_kernel,
        grid=(t // tr,),
        in_specs=[pl.BlockSpec((tr, ROUTER_LANES), lambda i: (i, 0))],
        out_specs=[pl.BlockSpec((tr, ROUTER_LANES), lambda i: (i, 0)),
                   pl.BlockSpec((1, ROUTER_LANES), lambda i: (0, 0))],
        out_shape=[jax.ShapeDtypeStruct((t, ROUTER_LANES), F32),
                   jax.ShapeDtypeStruct((1, ROUTER_LANES), F32)],
        scratch_shapes=[pltpu.VMEM((1, ROUTER_LANES), F32)],
        compiler_params=_cparams("arbitrary"),
        name="routing",
    )(logits)


SC_GATHER_WINDOW = 128


def _gather_rows(table, indices):
    n = indices.shape[0]
    width = table.shape[1]
    mesh = plsc.VectorSubcoreMesh(core_axis_name="core", subcore_axis_name="subcore")

    @pl.kernel(out_type=jax.ShapeDtypeStruct((n, width), table.dtype), mesh=mesh, scratch_types=[],
               name="moe_row_gather")
    def gather_kernel(table_hbm, idx_hbm, out_hbm):
        def body(idx_vmem, out_vmem):
            pltpu.sync_copy(table_hbm.at[idx_vmem.at[0]], out_vmem)

        pltpu.emit_pipeline(
            body,
            grid=(n // SC_GATHER_WINDOW,),
            in_specs=[pl.BlockSpec((1, SC_GATHER_WINDOW), index_map=lambda i: (0, i))],
            out_specs=[pl.BlockSpec((SC_GATHER_WINDOW, width), index_map=lambda i: (i, 0))],
            core_axis_name=("core", "subcore"),
            dimension_semantics=(pltpu.PARALLEL,),
        )(idx_hbm, out_hbm)

    return gather_kernel(table, indices.reshape(1, n))


def _scatter_rows(table, indices):
    n = indices.shape[0]
    rows, width = table.shape
    mesh = plsc.VectorSubcoreMesh(core_axis_name="core", subcore_axis_name="subcore")
    table_windows = rows // SC_GATHER_WINDOW

    @pl.kernel(out_type=jax.ShapeDtypeStruct((n, width), table.dtype), mesh=mesh, scratch_types=[],
               name="moe_row_scatter")
    def scatter_kernel(table_hbm, idx_hbm, out_hbm):
        def body(rows_vmem, idx_vmem):
            pltpu.sync_copy(rows_vmem, out_hbm.at[idx_vmem.at[0]])

        pltpu.emit_pipeline(
            body,
            grid=(n // SC_GATHER_WINDOW,),
            in_specs=[pl.BlockSpec((SC_GATHER_WINDOW, width), index_map=lambda i: (i % table_windows, 0)),
                      pl.BlockSpec((1, SC_GATHER_WINDOW), index_map=lambda i: (0, i))],
            out_specs=[],
            core_axis_name=("core", "subcore"),
            dimension_semantics=(pltpu.PARALLEL,),
        )(table_hbm, idx_hbm)

    return scatter_kernel(table, indices.reshape(1, n))


def _dispatch_targets(dest1, dest2, counts, pad_start, pad_end, n_rows):
    n_pad = n_rows - dest1.shape[0] - dest2.shape[0]
    pad_first = jnp.concatenate([pad_start + counts, pad_end[-1:]])
    pad_count = jnp.concatenate([pad_end - pad_start - counts, n_rows - pad_end[-1:]])
    cum = jnp.cumsum(pad_count)
    k = jnp.arange(n_pad, dtype=jnp.int32)
    seg = jnp.sum((cum[None, :] <= k[:, None]).astype(jnp.int32), axis=1)
    onehot = seg[:, None] == jnp.arange(cum.shape[0], dtype=jnp.int32)[None, :]
    offset = k - jnp.sum(jnp.where(onehot, (cum - pad_count)[None, :], 0), axis=1)
    pad_rows = jnp.sum(jnp.where(onehot, pad_first[None, :], 0), axis=1) + offset
    return jnp.concatenate([dest1, dest2, pad_rows])


def _expert_kernel(blk_e_ref, nvalid_ref, xs_lo_ref, xs_hi_ref, wg_ref, wu_ref, wd_ref, ys_lo_ref, ys_hi_ref,
                   wg_bf, wu_bf, wd_bf):
    j = pl.program_id(0)
    half = ys_lo_ref.shape[1]

    @pl.when((j == 0) | (blk_e_ref[j] != blk_e_ref[jnp.maximum(j - 1, 0)]))
    def _():
        wg_bf[...] = wg_ref[0, 0].astype(BF16)
        wu_bf[...] = wu_ref[0, 0].astype(BF16)
        wd_bf[...] = wd_ref[0, 0].astype(BF16)

    @pl.when(j < nvalid_ref[0])
    def _():
        xb = _unpack_bf16_pairs(jnp.concatenate([xs_lo_ref[...], xs_hi_ref[...]], axis=1)).astype(BF16)
        gate = _dot(xb, wg_bf[...])
        up = _dot(xb, wu_bf[...])
        hdn = (gate * jax.nn.sigmoid(gate)) * up
        packed = _pack_bf16_pairs(_dot(hdn.astype(BF16), wd_bf[...]))
        ys_lo_ref[...] = packed[:, :half]
        ys_hi_ref[...] = packed[:, half:]

    @pl.when(j >= nvalid_ref[0])
    def _():
        ys_lo_ref[...] = jnp.zeros_like(ys_lo_ref)
        ys_hi_ref[...] = jnp.zeros_like(ys_hi_ref)


def _expert_mlp(xs_lo, xs_hi, blk_e, nvalid, w_gate, w_up, w_down, layer):
    n_rows, quarter = xs_lo.shape
    nblk = n_rows // MOE_BLOCK
    d, de = w_gate.shape[2:]

    def x_map(j, be, nv):
        return (jnp.minimum(j, nv[0] - 1), 0)

    w_map = lambda j, be, nv: (layer, be[j], 0, 0)
    grid_spec = pltpu.PrefetchScalarGridSpec(
        num_scalar_prefetch=2,
        grid=(nblk,),
        in_specs=[
            pl.BlockSpec((MOE_BLOCK, quarter), x_map),
            pl.BlockSpec((MOE_BLOCK, quarter), x_map),
            pl.BlockSpec((1, 1, d, de), w_map),
            pl.BlockSpec((1, 1, d, de), w_map),
            pl.BlockSpec((1, 1, de, d), w_map),
        ],
        out_specs=[pl.BlockSpec((MOE_BLOCK, quarter), lambda j, be, nv: (j, 0))] * 2,
        scratch_shapes=[pltpu.VMEM((d, de), BF16), pltpu.VMEM((d, de), BF16), pltpu.VMEM((de, d), BF16)],
    )
    return pl.pallas_call(
        _expert_kernel,
        grid_spec=grid_spec,
        out_shape=[jax.ShapeDtypeStruct((n_rows, quarter), jnp.uint32)] * 2,
        compiler_params=_cparams("arbitrary"),
        name="expert_mlp",
    )(blk_e, nvalid, xs_lo, xs_hi, w_gate, w_up, w_down)


def _final_combine_kernel(x_ref, mod_ref, info_ref, gf_ref, y1_lo_ref, y1_hi_ref, y2_lo_ref, y2_hi_ref, o_ref):
    x2 = _moe_residual(x_ref, mod_ref, info_ref, y1_lo_ref, y1_hi_ref, y2_lo_ref, y2_hi_ref)
    ms = jnp.mean(x2 * x2, axis=-1, keepdims=True)
    o_ref[...] = x2 * lax.rsqrt(ms + EPS) * gf_ref[...]


def _gather_expert_rows(ys, dest1, dest2):
    dest = jnp.concatenate([dest1, dest2])
    return _gather_rows(ys[0], dest), _gather_rows(ys[1], dest)


def _final_combine(x1, mod_l, info, g_lo, g_hi, g_final, seq):
    t, d = x1.shape
    tc = 512
    tiles_per_batch = seq // tc
    n_tiles = t // tc
    quarter = g_lo.shape[1]
    row_spec = lambda width: pl.BlockSpec((tc, width), lambda i: (i, 0))
    slot2_spec = pl.BlockSpec((tc, quarter), lambda i: (i + n_tiles, 0))
    return pl.pallas_call(
        _final_combine_kernel,
        grid=(n_tiles,),
        in_specs=[
            row_spec(d),
            pl.BlockSpec((1, 6, d), lambda i: (i // tiles_per_batch, 0, 0)),
            row_spec(ROUTER_LANES),
            pl.BlockSpec((1, d), lambda i: (0, 0)),
            row_spec(quarter), row_spec(quarter), slot2_spec, slot2_spec,
        ],
        out_specs=row_spec(d),
        out_shape=jax.ShapeDtypeStruct((t, d), F32),
        compiler_params=_cparams("arbitrary"),
        name="moe_combine",
    )(x1, mod_l, info, g_final.reshape(1, d), g_lo, g_hi, g_lo, g_hi)


def _router_weights(w_rg, w_re):
    d = w_rg.shape[0]
    w_experts = jnp.transpose(w_re, (1, 0, 2)).reshape(d, N_EXPERTS)
    pad = jnp.zeros((d, ROUTER_LANES - N_GROUPS - N_EXPERTS), F32)
    return jnp.concatenate([w_rg, w_experts, pad], axis=1)


def _block_layout(counts_row, n_blocks):
    counts = counts_row[0, N_GROUPS:N_GROUPS + N_EXPERTS].astype(jnp.int32)
    padded = (counts + MOE_BLOCK - 1) // MOE_BLOCK * MOE_BLOCK
    pad_end = jnp.cumsum(padded)
    pad_start = pad_end - padded
    starts = jnp.arange(n_blocks, dtype=jnp.int32) * MOE_BLOCK
    blk_e = jnp.minimum(jnp.sum((pad_end[None, :] <= starts[:, None]).astype(jnp.int32), axis=1), N_EXPERTS - 1)
    nvalid = (pad_end[-1:] // MOE_BLOCK).astype(jnp.int32)
    return counts, pad_start, pad_end, blk_e, nvalid


def _destinations(info, pad_start):
    ids = info[:, 0:4].astype(jnp.int32)
    experts = jnp.arange(N_EXPERTS, dtype=jnp.int32)[None, :]

    def segment_start(e):
        return jnp.sum(jnp.where(e[:, None] == experts, pad_start[None, :], 0), axis=1)

    return segment_start(ids[:, 0]) + ids[:, 2], segment_start(ids[:, 1]) + ids[:, 3]


def kernel(x, c, w_ada, b_ada, g_norm1, g_norm2, w_in, diff_lambda, diff_subln, na_rpb, conv_dw, conv_b,
           conv_norm_g, conv_norm_b, w_out, w_router_group, w_router_expert, w_exp_gate, w_exp_up,
           w_exp_down, g_final):
    batch, seq, d = x.shape
    depth = w_ada.shape[0]
    t = batch * seq
    assert d == D_MODEL and seq == GRID_ROWS * GRID_W
    n_rows = t * 2 + N_EXPERTS * MOE_BLOCK

    mod = _ada_modulation(c, w_ada, b_ada).reshape(depth, batch, 6, d)
    rope_a = _rope_tables(seq, DIFF_DH)
    rope_d = _rope_tables(seq, HEAD_DIM)
    masks = _head_masks()
    gmat = _block_diag_ones(GROUP_WIDTH, HEAD_DIM)

    x2d = x.reshape(t, d)
    pending = None
    for l in range(depth):
        mod_l = mod[l]
        projected = _input_projection(x2d, mod_l, g_norm1[l], w_in[l].astype(BF16), rope_a, rope_d, seq, pending)
        if pending is not None:
            x2d, projected = projected[0], projected[1:]
        qa, ka, va, qb, kb, vb, pc, qd, kd, vd = projected
        ya = _diff_attention(qa, ka, va, diff_lambda[l], diff_subln[l], l, batch, seq, masks, gmat)
        yb = _neighborhood_attention(qb, kb, vb, _na_bias_table(na_rpb[l]), batch, seq, masks)
        yc = _conformer_conv(pc, conv_dw[l], conv_b[l], conv_norm_g[l], conv_norm_b[l], batch, seq, gmat)
        yd = _dilated_attention(qd, kd, vd, batch, seq, masks)
        x1, h2_lo, h2_hi, logits = _output_projection(
            (ya, yb, yc, yd), w_out[l].astype(BF16), x2d, mod_l, g_norm2[l],
            _router_weights(w_router_group[l], w_router_expert[l]).astype(BF16), seq)
        info, counts = _routing(logits)
        counts, pad_start, pad_end, blk_e, nvalid = _block_layout(counts, n_rows // MOE_BLOCK)
        dest1, dest2 = _destinations(info, pad_start)
        targets = _dispatch_targets(dest1, dest2, counts, pad_start, pad_end, n_rows)
        ys = _expert_mlp(_scatter_rows(h2_lo, targets), _scatter_rows(h2_hi, targets), blk_e, nvalid,
                         w_exp_gate, w_exp_up, w_exp_down, l)
        pending = (x1, mod_l, info, *_gather_expert_rows(ys, dest1, dest2))
    return _final_combine(*pending, g_final, seq).reshape(batch, seq, d)
```

```python
import functools
import math

import numpy as np
import jax
import jax.numpy as jnp
from jax import lax
from jax.experimental import pallas as pl
from jax.experimental.pallas import tpu as pltpu
from jax.experimental.pallas import tpu_sc as plsc

F32 = jnp.float32
BF16 = jnp.bfloat16

D_MODEL = 1024
GROUP_WIDTH = 256
HEAD_DIM = 64
N_HEADS = 4
DIFF_DH = 32
CONV_K = 31
CONV_GROUP_CH = 64
GRID_W = 64
NA_KH = 8
NA_KW = 16
GRID_ROWS = 32
NA_ROWS_PER_ITER = 4
ROPE_THETA = 10000.0
N_GROUPS = 4
EXPERTS_PER_GROUP = 8
N_EXPERTS = 32
MOE_BLOCK = 256
EPS = 1e-6
NEG_INF = -1e30
LOG2E = 1.4426950408889634
ROUTER_LANES = 128
DILATED_CFG = ((128, 1), (512, 4), (2048, 16))
DIFF_EXP2_SCALE = (DIFF_DH ** -0.5) * LOG2E

VMEM_LIMIT = 56 * 1024 * 1024

TOKEN_TILE = 512
DIFF_Q_TILE = 512
ADA_COL_TILE = 1024


def _cparams(*sem):
    return pltpu.CompilerParams(dimension_semantics=sem, vmem_limit_bytes=VMEM_LIMIT)


def _dot(a, b):
    return jnp.dot(a, b, preferred_element_type=F32)


def _dot_nt(a, b):
    return lax.dot_general(a, b, (((1,), (1,)), ((), ())), preferred_element_type=F32)


def _split(a):
    hi = a.astype(BF16)
    lo = (a - hi.astype(F32)).astype(BF16)
    return hi, lo


def _dot3(a, b):
    ah, al = _split(a)
    bh, bl = _split(b)
    return _dot(ah, bh) + (_dot(ah, bl) + _dot(al, bh))


def _group_mean(v, gmat, width):
    hi, lo = _split(v)
    return (_dot(hi, gmat) + _dot(lo, gmat)) * (1.0 / width)


HIGH_HALF = 0xFFFF0000


def _pack_bf16_pairs(a):
    n = a.shape[1] // 2
    bits = lax.bitcast_convert_type(a.astype(BF16).astype(F32), jnp.uint32)
    return (bits[:, :n] >> 16) | (bits[:, n:] & jnp.uint32(HIGH_HALF))


def _unpack_bf16_pairs(u):
    lo = lax.bitcast_convert_type(u << 16, F32)
    hi = lax.bitcast_convert_type(u & jnp.uint32(HIGH_HALF), F32)
    return jnp.concatenate([lo, hi], axis=1)


def _block_diag_ones(n, width):
    idx = np.arange(n) // width
    return jnp.asarray((idx[:, None] == idx[None, :]).astype(np.float32), dtype=BF16)


def _ada_kernel(c_ref, w_ref, b_ref, o_ref):
    c = c_ref[...]
    ca = c * jax.nn.sigmoid(c)
    o_ref[0] = _dot3(ca, w_ref[0]) + b_ref[0]


def _ada_modulation(c, w_ada, b_ada):
    depth, d, n = w_ada.shape
    b = c.shape[0]
    bn = ADA_COL_TILE
    return pl.pallas_call(
        _ada_kernel,
        grid=(depth, n // bn),
        in_specs=[
            pl.BlockSpec((b, d), lambda l, j: (0, 0)),
            pl.BlockSpec((1, d, bn), lambda l, j: (l, 0, j)),
            pl.BlockSpec((1, 1, bn), lambda l, j: (l, 0, j)),
        ],
        out_specs=pl.BlockSpec((1, b, bn), lambda l, j: (l, 0, j)),
        out_shape=jax.ShapeDtypeStruct((depth, b, n), F32),
        compiler_params=_cparams("arbitrary", "arbitrary"),
        name="ada_modulation",
    )(c, w_ada, b_ada.reshape(depth, 1, n))


def _rope_tables(seq, dim):
    half = dim // 2
    inv = ROPE_THETA ** (-jnp.arange(0, dim, 2, dtype=F32) / dim)
    ang = jnp.arange(seq, dtype=F32)[:, None] * inv[None, :]
    cos, sin = jnp.cos(ang), jnp.sin(ang)
    reps = 128 // dim
    zeros = jnp.zeros_like(sin)
    cos_t = jnp.tile(jnp.concatenate([cos, cos], axis=1), (1, reps))
    sin_hi = jnp.tile(jnp.concatenate([zeros, sin], axis=1), (1, reps))
    sin_lo = jnp.tile(jnp.concatenate([-sin, zeros], axis=1), (1, reps))
    return cos_t, sin_hi, sin_lo


def _rotary(v, cos_t, sin_hi, sin_lo, half):
    outs = []
    for j in range(v.shape[1] // 128):
        vj = v[:, j * 128:(j + 1) * 128]
        outs.append(vj * cos_t + pltpu.roll(vj, half, 1) * sin_hi + pltpu.roll(vj, 128 - half, 1) * sin_lo)
    return jnp.concatenate(outs, axis=1)


def _inproj_kernel(x_ref, *rest):
    _project_tokens(x_ref[...], *rest)


def _combine_inproj_kernel(x1_ref, modp_ref, info_ref, y1_lo_ref, y1_hi_ref, y2_lo_ref, y2_hi_ref, *rest):
    proj_refs, x2_ref, out_refs = rest[:9], rest[9], rest[10:]
    x = _moe_residual(x1_ref, modp_ref, info_ref, y1_lo_ref, y1_hi_ref, y2_lo_ref, y2_hi_ref)
    x2_ref[...] = x
    _project_tokens(x, *proj_refs, *out_refs)


def _moe_residual(x_ref, mod_ref, info_ref, y1_lo_ref, y1_hi_ref, y2_lo_ref, y2_hi_ref):
    info = info_ref[...]
    y1 = _unpack_bf16_pairs(jnp.concatenate([y1_lo_ref[...], y1_hi_ref[...]], axis=1))
    y2 = _unpack_bf16_pairs(jnp.concatenate([y2_lo_ref[...], y2_hi_ref[...]], axis=1))
    return x_ref[...] + mod_ref[0][5:6] * (info[:, 4:5] * y1 + info[:, 5:6] * y2)


def _project_tokens(x, mod_ref, g_ref, w_ref, ca_ref, sha_ref, sla_ref, cd_ref, shd_ref, sld_ref,
                    qa_ref, ka_ref, va_ref, qb_ref, kb_ref, vb_ref, pc_ref, qd_ref, kd_ref, vd_ref):
    ms = jnp.mean(x * x, axis=-1, keepdims=True)
    y = x * lax.rsqrt(ms + EPS)
    mod = mod_ref[0]
    h = (y * g_ref[...]) * (1.0 + mod[1:2]) + mod[0:1]
    hb = h.astype(BF16)
    gw = GROUP_WIDTH

    def proj(col):
        return _dot(hb, w_ref[:, col * gw:(col + 1) * gw])

    rot_a = functools.partial(_rotary, cos_t=ca_ref[...], sin_hi=sha_ref[...], sin_lo=sla_ref[...],
                              half=DIFF_DH // 2)
    rot_d = functools.partial(_rotary, cos_t=cd_ref[...], sin_hi=shd_ref[...], sin_lo=sld_ref[...],
                              half=HEAD_DIM // 2)
    na_scale = HEAD_DIM ** -0.5
    qa_ref[...] = (rot_a(proj(0)) * DIFF_EXP2_SCALE).astype(BF16)
    ka_ref[...] = rot_a(proj(1)).astype(BF16)
    va_ref[...] = proj(2).astype(BF16)
    qb_ref[...] = (proj(3) * na_scale).astype(BF16)
    kb_ref[...] = proj(4).astype(BF16)
    vb_ref[...] = proj(5).astype(BF16)
    pc_ref[:, 0:gw] = proj(6)
    pc_ref[:, gw:2 * gw] = proj(7)
    for ref, val in ((qd_ref, rot_d(proj(8)) * na_scale), (kd_ref, rot_d(proj(9))), (vd_ref, proj(10))):
        for j in range(gw // 128):
            ref[j] = val[:, j * 128:(j + 1) * 128]


def _input_projection(x2d, mod_l, g1, w_in_bf16, rope_a, rope_d, seq, pending=None):
    t, d = x2d.shape if pending is None else pending[0].shape
    tm = TOKEN_TILE
    tiles_per_batch = seq // tm
    p_in = w_in_bf16.shape[1]
    gw = GROUP_WIDTH
    row_spec = lambda width: pl.BlockSpec((tm, width), lambda i: (i, 0))
    tab_spec = pl.BlockSpec((tm, 128), lambda i: (i % tiles_per_batch, 0))
    out_shapes = []
    out_specs = []
    for name in ("qa", "ka", "va", "qb", "kb", "vb", "pc", "qd", "kd", "vd"):
        if name == "pc":
            out_shapes.append(jax.ShapeDtypeStruct((t, 2 * gw), F32))
            out_specs.append(row_spec(2 * gw))
        elif name[1] == "d":
            out_shapes.append(jax.ShapeDtypeStruct((gw // 128, t, 128), F32))
            out_specs.append(pl.BlockSpec((gw // 128, tm, 128), lambda i: (0, i, 0)))
        else:
            out_shapes.append(jax.ShapeDtypeStruct((t, gw), BF16))
            out_specs.append(row_spec(gw))
    mod_spec = pl.BlockSpec((1, 6, d), lambda i: (i // tiles_per_batch, 0, 0))
    proj_specs = [
        mod_spec,
        pl.BlockSpec((1, d), lambda i: (0, 0)),
        pl.BlockSpec((d, p_in), lambda i: (0, 0)),
        tab_spec, tab_spec, tab_spec, tab_spec, tab_spec, tab_spec,
    ]
    proj_args = (mod_l, g1.reshape(1, d), w_in_bf16, *rope_a, *rope_d)
    if pending is None:
        return pl.pallas_call(
            _inproj_kernel,
            grid=(t // tm,),
            in_specs=[row_spec(d)] + proj_specs,
            out_specs=out_specs,
            out_shape=out_shapes,
            compiler_params=_cparams("arbitrary"),
            name="input_projection",
        )(x2d, *proj_args)
    x1, mod_prev, info, g_lo, g_hi = pending
    quarter = g_lo.shape[1]
    n_tiles = t // tm
    slot2_spec = pl.BlockSpec((tm, quarter), lambda i: (i + n_tiles, 0))
    return pl.pallas_call(
        _combine_inproj_kernel,
        grid=(n_tiles,),
        in_specs=[row_spec(d), mod_spec, row_spec(ROUTER_LANES), row_spec(quarter), row_spec(quarter),
                  slot2_spec, slot2_spec] + proj_specs,
        out_specs=[row_spec(d)] + out_specs,
        out_shape=[jax.ShapeDtypeStruct((t, d), F32)] + out_shapes,
        compiler_params=_cparams("arbitrary"),
        name="combine_input_projection",
    )(x1, mod_prev, info, g_lo, g_hi, g_lo, g_hi, *proj_args)


def _head_masks():
    lane = np.arange(GROUP_WIDTH)
    head = np.stack([(lane // HEAD_DIM == h) for h in range(N_HEADS)]).astype(np.float32)
    diff = np.stack([(lane // DIFF_DH == j) for j in range(2 * N_HEADS)]).astype(np.float32)
    return (jnp.asarray(head[:, None, :], dtype=BF16), jnp.asarray(head[:, None, :], dtype=F32),
            jnp.asarray(diff[:, None, :], dtype=BF16))


DIFF_ONES_ROWS = 16


def _diff_attn_kernel(lam_init, q_ref, k_ref, v_ref, lp_ref, g_ref, dmask_ref, gmat_ref, o_ref, vt_ref, ot_ref):
    seq = k_ref.shape[0]

    @pl.when(pl.program_id(1) == 0)
    def _():
        vt = jnp.transpose(v_ref[...].astype(F32))
        for h in range(N_HEADS):
            vt_ref[h, 0:HEAD_DIM, :] = vt[h * HEAD_DIM:(h + 1) * HEAD_DIM].astype(BF16)
            vt_ref[h, HEAD_DIM:HEAD_DIM + DIFF_ONES_ROWS, :] = jnp.ones((DIFF_ONES_ROWS, seq), BF16)

    q = q_ref[...]
    k = k_ref[...]
    lp = lp_ref[...]
    lam = (jnp.exp(jnp.sum(lp[0:1] * lp[1:2], axis=-1, keepdims=True))
           - jnp.exp(jnp.sum(lp[2:3] * lp[3:4], axis=-1, keepdims=True)) + lam_init)

    def scores(j):
        return _dot_nt(k, q * dmask_ref[j])

    def weights(st):
        return jnp.exp2(st - jnp.max(st, axis=0, keepdims=True)).astype(BF16)

    def attend(j, e):
        num = _dot(vt_ref[j // 2], e)
        return num[0:HEAD_DIM] / num[HEAD_DIM:HEAD_DIM + 1]

    n_pairs = 2 * N_HEADS
    outs = [None] * n_pairs
    st_next = scores(0)
    e_prev = None
    for j in range(n_pairs):
        st = st_next
        if j + 1 < n_pairs:
            st_next = scores(j + 1)
        e = weights(st)
        if e_prev is not None:
            outs[j - 1] = attend(j - 1, e_prev)
        e_prev = e
    outs[n_pairs - 1] = attend(n_pairs - 1, e_prev)
    for h in range(N_HEADS):
        ot_ref[h * HEAD_DIM:(h + 1) * HEAD_DIM, :] = outs[2 * h] - lam * outs[2 * h + 1]
    o = jnp.transpose(ot_ref[...])
    ms = _group_mean(o * o, gmat_ref[...], HEAD_DIM)
    o_ref[...] = ((o * lax.rsqrt(ms + EPS) * g_ref[...]) * (1.0 - lam_init)).astype(BF16)


def _diff_attention(qa, ka, va, lam_params, subln_g, layer_idx, batch, seq, masks, gmat):
    t, gw = qa.shape
    tq = DIFF_Q_TILE
    nq = seq // tq
    lam_init = 0.8 - 0.6 * math.exp(-0.3 * layer_idx)
    _, _, dmask = masks
    g_tiled = jnp.tile(subln_g, N_HEADS).reshape(1, gw)
    kv_spec = pl.BlockSpec((seq, gw), lambda b, i: (b, 0))
    full = lambda shape: pl.BlockSpec(shape, lambda b, i: (0,) * len(shape))
    return pl.pallas_call(
        functools.partial(_diff_attn_kernel, lam_init),
        grid=(batch, nq),
        in_specs=[
            pl.BlockSpec((tq, gw), lambda b, i: (b * nq + i, 0)),
            kv_spec, kv_spec,
            full(lam_params.shape), full((1, gw)), full(dmask.shape), full(gmat.shape),
        ],
        out_specs=pl.BlockSpec((tq, gw), lambda b, i: (b * nq + i, 0)),
        out_shape=jax.ShapeDtypeStruct((t, gw), BF16),
        scratch_shapes=[pltpu.VMEM((N_HEADS, HEAD_DIM + DIFF_ONES_ROWS, seq), BF16), pltpu.VMEM((gw, tq), F32)],
        compiler_params=_cparams("arbitrary", "arbitrary"),
        name="diff_attention",
    )(qa, ka, va, lam_params, g_tiled, dmask, gmat)


def _na_bias_table(rpb):
    w = GRID_W
    n_heads = rpb.shape[0]
    cq = np.arange(w)[:, None]
    ck = np.arange(w)[None, :]
    dc = np.clip(ck - cq, -(NA_KW - 1), NA_KW - 1) + NA_KW - 1
    col_start = np.clip(cq - NA_KW // 2, 0, w - NA_KW)
    col_ok = (ck >= col_start) & (ck < col_start + NA_KW)
    onehot = jnp.asarray(dc[:, :, None] == np.arange(2 * NA_KW - 1), dtype=F32)
    toep = jnp.einsum('qkd,hrd->hrqk', onehot, rpb.astype(F32), precision=lax.Precision.HIGHEST)
    toep = jnp.where(col_ok[None, None], toep, NEG_INF)
    tabs = jnp.stack([toep[:, NA_KH - 1 - off:2 * NA_KH - 1 - off] for off in range(NA_KH)])
    return jnp.transpose(tabs, (0, 1, 3, 2, 4)).reshape(NA_KH, n_heads * w, NA_KH * w)


def _na_kernel(q_ref, k_ref, v_ref, tab_ref, hmask_ref, hmask_f32_ref, o_ref):
    w = GRID_W
    nk = NA_KH * w

    def scores(i):
        row_start = jnp.clip(i - NA_KH // 2, 0, GRID_ROWS - NA_KH)
        kstart = pl.multiple_of(row_start * w, w)
        q = q_ref[pl.ds(pl.multiple_of(i * w, w), w), :]
        qs = jnp.concatenate([q * hmask_ref[h] for h in range(N_HEADS)], axis=0)
        return _dot_nt(qs, k_ref[pl.ds(kstart, nk), :]) + tab_ref[i - row_start], kstart

    def attend(i, s, kstart):
        m = jnp.max(s, axis=-1, keepdims=True)
        e = jnp.exp(s - m)
        p = e * (1.0 / jnp.sum(e, axis=-1, keepdims=True))
        pv = _dot(p.astype(BF16), v_ref[pl.ds(kstart, nk), :])
        o = pv[(N_HEADS - 1) * w:N_HEADS * w]
        for h in range(N_HEADS - 2, -1, -1):
            o = jnp.where(hmask_f32_ref[h] > 0.5, pv[h * w:(h + 1) * w], o)
        o_ref[pl.ds(pl.multiple_of(i * w, w), w), :] = o.astype(BF16)

    def grid_rows(p, carry):
        rows = [NA_ROWS_PER_ITER * p + u for u in range(NA_ROWS_PER_ITER)]
        staged = [scores(i) for i in rows]
        for i, (s, kstart) in zip(rows, staged):
            attend(i, s, kstart)
        return carry

    lax.fori_loop(0, GRID_ROWS // NA_ROWS_PER_ITER, grid_rows, 0)


def _neighborhood_attention(qb, kb, vb, bias_table, batch, seq, masks):
    t, gw = qb.shape
    hmask, hmask_f32, _ = masks
    seq_spec = pl.BlockSpec((seq, gw), lambda b: (b, 0))
    full = lambda shape: pl.BlockSpec(shape, lambda b: (0,) * len(shape))
    return pl.pallas_call(
        _na_kernel,
        grid=(batch,),
        in_specs=[seq_spec, seq_spec, seq_spec, full(bias_table.shape), full(hmask.shape), full(hmask_f32.shape)],
        out_specs=seq_spec,
        out_shape=jax.ShapeDtypeStruct((t, gw), BF16),
        compiler_params=_cparams("arbitrary"),
        name="neighborhood_attention",
    )(qb, kb, vb, bias_table, hmask, hmask_f32)


SUBLANES = 8
CONV_PAD = 16
CONV_CHUNK = 128


def _conv_kernel(pc_ref, w_ref, b_ref, gn_ref, bn_ref, gmat_ref, o_ref, zp_ref, zs_ref):
    seq, ch = o_ref.shape
    a = pc_ref[:, 0:ch]
    gate = pc_ref[:, ch:2 * ch]
    zp_ref[0:CONV_PAD, :] = jnp.zeros((CONV_PAD, ch), F32)
    zp_ref[CONV_PAD + seq:2 * CONV_PAD + seq, :] = jnp.zeros((CONV_PAD, ch), F32)
    zp_ref[CONV_PAD:CONV_PAD + seq, :] = a * jax.nn.sigmoid(gate)
    span = seq + 2 * CONV_PAD - SUBLANES
    for b in range(1, SUBLANES):
        zs_ref[b - 1, 0:span, :] = zp_ref[b:b + span, :]
    gmat = gmat_ref[...]
    first = CONV_PAD - CONV_K // 2
    for c in range(seq // CONV_CHUNK):
        r0 = c * CONV_CHUNK
        acc = jnp.zeros((CONV_CHUNK, ch), F32)
        for j in range(CONV_K):
            shift, aligned = (first + j) % SUBLANES, r0 + (first + j) // SUBLANES * SUBLANES
            src = zp_ref if shift == 0 else zs_ref.at[shift - 1]
            acc = acc + w_ref[j] * src[aligned:aligned + CONV_CHUNK, :]
        z = acc + b_ref[...]
        mu = _group_mean(z, gmat, CONV_GROUP_CH)
        dz = z - mu
        var = _group_mean(dz * dz, gmat, CONV_GROUP_CH)
        zn = dz * lax.rsqrt(var + EPS) * gn_ref[...] + bn_ref[...]
        o_ref[r0:r0 + CONV_CHUNK, :] = (zn * jax.nn.sigmoid(zn)).astype(BF16)


def _conformer_conv(pc, w_dw, b_dw, g_n, b_n, batch, seq, gmat):
    t = pc.shape[0]
    ch = GROUP_WIDTH
    full = lambda shape: pl.BlockSpec(shape, lambda b: (0,) * len(shape))
    return pl.pallas_call(
        _conv_kernel,
        grid=(batch,),
        in_specs=[
            pl.BlockSpec((seq, 2 * ch), lambda b: (b, 0)),
            full((CONV_K, 1, ch)), full((1, ch)), full((1, ch)), full((1, ch)), full(gmat.shape),
        ],
        out_specs=pl.BlockSpec((seq, ch), lambda b: (b, 0)),
        out_shape=jax.ShapeDtypeStruct((t, ch), BF16),
        scratch_shapes=[pltpu.VMEM((seq + 2 * CONV_PAD, ch), F32),
                        pltpu.VMEM((SUBLANES - 1, seq + 2 * CONV_PAD, ch), F32)],
        compiler_params=_cparams("arbitrary"),
        name="conformer_conv",
    )(pc, w_dw.reshape(CONV_K, 1, ch), b_dw.reshape(1, ch), g_n.reshape(1, ch), b_n.reshape(1, ch), gmat)


DIL_QB = 128
DIL_KB = 256
DIL_UNITS_PER_ITER = 2


def _strided_rows(first, count, stride):
    return pl.ds(first, count) if stride == 1 else pl.ds(first, count, stride=stride)


def _load_rows(ref, rows):
    return jnp.concatenate([ref[j, rows, :] for j in range(ref.shape[0])], axis=1)


def _store_rows(ref, rows, val):
    for j in range(ref.shape[0]):
        ref[j, rows, :] = val[:, j * 128:(j + 1) * 128]


def _dilated_kernel(q_ref, k_ref, v_ref, hmask_ref, hmask_f32_ref, o_ref, acc_ref, m_ref, l_ref):
    seq = q_ref.shape[1]
    for branch, (window, dil) in enumerate(DILATED_CFG):
        n_side = window // (2 * dil)
        sub_len = seq // dil
        qb = min(DIL_QB, sub_len)
        kb = min(DIL_KB, sub_len)
        rel = (lax.broadcasted_iota(jnp.int32, (N_HEADS * qb, kb), 1)
               - (lax.broadcasted_iota(jnp.int32, (N_HEADS * qb, kb), 0) & (qb - 1)))

        def scores(r, c, dil=dil, n_side=n_side, sub_len=sub_len, qb=qb, kb=kb, rel=rel):
            l0 = c * qb
            kl0 = jnp.clip(l0 - n_side, 0, sub_len - kb)
            if dil == 1:
                l0, kl0 = pl.multiple_of(l0, qb), pl.multiple_of(kl0, n_side)
            q_rows = _strided_rows(r + dil * l0, qb, dil)
            k_rows = _strided_rows(r + dil * kl0, kb, dil)
            q = _load_rows(q_ref, q_rows).astype(BF16)
            qs = jnp.concatenate([q * hmask_ref[h] for h in range(N_HEADS)], axis=0)
            s = _dot_nt(qs, _load_rows(k_ref, k_rows).astype(BF16))
            return jnp.where(jnp.abs(rel + (kl0 - l0)) <= n_side, s, NEG_INF), q_rows, k_rows

        def attend(s, q_rows, k_rows, qb=qb, branch=branch):
            m = jnp.max(s, axis=-1, keepdims=True)
            e = jnp.exp(s - m)
            l = jnp.sum(e, axis=-1, keepdims=True)
            pv = _dot(e.astype(BF16), _load_rows(v_ref, k_rows).astype(BF16))

            def unstack(a):
                out = a[(N_HEADS - 1) * qb:N_HEADS * qb]
                for h in range(N_HEADS - 2, -1, -1):
                    out = jnp.where(hmask_f32_ref[h] > 0.5, a[h * qb:(h + 1) * qb], out)
                return out

            acc_new, m_new, l_new = unstack(pv), unstack(m), unstack(l)
            if branch == 0:
                _store_rows(acc_ref, q_rows, acc_new)
                _store_rows(m_ref, q_rows, m_new)
                _store_rows(l_ref, q_rows, l_new)
            else:
                m_old = _load_rows(m_ref, q_rows)
                m_max = jnp.maximum(m_old, m_new)
                w_old = jnp.exp(m_old - m_max)
                w_new = jnp.exp(m_new - m_max)
                _store_rows(acc_ref, q_rows, _load_rows(acc_ref, q_rows) * w_old + acc_new * w_new)
                _store_rows(l_ref, q_rows, _load_rows(l_ref, q_rows) * w_old + l_new * w_new)
                _store_rows(m_ref, q_rows, m_max)

        def run(units, scores=scores, attend=attend):
            staged = [scores(r, c) for r, c in units]
            for item in staged:
                attend(*item)

        n_blocks = sub_len // qb
        if n_blocks >= DIL_UNITS_PER_ITER:
            for r in range(dil):
                def block_group(p, carry, r=r, run=run):
                    run([(r, DIL_UNITS_PER_ITER * p + u) for u in range(DIL_UNITS_PER_ITER)])
                    return carry

                lax.fori_loop(0, n_blocks // DIL_UNITS_PER_ITER, block_group, 0)
        else:
            for r0 in range(0, dil, DIL_UNITS_PER_ITER):
                run([(r0 + u, 0) for u in range(DIL_UNITS_PER_ITER)])
    for j in range(acc_ref.shape[0]):
        o_ref[:, j * 128:(j + 1) * 128] = (acc_ref[j] / l_ref[j]).astype(BF16)


def _dilated_attention(qd, kd, vd, batch, seq, masks):
    tiles, t, _ = qd.shape
    gw = tiles * 128
    hmask, hmask_f32, _ = masks
    in_spec = pl.BlockSpec((tiles, seq, 128), lambda b: (0, b, 0))
    stat = pltpu.VMEM((tiles, seq, 128), F32)
    return pl.pallas_call(
        _dilated_kernel,
        grid=(batch,),
        in_specs=[
            in_spec, in_spec, in_spec,
            pl.BlockSpec(hmask.shape, lambda b: (0, 0, 0)),
            pl.BlockSpec(hmask_f32.shape, lambda b: (0, 0, 0)),
        ],
        out_specs=pl.BlockSpec((seq, gw), lambda b: (b, 0)),
        out_shape=jax.ShapeDtypeStruct((t, gw), BF16),
        scratch_shapes=[stat, stat, stat],
        compiler_params=_cparams("arbitrary"),
        name="dilated_attention",
    )(qd, kd, vd, hmask, hmask_f32)


def _outproj_kernel(ya_ref, yb_ref, yc_ref, yd_ref, w_ref, x_ref, mod_ref, g_ref, wr_ref,
                    x1_ref, h2_lo_ref, h2_hi_ref, lg_ref):
    gw = GROUP_WIDTH
    mix = _dot(ya_ref[...], w_ref[0:gw, :])
    mix = mix + _dot(yb_ref[...], w_ref[gw:2 * gw, :])
    mix = mix + _dot(yc_ref[...], w_ref[2 * gw:3 * gw, :])
    mix = mix + _dot(yd_ref[...], w_ref[3 * gw:4 * gw, :])
    mod = mod_ref[0]
    x1 = x_ref[...] + mod[2:3] * mix
    x1_ref[...] = x1
    ms = jnp.mean(x1 * x1, axis=-1, keepdims=True)
    h2 = (x1 * lax.rsqrt(ms + EPS) * g_ref[...]) * (1.0 + mod[4:5]) + mod[3:4]
    packed = _pack_bf16_pairs(h2)
    quarter = h2_lo_ref.shape[1]
    h2_lo_ref[...] = packed[:, :quarter]
    h2_hi_ref[...] = packed[:, quarter:]
    lg_ref[...] = _dot(h2.astype(BF16), wr_ref[...])


def _output_projection(ys, w_out_bf16, x2d, mod_l, g2, w_router, seq):
    t, d = x2d.shape
    tm = TOKEN_TILE
    tiles_per_batch = seq // tm
    gw = GROUP_WIDTH
    row_spec = lambda width: pl.BlockSpec((tm, width), lambda i: (i, 0))
    return pl.pallas_call(
        _outproj_kernel,
        grid=(t // tm,),
        in_specs=[
            row_spec(gw), row_spec(gw), row_spec(gw), row_spec(gw),
            pl.BlockSpec((d, d), lambda i: (0, 0)),
            row_spec(d),
            pl.BlockSpec((1, 6, d), lambda i: (i // tiles_per_batch, 0, 0)),
            pl.BlockSpec((1, d), lambda i: (0, 0)),
            pl.BlockSpec((d, ROUTER_LANES), lambda i: (0, 0)),
        ],
        out_specs=[row_spec(d), row_spec(d // 4), row_spec(d // 4), row_spec(ROUTER_LANES)],
        out_shape=[jax.ShapeDtypeStruct((t, d), F32), jax.ShapeDtypeStruct((t, d // 4), jnp.uint32),
                   jax.ShapeDtypeStruct((t, d // 4), jnp.uint32), jax.ShapeDtypeStruct((t, ROUTER_LANES), F32)],
        compiler_params=_cparams("arbitrary"),
        name="output_projection",
    )(*ys, w_out_bf16, x2d, mod_l, g2.reshape(1, d), w_router)


def _routing_kernel(lg_ref, info_ref, cnt_ref, carry_ref):
    tr = lg_ref.shape[0]

    @pl.when(pl.program_id(0) == 0)
    def _():
        carry_ref[...] = jnp.zeros_like(carry_ref)

    lg = lg_ref[...]
    lane = lax.broadcasted_iota(jnp.int32, lg.shape, 1).astype(F32)
    big = float(ROUTER_LANES)
    glog = jnp.where(lane < N_GROUPS, lg, -jnp.inf)
    gmax = jnp.max(glog, axis=-1, keepdims=True)
    p_grp = 1.0 / jnp.sum(jnp.exp(glog - gmax), axis=-1, keepdims=True)
    grp = jnp.min(jnp.where(glog == gmax, lane, big), axis=-1, keepdims=True)
    lo = N_GROUPS + EXPERTS_PER_GROUP * grp
    elog = jnp.where((lane >= lo) & (lane < lo + EXPERTS_PER_GROUP), lg, -jnp.inf)
    v1 = jnp.max(elog, axis=-1, keepdims=True)
    i1 = jnp.min(jnp.where(elog == v1, lane, big), axis=-1, keepdims=True)
    elog2 = jnp.where(lane == i1, -jnp.inf, elog)
    v2 = jnp.max(elog2, axis=-1, keepdims=True)
    i2 = jnp.min(jnp.where(elog2 == v2, lane, big), axis=-1, keepdims=True)
    d = jnp.exp(v2 - v1)
    gate1 = p_grp / (1.0 + d)
    gate2 = p_grp * d / (1.0 + d)
    sel1 = lane == i1
    sel2 = lane == i2
    sel = jnp.where(sel1 | sel2, 1.0, 0.0)
    row = lax.broadcasted_iota(jnp.int32, (tr, tr), 0)
    col = lax.broadcasted_iota(jnp.int32, (tr, tr), 1)
    before = jnp.where(col < row, 1.0, 0.0).astype(BF16)
    rank = _dot(before, sel.astype(BF16)) + carry_ref[...]
    r1 = jnp.sum(jnp.where(sel1, rank, 0.0), axis=-1, keepdims=True)
    r2 = jnp.sum(jnp.where(sel2, rank, 0.0), axis=-1, keepdims=True)
    carry_ref[...] += jnp.sum(sel, axis=0, keepdims=True)
    cnt_ref[...] = carry_ref[...]
    info = jnp.zeros_like(lg)
    for idx, val in enumerate((i1 - N_GROUPS, i2 - N_GROUPS, r1, r2, gate1, gate2)):
        info = jnp.where(lane == idx, val, info)
    info_ref[...] = info


def _routing(logits):
    t = logits.shape[0]
    tr = TOKEN_TILE
    return pl.pallas_call(
        _routing_kernel,
        grid=(t // tr,),
        in_specs=[pl.BlockSpec((tr, ROUTER_LANES), lambda i: (i, 0))],
        out_specs=[pl.BlockSpec((tr, ROUTER_LANES), lambda i: (i, 0)),
                   pl.BlockSpec((1, ROUTER_LANES), lambda i: (0, 0))],
        out_shape=[jax.ShapeDtypeStruct((t, ROUTER_LANES), F32),
                   jax.ShapeDtypeStruct((1, ROUTER_LANES), F32)],
        scratch_shapes=[pltpu.VMEM((1, ROUTER_LANES), F32)],
        compiler_params=_cparams("arbitrary"),
        name="routing",
    )(logits)


SC_GATHER_WINDOW = 128


def _gather_rows(table, indices):
    n = indices.shape[0]
    width = table.shape[1]
    mesh = plsc.VectorSubcoreMesh(core_axis_name="core", subcore_axis_name="subcore")

    @pl.kernel(out_type=jax.ShapeDtypeStruct((n, width), table.dtype), mesh=mesh, scratch_types=[],
               name="moe_row_gather")
    def gather_kernel(table_hbm, idx_hbm, out_hbm):
        def body(idx_vmem, out_vmem):
            pltpu.sync_copy(table_hbm.at[idx_vmem.at[0]], out_vmem)

        pltpu.emit_pipeline(
            body,
            grid=(n // SC_GATHER_WINDOW,),
            in_specs=[pl.BlockSpec((1, SC_GATHER_WINDOW), index_map=lambda i: (0, i))],
            out_specs=[pl.BlockSpec((SC_GATHER_WINDOW, width), index_map=lambda i: (i, 0))],
            core_axis_name=("core", "subcore"),
            dimension_semantics=(pltpu.PARALLEL,),
        )(idx_hbm, out_hbm)

    return gather_kernel(table, indices.reshape(1, n))


def _scatter_rows(table, indices):
    n = indices.shape[0]
    rows, width = table.shape
    mesh = plsc.VectorSubcoreMesh(core_axis_name="core", subcore_axis_name="subcore")
    table_windows = rows // SC_GATHER_WINDOW

    @pl.kernel(out_type=jax.ShapeDtypeStruct((n, width), table.dtype), mesh=mesh, scratch_types=[],
               name="moe_row_scatter")
    def scatter_kernel(table_hbm, idx_hbm, out_hbm):
        def body(rows_vmem, idx_vmem):
            pltpu.sync_copy(rows_vmem, out_hbm.at[idx_vmem.at[0]])

        pltpu.emit_pipeline(
            body,
            grid=(n // SC_GATHER_WINDOW,),
            in_specs=[pl.BlockSpec((SC_GATHER_WINDOW, width), index_map=lambda i: (i % table_windows, 0)),
                      pl.BlockSpec((1, SC_GATHER_WINDOW), index_map=lambda i: (0, i))],
            out_specs=[],
            core_axis_name=("core", "subcore"),
            dimension_semantics=(pltpu.PARALLEL,),
        )(table_hbm, idx_hbm)

    return scatter_kernel(table, indices.reshape(1, n))


def _dispatch_targets(dest1, dest2, counts, pad_start, pad_end, n_rows):
    n_pad = n_rows - dest1.shape[0] - dest2.shape[0]
    pad_first = jnp.concatenate([pad_start + counts, pad_end[-1:]])
    pad_count = jnp.concatenate([pad_end - pad_start - counts, n_rows - pad_end[-1:]])
    cum = jnp.cumsum(pad_count)
    k = jnp.arange(n_pad, dtype=jnp.int32)
    seg = jnp.sum((cum[None, :] <= k[:, None]).astype(jnp.int32), axis=1)
    onehot = seg[:, None] == jnp.arange(cum.shape[0], dtype=jnp.int32)[None, :]
    offset = k - jnp.sum(jnp.where(onehot, (cum - pad_count)[None, :], 0), axis=1)
    pad_rows = jnp.sum(jnp.where(onehot, pad_first[None, :], 0), axis=1) + offset
    return jnp.concatenate([dest1, dest2, pad_rows])


def _expert_kernel(blk_e_ref, nvalid_ref, xs_lo_ref, xs_hi_ref, wg_ref, wu_ref, wd_ref, ys_lo_ref, ys_hi_ref,
                   wg_bf, wu_bf, wd_bf):
    j = pl.program_id(0)
    half = ys_lo_ref.shape[1]

    @pl.when((j == 0) | (blk_e_ref[j] != blk_e_ref[jnp.maximum(j - 1, 0)]))
    def _():
        wg_bf[...] = wg_ref[0, 0].astype(BF16)
        wu_bf[...] = wu_ref[0, 0].astype(BF16)
        wd_bf[...] = wd_ref[0, 0].astype(BF16)

    @pl.when(j < nvalid_ref[0])
    def _():
        xb = _unpack_bf16_pairs(jnp.concatenate([xs_lo_ref[...], xs_hi_ref[...]], axis=1)).astype(BF16)
        gate = _dot(xb, wg_bf[...])
        up = _dot(xb, wu_bf[...])
        hdn = (gate * jax.nn.sigmoid(gate)) * up
        packed = _pack_bf16_pairs(_dot(hdn.astype(BF16), wd_bf[...]))
        ys_lo_ref[...] = packed[:, :half]
        ys_hi_ref[...] = packed[:, half:]

    @pl.when(j >= nvalid_ref[0])
    def _():
        ys_lo_ref[...] = jnp.zeros_like(ys_lo_ref)
        ys_hi_ref[...] = jnp.zeros_like(ys_hi_ref)


def _expert_mlp(xs_lo, xs_hi, blk_e, nvalid, w_gate, w_up, w_down, layer):
    n_rows, quarter = xs_lo.shape
    nblk = n_rows // MOE_BLOCK
    d, de = w_gate.shape[2:]

    def x_map(j, be, nv):
        return (jnp.minimum(j, nv[0] - 1), 0)

    w_map = lambda j, be, nv: (layer, be[j], 0, 0)
    grid_spec = pltpu.PrefetchScalarGridSpec(
        num_scalar_prefetch=2,
        grid=(nblk,),
        in_specs=[
            pl.BlockSpec((MOE_BLOCK, quarter), x_map),
            pl.BlockSpec((MOE_BLOCK, quarter), x_map),
            pl.BlockSpec((1, 1, d, de), w_map),
            pl.BlockSpec((1, 1, d, de), w_map),
            pl.BlockSpec((1, 1, de, d), w_map),
        ],
        out_specs=[pl.BlockSpec((MOE_BLOCK, quarter), lambda j, be, nv: (j, 0))] * 2,
        scratch_shapes=[pltpu.VMEM((d, de), BF16), pltpu.VMEM((d, de), BF16), pltpu.VMEM((de, d), BF16)],
    )
    return pl.pallas_call(
        _expert_kernel,
        grid_spec=grid_spec,
        out_shape=[jax.ShapeDtypeStruct((n_rows, quarter), jnp.uint32)] * 2,
        compiler_params=_cparams("arbitrary"),
        name="expert_mlp",
    )(blk_e, nvalid, xs_lo, xs_hi, w_gate, w_up, w_down)


def _final_combine_kernel(x_ref, mod_ref, info_ref, gf_ref, y1_lo_ref, y1_hi_ref, y2_lo_ref, y2_hi_ref, o_ref):
    x2 = _moe_residual(x_ref, mod_ref, info_ref, y1_lo_ref, y1_hi_ref, y2_lo_ref, y2_hi_ref)
    ms = jnp.mean(x2 * x2, axis=-1, keepdims=True)
    o_ref[...] = x2 * lax.rsqrt(ms + EPS) * gf_ref[...]


def _gather_expert_rows(ys, dest1, dest2):
    dest = jnp.concatenate([dest1, dest2])
    return _gather_rows(ys[0], dest), _gather_rows(ys[1], dest)


def _final_combine(x1, mod_l, info, g_lo, g_hi, g_final, seq):
    t, d = x1.shape
    tc = TOKEN_TILE
    tiles_per_batch = seq // tc
    n_tiles = t // tc
    quarter = g_lo.shape[1]
    row_spec = lambda width: pl.BlockSpec((tc, width), lambda i: (i, 0))
    slot2_spec = pl.BlockSpec((tc, quarter), lambda i: (i + n_tiles, 0))
    return pl.pallas_call(
        _final_combine_kernel,
        grid=(n_tiles,),
        in_specs=[
            row_spec(d),
            pl.BlockSpec((1, 6, d), lambda i: (i // tiles_per_batch, 0, 0)),
            row_spec(ROUTER_LANES),
            pl.BlockSpec((1, d), lambda i: (0, 0)),
            row_spec(quarter), row_spec(quarter), slot2_spec, slot2_spec,
        ],
        out_specs=row_spec(d),
        out_shape=jax.ShapeDtypeStruct((t, d), F32),
        compiler_params=_cparams("arbitrary"),
        name="moe_combine",
    )(x1, mod_l, info, g_final.reshape(1, d), g_lo, g_hi, g_lo, g_hi)


def _router_weights(w_rg, w_re):
    d = w_rg.shape[0]
    w_experts = jnp.transpose(w_re, (1, 0, 2)).reshape(d, N_EXPERTS)
    pad = jnp.zeros((d, ROUTER_LANES - N_GROUPS - N_EXPERTS), F32)
    return jnp.concatenate([w_rg, w_experts, pad], axis=1)


def _block_layout(counts_row, n_blocks):
    counts = counts_row[0, N_GROUPS:N_GROUPS + N_EXPERTS].astype(jnp.int32)
    padded = (counts + MOE_BLOCK - 1) // MOE_BLOCK * MOE_BLOCK
    pad_end = jnp.cumsum(padded)
    pad_start = pad_end - padded
    starts = jnp.arange(n_blocks, dtype=jnp.int32) * MOE_BLOCK
    blk_e = jnp.minimum(jnp.sum((pad_end[None, :] <= starts[:, None]).astype(jnp.int32), axis=1), N_EXPERTS - 1)
    nvalid = (pad_end[-1:] // MOE_BLOCK).astype(jnp.int32)
    return counts, pad_start, pad_end, blk_e, nvalid


def _destinations(info, pad_start):
    ids = info[:, 0:4].astype(jnp.int32)
    experts = jnp.arange(N_EXPERTS, dtype=jnp.int32)[None, :]

    def segment_start(e):
        return jnp.sum(jnp.where(e[:, None] == experts, pad_start[None, :], 0), axis=1)

    return segment_start(ids[:, 0]) + ids[:, 2], segment_start(ids[:, 1]) + ids[:, 3]


def kernel(x, c, w_ada, b_ada, g_norm1, g_norm2, w_in, diff_lambda, diff_subln, na_rpb, conv_dw, conv_b,
           conv_norm_g, conv_norm_b, w_out, w_router_group, w_router_expert, w_exp_gate, w_exp_up,
           w_exp_down, g_final):
    batch, seq, d = x.shape
    depth = w_ada.shape[0]
    t = batch * seq
    assert d == D_MODEL and seq == GRID_ROWS * GRID_W
    n_rows = t * 2 + N_EXPERTS * MOE_BLOCK

    mod = _ada_modulation(c, w_ada, b_ada).reshape(depth, batch, 6, d)
    rope_a = _rope_tables(seq, DIFF_DH)
    rope_d = _rope_tables(seq, HEAD_DIM)
    masks = _head_masks()
    gmat = _block_diag_ones(GROUP_WIDTH, HEAD_DIM)

    x2d = x.reshape(t, d)
    pending = None
    for l in range(depth):
        mod_l = mod[l]
        projected = _input_projection(x2d, mod_l, g_norm1[l], w_in[l].astype(BF16), rope_a, rope_d, seq, pending)
        if pending is not None:
            x2d, projected = projected[0], projected[1:]
        qa, ka, va, qb, kb, vb, pc, qd, kd, vd = projected
        ya = _diff_attention(qa, ka, va, diff_lambda[l], diff_subln[l], l, batch, seq, masks, gmat)
        yb = _neighborhood_attention(qb, kb, vb, _na_bias_table(na_rpb[l]), batch, seq, masks)
        yc = _conformer_conv(pc, conv_dw[l], conv_b[l], conv_norm_g[l], conv_norm_b[l], batch, seq, gmat)
        yd = _dilated_attention(qd, kd, vd, batch, seq, masks)
        x1, h2_lo, h2_hi, logits = _output_projection(
            (ya, yb, yc, yd), w_out[l].astype(BF16), x2d, mod_l, g_norm2[l],
            _router_weights(w_router_group[l], w_router_expert[l]).astype(BF16), seq)
        info, counts = _routing(logits)
        counts, pad_start, pad_end, blk_e, nvalid = _block_layout(counts, n_rows // MOE_BLOCK)
        dest1, dest2 = _destinations(info, pad_start)
        targets = _dispatch_targets(dest1, dest2, counts, pad_start, pad_end, n_rows)
        ys = _expert_mlp(_scatter_rows(h2_lo, targets), _scatter_rows(h2_hi, targets), blk_e, nvalid,
                         w_exp_gate, w_exp_up, w_exp_down, l)
        pending = (x1, mod_l, info, *_gather_expert_rows(ys, dest1, dest2))
    return _final_combine(*pending, g_final, seq).reshape(batch, seq, d)
```

```python
import functools
import math

import numpy as np
import jax
import jax.numpy as jnp
from jax import lax
from jax.experimental import pallas as pl
from jax.experimental.pallas import tpu as pltpu
from jax.experimental.pallas import tpu_sc as plsc

F32 = jnp.float32
BF16 = jnp.bfloat16

D_MODEL = 1024
GROUP_WIDTH = 256
HEAD_DIM = 64
N_HEADS = 4
DIFF_DH = 32
CONV_K = 31
CONV_GROUP_CH = 64
GRID_W = 64
NA_KH = 8
NA_KW = 16
GRID_ROWS = 32
NA_ROWS_PER_ITER = 4
ROPE_THETA = 10000.0
N_GROUPS = 4
EXPERTS_PER_GROUP = 8
N_EXPERTS = 32
MOE_BLOCK = 256
EPS = 1e-6
NEG_INF = -1e30
LOG2E = 1.4426950408889634
ROUTER_LANES = 128
DILATED_CFG = ((128, 1), (512, 4), (2048, 16))
DIFF_EXP2_SCALE = (DIFF_DH ** -0.5) * LOG2E

VMEM_LIMIT = 56 * 1024 * 1024

TOKEN_TILE = 512
DIFF_Q_TILE = 1024
ADA_COL_TILE = 1024


def _cparams(*sem):
    return pltpu.CompilerParams(dimension_semantics=sem, vmem_limit_bytes=VMEM_LIMIT)


def _dot(a, b):
    return jnp.dot(a, b, preferred_element_type=F32)


def _dot_nt(a, b):
    return lax.dot_general(a, b, (((1,), (1,)), ((), ())), preferred_element_type=F32)


def _split(a):
    hi = a.astype(BF16)
    lo = (a - hi.astype(F32)).astype(BF16)
    return hi, lo


def _dot3(a, b):
    ah, al = _split(a)
    bh, bl = _split(b)
    return _dot(ah, bh) + (_dot(ah, bl) + _dot(al, bh))


def _group_mean(v, gmat, width):
    hi, lo = _split(v)
    return (_dot(hi, gmat) + _dot(lo, gmat)) * (1.0 / width)


HIGH_HALF = 0xFFFF0000


def _pack_bf16_pairs(a):
    n = a.shape[1] // 2
    bits = lax.bitcast_convert_type(a.astype(BF16).astype(F32), jnp.uint32)
    return (bits[:, :n] >> 16) | (bits[:, n:] & jnp.uint32(HIGH_HALF))


def _unpack_bf16_pairs(u):
    lo = lax.bitcast_convert_type(u << 16, F32)
    hi = lax.bitcast_convert_type(u & jnp.uint32(HIGH_HALF), F32)
    return jnp.concatenate([lo, hi], axis=1)


def _block_diag_ones(n, width):
    idx = np.arange(n) // width
    return jnp.asarray((idx[:, None] == idx[None, :]).astype(np.float32), dtype=BF16)


def _ada_kernel(c_ref, w_ref, b_ref, o_ref):
    c = c_ref[...]
    ca = c * jax.nn.sigmoid(c)
    o_ref[0] = _dot3(ca, w_ref[0]) + b_ref[0]


def _ada_modulation(c, w_ada, b_ada):
    depth, d, n = w_ada.shape
    b = c.shape[0]
    bn = ADA_COL_TILE
    return pl.pallas_call(
        _ada_kernel,
        grid=(depth, n // bn),
        in_specs=[
            pl.BlockSpec((b, d), lambda l, j: (0, 0)),
            pl.BlockSpec((1, d, bn), lambda l, j: (l, 0, j)),
            pl.BlockSpec((1, 1, bn), lambda l, j: (l, 0, j)),
        ],
        out_specs=pl.BlockSpec((1, b, bn), lambda l, j: (l, 0, j)),
        out_shape=jax.ShapeDtypeStruct((depth, b, n), F32),
        compiler_params=_cparams("arbitrary", "arbitrary"),
        name="ada_modulation",
    )(c, w_ada, b_ada.reshape(depth, 1, n))


def _rope_tables(seq, dim):
    half = dim // 2
    inv = ROPE_THETA ** (-jnp.arange(0, dim, 2, dtype=F32) / dim)
    ang = jnp.arange(seq, dtype=F32)[:, None] * inv[None, :]
    cos, sin = jnp.cos(ang), jnp.sin(ang)
    reps = 128 // dim
    zeros = jnp.zeros_like(sin)
    cos_t = jnp.tile(jnp.concatenate([cos, cos], axis=1), (1, reps))
    sin_hi = jnp.tile(jnp.concatenate([zeros, sin], axis=1), (1, reps))
    sin_lo = jnp.tile(jnp.concatenate([-sin, zeros], axis=1), (1, reps))
    return cos_t, sin_hi, sin_lo


def _rotary(v, cos_t, sin_hi, sin_lo, half):
    outs = []
    for j in range(v.shape[1] // 128):
        vj = v[:, j * 128:(j + 1) * 128]
        outs.append(vj * cos_t + pltpu.roll(vj, half, 1) * sin_hi + pltpu.roll(vj, 128 - half, 1) * sin_lo)
    return jnp.concatenate(outs, axis=1)


def _inproj_kernel(x_ref, *rest):
    _project_tokens(x_ref[...], *rest)


def _combine_inproj_kernel(x1_ref, modp_ref, info_ref, y1_lo_ref, y1_hi_ref, y2_lo_ref, y2_hi_ref, *rest):
    proj_refs, x2_ref, out_refs = rest[:9], rest[9], rest[10:]
    x = _moe_residual(x1_ref, modp_ref, info_ref, y1_lo_ref, y1_hi_ref, y2_lo_ref, y2_hi_ref)
    x2_ref[...] = x
    _project_tokens(x, *proj_refs, *out_refs)


def _moe_residual(x_ref, mod_ref, info_ref, y1_lo_ref, y1_hi_ref, y2_lo_ref, y2_hi_ref):
    info = info_ref[...]
    y1 = _unpack_bf16_pairs(jnp.concatenate([y1_lo_ref[...], y1_hi_ref[...]], axis=1))
    y2 = _unpack_bf16_pairs(jnp.concatenate([y2_lo_ref[...], y2_hi_ref[...]], axis=1))
    return x_ref[...] + mod_ref[0][5:6] * (info[:, 4:5] * y1 + info[:, 5:6] * y2)


def _project_tokens(x, mod_ref, g_ref, w_ref, ca_ref, sha_ref, sla_ref, cd_ref, shd_ref, sld_ref,
                    qa_ref, ka_ref, va_ref, qb_ref, kb_ref, vb_ref, pc_ref, qd_ref, kd_ref, vd_ref):
    ms = jnp.mean(x * x, axis=-1, keepdims=True)
    y = x * lax.rsqrt(ms + EPS)
    mod = mod_ref[0]
    h = (y * g_ref[...]) * (1.0 + mod[1:2]) + mod[0:1]
    hb = h.astype(BF16)
    gw = GROUP_WIDTH

    def proj(col):
        return _dot(hb, w_ref[:, col * gw:(col + 1) * gw])

    rot_a = functools.partial(_rotary, cos_t=ca_ref[...], sin_hi=sha_ref[...], sin_lo=sla_ref[...],
                              half=DIFF_DH // 2)
    rot_d = functools.partial(_rotary, cos_t=cd_ref[...], sin_hi=shd_ref[...], sin_lo=sld_ref[...],
                              half=HEAD_DIM // 2)
    na_scale = HEAD_DIM ** -0.5
    qa_ref[...] = (rot_a(proj(0)) * DIFF_EXP2_SCALE).astype(BF16)
    ka_ref[...] = rot_a(proj(1)).astype(BF16)
    va_ref[...] = proj(2).astype(BF16)
    qb_ref[...] = (proj(3) * na_scale).astype(BF16)
    kb_ref[...] = proj(4).astype(BF16)
    vb_ref[...] = proj(5).astype(BF16)
    pc_ref[:, 0:gw] = proj(6)
    pc_ref[:, gw:2 * gw] = proj(7)
    for ref, val in ((qd_ref, rot_d(proj(8)) * na_scale), (kd_ref, rot_d(proj(9))), (vd_ref, proj(10))):
        for j in range(gw // 128):
            ref[j] = val[:, j * 128:(j + 1) * 128]


def _input_projection(x2d, mod_l, g1, w_in_bf16, rope_a, rope_d, seq, pending=None):
    t, d = x2d.shape if pending is None else pending[0].shape
    tm = TOKEN_TILE
    tiles_per_batch = seq // tm
    p_in = w_in_bf16.shape[1]
    gw = GROUP_WIDTH
    row_spec = lambda width: pl.BlockSpec((tm, width), lambda i: (i, 0))
    tab_spec = pl.BlockSpec((tm, 128), lambda i: (i % tiles_per_batch, 0))
    out_shapes = []
    out_specs = []
    for name in ("qa", "ka", "va", "qb", "kb", "vb", "pc", "qd", "kd", "vd"):
        if name == "pc":
            out_shapes.append(jax.ShapeDtypeStruct((t, 2 * gw), F32))
            out_specs.append(row_spec(2 * gw))
        elif name[1] == "d":
            out_shapes.append(jax.ShapeDtypeStruct((gw // 128, t, 128), F32))
            out_specs.append(pl.BlockSpec((gw // 128, tm, 128), lambda i: (0, i, 0)))
        else:
            out_shapes.append(jax.ShapeDtypeStruct((t, gw), BF16))
            out_specs.append(row_spec(gw))
    mod_spec = pl.BlockSpec((1, 6, d), lambda i: (i // tiles_per_batch, 0, 0))
    proj_specs = [
        mod_spec,
        pl.BlockSpec((1, d), lambda i: (0, 0)),
        pl.BlockSpec((d, p_in), lambda i: (0, 0)),
        tab_spec, tab_spec, tab_spec, tab_spec, tab_spec, tab_spec,
    ]
    proj_args = (mod_l, g1.reshape(1, d), w_in_bf16, *rope_a, *rope_d)
    if pending is None:
        return pl.pallas_call(
            _inproj_kernel,
            grid=(t // tm,),
            in_specs=[row_spec(d)] + proj_specs,
            out_specs=out_specs,
            out_shape=out_shapes,
            compiler_params=_cparams("arbitrary"),
            name="input_projection",
        )(x2d, *proj_args)
    x1, mod_prev, info, g_lo, g_hi = pending
    quarter = g_lo.shape[1]
    n_tiles = t // tm
    slot2_spec = pl.BlockSpec((tm, quarter), lambda i: (i + n_tiles, 0))
    return pl.pallas_call(
        _combine_inproj_kernel,
        grid=(n_tiles,),
        in_specs=[row_spec(d), mod_spec, row_spec(ROUTER_LANES), row_spec(quarter), row_spec(quarter),
                  slot2_spec, slot2_spec] + proj_specs,
        out_specs=[row_spec(d)] + out_specs,
        out_shape=[jax.ShapeDtypeStruct((t, d), F32)] + out_shapes,
        compiler_params=_cparams("arbitrary"),
        name="combine_input_projection",
    )(x1, mod_prev, info, g_lo, g_hi, g_lo, g_hi, *proj_args)


def _head_masks():
    lane = np.arange(GROUP_WIDTH)
    head = np.stack([(lane // HEAD_DIM == h) for h in range(N_HEADS)]).astype(np.float32)
    diff = np.stack([(lane // DIFF_DH == j) for j in range(2 * N_HEADS)]).astype(np.float32)
    return (jnp.asarray(head[:, None, :], dtype=BF16), jnp.asarray(head[:, None, :], dtype=F32),
            jnp.asarray(diff[:, None, :], dtype=BF16))


DIFF_ONES_ROWS = 16


def _diff_attn_kernel(lam_init, q_ref, k_ref, v_ref, lp_ref, g_ref, dmask_ref, gmat_ref, o_ref, vt_ref, ot_ref):
    seq = k_ref.shape[0]

    @pl.when(pl.program_id(1) == 0)
    def _():
        vt = jnp.transpose(v_ref[...].astype(F32))
        for h in range(N_HEADS):
            vt_ref[h, 0:HEAD_DIM, :] = vt[h * HEAD_DIM:(h + 1) * HEAD_DIM].astype(BF16)
            vt_ref[h, HEAD_DIM:HEAD_DIM + DIFF_ONES_ROWS, :] = jnp.ones((DIFF_ONES_ROWS, seq), BF16)

    q = q_ref[...]
    k = k_ref[...]
    lp = lp_ref[...]
    lam = (jnp.exp(jnp.sum(lp[0:1] * lp[1:2], axis=-1, keepdims=True))
           - jnp.exp(jnp.sum(lp[2:3] * lp[3:4], axis=-1, keepdims=True)) + lam_init)

    def scores(j):
        return _dot_nt(k, q * dmask_ref[j])

    def weights(st):
        return jnp.exp2(st - jnp.max(st, axis=0, keepdims=True)).astype(BF16)

    def attend(j, e):
        num = _dot(vt_ref[j // 2], e)
        return num[0:HEAD_DIM] / num[HEAD_DIM:HEAD_DIM + 1]

    n_pairs = 2 * N_HEADS
    outs = [None] * n_pairs
    st_next = scores(0)
    e_prev = None
    for j in range(n_pairs):
        st = st_next
        if j + 1 < n_pairs:
            st_next = scores(j + 1)
        e = weights(st)
        if e_prev is not None:
            outs[j - 1] = attend(j - 1, e_prev)
        e_prev = e
    outs[n_pairs - 1] = attend(n_pairs - 1, e_prev)
    for h in range(N_HEADS):
        ot_ref[h * HEAD_DIM:(h + 1) * HEAD_DIM, :] = outs[2 * h] - lam * outs[2 * h + 1]
    o = jnp.transpose(ot_ref[...])
    ms = _group_mean(o * o, gmat_ref[...], HEAD_DIM)
    o_ref[...] = ((o * lax.rsqrt(ms + EPS) * g_ref[...]) * (1.0 - lam_init)).astype(BF16)


def _diff_attention(qa, ka, va, lam_params, subln_g, layer_idx, batch, seq, masks, gmat):
    t, gw = qa.shape
    tq = DIFF_Q_TILE
    nq = seq // tq
    lam_init = 0.8 - 0.6 * math.exp(-0.3 * layer_idx)
    _, _, dmask = masks
    g_tiled = jnp.tile(subln_g, N_HEADS).reshape(1, gw)
    kv_spec = pl.BlockSpec((seq, gw), lambda b, i: (b, 0))
    full = lambda shape: pl.BlockSpec(shape, lambda b, i: (0,) * len(shape))
    return pl.pallas_call(
        functools.partial(_diff_attn_kernel, lam_init),
        grid=(batch, nq),
        in_specs=[
            pl.BlockSpec((tq, gw), lambda b, i: (b * nq + i, 0)),
            kv_spec, kv_spec,
            full(lam_params.shape), full((1, gw)), full(dmask.shape), full(gmat.shape),
        ],
        out_specs=pl.BlockSpec((tq, gw), lambda b, i: (b * nq + i, 0)),
        out_shape=jax.ShapeDtypeStruct((t, gw), BF16),
        scratch_shapes=[pltpu.VMEM((N_HEADS, HEAD_DIM + DIFF_ONES_ROWS, seq), BF16), pltpu.VMEM((gw, tq), F32)],
        compiler_params=_cparams("arbitrary", "arbitrary"),
        name="diff_attention",
    )(qa, ka, va, lam_params, g_tiled, dmask, gmat)


def _na_bias_table(rpb):
    w = GRID_W
    n_heads = rpb.shape[0]
    cq = np.arange(w)[:, None]
    ck = np.arange(w)[None, :]
    dc = np.clip(ck - cq, -(NA_KW - 1), NA_KW - 1) + NA_KW - 1
    col_start = np.clip(cq - NA_KW // 2, 0, w - NA_KW)
    col_ok = (ck >= col_start) & (ck < col_start + NA_KW)
    onehot = jnp.asarray(dc[:, :, None] == np.arange(2 * NA_KW - 1), dtype=F32)
    toep = jnp.einsum('qkd,hrd->hrqk', onehot, rpb.astype(F32), precision=lax.Precision.HIGHEST)
    toep = jnp.where(col_ok[None, None], toep, NEG_INF)
    tabs = jnp.stack([toep[:, NA_KH - 1 - off:2 * NA_KH - 1 - off] for off in range(NA_KH)])
    return jnp.transpose(tabs, (0, 1, 3, 2, 4)).reshape(NA_KH, n_heads * w, NA_KH * w)


def _na_kernel(q_ref, k_ref, v_ref, tab_ref, hmask_ref, hmask_f32_ref, o_ref):
    w = GRID_W
    nk = NA_KH * w

    def scores(i):
        row_start = jnp.clip(i - NA_KH // 2, 0, GRID_ROWS - NA_KH)
        kstart = pl.multiple_of(row_start * w, w)
        q = q_ref[pl.ds(pl.multiple_of(i * w, w), w), :]
        qs = jnp.concatenate([q * hmask_ref[h] for h in range(N_HEADS)], axis=0)
        return _dot_nt(qs, k_ref[pl.ds(kstart, nk), :]) + tab_ref[i - row_start], kstart

    def attend(i, s, kstart):
        m = jnp.max(s, axis=-1, keepdims=True)
        e = jnp.exp(s - m)
        p = e * (1.0 / jnp.sum(e, axis=-1, keepdims=True))
        pv = _dot(p.astype(BF16), v_ref[pl.ds(kstart, nk), :])
        o = pv[(N_HEADS - 1) * w:N_HEADS * w]
        for h in range(N_HEADS - 2, -1, -1):
            o = jnp.where(hmask_f32_ref[h] > 0.5, pv[h * w:(h + 1) * w], o)
        o_ref[pl.ds(pl.multiple_of(i * w, w), w), :] = o.astype(BF16)

    def grid_rows(p, carry):
        rows = [NA_ROWS_PER_ITER * p + u for u in range(NA_ROWS_PER_ITER)]
        staged = [scores(i) for i in rows]
        for i, (s, kstart) in zip(rows, staged):
            attend(i, s, kstart)
        return carry

    lax.fori_loop(0, GRID_ROWS // NA_ROWS_PER_ITER, grid_rows, 0)


def _neighborhood_attention(qb, kb, vb, bias_table, batch, seq, masks):
    t, gw = qb.shape
    hmask, hmask_f32, _ = masks
    seq_spec = pl.BlockSpec((seq, gw), lambda b: (b, 0))
    full = lambda shape: pl.BlockSpec(shape, lambda b: (0,) * len(shape))
    return pl.pallas_call(
        _na_kernel,
        grid=(batch,),
        in_specs=[seq_spec, seq_spec, seq_spec, full(bias_table.shape), full(hmask.shape), full(hmask_f32.shape)],
        out_specs=seq_spec,
        out_shape=jax.ShapeDtypeStruct((t, gw), BF16),
        compiler_params=_cparams("arbitrary"),
        name="neighborhood_attention",
    )(qb, kb, vb, bias_table, hmask, hmask_f32)


SUBLANES = 8
CONV_PAD = 16
CONV_CHUNK = 128


def _conv_kernel(pc_ref, w_ref, b_ref, gn_ref, bn_ref, gmat_ref, o_ref, zp_ref, zs_ref):
    seq, ch = o_ref.shape
    a = pc_ref[:, 0:ch]
    gate = pc_ref[:, ch:2 * ch]
    zp_ref[0:CONV_PAD, :] = jnp.zeros((CONV_PAD, ch), F32)
    zp_ref[CONV_PAD + seq:2 * CONV_PAD + seq, :] = jnp.zeros((CONV_PAD, ch), F32)
    zp_ref[CONV_PAD:CONV_PAD + seq, :] = a * jax.nn.sigmoid(gate)
    span = seq + 2 * CONV_PAD - SUBLANES
    for b in range(1, SUBLANES):
        zs_ref[b - 1, 0:span, :] = zp_ref[b:b + span, :]
    gmat = gmat_ref[...]
    first = CONV_PAD - CONV_K // 2
    for c in range(seq // CONV_CHUNK):
        r0 = c * CONV_CHUNK
        acc = jnp.zeros((CONV_CHUNK, ch), F32)
        for j in range(CONV_K):
            shift, aligned = (first + j) % SUBLANES, r0 + (first + j) // SUBLANES * SUBLANES
            src = zp_ref if shift == 0 else zs_ref.at[shift - 1]
            acc = acc + w_ref[j] * src[aligned:aligned + CONV_CHUNK, :]
        z = acc + b_ref[...]
        mu = _group_mean(z, gmat, CONV_GROUP_CH)
        dz = z - mu
        var = _group_mean(dz * dz, gmat, CONV_GROUP_CH)
        zn = dz * lax.rsqrt(var + EPS) * gn_ref[...] + bn_ref[...]
        o_ref[r0:r0 + CONV_CHUNK, :] = (zn * jax.nn.sigmoid(zn)).astype(BF16)


def _conformer_conv(pc, w_dw, b_dw, g_n, b_n, batch, seq, gmat):
    t = pc.shape[0]
    ch = GROUP_WIDTH
    full = lambda shape: pl.BlockSpec(shape, lambda b: (0,) * len(shape))
    return pl.pallas_call(
        _conv_kernel,
        grid=(batch,),
        in_specs=[
            pl.BlockSpec((seq, 2 * ch), lambda b: (b, 0)),
            full((CONV_K, 1, ch)), full((1, ch)), full((1, ch)), full((1, ch)), full(gmat.shape),
        ],
        out_specs=pl.BlockSpec((seq, ch), lambda b: (b, 0)),
        out_shape=jax.ShapeDtypeStruct((t, ch), BF16),
        scratch_shapes=[pltpu.VMEM((seq + 2 * CONV_PAD, ch), F32),
                        pltpu.VMEM((SUBLANES - 1, seq + 2 * CONV_PAD, ch), F32)],
        compiler_params=_cparams("arbitrary"),
        name="conformer_conv",
    )(pc, w_dw.reshape(CONV_K, 1, ch), b_dw.reshape(1, ch), g_n.reshape(1, ch), b_n.reshape(1, ch), gmat)


DIL_QB = 128
DIL_KB = 256
DIL_UNITS_PER_ITER = 2


def _strided_rows(first, count, stride):
    return pl.ds(first, count) if stride == 1 else pl.ds(first, count, stride=stride)


def _load_rows(ref, rows):
    return jnp.concatenate([ref[j, rows, :] for j in range(ref.shape[0])], axis=1)


def _store_rows(ref, rows, val):
    for j in range(ref.shape[0]):
        ref[j, rows, :] = val[:, j * 128:(j + 1) * 128]


def _dilated_kernel(q_ref, k_ref, v_ref, hmask_ref, hmask_f32_ref, o_ref, acc_ref, m_ref, l_ref):
    seq = q_ref.shape[1]
    for branch, (window, dil) in enumerate(DILATED_CFG):
        n_side = window // (2 * dil)
        sub_len = seq // dil
        qb = min(DIL_QB, sub_len)
        kb = min(DIL_KB, sub_len)
        rel = (lax.broadcasted_iota(jnp.int32, (N_HEADS * qb, kb), 1)
               - (lax.broadcasted_iota(jnp.int32, (N_HEADS * qb, kb), 0) & (qb - 1)))

        def scores(r, c, dil=dil, n_side=n_side, sub_len=sub_len, qb=qb, kb=kb, rel=rel):
            l0 = c * qb
            kl0 = jnp.clip(l0 - n_side, 0, sub_len - kb)
            if dil == 1:
                l0, kl0 = pl.multiple_of(l0, qb), pl.multiple_of(kl0, n_side)
            q_rows = _strided_rows(r + dil * l0, qb, dil)
            k_rows = _strided_rows(r + dil * kl0, kb, dil)
            q = _load_rows(q_ref, q_rows).astype(BF16)
            qs = jnp.concatenate([q * hmask_ref[h] for h in range(N_HEADS)], axis=0)
            s = _dot_nt(qs, _load_rows(k_ref, k_rows).astype(BF16))
            return jnp.where(jnp.abs(rel + (kl0 - l0)) <= n_side, s, NEG_INF), q_rows, k_rows

        def attend(s, q_rows, k_rows, qb=qb, branch=branch):
            m = jnp.max(s, axis=-1, keepdims=True)
            e = jnp.exp(s - m)
            l = jnp.sum(e, axis=-1, keepdims=True)
            pv = _dot(e.astype(BF16), _load_rows(v_ref, k_rows).astype(BF16))

            def unstack(a):
                out = a[(N_HEADS - 1) * qb:N_HEADS * qb]
                for h in range(N_HEADS - 2, -1, -1):
                    out = jnp.where(hmask_f32_ref[h] > 0.5, a[h * qb:(h + 1) * qb], out)
                return out

            acc_new, m_new, l_new = unstack(pv), unstack(m), unstack(l)
            if branch == 0:
                _store_rows(acc_ref, q_rows, acc_new)
                _store_rows(m_ref, q_rows, m_new)
                _store_rows(l_ref, q_rows, l_new)
            else:
                m_old = _load_rows(m_ref, q_rows)
                m_max = jnp.maximum(m_old, m_new)
                w_old = jnp.exp(m_old - m_max)
                w_new = jnp.exp(m_new - m_max)
                _store_rows(acc_ref, q_rows, _load_rows(acc_ref, q_rows) * w_old + acc_new * w_new)
                _store_rows(l_ref, q_rows, _load_rows(l_ref, q_rows) * w_old + l_new * w_new)
                _store_rows(m_ref, q_rows, m_max)

        def run(units, scores=scores, attend=attend):
            staged = [scores(r, c) for r, c in units]
            for item in staged:
                attend(*item)

        n_blocks = sub_len // qb
        if n_blocks >= DIL_UNITS_PER_ITER:
            for r in range(dil):
                def block_group(p, carry, r=r, run=run):
                    run([(r, DIL_UNITS_PER_ITER * p + u) for u in range(DIL_UNITS_PER_ITER)])
                    return carry

                lax.fori_loop(0, n_blocks // DIL_UNITS_PER_ITER, block_group, 0)
        else:
            for r0 in range(0, dil, DIL_UNITS_PER_ITER):
                run([(r0 + u, 0) for u in range(DIL_UNITS_PER_ITER)])
    for j in range(acc_ref.shape[0]):
        o_ref[:, j * 128:(j + 1) * 128] = (acc_ref[j] / l_ref[j]).astype(BF16)


def _dilated_attention(qd, kd, vd, batch, seq, masks):
    tiles, t, _ = qd.shape
    gw = tiles * 128
    hmask, hmask_f32, _ = masks
    in_spec = pl.BlockSpec((tiles, seq, 128), lambda b: (0, b, 0))
    stat = pltpu.VMEM((tiles, seq, 128), F32)
    return pl.pallas_call(
        _dilated_kernel,
        grid=(batch,),
        in_specs=[
            in_spec, in_spec, in_spec,
            pl.BlockSpec(hmask.shape, lambda b: (0, 0, 0)),
            pl.BlockSpec(hmask_f32.shape, lambda b: (0, 0, 0)),
        ],
        out_specs=pl.BlockSpec((seq, gw), lambda b: (b, 0)),
        out_shape=jax.ShapeDtypeStruct((t, gw), BF16),
        scratch_shapes=[stat, stat, stat],
        compiler_params=_cparams("arbitrary"),
        name="dilated_attention",
    )(qd, kd, vd, hmask, hmask_f32)


def _outproj_kernel(ya_ref, yb_ref, yc_ref, yd_ref, w_ref, x_ref, mod_ref, g_ref, wr_ref,
                    x1_ref, h2_lo_ref, h2_hi_ref, lg_ref):
    gw = GROUP_WIDTH
    mix = _dot(ya_ref[...], w_ref[0:gw, :])
    mix = mix + _dot(yb_ref[...], w_ref[gw:2 * gw, :])
    mix = mix + _dot(yc_ref[...], w_ref[2 * gw:3 * gw, :])
    mix = mix + _dot(yd_ref[...], w_ref[3 * gw:4 * gw, :])
    mod = mod_ref[0]
    x1 = x_ref[...] + mod[2:3] * mix
    x1_ref[...] = x1
    ms = jnp.mean(x1 * x1, axis=-1, keepdims=True)
    h2 = (x1 * lax.rsqrt(ms + EPS) * g_ref[...]) * (1.0 + mod[4:5]) + mod[3:4]
    packed = _pack_bf16_pairs(h2)
    quarter = h2_lo_ref.shape[1]
    h2_lo_ref[...] = packed[:, :quarter]
    h2_hi_ref[...] = packed[:, quarter:]
    lg_ref[...] = _dot(h2.astype(BF16), wr_ref[...])


def _output_projection(ys, w_out_bf16, x2d, mod_l, g2, w_router, seq):
    t, d = x2d.shape
    tm = TOKEN_TILE
    tiles_per_batch = seq // tm
    gw = GROUP_WIDTH
    row_spec = lambda width: pl.BlockSpec((tm, width), lambda i: (i, 0))
    return pl.pallas_call(
        _outproj_kernel,
        grid=(t // tm,),
        in_specs=[
            row_spec(gw), row_spec(gw), row_spec(gw), row_spec(gw),
            pl.BlockSpec((d, d), lambda i: (0, 0)),
            row_spec(d),
            pl.BlockSpec((1, 6, d), lambda i: (i // tiles_per_batch, 0, 0)),
            pl.BlockSpec((1, d), lambda i: (0, 0)),
            pl.BlockSpec((d, ROUTER_LANES), lambda i: (0, 0)),
        ],
        out_specs=[row_spec(d), row_spec(d // 4), row_spec(d // 4), row_spec(ROUTER_LANES)],
        out_shape=[jax.ShapeDtypeStruct((t, d), F32), jax.ShapeDtypeStruct((t, d // 4), jnp.uint32),
                   jax.ShapeDtypeStruct((t, d // 4), jnp.uint32), jax.ShapeDtypeStruct((t, ROUTER_LANES), F32)],
        compiler_params=_cparams("arbitrary"),
        name="output_projection",
    )(*ys, w_out_bf16, x2d, mod_l, g2.reshape(1, d), w_router)


def _routing_kernel(lg_ref, info_ref, cnt_ref, carry_ref):
    tr = lg_ref.shape[0]

    @pl.when(pl.program_id(0) == 0)
    def _():
        carry_ref[...] = jnp.zeros_like(carry_ref)

    lg = lg_ref[...]
    lane = lax.broadcasted_iota(jnp.int32, lg.shape, 1).astype(F32)
    big = float(ROUTER_LANES)
    glog = jnp.where(lane < N_GROUPS, lg, -jnp.inf)
    gmax = jnp.max(glog, axis=-1, keepdims=True)
    p_grp = 1.0 / jnp.sum(jnp.exp(glog - gmax), axis=-1, keepdims=True)
    grp = jnp.min(jnp.where(glog == gmax, lane, big), axis=-1, keepdims=True)
    lo = N_GROUPS + EXPERTS_PER_GROUP * grp
    elog = jnp.where((lane >= lo) & (lane < lo + EXPERTS_PER_GROUP), lg, -jnp.inf)
    v1 = jnp.max(elog, axis=-1, keepdims=True)
    i1 = jnp.min(jnp.where(elog == v1, lane, big), axis=-1, keepdims=True)
    elog2 = jnp.where(lane == i1, -jnp.inf, elog)
    v2 = jnp.max(elog2, axis=-1, keepdims=True)
    i2 = jnp.min(jnp.where(elog2 == v2, lane, big), axis=-1, keepdims=True)
    d = jnp.exp(v2 - v1)
    gate1 = p_grp / (1.0 + d)
    gate2 = p_grp * d / (1.0 + d)
    sel1 = lane == i1
    sel2 = lane == i2
    sel = jnp.where(sel1 | sel2, 1.0, 0.0)
    row = lax.broadcasted_iota(jnp.int32, (tr, tr), 0)
    col = lax.broadcasted_iota(jnp.int32, (tr, tr), 1)
    before = jnp.where(col < row, 1.0, 0.0).astype(BF16)
    rank = _dot(before, sel.astype(BF16)) + carry_ref[...]
    r1 = jnp.sum(jnp.where(sel1, rank, 0.0), axis=-1, keepdims=True)
    r2 = jnp.sum(jnp.where(sel2, rank, 0.0), axis=-1, keepdims=True)
    carry_ref[...] += jnp.sum(sel, axis=0, keepdims=True)
    cnt_ref[...] = carry_ref[...]
    info = jnp.zeros_like(lg)
    for idx, val in enumerate((i1 - N_GROUPS, i2 - N_GROUPS, r1, r2, gate1, gate2)):
        info = jnp.where(lane == idx, val, info)
    info_ref[...] = info


def _routing(logits):
    t = logits.shape[0]
    tr = TOKEN_TILE
    return pl.pallas_call(
        _routing_kernel,
        grid=(t // tr,),
        in_specs=[pl.BlockSpec((tr, ROUTER_LANES), lambda i: (i, 0))],
        out_specs=[pl.BlockSpec((tr, ROUTER_LANES), lambda i: (i, 0)),
                   pl.BlockSpec((1, ROUTER_LANES), lambda i: (0, 0))],
        out_shape=[jax.ShapeDtypeStruct((t, ROUTER_LANES), F32),
                   jax.ShapeDtypeStruct((1, ROUTER_LANES), F32)],
        scratch_shapes=[pltpu.VMEM((1, ROUTER_LANES), F32)],
        compiler_params=_cparams("arbitrary"),
        name="routing",
    )(logits)


SC_GATHER_WINDOW = 128


def _gather_rows(table, indices):
    n = indices.shape[0]
    width = table.shape[1]
    mesh = plsc.VectorSubcoreMesh(core_axis_name="core", subcore_axis_name="subcore")

    @pl.kernel(out_type=jax.ShapeDtypeStruct((n, width), table.dtype), mesh=mesh, scratch_types=[],
               name="moe_row_gather")
    def gather_kernel(table_hbm, idx_hbm, out_hbm):
        def body(idx_vmem, out_vmem):
            pltpu.sync_copy(table_hbm.at[idx_vmem.at[0]], out_vmem)

        pltpu.emit_pipeline(
            body,
            grid=(n // SC_GATHER_WINDOW,),
            in_specs=[pl.BlockSpec((1, SC_GATHER_WINDOW), index_map=lambda i: (0, i))],
            out_specs=[pl.BlockSpec((SC_GATHER_WINDOW, width), index_map=lambda i: (i, 0))],
            core_axis_name=("core", "subcore"),
            dimension_semantics=(pltpu.PARALLEL,),
        )(idx_hbm, out_hbm)

    return gather_kernel(table, indices.reshape(1, n))


def _scatter_rows(table, indices):
    n = indices.shape[0]
    rows, width = table.shape
    mesh = plsc.VectorSubcoreMesh(core_axis_name="core", subcore_axis_name="subcore")
    table_windows = rows // SC_GATHER_WINDOW

    @pl.kernel(out_type=jax.ShapeDtypeStruct((n, width), table.dtype), mesh=mesh, scratch_types=[],
               name="moe_row_scatter")
    def scatter_kernel(table_hbm, idx_hbm, out_hbm):
        def body(rows_vmem, idx_vmem):
            pltpu.sync_copy(rows_vmem, out_hbm.at[idx_vmem.at[0]])

        pltpu.emit_pipeline(
            body,
            grid=(n // SC_GATHER_WINDOW,),
            in_specs=[pl.BlockSpec((SC_GATHER_WINDOW, width), index_map=lambda i: (i % table_windows, 0)),
                      pl.BlockSpec((1, SC_GATHER_WINDOW), index_map=lambda i: (0, i))],
            out_specs=[],
            core_axis_name=("core", "subcore"),
            dimension_semantics=(pltpu.PARALLEL,),
        )(table_hbm, idx_hbm)

    return scatter_kernel(table, indices.reshape(1, n))


def _dispatch_targets(dest1, dest2, counts, pad_start, pad_end, n_rows):
    n_pad = n_rows - dest1.shape[0] - dest2.shape[0]
    pad_first = jnp.concatenate([pad_start + counts, pad_end[-1:]])
    pad_count = jnp.concatenate([pad_end - pad_start - counts, n_rows - pad_end[-1:]])
    cum = jnp.cumsum(pad_count)
    k = jnp.arange(n_pad, dtype=jnp.int32)
    seg = jnp.sum((cum[None, :] <= k[:, None]).astype(jnp.int32), axis=1)
    onehot = seg[:, None] == jnp.arange(cum.shape[0], dtype=jnp.int32)[None, :]
    offset = k - jnp.sum(jnp.where(onehot, (cum - pad_count)[None, :], 0), axis=1)
    pad_rows = jnp.sum(jnp.where(onehot, pad_first[None, :], 0), axis=1) + offset
    return jnp.concatenate([dest1, dest2, pad_rows])


def _expert_kernel(blk_e_ref, nvalid_ref, xs_lo_ref, xs_hi_ref, wg_ref, wu_ref, wd_ref, ys_lo_ref, ys_hi_ref,
                   wg_bf, wu_bf, wd_bf):
    j = pl.program_id(0)
    half = ys_lo_ref.shape[1]

    @pl.when((j == 0) | (blk_e_ref[j] != blk_e_ref[jnp.maximum(j - 1, 0)]))
    def _():
        wg_bf[...] = wg_ref[0, 0].astype(BF16)
        wu_bf[...] = wu_ref[0, 0].astype(BF16)
        wd_bf[...] = wd_ref[0, 0].astype(BF16)

    @pl.when(j < nvalid_ref[0])
    def _():
        xb = _unpack_bf16_pairs(jnp.concatenate([xs_lo_ref[...], xs_hi_ref[...]], axis=1)).astype(BF16)
        gate = _dot(xb, wg_bf[...])
        up = _dot(xb, wu_bf[...])
        hdn = (gate * jax.nn.sigmoid(gate)) * up
        packed = _pack_bf16_pairs(_dot(hdn.astype(BF16), wd_bf[...]))
        ys_lo_ref[...] = packed[:, :half]
        ys_hi_ref[...] = packed[:, half:]

    @pl.when(j >= nvalid_ref[0])
    def _():
        ys_lo_ref[...] = jnp.zeros_like(ys_lo_ref)
        ys_hi_ref[...] = jnp.zeros_like(ys_hi_ref)


def _expert_mlp(xs_lo, xs_hi, blk_e, nvalid, w_gate, w_up, w_down, layer):
    n_rows, quarter = xs_lo.shape
    nblk = n_rows // MOE_BLOCK
    d, de = w_gate.shape[2:]

    def x_map(j, be, nv):
        return (jnp.minimum(j, nv[0] - 1), 0)

    w_map = lambda j, be, nv: (layer, be[j], 0, 0)
    grid_spec = pltpu.PrefetchScalarGridSpec(
        num_scalar_prefetch=2,
        grid=(nblk,),
        in_specs=[
            pl.BlockSpec((MOE_BLOCK, quarter), x_map),
            pl.BlockSpec((MOE_BLOCK, quarter), x_map),
            pl.BlockSpec((1, 1, d, de), w_map),
            pl.BlockSpec((1, 1, d, de), w_map),
            pl.BlockSpec((1, 1, de, d), w_map),
        ],
        out_specs=[pl.BlockSpec((MOE_BLOCK, quarter), lambda j, be, nv: (j, 0))] * 2,
        scratch_shapes=[pltpu.VMEM((d, de), BF16), pltpu.VMEM((d, de), BF16), pltpu.VMEM((de, d), BF16)],
    )
    return pl.pallas_call(
        _expert_kernel,
        grid_spec=grid_spec,
        out_shape=[jax.ShapeDtypeStruct((n_rows, quarter), jnp.uint32)] * 2,
        compiler_params=_cparams("arbitrary"),
        name="expert_mlp",
    )(blk_e, nvalid, xs_lo, xs_hi, w_gate, w_up, w_down)


def _final_combine_kernel(x_ref, mod_ref, info_ref, gf_ref, y1_lo_ref, y1_hi_ref, y2_lo_ref, y2_hi_ref, o_ref):
    x2 = _moe_residual(x_ref, mod_ref, info_ref, y1_lo_ref, y1_hi_ref, y2_lo_ref, y2_hi_ref)
    ms = jnp.mean(x2 * x2, axis=-1, keepdims=True)
    o_ref[...] = x2 * lax.rsqrt(ms + EPS) * gf_ref[...]


def _gather_expert_rows(ys, dest1, dest2):
    dest = jnp.concatenate([dest1, dest2])
    return _gather_rows(ys[0], dest), _gather_rows(ys[1], dest)


def _final_combine(x1, mod_l, info, g_lo, g_hi, g_final, seq):
    t, d = x1.shape
    tc = TOKEN_TILE
    tiles_per_batch = seq // tc
    n_tiles = t // tc
    quarter = g_lo.shape[1]
    row_spec = lambda width: pl.BlockSpec((tc, width), lambda i: (i, 0))
    slot2_spec = pl.BlockSpec((tc, quarter), lambda i: (i + n_tiles, 0))
    return pl.pallas_call(
        _final_combine_kernel,
        grid=(n_tiles,),
        in_specs=[
            row_spec(d),
            pl.BlockSpec((1, 6, d), lambda i: (i // tiles_per_batch, 0, 0)),
            row_spec(ROUTER_LANES),
            pl.BlockSpec((1, d), lambda i: (0, 0)),
            row_spec(quarter), row_spec(quarter), slot2_spec, slot2_spec,
        ],
        out_specs=row_spec(d),
        out_shape=jax.ShapeDtypeStruct((t, d), F32),
        compiler_params=_cparams("arbitrary"),
        name="moe_combine",
    )(x1, mod_l, info, g_final.reshape(1, d), g_lo, g_hi, g_lo, g_hi)


def _router_weights(w_rg, w_re):
    d = w_rg.shape[0]
    w_experts = jnp.transpose(w_re, (1, 0, 2)).reshape(d, N_EXPERTS)
    pad = jnp.zeros((d, ROUTER_LANES - N_GROUPS - N_EXPERTS), F32)
    return jnp.concatenate([w_rg, w_experts, pad], axis=1)


def _block_layout(counts_row, n_blocks):
    counts = counts_row[0, N_GROUPS:N_GROUPS + N_EXPERTS].astype(jnp.int32)
    padded = (counts + MOE_BLOCK - 1) // MOE_BLOCK * MOE_BLOCK
    pad_end = jnp.cumsum(padded)
    pad_start = pad_end - padded
    starts = jnp.arange(n_blocks, dtype=jnp.int32) * MOE_BLOCK
    blk_e = jnp.minimum(jnp.sum((pad_end[None, :] <= starts[:, None]).astype(jnp.int32), axis=1), N_EXPERTS - 1)
    nvalid = (pad_end[-1:] // MOE_BLOCK).astype(jnp.int32)
    return counts, pad_start, pad_end, blk_e, nvalid


def _destinations(info, pad_start):
    ids = info[:, 0:4].astype(jnp.int32)
    experts = jnp.arange(N_EXPERTS, dtype=jnp.int32)[None, :]

    def segment_start(e):
        return jnp.sum(jnp.where(e[:, None] == experts, pad_start[None, :], 0), axis=1)

    return segment_start(ids[:, 0]) + ids[:, 2], segment_start(ids[:, 1]) + ids[:, 3]


def kernel(x, c, w_ada, b_ada, g_norm1, g_norm2, w_in, diff_lambda, diff_subln, na_rpb, conv_dw, conv_b,
           conv_norm_g, conv_norm_b, w_out, w_router_group, w_router_expert, w_exp_gate, w_exp_up,
           w_exp_down, g_final):
    batch, seq, d = x.shape
    depth = w_ada.shape[0]
    t = batch * seq
    assert d == D_MODEL and seq == GRID_ROWS * GRID_W
    n_rows = t * 2 + N_EXPERTS * MOE_BLOCK

    mod = _ada_modulation(c, w_ada, b_ada).reshape(depth, batch, 6, d)
    rope_a = _rope_tables(seq, DIFF_DH)
    rope_d = _rope_tables(seq, HEAD_DIM)
    masks = _head_masks()
    gmat = _block_diag_ones(GROUP_WIDTH, HEAD_DIM)

    x2d = x.reshape(t, d)
    pending = None
    for l in range(depth):
        mod_l = mod[l]
        projected = _input_projection(x2d, mod_l, g_norm1[l], w_in[l].astype(BF16), rope_a, rope_d, seq, pending)
        if pending is not None:
            x2d, projected = projected[0], projected[1:]
        qa, ka, va, qb, kb, vb, pc, qd, kd, vd = projected
        ya = _diff_attention(qa, ka, va, diff_lambda[l], diff_subln[l], l, batch, seq, masks, gmat)
        yb = _neighborhood_attention(qb, kb, vb, _na_bias_table(na_rpb[l]), batch, seq, masks)
        yc = _conformer_conv(pc, conv_dw[l], conv_b[l], conv_norm_g[l], conv_norm_b[l], batch, seq, gmat)
        yd = _dilated_attention(qd, kd, vd, batch, seq, masks)
        x1, h2_lo, h2_hi, logits = _output_projection(
            (ya, yb, yc, yd), w_out[l].astype(BF16), x2d, mod_l, g_norm2[l],
            _router_weights(w_router_group[l], w_router_expert[l]).astype(BF16), seq)
        info, counts = _routing(logits)
        counts, pad_start, pad_end, blk_e, nvalid = _block_layout(counts, n_rows // MOE_BLOCK)
        dest1, dest2 = _destinations(info, pad_start)
        targets = _dispatch_targets(dest1, dest2, counts, pad_start, pad_end, n_rows)
        ys = _expert_mlp(_scatter_rows(h2_lo, targets), _scatter_rows(h2_hi, targets), blk_e, nvalid,
                         w_exp_gate, w_exp_up, w_exp_down, l)
        pending = (x1, mod_l, info, *_gather_expert_rows(ys, dest1, dest2))
    return _final_combine(*pending, g_final, seq).reshape(batch, seq, d)
```

```python
import functools
import math

import numpy as np
import jax
import jax.numpy as jnp
from jax import lax
from jax.experimental import pallas as pl
from jax.experimental.pallas import tpu as pltpu
from jax.experimental.pallas import tpu_sc as plsc

F32 = jnp.float32
BF16 = jnp.bfloat16

D_MODEL = 1024
GROUP_WIDTH = 256
HEAD_DIM = 64
N_HEADS = 4
DIFF_DH = 32
CONV_K = 31
CONV_GROUP_CH = 64
GRID_W = 64
NA_KH = 8
NA_KW = 16
GRID_ROWS = 32
NA_ROWS_PER_ITER = 4
ROPE_THETA = 10000.0
N_GROUPS = 4
EXPERTS_PER_GROUP = 8
N_EXPERTS = 32
MOE_BLOCK = 256
EPS = 1e-6
NEG_INF = -1e30
LOG2E = 1.4426950408889634
LANES = 128
SUBLANES = 8
ROUTER_LANES = LANES
DILATED_CFG = ((128, 1), (512, 4), (2048, 16))
DIFF_EXP2_SCALE = (DIFF_DH ** -0.5) * LOG2E

VMEM_LIMIT = 56 * 1024 * 1024

TOKEN_TILE = 512
DIFF_Q_TILE = 1024
ADA_COL_TILE = 1024


def _cparams(*sem):
    return pltpu.CompilerParams(dimension_semantics=sem, vmem_limit_bytes=VMEM_LIMIT)


def _dot(a, b):
    return jnp.dot(a, b, preferred_element_type=F32)


def _dot_nt(a, b):
    return lax.dot_general(a, b, (((1,), (1,)), ((), ())), preferred_element_type=F32)


def _split(a):
    hi = a.astype(BF16)
    lo = (a - hi.astype(F32)).astype(BF16)
    return hi, lo


def _dot3(a, b):
    ah, al = _split(a)
    bh, bl = _split(b)
    return _dot(ah, bh) + (_dot(ah, bl) + _dot(al, bh))


def _group_mean(v, gmat, width):
    hi, lo = _split(v)
    return (_dot(hi, gmat) + _dot(lo, gmat)) * (1.0 / width)


HIGH_HALF = 0xFFFF0000


def _pack_bf16_pairs(a):
    n = a.shape[1] // 2
    bits = lax.bitcast_convert_type(a.astype(BF16).astype(F32), jnp.uint32)
    return (bits[:, :n] >> 16) | (bits[:, n:] & jnp.uint32(HIGH_HALF))


def _unpack_bf16_pairs(u):
    lo = lax.bitcast_convert_type(u << 16, F32)
    hi = lax.bitcast_convert_type(u & jnp.uint32(HIGH_HALF), F32)
    return jnp.concatenate([lo, hi], axis=1)


def _block_diag_ones(n, width):
    idx = np.arange(n) // width
    return jnp.asarray((idx[:, None] == idx[None, :]).astype(np.float32), dtype=BF16)


def _ada_kernel(c_ref, w_ref, b_ref, o_ref):
    c = c_ref[...]
    ca = c * jax.nn.sigmoid(c)
    o_ref[0] = _dot3(ca, w_ref[0]) + b_ref[0]


def _ada_modulation(c, w_ada, b_ada):
    depth, d, n = w_ada.shape
    b = c.shape[0]
    bn = ADA_COL_TILE
    return pl.pallas_call(
        _ada_kernel,
        grid=(depth, n // bn),
        in_specs=[
            pl.BlockSpec((b, d), lambda l, j: (0, 0)),
            pl.BlockSpec((1, d, bn), lambda l, j: (l, 0, j)),
            pl.BlockSpec((1, 1, bn), lambda l, j: (l, 0, j)),
        ],
        out_specs=pl.BlockSpec((1, b, bn), lambda l, j: (l, 0, j)),
        out_shape=jax.ShapeDtypeStruct((depth, b, n), F32),
        compiler_params=_cparams("arbitrary", "arbitrary"),
        name="ada_modulation",
    )(c, w_ada, b_ada.reshape(depth, 1, n))


def _rope_tables(seq, dim):
    half = dim // 2
    inv = ROPE_THETA ** (-jnp.arange(0, dim, 2, dtype=F32) / dim)
    ang = jnp.arange(seq, dtype=F32)[:, None] * inv[None, :]
    cos, sin = jnp.cos(ang), jnp.sin(ang)
    reps = LANES // dim
    zeros = jnp.zeros_like(sin)
    cos_t = jnp.tile(jnp.concatenate([cos, cos], axis=1), (1, reps))
    sin_hi = jnp.tile(jnp.concatenate([zeros, sin], axis=1), (1, reps))
    sin_lo = jnp.tile(jnp.concatenate([-sin, zeros], axis=1), (1, reps))
    return cos_t, sin_hi, sin_lo


def _rotary(v, cos_t, sin_hi, sin_lo, half):
    outs = []
    for j in range(v.shape[1] // LANES):
        vj = v[:, j * LANES:(j + 1) * LANES]
        outs.append(vj * cos_t + pltpu.roll(vj, half, 1) * sin_hi + pltpu.roll(vj, LANES - half, 1) * sin_lo)
    return jnp.concatenate(outs, axis=1)


def _inproj_kernel(x_ref, *rest):
    _project_tokens(x_ref[...], *rest)


def _combine_inproj_kernel(x1_ref, modp_ref, info_ref, y1_lo_ref, y1_hi_ref, y2_lo_ref, y2_hi_ref, *rest):
    proj_refs, x2_ref, out_refs = rest[:9], rest[9], rest[10:]
    x = _moe_residual(x1_ref, modp_ref, info_ref, y1_lo_ref, y1_hi_ref, y2_lo_ref, y2_hi_ref)
    x2_ref[...] = x
    _project_tokens(x, *proj_refs, *out_refs)


def _moe_residual(x_ref, mod_ref, info_ref, y1_lo_ref, y1_hi_ref, y2_lo_ref, y2_hi_ref):
    info = info_ref[...]
    y1 = _unpack_bf16_pairs(jnp.concatenate([y1_lo_ref[...], y1_hi_ref[...]], axis=1))
    y2 = _unpack_bf16_pairs(jnp.concatenate([y2_lo_ref[...], y2_hi_ref[...]], axis=1))
    return x_ref[...] + mod_ref[0][5:6] * (info[:, 4:5] * y1 + info[:, 5:6] * y2)


def _project_tokens(x, mod_ref, g_ref, w_ref, ca_ref, sha_ref, sla_ref, cd_ref, shd_ref, sld_ref,
                    qa_ref, ka_ref, va_ref, qb_ref, kb_ref, vb_ref, pc_ref, qd_ref, kd_ref, vd_ref):
    ms = jnp.mean(x * x, axis=-1, keepdims=True)
    y = x * lax.rsqrt(ms + EPS)
    mod = mod_ref[0]
    h = (y * g_ref[...]) * (1.0 + mod[1:2]) + mod[0:1]
    hb = h.astype(BF16)
    gw = GROUP_WIDTH

    def proj(col):
        return _dot(hb, w_ref[:, col * gw:(col + 1) * gw])

    rot_a = functools.partial(_rotary, cos_t=ca_ref[...], sin_hi=sha_ref[...], sin_lo=sla_ref[...],
                              half=DIFF_DH // 2)
    rot_d = functools.partial(_rotary, cos_t=cd_ref[...], sin_hi=shd_ref[...], sin_lo=sld_ref[...],
                              half=HEAD_DIM // 2)
    na_scale = HEAD_DIM ** -0.5
    qa_ref[...] = (rot_a(proj(0)) * DIFF_EXP2_SCALE).astype(BF16)
    ka_ref[...] = rot_a(proj(1)).astype(BF16)
    va_ref[...] = proj(2).astype(BF16)
    qb_ref[...] = (proj(3) * na_scale).astype(BF16)
    kb_ref[...] = proj(4).astype(BF16)
    vb_ref[...] = proj(5).astype(BF16)
    pc_ref[:, 0:gw] = proj(6)
    pc_ref[:, gw:2 * gw] = proj(7)
    for ref, val in ((qd_ref, rot_d(proj(8)) * na_scale), (kd_ref, rot_d(proj(9))), (vd_ref, proj(10))):
        for j in range(gw // LANES):
            ref[j] = val[:, j * LANES:(j + 1) * LANES]


def _input_projection(x2d, mod_l, g1, w_in_bf16, rope_a, rope_d, seq, pending=None):
    t, d = x2d.shape if pending is None else pending[0].shape
    tm = TOKEN_TILE
    tiles_per_batch = seq // tm
    p_in = w_in_bf16.shape[1]
    gw = GROUP_WIDTH
    row_spec = lambda width: pl.BlockSpec((tm, width), lambda i: (i, 0))
    tab_spec = pl.BlockSpec((tm, LANES), lambda i: (i % tiles_per_batch, 0))
    out_shapes = []
    out_specs = []
    for name in ("qa", "ka", "va", "qb", "kb", "vb", "pc", "qd", "kd", "vd"):
        if name == "pc":
            out_shapes.append(jax.ShapeDtypeStruct((t, 2 * gw), F32))
            out_specs.append(row_spec(2 * gw))
        elif name[1] == "d":
            out_shapes.append(jax.ShapeDtypeStruct((gw // LANES, t, LANES), F32))
            out_specs.append(pl.BlockSpec((gw // LANES, tm, LANES), lambda i: (0, i, 0)))
        else:
            out_shapes.append(jax.ShapeDtypeStruct((t, gw), BF16))
            out_specs.append(row_spec(gw))
    mod_spec = pl.BlockSpec((1, 6, d), lambda i: (i // tiles_per_batch, 0, 0))
    proj_specs = [
        mod_spec,
        pl.BlockSpec((1, d), lambda i: (0, 0)),
        pl.BlockSpec((d, p_in), lambda i: (0, 0)),
        tab_spec, tab_spec, tab_spec, tab_spec, tab_spec, tab_spec,
    ]
    proj_args = (mod_l, g1.reshape(1, d), w_in_bf16, *rope_a, *rope_d)
    if pending is None:
        return pl.pallas_call(
            _inproj_kernel,
            grid=(t // tm,),
            in_specs=[row_spec(d)] + proj_specs,
            out_specs=out_specs,
            out_shape=out_shapes,
            compiler_params=_cparams("arbitrary"),
            name="input_projection",
        )(x2d, *proj_args)
    x1, mod_prev, info, g_lo, g_hi = pending
    quarter = g_lo.shape[1]
    n_tiles = t // tm
    slot2_spec = pl.BlockSpec((tm, quarter), lambda i: (i + n_tiles, 0))
    return pl.pallas_call(
        _combine_inproj_kernel,
        grid=(n_tiles,),
        in_specs=[row_spec(d), mod_spec, row_spec(ROUTER_LANES), row_spec(quarter), row_spec(quarter),
                  slot2_spec, slot2_spec] + proj_specs,
        out_specs=[row_spec(d)] + out_specs,
        out_shape=[jax.ShapeDtypeStruct((t, d), F32)] + out_shapes,
        compiler_params=_cparams("arbitrary"),
        name="combine_input_projection",
    )(x1, mod_prev, info, g_lo, g_hi, g_lo, g_hi, *proj_args)


def _head_masks():
    lane = np.arange(GROUP_WIDTH)
    head = np.stack([(lane // HEAD_DIM == h) for h in range(N_HEADS)]).astype(np.float32)
    diff = np.stack([(lane // DIFF_DH == j) for j in range(2 * N_HEADS)]).astype(np.float32)
    return (jnp.asarray(head[:, None, :], dtype=BF16), jnp.asarray(head[:, None, :], dtype=F32),
            jnp.asarray(diff[:, None, :], dtype=BF16))


DIFF_ONES_ROWS = 16


def _diff_attn_kernel(lam_init, q_ref, k_ref, v_ref, lp_ref, g_ref, dmask_ref, gmat_ref, o_ref, vt_ref, ot_ref):
    seq = k_ref.shape[0]

    @pl.when(pl.program_id(1) == 0)
    def _():
        vt = jnp.transpose(v_ref[...].astype(F32))
        for h in range(N_HEADS):
            vt_ref[h, 0:HEAD_DIM, :] = vt[h * HEAD_DIM:(h + 1) * HEAD_DIM].astype(BF16)
            vt_ref[h, HEAD_DIM:HEAD_DIM + DIFF_ONES_ROWS, :] = jnp.ones((DIFF_ONES_ROWS, seq), BF16)

    q = q_ref[...]
    k = k_ref[...]
    lp = lp_ref[...]
    lam = (jnp.exp(jnp.sum(lp[0:1] * lp[1:2], axis=-1, keepdims=True))
           - jnp.exp(jnp.sum(lp[2:3] * lp[3:4], axis=-1, keepdims=True)) + lam_init)

    def scores(j):
        return _dot_nt(k, q * dmask_ref[j])

    def weights(st):
        return jnp.exp2(st - jnp.max(st, axis=0, keepdims=True)).astype(BF16)

    def attend(j, e):
        num = _dot(vt_ref[j // 2], e)
        return num[0:HEAD_DIM] / num[HEAD_DIM:HEAD_DIM + 1]

    n_pairs = 2 * N_HEADS
    outs = [None] * n_pairs
    st_next = scores(0)
    e_prev = None
    for j in range(n_pairs):
        st = st_next
        if j + 1 < n_pairs:
            st_next = scores(j + 1)
        e = weights(st)
        if e_prev is not None:
            outs[j - 1] = attend(j - 1, e_prev)
        e_prev = e
    outs[n_pairs - 1] = attend(n_pairs - 1, e_prev)
    for h in range(N_HEADS):
        ot_ref[h * HEAD_DIM:(h + 1) * HEAD_DIM, :] = outs[2 * h] - lam * outs[2 * h + 1]
    o = jnp.transpose(ot_ref[...])
    ms = _group_mean(o * o, gmat_ref[...], HEAD_DIM)
    o_ref[...] = ((o * lax.rsqrt(ms + EPS) * g_ref[...]) * (1.0 - lam_init)).astype(BF16)


def _diff_attention(qa, ka, va, lam_params, subln_g, layer_idx, batch, seq, masks, gmat):
    t, gw = qa.shape
    tq = DIFF_Q_TILE
    nq = seq // tq
    lam_init = 0.8 - 0.6 * math.exp(-0.3 * layer_idx)
    _, _, dmask = masks
    g_tiled = jnp.tile(subln_g, N_HEADS).reshape(1, gw)
    kv_spec = pl.BlockSpec((seq, gw), lambda b, i: (b, 0))
    full = lambda shape: pl.BlockSpec(shape, lambda b, i: (0,) * len(shape))
    return pl.pallas_call(
        functools.partial(_diff_attn_kernel, lam_init),
        grid=(batch, nq),
        in_specs=[
            pl.BlockSpec((tq, gw), lambda b, i: (b * nq + i, 0)),
            kv_spec, kv_spec,
            full(lam_params.shape), full((1, gw)), full(dmask.shape), full(gmat.shape),
        ],
        out_specs=pl.BlockSpec((tq, gw), lambda b, i: (b * nq + i, 0)),
        out_shape=jax.ShapeDtypeStruct((t, gw), BF16),
        scratch_shapes=[pltpu.VMEM((N_HEADS, HEAD_DIM + DIFF_ONES_ROWS, seq), BF16), pltpu.VMEM((gw, tq), F32)],
        compiler_params=_cparams("arbitrary", "arbitrary"),
        name="diff_attention",
    )(qa, ka, va, lam_params, g_tiled, dmask, gmat)


def _na_bias_table(rpb):
    w = GRID_W
    n_heads = rpb.shape[0]
    cq = np.arange(w)[:, None]
    ck = np.arange(w)[None, :]
    dc = np.clip(ck - cq, -(NA_KW - 1), NA_KW - 1) + NA_KW - 1
    col_start = np.clip(cq - NA_KW // 2, 0, w - NA_KW)
    col_ok = (ck >= col_start) & (ck < col_start + NA_KW)
    onehot = jnp.asarray(dc[:, :, None] == np.arange(2 * NA_KW - 1), dtype=F32)
    toep = jnp.einsum('qkd,hrd->hrqk', onehot, rpb.astype(F32), precision=lax.Precision.HIGHEST)
    toep = jnp.where(col_ok[None, None], toep, NEG_INF)
    tabs = jnp.stack([toep[:, NA_KH - 1 - off:2 * NA_KH - 1 - off] for off in range(NA_KH)])
    return jnp.transpose(tabs, (0, 1, 3, 2, 4)).reshape(NA_KH, n_heads * w, NA_KH * w)


def _na_kernel(q_ref, k_ref, v_ref, tab_ref, hmask_ref, hmask_f32_ref, o_ref):
    w = GRID_W
    nk = NA_KH * w

    def scores(i):
        row_start = jnp.clip(i - NA_KH // 2, 0, GRID_ROWS - NA_KH)
        kstart = pl.multiple_of(row_start * w, w)
        q = q_ref[pl.ds(pl.multiple_of(i * w, w), w), :]
        qs = jnp.concatenate([q * hmask_ref[h] for h in range(N_HEADS)], axis=0)
        return _dot_nt(qs, k_ref[pl.ds(kstart, nk), :]) + tab_ref[i - row_start], kstart

    def attend(i, s, kstart):
        m = jnp.max(s, axis=-1, keepdims=True)
        e = jnp.exp(s - m)
        p = e * (1.0 / jnp.sum(e, axis=-1, keepdims=True))
        pv = _dot(p.astype(BF16), v_ref[pl.ds(kstart, nk), :])
        o = pv[(N_HEADS - 1) * w:N_HEADS * w]
        for h in range(N_HEADS - 2, -1, -1):
            o = jnp.where(hmask_f32_ref[h] > 0.5, pv[h * w:(h + 1) * w], o)
        o_ref[pl.ds(pl.multiple_of(i * w, w), w), :] = o.astype(BF16)

    def grid_rows(p, carry):
        rows = [NA_ROWS_PER_ITER * p + u for u in range(NA_ROWS_PER_ITER)]
        staged = [scores(i) for i in rows]
        for i, (s, kstart) in zip(rows, staged):
            attend(i, s, kstart)
        return carry

    lax.fori_loop(0, GRID_ROWS // NA_ROWS_PER_ITER, grid_rows, 0)


def _neighborhood_attention(qb, kb, vb, bias_table, batch, seq, masks):
    t, gw = qb.shape
    hmask, hmask_f32, _ = masks
    seq_spec = pl.BlockSpec((seq, gw), lambda b: (b, 0))
    full = lambda shape: pl.BlockSpec(shape, lambda b: (0,) * len(shape))
    return pl.pallas_call(
        _na_kernel,
        grid=(batch,),
        in_specs=[seq_spec, seq_spec, seq_spec, full(bias_table.shape), full(hmask.shape), full(hmask_f32.shape)],
        out_specs=seq_spec,
        out_shape=jax.ShapeDtypeStruct((t, gw), BF16),
        compiler_params=_cparams("arbitrary"),
        name="neighborhood_attention",
    )(qb, kb, vb, bias_table, hmask, hmask_f32)


CONV_PAD = 16
CONV_CHUNK = 128


def _conv_kernel(pc_ref, w_ref, b_ref, gn_ref, bn_ref, gmat_ref, o_ref, zp_ref, zs_ref):
    seq, ch = o_ref.shape
    a = pc_ref[:, 0:ch]
    gate = pc_ref[:, ch:2 * ch]
    zp_ref[0:CONV_PAD, :] = jnp.zeros((CONV_PAD, ch), F32)
    zp_ref[CONV_PAD + seq:2 * CONV_PAD + seq, :] = jnp.zeros((CONV_PAD, ch), F32)
    zp_ref[CONV_PAD:CONV_PAD + seq, :] = a * jax.nn.sigmoid(gate)
    span = seq + 2 * CONV_PAD - SUBLANES
    for b in range(1, SUBLANES):
        zs_ref[b - 1, 0:span, :] = zp_ref[b:b + span, :]
    gmat = gmat_ref[...]
    first = CONV_PAD - CONV_K // 2
    for c in range(seq // CONV_CHUNK):
        r0 = c * CONV_CHUNK
        acc = jnp.zeros((CONV_CHUNK, ch), F32)
        for j in range(CONV_K):
            shift, aligned = (first + j) % SUBLANES, r0 + (first + j) // SUBLANES * SUBLANES
            src = zp_ref if shift == 0 else zs_ref.at[shift - 1]
            acc = acc + w_ref[j] * src[aligned:aligned + CONV_CHUNK, :]
        z = acc + b_ref[...]
        mu = _group_mean(z, gmat, CONV_GROUP_CH)
        dz = z - mu
        var = _group_mean(dz * dz, gmat, CONV_GROUP_CH)
        zn = dz * lax.rsqrt(var + EPS) * gn_ref[...] + bn_ref[...]
        o_ref[r0:r0 + CONV_CHUNK, :] = (zn * jax.nn.sigmoid(zn)).astype(BF16)


def _conformer_conv(pc, w_dw, b_dw, g_n, b_n, batch, seq, gmat):
    t = pc.shape[0]
    ch = GROUP_WIDTH
    full = lambda shape: pl.BlockSpec(shape, lambda b: (0,) * len(shape))
    return pl.pallas_call(
        _conv_kernel,
        grid=(batch,),
        in_specs=[
            pl.BlockSpec((seq, 2 * ch), lambda b: (b, 0)),
            full((CONV_K, 1, ch)), full((1, ch)), full((1, ch)), full((1, ch)), full(gmat.shape),
        ],
        out_specs=pl.BlockSpec((seq, ch), lambda b: (b, 0)),
        out_shape=jax.ShapeDtypeStruct((t, ch), BF16),
        scratch_shapes=[pltpu.VMEM((seq + 2 * CONV_PAD, ch), F32),
                        pltpu.VMEM((SUBLANES - 1, seq + 2 * CONV_PAD, ch), F32)],
        compiler_params=_cparams("arbitrary"),
        name="conformer_conv",
    )(pc, w_dw.reshape(CONV_K, 1, ch), b_dw.reshape(1, ch), g_n.reshape(1, ch), b_n.reshape(1, ch), gmat)


DIL_QB = 128
DIL_KB = 256
DIL_UNITS_PER_ITER = 2


def _strided_rows(first, count, stride):
    return pl.ds(first, count) if stride == 1 else pl.ds(first, count, stride=stride)


def _load_rows(ref, rows):
    return jnp.concatenate([ref[j, rows, :] for j in range(ref.shape[0])], axis=1)


def _store_rows(ref, rows, val):
    for j in range(ref.shape[0]):
        ref[j, rows, :] = val[:, j * LANES:(j + 1) * LANES]


def _dilated_kernel(q_ref, k_ref, v_ref, hmask_ref, hmask_f32_ref, o_ref, acc_ref, m_ref, l_ref):
    seq = q_ref.shape[1]
    for branch, (window, dil) in enumerate(DILATED_CFG):
        n_side = window // (2 * dil)
        sub_len = seq // dil
        qb = min(DIL_QB, sub_len)
        kb = min(DIL_KB, sub_len)
        rel = (lax.broadcasted_iota(jnp.int32, (N_HEADS * qb, kb), 1)
               - (lax.broadcasted_iota(jnp.int32, (N_HEADS * qb, kb), 0) & (qb - 1)))

        def scores(r, c, dil=dil, n_side=n_side, sub_len=sub_len, qb=qb, kb=kb, rel=rel):
            l0 = c * qb
            kl0 = jnp.clip(l0 - n_side, 0, sub_len - kb)
            if dil == 1:
                l0, kl0 = pl.multiple_of(l0, qb), pl.multiple_of(kl0, n_side)
            q_rows = _strided_rows(r + dil * l0, qb, dil)
            k_rows = _strided_rows(r + dil * kl0, kb, dil)
            q = _load_rows(q_ref, q_rows).astype(BF16)
            qs = jnp.concatenate([q * hmask_ref[h] for h in range(N_HEADS)], axis=0)
            s = _dot_nt(qs, _load_rows(k_ref, k_rows).astype(BF16))
            return jnp.where(jnp.abs(rel + (kl0 - l0)) <= n_side, s, NEG_INF), q_rows, k_rows

        def attend(s, q_rows, k_rows, qb=qb, branch=branch):
            m = jnp.max(s, axis=-1, keepdims=True)
            e = jnp.exp(s - m)
            l = jnp.sum(e, axis=-1, keepdims=True)
            pv = _dot(e.astype(BF16), _load_rows(v_ref, k_rows).astype(BF16))

            def unstack(a):
                out = a[(N_HEADS - 1) * qb:N_HEADS * qb]
                for h in range(N_HEADS - 2, -1, -1):
                    out = jnp.where(hmask_f32_ref[h] > 0.5, a[h * qb:(h + 1) * qb], out)
                return out

            acc_new, m_new, l_new = unstack(pv), unstack(m), unstack(l)
            if branch == 0:
                _store_rows(acc_ref, q_rows, acc_new)
                _store_rows(m_ref, q_rows, m_new)
                _store_rows(l_ref, q_rows, l_new)
            else:
                m_old = _load_rows(m_ref, q_rows)
                m_max = jnp.maximum(m_old, m_new)
                w_old = jnp.exp(m_old - m_max)
                w_new = jnp.exp(m_new - m_max)
                _store_rows(acc_ref, q_rows, _load_rows(acc_ref, q_rows) * w_old + acc_new * w_new)
                _store_rows(l_ref, q_rows, _load_rows(l_ref, q_rows) * w_old + l_new * w_new)
                _store_rows(m_ref, q_rows, m_max)

        def run(units, scores=scores, attend=attend):
            staged = [scores(r, c) for r, c in units]
            for item in staged:
                attend(*item)

        n_blocks = sub_len // qb
        if n_blocks >= DIL_UNITS_PER_ITER:
            for r in range(dil):
                def block_group(p, carry, r=r, run=run):
                    run([(r, DIL_UNITS_PER_ITER * p + u) for u in range(DIL_UNITS_PER_ITER)])
                    return carry

                lax.fori_loop(0, n_blocks // DIL_UNITS_PER_ITER, block_group, 0)
        else:
            for r0 in range(0, dil, DIL_UNITS_PER_ITER):
                run([(r0 + u, 0) for u in range(DIL_UNITS_PER_ITER)])
    for j in range(acc_ref.shape[0]):
        o_ref[:, j * LANES:(j + 1) * LANES] = (acc_ref[j] / l_ref[j]).astype(BF16)


def _dilated_attention(qd, kd, vd, batch, seq, masks):
    tiles, t, _ = qd.shape
    gw = tiles * LANES
    hmask, hmask_f32, _ = masks
    in_spec = pl.BlockSpec((tiles, seq, LANES), lambda b: (0, b, 0))
    stat = pltpu.VMEM((tiles, seq, LANES), F32)
    return pl.pallas_call(
        _dilated_kernel,
        grid=(batch,),
        in_specs=[
            in_spec, in_spec, in_spec,
            pl.BlockSpec(hmask.shape, lambda b: (0, 0, 0)),
            pl.BlockSpec(hmask_f32.shape, lambda b: (0, 0, 0)),
        ],
        out_specs=pl.BlockSpec((seq, gw), lambda b: (b, 0)),
        out_shape=jax.ShapeDtypeStruct((t, gw), BF16),
        scratch_shapes=[stat, stat, stat],
        compiler_params=_cparams("arbitrary"),
        name="dilated_attention",
    )(qd, kd, vd, hmask, hmask_f32)


def _outproj_kernel(ya_ref, yb_ref, yc_ref, yd_ref, w_ref, x_ref, mod_ref, g_ref, wr_ref,
                    x1_ref, h2_lo_ref, h2_hi_ref, lg_ref):
    gw = GROUP_WIDTH
    mix = _dot(ya_ref[...], w_ref[0:gw, :])
    mix = mix + _dot(yb_ref[...], w_ref[gw:2 * gw, :])
    mix = mix + _dot(yc_ref[...], w_ref[2 * gw:3 * gw, :])
    mix = mix + _dot(yd_ref[...], w_ref[3 * gw:4 * gw, :])
    mod = mod_ref[0]
    x1 = x_ref[...] + mod[2:3] * mix
    x1_ref[...] = x1
    ms = jnp.mean(x1 * x1, axis=-1, keepdims=True)
    h2 = (x1 * lax.rsqrt(ms + EPS) * g_ref[...]) * (1.0 + mod[4:5]) + mod[3:4]
    packed = _pack_bf16_pairs(h2)
    quarter = h2_lo_ref.shape[1]
    h2_lo_ref[...] = packed[:, :quarter]
    h2_hi_ref[...] = packed[:, quarter:]
    lg_ref[...] = _dot(h2.astype(BF16), wr_ref[...])


def _output_projection(ys, w_out_bf16, x2d, mod_l, g2, w_router, seq):
    t, d = x2d.shape
    tm = TOKEN_TILE
    tiles_per_batch = seq // tm
    gw = GROUP_WIDTH
    row_spec = lambda width: pl.BlockSpec((tm, width), lambda i: (i, 0))
    return pl.pallas_call(
        _outproj_kernel,
        grid=(t // tm,),
        in_specs=[
            row_spec(gw), row_spec(gw), row_spec(gw), row_spec(gw),
            pl.BlockSpec((d, d), lambda i: (0, 0)),
            row_spec(d),
            pl.BlockSpec((1, 6, d), lambda i: (i // tiles_per_batch, 0, 0)),
            pl.BlockSpec((1, d), lambda i: (0, 0)),
            pl.BlockSpec((d, ROUTER_LANES), lambda i: (0, 0)),
        ],
        out_specs=[row_spec(d), row_spec(d // 4), row_spec(d // 4), row_spec(ROUTER_LANES)],
        out_shape=[jax.ShapeDtypeStruct((t, d), F32), jax.ShapeDtypeStruct((t, d // 4), jnp.uint32),
                   jax.ShapeDtypeStruct((t, d // 4), jnp.uint32), jax.ShapeDtypeStruct((t, ROUTER_LANES), F32)],
        compiler_params=_cparams("arbitrary"),
        name="output_projection",
    )(*ys, w_out_bf16, x2d, mod_l, g2.reshape(1, d), w_router)


def _routing_kernel(lg_ref, info_ref, cnt_ref, carry_ref):
    tr = lg_ref.shape[0]

    @pl.when(pl.program_id(0) == 0)
    def _():
        carry_ref[...] = jnp.zeros_like(carry_ref)

    lg = lg_ref[...]
    lane = lax.broadcasted_iota(jnp.int32, lg.shape, 1).astype(F32)
    big = float(ROUTER_LANES)
    glog = jnp.where(lane < N_GROUPS, lg, -jnp.inf)
    gmax = jnp.max(glog, axis=-1, keepdims=True)
    p_grp = 1.0 / jnp.sum(jnp.exp(glog - gmax), axis=-1, keepdims=True)
    grp = jnp.min(jnp.where(glog == gmax, lane, big), axis=-1, keepdims=True)
    lo = N_GROUPS + EXPERTS_PER_GROUP * grp
    elog = jnp.where((lane >= lo) & (lane < lo + EXPERTS_PER_GROUP), lg, -jnp.inf)
    v1 = jnp.max(elog, axis=-1, keepdims=True)
    i1 = jnp.min(jnp.where(elog == v1, lane, big), axis=-1, keepdims=True)
    elog2 = jnp.where(lane == i1, -jnp.inf, elog)
    v2 = jnp.max(elog2, axis=-1, keepdims=True)
    i2 = jnp.min(jnp.where(elog2 == v2, lane, big), axis=-1, keepdims=True)
    d = jnp.exp(v2 - v1)
    gate1 = p_grp / (1.0 + d)
    gate2 = p_grp * d / (1.0 + d)
    sel1 = lane == i1
    sel2 = lane == i2
    sel = jnp.where(sel1 | sel2, 1.0, 0.0)
    row = lax.broadcasted_iota(jnp.int32, (tr, tr), 0)
    col = lax.broadcasted_iota(jnp.int32, (tr, tr), 1)
    before = jnp.where(col < row, 1.0, 0.0).astype(BF16)
    rank = _dot(before, sel.astype(BF16)) + carry_ref[...]
    r1 = jnp.sum(jnp.where(sel1, rank, 0.0), axis=-1, keepdims=True)
    r2 = jnp.sum(jnp.where(sel2, rank, 0.0), axis=-1, keepdims=True)
    carry_ref[...] += jnp.sum(sel, axis=0, keepdims=True)
    cnt_ref[...] = carry_ref[...]
    info = jnp.zeros_like(lg)
    for idx, val in enumerate((i1 - N_GROUPS, i2 - N_GROUPS, r1, r2, gate1, gate2)):
        info = jnp.where(lane == idx, val, info)
    info_ref[...] = info


def _routing(logits):
    t = logits.shape[0]
    tr = TOKEN_TILE
    return pl.pallas_call(
        _routing_kernel,
        grid=(t // tr,),
        in_specs=[pl.BlockSpec((tr, ROUTER_LANES), lambda i: (i, 0))],
        out_specs=[pl.BlockSpec((tr, ROUTER_LANES), lambda i: (i, 0)),
                   pl.BlockSpec((1, ROUTER_LANES), lambda i: (0, 0))],
        out_shape=[jax.ShapeDtypeStruct((t, ROUTER_LANES), F32),
                   jax.ShapeDtypeStruct((1, ROUTER_LANES), F32)],
        scratch_shapes=[pltpu.VMEM((1, ROUTER_LANES), F32)],
        compiler_params=_cparams("arbitrary"),
        name="routing",
    )(logits)


SC_GATHER_WINDOW = LANES


def _gather_rows(table, indices):
    n = indices.shape[0]
    width = table.shape[1]
    mesh = plsc.VectorSubcoreMesh(core_axis_name="core", subcore_axis_name="subcore")

    @pl.kernel(out_type=jax.ShapeDtypeStruct((n, width), table.dtype), mesh=mesh, scratch_types=[],
               name="moe_row_gather")
    def gather_kernel(table_hbm, idx_hbm, out_hbm):
        def body(idx_vmem, out_vmem):
            pltpu.sync_copy(table_hbm.at[idx_vmem.at[0]], out_vmem)

        pltpu.emit_pipeline(
            body,
            grid=(n // SC_GATHER_WINDOW,),
            in_specs=[pl.BlockSpec((1, SC_GATHER_WINDOW), index_map=lambda i: (0, i))],
            out_specs=[pl.BlockSpec((SC_GATHER_WINDOW, width), index_map=lambda i: (i, 0))],
            core_axis_name=("core", "subcore"),
            dimension_semantics=(pltpu.PARALLEL,),
        )(idx_hbm, out_hbm)

    return gather_kernel(table, indices.reshape(1, n))


def _scatter_rows(table, indices):
    n = indices.shape[0]
    rows, width = table.shape
    mesh = plsc.VectorSubcoreMesh(core_axis_name="core", subcore_axis_name="subcore")
    table_windows = rows // SC_GATHER_WINDOW

    @pl.kernel(out_type=jax.ShapeDtypeStruct((n, width), table.dtype), mesh=mesh, scratch_types=[],
               name="moe_row_scatter")
    def scatter_kernel(table_hbm, idx_hbm, out_hbm):
        def body(rows_vmem, idx_vmem):
            pltpu.sync_copy(rows_vmem, out_hbm.at[idx_vmem.at[0]])

        pltpu.emit_pipeline(
            body,
            grid=(n // SC_GATHER_WINDOW,),
            in_specs=[pl.BlockSpec((SC_GATHER_WINDOW, width), index_map=lambda i: (i % table_windows, 0)),
                      pl.BlockSpec((1, SC_GATHER_WINDOW), index_map=lambda i: (0, i))],
            out_specs=[],
            core_axis_name=("core", "subcore"),
            dimension_semantics=(pltpu.PARALLEL,),
        )(table_hbm, idx_hbm)

    return scatter_kernel(table, indices.reshape(1, n))


def _dispatch_targets(dest1, dest2, counts, pad_start, pad_end, n_rows):
    n_pad = n_rows - dest1.shape[0] - dest2.shape[0]
    pad_first = jnp.concatenate([pad_start + counts, pad_end[-1:]])
    pad_count = jnp.concatenate([pad_end - pad_start - counts, n_rows - pad_end[-1:]])
    cum = jnp.cumsum(pad_count)
    k = jnp.arange(n_pad, dtype=jnp.int32)
    seg = jnp.sum((cum[None, :] <= k[:, None]).astype(jnp.int32), axis=1)
    onehot = seg[:, None] == jnp.arange(cum.shape[0], dtype=jnp.int32)[None, :]
    offset = k - jnp.sum(jnp.where(onehot, (cum - pad_count)[None, :], 0), axis=1)
    pad_rows = jnp.sum(jnp.where(onehot, pad_first[None, :], 0), axis=1) + offset
    return jnp.concatenate([dest1, dest2, pad_rows])


def _expert_kernel(blk_e_ref, nvalid_ref, xs_lo_ref, xs_hi_ref, wg_ref, wu_ref, wd_ref, ys_lo_ref, ys_hi_ref,
                   wg_bf, wu_bf, wd_bf):
    j = pl.program_id(0)
    half = ys_lo_ref.shape[1]

    @pl.when((j == 0) | (blk_e_ref[j] != blk_e_ref[jnp.maximum(j - 1, 0)]))
    def _():
        wg_bf[...] = wg_ref[0, 0].astype(BF16)
        wu_bf[...] = wu_ref[0, 0].astype(BF16)
        wd_bf[...] = wd_ref[0, 0].astype(BF16)

    @pl.when(j < nvalid_ref[0])
    def _():
        xb = _unpack_bf16_pairs(jnp.concatenate([xs_lo_ref[...], xs_hi_ref[...]], axis=1)).astype(BF16)
        gate = _dot(xb, wg_bf[...])
        up = _dot(xb, wu_bf[...])
        hdn = (gate * jax.nn.sigmoid(gate)) * up
        packed = _pack_bf16_pairs(_dot(hdn.astype(BF16), wd_bf[...]))
        ys_lo_ref[...] = packed[:, :half]
        ys_hi_ref[...] = packed[:, half:]

    @pl.when(j >= nvalid_ref[0])
    def _():
        ys_lo_ref[...] = jnp.zeros_like(ys_lo_ref)
        ys_hi_ref[...] = jnp.zeros_like(ys_hi_ref)


def _expert_mlp(xs_lo, xs_hi, blk_e, nvalid, w_gate, w_up, w_down, layer):
    n_rows, quarter = xs_lo.shape
    nblk = n_rows // MOE_BLOCK
    d, de = w_gate.shape[2:]

    def x_map(j, be, nv):
        return (jnp.minimum(j, nv[0] - 1), 0)

    w_map = lambda j, be, nv: (layer, be[j], 0, 0)
    grid_spec = pltpu.PrefetchScalarGridSpec(
        num_scalar_prefetch=2,
        grid=(nblk,),
        in_specs=[
            pl.BlockSpec((MOE_BLOCK, quarter), x_map),
            pl.BlockSpec((MOE_BLOCK, quarter), x_map),
            pl.BlockSpec((1, 1, d, de), w_map),
            pl.BlockSpec((1, 1, d, de), w_map),
            pl.BlockSpec((1, 1, de, d), w_map),
        ],
        out_specs=[pl.BlockSpec((MOE_BLOCK, quarter), lambda j, be, nv: (j, 0))] * 2,
        scratch_shapes=[pltpu.VMEM((d, de), BF16), pltpu.VMEM((d, de), BF16), pltpu.VMEM((de, d), BF16)],
    )
    return pl.pallas_call(
        _expert_kernel,
        grid_spec=grid_spec,
        out_shape=[jax.ShapeDtypeStruct((n_rows, quarter), jnp.uint32)] * 2,
        compiler_params=_cparams("arbitrary"),
        name="expert_mlp",
    )(blk_e, nvalid, xs_lo, xs_hi, w_gate, w_up, w_down)


def _final_combine_kernel(x_ref, mod_ref, info_ref, gf_ref, y1_lo_ref, y1_hi_ref, y2_lo_ref, y2_hi_ref, o_ref):
    x2 = _moe_residual(x_ref, mod_ref, info_ref, y1_lo_ref, y1_hi_ref, y2_lo_ref, y2_hi_ref)
    ms = jnp.mean(x2 * x2, axis=-1, keepdims=True)
    o_ref[...] = x2 * lax.rsqrt(ms + EPS) * gf_ref[...]


def _gather_expert_rows(ys, dest1, dest2):
    dest = jnp.concatenate([dest1, dest2])
    return _gather_rows(ys[0], dest), _gather_rows(ys[1], dest)


def _final_combine(x1, mod_l, info, g_lo, g_hi, g_final, seq):
    t, d = x1.shape
    tc = TOKEN_TILE
    tiles_per_batch = seq // tc
    n_tiles = t // tc
    quarter = g_lo.shape[1]
    row_spec = lambda width: pl.BlockSpec((tc, width), lambda i: (i, 0))
    slot2_spec = pl.BlockSpec((tc, quarter), lambda i: (i + n_tiles, 0))
    return pl.pallas_call(
        _final_combine_kernel,
        grid=(n_tiles,),
        in_specs=[
            row_spec(d),
            pl.BlockSpec((1, 6, d), lambda i: (i // tiles_per_batch, 0, 0)),
            row_spec(ROUTER_LANES),
            pl.BlockSpec((1, d), lambda i: (0, 0)),
            row_spec(quarter), row_spec(quarter), slot2_spec, slot2_spec,
        ],
        out_specs=row_spec(d),
        out_shape=jax.ShapeDtypeStruct((t, d), F32),
        compiler_params=_cparams("arbitrary"),
        name="moe_combine",
    )(x1, mod_l, info, g_final.reshape(1, d), g_lo, g_hi, g_lo, g_hi)


def _router_weights(w_rg, w_re):
    d = w_rg.shape[0]
    w_experts = jnp.transpose(w_re, (1, 0, 2)).reshape(d, N_EXPERTS)
    pad = jnp.zeros((d, ROUTER_LANES - N_GROUPS - N_EXPERTS), F32)
    return jnp.concatenate([w_rg, w_experts, pad], axis=1)


def _block_layout(counts_row, n_blocks):
    counts = counts_row[0, N_GROUPS:N_GROUPS + N_EXPERTS].astype(jnp.int32)
    padded = (counts + MOE_BLOCK - 1) // MOE_BLOCK * MOE_BLOCK
    pad_end = jnp.cumsum(padded)
    pad_start = pad_end - padded
    starts = jnp.arange(n_blocks, dtype=jnp.int32) * MOE_BLOCK
    blk_e = jnp.minimum(jnp.sum((pad_end[None, :] <= starts[:, None]).astype(jnp.int32), axis=1), N_EXPERTS - 1)
    nvalid = (pad_end[-1:] // MOE_BLOCK).astype(jnp.int32)
    return counts, pad_start, pad_end, blk_e, nvalid


def _destinations(info, pad_start):
    ids = info[:, 0:4].astype(jnp.int32)
    experts = jnp.arange(N_EXPERTS, dtype=jnp.int32)[None, :]

    def segment_start(e):
        return jnp.sum(jnp.where(e[:, None] == experts, pad_start[None, :], 0), axis=1)

    return segment_start(ids[:, 0]) + ids[:, 2], segment_start(ids[:, 1]) + ids[:, 3]


def kernel(x, c, w_ada, b_ada, g_norm1, g_norm2, w_in, diff_lambda, diff_subln, na_rpb, conv_dw, conv_b,
           conv_norm_g, conv_norm_b, w_out, w_router_group, w_router_expert, w_exp_gate, w_exp_up,
           w_exp_down, g_final):
    batch, seq, d = x.shape
    depth = w_ada.shape[0]
    t = batch * seq
    assert d == D_MODEL and seq == GRID_ROWS * GRID_W
    n_rows = t * 2 + N_EXPERTS * MOE_BLOCK

    mod = _ada_modulation(c, w_ada, b_ada).reshape(depth, batch, 6, d)
    rope_a = _rope_tables(seq, DIFF_DH)
    rope_d = _rope_tables(seq, HEAD_DIM)
    masks = _head_masks()
    gmat = _block_diag_ones(GROUP_WIDTH, HEAD_DIM)

    x2d = x.reshape(t, d)
    pending = None
    for l in range(depth):
        mod_l = mod[l]
        projected = _input_projection(x2d, mod_l, g_norm1[l], w_in[l].astype(BF16), rope_a, rope_d, seq, pending)
        if pending is not None:
            x2d, projected = projected[0], projected[1:]
        qa, ka, va, qb, kb, vb, pc, qd, kd, vd = projected
        ya = _diff_attention(qa, ka, va, diff_lambda[l], diff_subln[l], l, batch, seq, masks, gmat)
        yb = _neighborhood_attention(qb, kb, vb, _na_bias_table(na_rpb[l]), batch, seq, masks)
        yc = _conformer_conv(pc, conv_dw[l], conv_b[l], conv_norm_g[l], conv_norm_b[l], batch, seq, gmat)
        yd = _dilated_attention(qd, kd, vd, batch, seq, masks)
        x1, h2_lo, h2_hi, logits = _output_projection(
            (ya, yb, yc, yd), w_out[l].astype(BF16), x2d, mod_l, g_norm2[l],
            _router_weights(w_router_group[l], w_router_expert[l]).astype(BF16), seq)
        info, counts = _routing(logits)
        counts, pad_start, pad_end, blk_e, nvalid = _block_layout(counts, n_rows // MOE_BLOCK)
        dest1, dest2 = _destinations(info, pad_start)
        targets = _dispatch_targets(dest1, dest2, counts, pad_start, pad_end, n_rows)
        ys = _expert_mlp(_scatter_rows(h2_lo, targets), _scatter_rows(h2_hi, targets), blk_e, nvalid,
                         w_exp_gate, w_exp_up, w_exp_down, l)
        pending = (x1, mod_l, info, *_gather_expert_rows(ys, dest1, dest2))
    return _final_combine(*pending, g_final, seq).reshape(batch, seq, d)
```

```python
import functools
import math

import numpy as np
import jax
import jax.numpy as jnp
from jax import lax
from jax.experimental import pallas as pl
from jax.experimental.pallas import tpu as pltpu
from jax.experimental.pallas import tpu_sc as plsc

F32 = jnp.float32
BF16 = jnp.bfloat16

D_MODEL = 1024
GROUP_WIDTH = 256
HEAD_DIM = 64
N_HEADS = 4
DIFF_DH = 32
CONV_K = 31
CONV_GROUP_CH = 64
GRID_W = 64
NA_KH = 8
NA_KW = 16
GRID_ROWS = 32
NA_ROWS_PER_ITER = 4
ROPE_THETA = 10000.0
N_GROUPS = 4
EXPERTS_PER_GROUP = 8
N_EXPERTS = 32
MOE_BLOCK = 512
EPS = 1e-6
NEG_INF = -1e30
LOG2E = 1.4426950408889634
LANES = 128
SUBLANES = 8
ROUTER_LANES = LANES
DILATED_CFG = ((128, 1), (512, 4), (2048, 16))
DIFF_EXP2_SCALE = (DIFF_DH ** -0.5) * LOG2E

VMEM_LIMIT = 56 * 1024 * 1024

TOKEN_TILE = 512
DIFF_Q_TILE = 1024
ADA_COL_TILE = 1024


def _cparams(*sem):
    return pltpu.CompilerParams(dimension_semantics=sem, vmem_limit_bytes=VMEM_LIMIT)


def _dot(a, b):
    return jnp.dot(a, b, preferred_element_type=F32)


def _dot_nt(a, b):
    return lax.dot_general(a, b, (((1,), (1,)), ((), ())), preferred_element_type=F32)


def _split(a):
    hi = a.astype(BF16)
    lo = (a - hi.astype(F32)).astype(BF16)
    return hi, lo


def _dot3(a, b):
    ah, al = _split(a)
    bh, bl = _split(b)
    return _dot(ah, bh) + (_dot(ah, bl) + _dot(al, bh))


def _group_mean(v, gmat, width):
    hi, lo = _split(v)
    return (_dot(hi, gmat) + _dot(lo, gmat)) * (1.0 / width)


HIGH_HALF = 0xFFFF0000


def _pack_bf16_pairs(a):
    n = a.shape[1] // 2
    bits = lax.bitcast_convert_type(a.astype(BF16).astype(F32), jnp.uint32)
    return (bits[:, :n] >> 16) | (bits[:, n:] & jnp.uint32(HIGH_HALF))


def _unpack_bf16_pairs(u):
    lo = lax.bitcast_convert_type(u << 16, F32)
    hi = lax.bitcast_convert_type(u & jnp.uint32(HIGH_HALF), F32)
    return jnp.concatenate([lo, hi], axis=1)


def _block_diag_ones(n, width):
    idx = np.arange(n) // width
    return jnp.asarray((idx[:, None] == idx[None, :]).astype(np.float32), dtype=BF16)


def _ada_kernel(c_ref, w_ref, b_ref, o_ref):
    c = c_ref[...]
    ca = c * jax.nn.sigmoid(c)
    o_ref[0] = _dot3(ca, w_ref[0]) + b_ref[0]


def _ada_modulation(c, w_ada, b_ada):
    depth, d, n = w_ada.shape
    b = c.shape[0]
    bn = ADA_COL_TILE
    return pl.pallas_call(
        _ada_kernel,
        grid=(depth, n // bn),
        in_specs=[
            pl.BlockSpec((b, d), lambda l, j: (0, 0)),
            pl.BlockSpec((1, d, bn), lambda l, j: (l, 0, j)),
            pl.BlockSpec((1, 1, bn), lambda l, j: (l, 0, j)),
        ],
        out_specs=pl.BlockSpec((1, b, bn), lambda l, j: (l, 0, j)),
        out_shape=jax.ShapeDtypeStruct((depth, b, n), F32),
        compiler_params=_cparams("arbitrary", "arbitrary"),
        name="ada_modulation",
    )(c, w_ada, b_ada.reshape(depth, 1, n))


def _rope_tables(seq, dim):
    half = dim // 2
    inv = ROPE_THETA ** (-jnp.arange(0, dim, 2, dtype=F32) / dim)
    ang = jnp.arange(seq, dtype=F32)[:, None] * inv[None, :]
    cos, sin = jnp.cos(ang), jnp.sin(ang)
    reps = LANES // dim
    zeros = jnp.zeros_like(sin)
    cos_t = jnp.tile(jnp.concatenate([cos, cos], axis=1), (1, reps))
    sin_hi = jnp.tile(jnp.concatenate([zeros, sin], axis=1), (1, reps))
    sin_lo = jnp.tile(jnp.concatenate([-sin, zeros], axis=1), (1, reps))
    return cos_t, sin_hi, sin_lo


def _rotary(v, cos_t, sin_hi, sin_lo, half):
    outs = []
    for j in range(v.shape[1] // LANES):
        vj = v[:, j * LANES:(j + 1) * LANES]
        outs.append(vj * cos_t + pltpu.roll(vj, half, 1) * sin_hi + pltpu.roll(vj, LANES - half, 1) * sin_lo)
    return jnp.concatenate(outs, axis=1)


def _inproj_kernel(x_ref, *rest):
    _project_tokens(x_ref[...], *rest)


def _combine_inproj_kernel(x1_ref, modp_ref, info_ref, y1_lo_ref, y1_hi_ref, y2_lo_ref, y2_hi_ref, *rest):
    proj_refs, x2_ref, out_refs = rest[:9], rest[9], rest[10:]
    x = _moe_residual(x1_ref, modp_ref, info_ref, y1_lo_ref, y1_hi_ref, y2_lo_ref, y2_hi_ref)
    x2_ref[...] = x
    _project_tokens(x, *proj_refs, *out_refs)


def _moe_residual(x_ref, mod_ref, info_ref, y1_lo_ref, y1_hi_ref, y2_lo_ref, y2_hi_ref):
    info = info_ref[...]
    y1 = _unpack_bf16_pairs(jnp.concatenate([y1_lo_ref[...], y1_hi_ref[...]], axis=1))
    y2 = _unpack_bf16_pairs(jnp.concatenate([y2_lo_ref[...], y2_hi_ref[...]], axis=1))
    return x_ref[...] + mod_ref[0][5:6] * (info[:, 4:5] * y1 + info[:, 5:6] * y2)


def _project_tokens(x, mod_ref, g_ref, w_ref, ca_ref, sha_ref, sla_ref, cd_ref, shd_ref, sld_ref,
                    qa_ref, ka_ref, va_ref, qb_ref, kb_ref, vb_ref, pc_ref, qd_ref, kd_ref, vd_ref):
    ms = jnp.mean(x * x, axis=-1, keepdims=True)
    y = x * lax.rsqrt(ms + EPS)
    mod = mod_ref[0]
    h = (y * g_ref[...]) * (1.0 + mod[1:2]) + mod[0:1]
    hb = h.astype(BF16)
    gw = GROUP_WIDTH

    def proj(col):
        return _dot(hb, w_ref[:, col * gw:(col + 1) * gw])

    rot_a = functools.partial(_rotary, cos_t=ca_ref[...], sin_hi=sha_ref[...], sin_lo=sla_ref[...],
                              half=DIFF_DH // 2)
    rot_d = functools.partial(_rotary, cos_t=cd_ref[...], sin_hi=shd_ref[...], sin_lo=sld_ref[...],
                              half=HEAD_DIM // 2)
    na_scale = HEAD_DIM ** -0.5
    qa_ref[...] = (rot_a(proj(0)) * DIFF_EXP2_SCALE).astype(BF16)
    ka_ref[...] = rot_a(proj(1)).astype(BF16)
    va_ref[...] = proj(2).astype(BF16)
    qb_ref[...] = (proj(3) * na_scale).astype(BF16)
    kb_ref[...] = proj(4).astype(BF16)
    vb_ref[...] = proj(5).astype(BF16)
    pc_ref[:, 0:gw] = proj(6)
    pc_ref[:, gw:2 * gw] = proj(7)
    for ref, val in ((qd_ref, rot_d(proj(8)) * na_scale), (kd_ref, rot_d(proj(9))), (vd_ref, proj(10))):
        for j in range(gw // LANES):
            ref[j] = val[:, j * LANES:(j + 1) * LANES]


def _input_projection(x2d, mod_l, g1, w_in_bf16, rope_a, rope_d, seq, pending=None):
    t, d = x2d.shape if pending is None else pending[0].shape
    tm = TOKEN_TILE
    tiles_per_batch = seq // tm
    p_in = w_in_bf16.shape[1]
    gw = GROUP_WIDTH
    row_spec = lambda width: pl.BlockSpec((tm, width), lambda i: (i, 0))
    tab_spec = pl.BlockSpec((tm, LANES), lambda i: (i % tiles_per_batch, 0))
    out_shapes = []
    out_specs = []
    for name in ("qa", "ka", "va", "qb", "kb", "vb", "pc", "qd", "kd", "vd"):
        if name == "pc":
            out_shapes.append(jax.ShapeDtypeStruct((t, 2 * gw), F32))
            out_specs.append(row_spec(2 * gw))
        elif name[1] == "d":
            out_shapes.append(jax.ShapeDtypeStruct((gw // LANES, t, LANES), F32))
            out_specs.append(pl.BlockSpec((gw // LANES, tm, LANES), lambda i: (0, i, 0)))
        else:
            out_shapes.append(jax.ShapeDtypeStruct((t, gw), BF16))
            out_specs.append(row_spec(gw))
    mod_spec = pl.BlockSpec((1, 6, d), lambda i: (i // tiles_per_batch, 0, 0))
    proj_specs = [
        mod_spec,
        pl.BlockSpec((1, d), lambda i: (0, 0)),
        pl.BlockSpec((d, p_in), lambda i: (0, 0)),
        tab_spec, tab_spec, tab_spec, tab_spec, tab_spec, tab_spec,
    ]
    proj_args = (mod_l, g1.reshape(1, d), w_in_bf16, *rope_a, *rope_d)
    if pending is None:
        return pl.pallas_call(
            _inproj_kernel,
            grid=(t // tm,),
            in_specs=[row_spec(d)] + proj_specs,
            out_specs=out_specs,
            out_shape=out_shapes,
            compiler_params=_cparams("arbitrary"),
            name="input_projection",
        )(x2d, *proj_args)
    x1, mod_prev, info, g_lo, g_hi = pending
    quarter = g_lo.shape[1]
    n_tiles = t // tm
    slot2_spec = pl.BlockSpec((tm, quarter), lambda i: (i + n_tiles, 0))
    return pl.pallas_call(
        _combine_inproj_kernel,
        grid=(n_tiles,),
        in_specs=[row_spec(d), mod_spec, row_spec(ROUTER_LANES), row_spec(quarter), row_spec(quarter),
                  slot2_spec, slot2_spec] + proj_specs,
        out_specs=[row_spec(d)] + out_specs,
        out_shape=[jax.ShapeDtypeStruct((t, d), F32)] + out_shapes,
        compiler_params=_cparams("arbitrary"),
        name="combine_input_projection",
    )(x1, mod_prev, info, g_lo, g_hi, g_lo, g_hi, *proj_args)


def _head_masks():
    lane = np.arange(GROUP_WIDTH)
    head = np.stack([(lane // HEAD_DIM == h) for h in range(N_HEADS)]).astype(np.float32)
    diff = np.stack([(lane // DIFF_DH == j) for j in range(2 * N_HEADS)]).astype(np.float32)
    return (jnp.asarray(head[:, None, :], dtype=BF16), jnp.asarray(head[:, None, :], dtype=F32),
            jnp.asarray(diff[:, None, :], dtype=BF16))


DIFF_ONES_ROWS = 16


def _diff_attn_kernel(lam_init, q_ref, k_ref, v_ref, lp_ref, g_ref, dmask_ref, gmat_ref, o_ref, vt_ref, ot_ref):
    seq = k_ref.shape[0]

    @pl.when(pl.program_id(1) == 0)
    def _():
        vt = jnp.transpose(v_ref[...].astype(F32))
        for h in range(N_HEADS):
            vt_ref[h, 0:HEAD_DIM, :] = vt[h * HEAD_DIM:(h + 1) * HEAD_DIM].astype(BF16)
            vt_ref[h, HEAD_DIM:HEAD_DIM + DIFF_ONES_ROWS, :] = jnp.ones((DIFF_ONES_ROWS, seq), BF16)

    q = q_ref[...]
    k = k_ref[...]
    lp = lp_ref[...]
    lam = (jnp.exp(jnp.sum(lp[0:1] * lp[1:2], axis=-1, keepdims=True))
           - jnp.exp(jnp.sum(lp[2:3] * lp[3:4], axis=-1, keepdims=True)) + lam_init)

    def scores(j):
        return _dot_nt(k, q * dmask_ref[j])

    def weights(st):
        return jnp.exp2(st - jnp.max(st, axis=0, keepdims=True)).astype(BF16)

    def attend(j, e):
        num = _dot(vt_ref[j // 2], e)
        return num[0:HEAD_DIM] / num[HEAD_DIM:HEAD_DIM + 1]

    n_pairs = 2 * N_HEADS
    outs = [None] * n_pairs
    st_next = scores(0)
    e_prev = None
    for j in range(n_pairs):
        st = st_next
        if j + 1 < n_pairs:
            st_next = scores(j + 1)
        e = weights(st)
        if e_prev is not None:
            outs[j - 1] = attend(j - 1, e_prev)
        e_prev = e
    outs[n_pairs - 1] = attend(n_pairs - 1, e_prev)
    for h in range(N_HEADS):
        ot_ref[h * HEAD_DIM:(h + 1) * HEAD_DIM, :] = outs[2 * h] - lam * outs[2 * h + 1]
    o = jnp.transpose(ot_ref[...])
    ms = _group_mean(o * o, gmat_ref[...], HEAD_DIM)
    o_ref[...] = ((o * lax.rsqrt(ms + EPS) * g_ref[...]) * (1.0 - lam_init)).astype(BF16)


def _diff_attention(qa, ka, va, lam_params, subln_g, layer_idx, batch, seq, masks, gmat):
    t, gw = qa.shape
    tq = DIFF_Q_TILE
    nq = seq // tq
    lam_init = 0.8 - 0.6 * math.exp(-0.3 * layer_idx)
    _, _, dmask = masks
    g_tiled = jnp.tile(subln_g, N_HEADS).reshape(1, gw)
    kv_spec = pl.BlockSpec((seq, gw), lambda b, i: (b, 0))
    full = lambda shape: pl.BlockSpec(shape, lambda b, i: (0,) * len(shape))
    return pl.pallas_call(
        functools.partial(_diff_attn_kernel, lam_init),
        grid=(batch, nq),
        in_specs=[
            pl.BlockSpec((tq, gw), lambda b, i: (b * nq + i, 0)),
            kv_spec, kv_spec,
            full(lam_params.shape), full((1, gw)), full(dmask.shape), full(gmat.shape),
        ],
        out_specs=pl.BlockSpec((tq, gw), lambda b, i: (b * nq + i, 0)),
        out_shape=jax.ShapeDtypeStruct((t, gw), BF16),
        scratch_shapes=[pltpu.VMEM((N_HEADS, HEAD_DIM + DIFF_ONES_ROWS, seq), BF16), pltpu.VMEM((gw, tq), F32)],
        compiler_params=_cparams("arbitrary", "arbitrary"),
        name="diff_attention",
    )(qa, ka, va, lam_params, g_tiled, dmask, gmat)


def _na_bias_table(rpb):
    w = GRID_W
    n_heads = rpb.shape[0]
    cq = np.arange(w)[:, None]
    ck = np.arange(w)[None, :]
    dc = np.clip(ck - cq, -(NA_KW - 1), NA_KW - 1) + NA_KW - 1
    col_start = np.clip(cq - NA_KW // 2, 0, w - NA_KW)
    col_ok = (ck >= col_start) & (ck < col_start + NA_KW)
    onehot = jnp.asarray(dc[:, :, None] == np.arange(2 * NA_KW - 1), dtype=F32)
    toep = jnp.einsum('qkd,hrd->hrqk', onehot, rpb.astype(F32), precision=lax.Precision.HIGHEST)
    toep = jnp.where(col_ok[None, None], toep, NEG_INF)
    tabs = jnp.stack([toep[:, NA_KH - 1 - off:2 * NA_KH - 1 - off] for off in range(NA_KH)])
    return jnp.transpose(tabs, (0, 1, 3, 2, 4)).reshape(NA_KH, n_heads * w, NA_KH * w)


def _na_kernel(q_ref, k_ref, v_ref, tab_ref, hmask_ref, hmask_f32_ref, o_ref):
    w = GRID_W
    nk = NA_KH * w

    def scores(i):
        row_start = jnp.clip(i - NA_KH // 2, 0, GRID_ROWS - NA_KH)
        kstart = pl.multiple_of(row_start * w, w)
        q = q_ref[pl.ds(pl.multiple_of(i * w, w), w), :]
        qs = jnp.concatenate([q * hmask_ref[h] for h in range(N_HEADS)], axis=0)
        return _dot_nt(qs, k_ref[pl.ds(kstart, nk), :]) + tab_ref[i - row_start], kstart

    def attend(i, s, kstart):
        m = jnp.max(s, axis=-1, keepdims=True)
        e = jnp.exp(s - m)
        p = e * (1.0 / jnp.sum(e, axis=-1, keepdims=True))
        pv = _dot(p.astype(BF16), v_ref[pl.ds(kstart, nk), :])
        o = pv[(N_HEADS - 1) * w:N_HEADS * w]
        for h in range(N_HEADS - 2, -1, -1):
            o = jnp.where(hmask_f32_ref[h] > 0.5, pv[h * w:(h + 1) * w], o)
        o_ref[pl.ds(pl.multiple_of(i * w, w), w), :] = o.astype(BF16)

    def grid_rows(p, carry):
        rows = [NA_ROWS_PER_ITER * p + u for u in range(NA_ROWS_PER_ITER)]
        staged = [scores(i) for i in rows]
        for i, (s, kstart) in zip(rows, staged):
            attend(i, s, kstart)
        return carry

    lax.fori_loop(0, GRID_ROWS // NA_ROWS_PER_ITER, grid_rows, 0)


def _neighborhood_attention(qb, kb, vb, bias_table, batch, seq, masks):
    t, gw = qb.shape
    hmask, hmask_f32, _ = masks
    seq_spec = pl.BlockSpec((seq, gw), lambda b: (b, 0))
    full = lambda shape: pl.BlockSpec(shape, lambda b: (0,) * len(shape))
    return pl.pallas_call(
        _na_kernel,
        grid=(batch,),
        in_specs=[seq_spec, seq_spec, seq_spec, full(bias_table.shape), full(hmask.shape), full(hmask_f32.shape)],
        out_specs=seq_spec,
        out_shape=jax.ShapeDtypeStruct((t, gw), BF16),
        compiler_params=_cparams("arbitrary"),
        name="neighborhood_attention",
    )(qb, kb, vb, bias_table, hmask, hmask_f32)


CONV_PAD = 16
CONV_CHUNK = 128


def _conv_kernel(pc_ref, w_ref, b_ref, gn_ref, bn_ref, gmat_ref, o_ref, zp_ref, zs_ref):
    seq, ch = o_ref.shape
    a = pc_ref[:, 0:ch]
    gate = pc_ref[:, ch:2 * ch]
    zp_ref[0:CONV_PAD, :] = jnp.zeros((CONV_PAD, ch), F32)
    zp_ref[CONV_PAD + seq:2 * CONV_PAD + seq, :] = jnp.zeros((CONV_PAD, ch), F32)
    zp_ref[CONV_PAD:CONV_PAD + seq, :] = a * jax.nn.sigmoid(gate)
    span = seq + 2 * CONV_PAD - SUBLANES
    for b in range(1, SUBLANES):
        zs_ref[b - 1, 0:span, :] = zp_ref[b:b + span, :]
    gmat = gmat_ref[...]
    first = CONV_PAD - CONV_K // 2
    for c in range(seq // CONV_CHUNK):
        r0 = c * CONV_CHUNK
        acc = jnp.zeros((CONV_CHUNK, ch), F32)
        for j in range(CONV_K):
            shift, aligned = (first + j) % SUBLANES, r0 + (first + j) // SUBLANES * SUBLANES
            src = zp_ref if shift == 0 else zs_ref.at[shift - 1]
            acc = acc + w_ref[j] * src[aligned:aligned + CONV_CHUNK, :]
        z = acc + b_ref[...]
        mu = _group_mean(z, gmat, CONV_GROUP_CH)
        dz = z - mu
        var = _group_mean(dz * dz, gmat, CONV_GROUP_CH)
        zn = dz * lax.rsqrt(var + EPS) * gn_ref[...] + bn_ref[...]
        o_ref[r0:r0 + CONV_CHUNK, :] = (zn * jax.nn.sigmoid(zn)).astype(BF16)


def _conformer_conv(pc, w_dw, b_dw, g_n, b_n, batch, seq, gmat):
    t = pc.shape[0]
    ch = GROUP_WIDTH
    full = lambda shape: pl.BlockSpec(shape, lambda b: (0,) * len(shape))
    return pl.pallas_call(
        _conv_kernel,
        grid=(batch,),
        in_specs=[
            pl.BlockSpec((seq, 2 * ch), lambda b: (b, 0)),
            full((CONV_K, 1, ch)), full((1, ch)), full((1, ch)), full((1, ch)), full(gmat.shape),
        ],
        out_specs=pl.BlockSpec((seq, ch), lambda b: (b, 0)),
        out_shape=jax.ShapeDtypeStruct((t, ch), BF16),
        scratch_shapes=[pltpu.VMEM((seq + 2 * CONV_PAD, ch), F32),
                        pltpu.VMEM((SUBLANES - 1, seq + 2 * CONV_PAD, ch), F32)],
        compiler_params=_cparams("arbitrary"),
        name="conformer_conv",
    )(pc, w_dw.reshape(CONV_K, 1, ch), b_dw.reshape(1, ch), g_n.reshape(1, ch), b_n.reshape(1, ch), gmat)


DIL_QB = 128
DIL_KB = 256
DIL_UNITS_PER_ITER = 2


def _strided_rows(first, count, stride):
    return pl.ds(first, count) if stride == 1 else pl.ds(first, count, stride=stride)


def _load_rows(ref, rows):
    return jnp.concatenate([ref[j, rows, :] for j in range(ref.shape[0])], axis=1)


def _store_rows(ref, rows, val):
    for j in range(ref.shape[0]):
        ref[j, rows, :] = val[:, j * LANES:(j + 1) * LANES]


def _dilated_kernel(q_ref, k_ref, v_ref, hmask_ref, hmask_f32_ref, o_ref, acc_ref, m_ref, l_ref):
    seq = q_ref.shape[1]
    for branch, (window, dil) in enumerate(DILATED_CFG):
        n_side = window // (2 * dil)
        sub_len = seq // dil
        qb = min(DIL_QB, sub_len)
        kb = min(DIL_KB, sub_len)
        rel = (lax.broadcasted_iota(jnp.int32, (N_HEADS * qb, kb), 1)
               - (lax.broadcasted_iota(jnp.int32, (N_HEADS * qb, kb), 0) & (qb - 1)))

        def scores(r, c, dil=dil, n_side=n_side, sub_len=sub_len, qb=qb, kb=kb, rel=rel):
            l0 = c * qb
            kl0 = jnp.clip(l0 - n_side, 0, sub_len - kb)
            if dil == 1:
                l0, kl0 = pl.multiple_of(l0, qb), pl.multiple_of(kl0, n_side)
            q_rows = _strided_rows(r + dil * l0, qb, dil)
            k_rows = _strided_rows(r + dil * kl0, kb, dil)
            q = _load_rows(q_ref, q_rows).astype(BF16)
            qs = jnp.concatenate([q * hmask_ref[h] for h in range(N_HEADS)], axis=0)
            s = _dot_nt(qs, _load_rows(k_ref, k_rows).astype(BF16))
            return jnp.where(jnp.abs(rel + (kl0 - l0)) <= n_side, s, NEG_INF), q_rows, k_rows

        def attend(s, q_rows, k_rows, qb=qb, branch=branch):
            m = jnp.max(s, axis=-1, keepdims=True)
            e = jnp.exp(s - m)
            l = jnp.sum(e, axis=-1, keepdims=True)
            pv = _dot(e.astype(BF16), _load_rows(v_ref, k_rows).astype(BF16))

            def unstack(a):
                out = a[(N_HEADS - 1) * qb:N_HEADS * qb]
                for h in range(N_HEADS - 2, -1, -1):
                    out = jnp.where(hmask_f32_ref[h] > 0.5, a[h * qb:(h + 1) * qb], out)
                return out

            acc_new, m_new, l_new = unstack(pv), unstack(m), unstack(l)
            if branch == 0:
                _store_rows(acc_ref, q_rows, acc_new)
                _store_rows(m_ref, q_rows, m_new)
                _store_rows(l_ref, q_rows, l_new)
            else:
                m_old = _load_rows(m_ref, q_rows)
                m_max = jnp.maximum(m_old, m_new)
                w_old = jnp.exp(m_old - m_max)
                w_new = jnp.exp(m_new - m_max)
                _store_rows(acc_ref, q_rows, _load_rows(acc_ref, q_rows) * w_old + acc_new * w_new)
                _store_rows(l_ref, q_rows, _load_rows(l_ref, q_rows) * w_old + l_new * w_new)
                _store_rows(m_ref, q_rows, m_max)

        def run(units, scores=scores, attend=attend):
            staged = [scores(r, c) for r, c in units]
            for item in staged:
                attend(*item)

        n_blocks = sub_len // qb
        if n_blocks >= DIL_UNITS_PER_ITER:
            for r in range(dil):
                def block_group(p, carry, r=r, run=run):
                    run([(r, DIL_UNITS_PER_ITER * p + u) for u in range(DIL_UNITS_PER_ITER)])
                    return carry

                lax.fori_loop(0, n_blocks // DIL_UNITS_PER_ITER, block_group, 0)
        else:
            for r0 in range(0, dil, DIL_UNITS_PER_ITER):
                run([(r0 + u, 0) for u in range(DIL_UNITS_PER_ITER)])
    for j in range(acc_ref.shape[0]):
        o_ref[:, j * LANES:(j + 1) * LANES] = (acc_ref[j] / l_ref[j]).astype(BF16)


def _dilated_attention(qd, kd, vd, batch, seq, masks):
    tiles, t, _ = qd.shape
    gw = tiles * LANES
    hmask, hmask_f32, _ = masks
    in_spec = pl.BlockSpec((tiles, seq, LANES), lambda b: (0, b, 0))
    stat = pltpu.VMEM((tiles, seq, LANES), F32)
    return pl.pallas_call(
        _dilated_kernel,
        grid=(batch,),
        in_specs=[
            in_spec, in_spec, in_spec,
            pl.BlockSpec(hmask.shape, lambda b: (0, 0, 0)),
            pl.BlockSpec(hmask_f32.shape, lambda b: (0, 0, 0)),
        ],
        out_specs=pl.BlockSpec((seq, gw), lambda b: (b, 0)),
        out_shape=jax.ShapeDtypeStruct((t, gw), BF16),
        scratch_shapes=[stat, stat, stat],
        compiler_params=_cparams("arbitrary"),
        name="dilated_attention",
    )(qd, kd, vd, hmask, hmask_f32)


def _outproj_kernel(ya_ref, yb_ref, yc_ref, yd_ref, w_ref, x_ref, mod_ref, g_ref, wr_ref,
                    x1_ref, h2_lo_ref, h2_hi_ref, lg_ref):
    gw = GROUP_WIDTH
    mix = _dot(ya_ref[...], w_ref[0:gw, :])
    mix = mix + _dot(yb_ref[...], w_ref[gw:2 * gw, :])
    mix = mix + _dot(yc_ref[...], w_ref[2 * gw:3 * gw, :])
    mix = mix + _dot(yd_ref[...], w_ref[3 * gw:4 * gw, :])
    mod = mod_ref[0]
    x1 = x_ref[...] + mod[2:3] * mix
    x1_ref[...] = x1
    ms = jnp.mean(x1 * x1, axis=-1, keepdims=True)
    h2 = (x1 * lax.rsqrt(ms + EPS) * g_ref[...]) * (1.0 + mod[4:5]) + mod[3:4]
    packed = _pack_bf16_pairs(h2)
    quarter = h2_lo_ref.shape[1]
    h2_lo_ref[...] = packed[:, :quarter]
    h2_hi_ref[...] = packed[:, quarter:]
    lg_ref[...] = _dot(h2.astype(BF16), wr_ref[...])


def _output_projection(ys, w_out_bf16, x2d, mod_l, g2, w_router, seq):
    t, d = x2d.shape
    tm = TOKEN_TILE
    tiles_per_batch = seq // tm
    gw = GROUP_WIDTH
    row_spec = lambda width: pl.BlockSpec((tm, width), lambda i: (i, 0))
    return pl.pallas_call(
        _outproj_kernel,
        grid=(t // tm,),
        in_specs=[
            row_spec(gw), row_spec(gw), row_spec(gw), row_spec(gw),
            pl.BlockSpec((d, d), lambda i: (0, 0)),
            row_spec(d),
            pl.BlockSpec((1, 6, d), lambda i: (i // tiles_per_batch, 0, 0)),
            pl.BlockSpec((1, d), lambda i: (0, 0)),
            pl.BlockSpec((d, ROUTER_LANES), lambda i: (0, 0)),
        ],
        out_specs=[row_spec(d), row_spec(d // 4), row_spec(d // 4), row_spec(ROUTER_LANES)],
        out_shape=[jax.ShapeDtypeStruct((t, d), F32), jax.ShapeDtypeStruct((t, d // 4), jnp.uint32),
                   jax.ShapeDtypeStruct((t, d // 4), jnp.uint32), jax.ShapeDtypeStruct((t, ROUTER_LANES), F32)],
        compiler_params=_cparams("arbitrary"),
        name="output_projection",
    )(*ys, w_out_bf16, x2d, mod_l, g2.reshape(1, d), w_router)


def _routing_kernel(lg_ref, info_ref, cnt_ref, carry_ref):
    tr = lg_ref.shape[0]

    @pl.when(pl.program_id(0) == 0)
    def _():
        carry_ref[...] = jnp.zeros_like(carry_ref)

    lg = lg_ref[...]
    lane = lax.broadcasted_iota(jnp.int32, lg.shape, 1).astype(F32)
    big = float(ROUTER_LANES)
    glog = jnp.where(lane < N_GROUPS, lg, -jnp.inf)
    gmax = jnp.max(glog, axis=-1, keepdims=True)
    p_grp = 1.0 / jnp.sum(jnp.exp(glog - gmax), axis=-1, keepdims=True)
    grp = jnp.min(jnp.where(glog == gmax, lane, big), axis=-1, keepdims=True)
    lo = N_GROUPS + EXPERTS_PER_GROUP * grp
    elog = jnp.where((lane >= lo) & (lane < lo + EXPERTS_PER_GROUP), lg, -jnp.inf)
    v1 = jnp.max(elog, axis=-1, keepdims=True)
    i1 = jnp.min(jnp.where(elog == v1, lane, big), axis=-1, keepdims=True)
    elog2 = jnp.where(lane == i1, -jnp.inf, elog)
    v2 = jnp.max(elog2, axis=-1, keepdims=True)
    i2 = jnp.min(jnp.where(elog2 == v2, lane, big), axis=-1, keepdims=True)
    d = jnp.exp(v2 - v1)
    gate1 = p_grp / (1.0 + d)
    gate2 = p_grp * d / (1.0 + d)
    sel1 = lane == i1
    sel2 = lane == i2
    sel = jnp.where(sel1 | sel2, 1.0, 0.0)
    row = lax.broadcasted_iota(jnp.int32, (tr, tr), 0)
    col = lax.broadcasted_iota(jnp.int32, (tr, tr), 1)
    before = jnp.where(col < row, 1.0, 0.0).astype(BF16)
    rank = _dot(before, sel.astype(BF16)) + carry_ref[...]
    r1 = jnp.sum(jnp.where(sel1, rank, 0.0), axis=-1, keepdims=True)
    r2 = jnp.sum(jnp.where(sel2, rank, 0.0), axis=-1, keepdims=True)
    carry_ref[...] += jnp.sum(sel, axis=0, keepdims=True)
    cnt_ref[...] = carry_ref[...]
    info = jnp.zeros_like(lg)
    for idx, val in enumerate((i1 - N_GROUPS, i2 - N_GROUPS, r1, r2, gate1, gate2)):
        info = jnp.where(lane == idx, val, info)
    info_ref[...] = info


def _routing(logits):
    t = logits.shape[0]
    tr = TOKEN_TILE
    return pl.pallas_call(
        _routing_kernel,
        grid=(t // tr,),
        in_specs=[pl.BlockSpec((tr, ROUTER_LANES), lambda i: (i, 0))],
        out_specs=[pl.BlockSpec((tr, ROUTER_LANES), lambda i: (i, 0)),
                   pl.BlockSpec((1, ROUTER_LANES), lambda i: (0, 0))],
        out_shape=[jax.ShapeDtypeStruct((t, ROUTER_LANES), F32),
                   jax.ShapeDtypeStruct((1, ROUTER_LANES), F32)],
        scratch_shapes=[pltpu.VMEM((1, ROUTER_LANES), F32)],
        compiler_params=_cparams("arbitrary"),
        name="routing",
    )(logits)


SC_GATHER_WINDOW = LANES


def _gather_rows(table, indices):
    n = indices.shape[0]
    width = table.shape[1]
    mesh = plsc.VectorSubcoreMesh(core_axis_name="core", subcore_axis_name="subcore")

    @pl.kernel(out_type=jax.ShapeDtypeStruct((n, width), table.dtype), mesh=mesh, scratch_types=[],
               name="moe_row_gather")
    def gather_kernel(table_hbm, idx_hbm, out_hbm):
        def body(idx_vmem, out_vmem):
            pltpu.sync_copy(table_hbm.at[idx_vmem.at[0]], out_vmem)

        pltpu.emit_pipeline(
            body,
            grid=(n // SC_GATHER_WINDOW,),
            in_specs=[pl.BlockSpec((1, SC_GATHER_WINDOW), index_map=lambda i: (0, i))],
            out_specs=[pl.BlockSpec((SC_GATHER_WINDOW, width), index_map=lambda i: (i, 0))],
            core_axis_name=("core", "subcore"),
            dimension_semantics=(pltpu.PARALLEL,),
        )(idx_hbm, out_hbm)

    return gather_kernel(table, indices.reshape(1, n))


def _scatter_rows(table, indices):
    n = indices.shape[0]
    rows, width = table.shape
    mesh = plsc.VectorSubcoreMesh(core_axis_name="core", subcore_axis_name="subcore")
    table_windows = rows // SC_GATHER_WINDOW

    @pl.kernel(out_type=jax.ShapeDtypeStruct((n, width), table.dtype), mesh=mesh, scratch_types=[],
               name="moe_row_scatter")
    def scatter_kernel(table_hbm, idx_hbm, out_hbm):
        def body(rows_vmem, idx_vmem):
            pltpu.sync_copy(rows_vmem, out_hbm.at[idx_vmem.at[0]])

        pltpu.emit_pipeline(
            body,
            grid=(n // SC_GATHER_WINDOW,),
            in_specs=[pl.BlockSpec((SC_GATHER_WINDOW, width), index_map=lambda i: (i % table_windows, 0)),
                      pl.BlockSpec((1, SC_GATHER_WINDOW), index_map=lambda i: (0, i))],
            out_specs=[],
            core_axis_name=("core", "subcore"),
            dimension_semantics=(pltpu.PARALLEL,),
        )(table_hbm, idx_hbm)

    return scatter_kernel(table, indices.reshape(1, n))


def _dispatch_targets(dest1, dest2, counts, pad_start, pad_end, n_rows):
    n_pad = n_rows - dest1.shape[0] - dest2.shape[0]
    pad_first = jnp.concatenate([pad_start + counts, pad_end[-1:]])
    pad_count = jnp.concatenate([pad_end - pad_start - counts, n_rows - pad_end[-1:]])
    cum = jnp.cumsum(pad_count)
    k = jnp.arange(n_pad, dtype=jnp.int32)
    seg = jnp.sum((cum[None, :] <= k[:, None]).astype(jnp.int32), axis=1)
    onehot = seg[:, None] == jnp.arange(cum.shape[0], dtype=jnp.int32)[None, :]
    offset = k - jnp.sum(jnp.where(onehot, (cum - pad_count)[None, :], 0), axis=1)
    pad_rows = jnp.sum(jnp.where(onehot, pad_first[None, :], 0), axis=1) + offset
    return jnp.concatenate([dest1, dest2, pad_rows])


def _expert_kernel(blk_e_ref, nvalid_ref, xs_lo_ref, xs_hi_ref, wg_ref, wu_ref, wd_ref, ys_lo_ref, ys_hi_ref,
                   wg_bf, wu_bf, wd_bf):
    j = pl.program_id(0)
    half = ys_lo_ref.shape[1]

    @pl.when((j == 0) | (blk_e_ref[j] != blk_e_ref[jnp.maximum(j - 1, 0)]))
    def _():
        wg_bf[...] = wg_ref[0, 0].astype(BF16)
        wu_bf[...] = wu_ref[0, 0].astype(BF16)
        wd_bf[...] = wd_ref[0, 0].astype(BF16)

    @pl.when(j < nvalid_ref[0])
    def _():
        xb = _unpack_bf16_pairs(jnp.concatenate([xs_lo_ref[...], xs_hi_ref[...]], axis=1)).astype(BF16)
        gate = _dot(xb, wg_bf[...])
        up = _dot(xb, wu_bf[...])
        hdn = (gate * jax.nn.sigmoid(gate)) * up
        packed = _pack_bf16_pairs(_dot(hdn.astype(BF16), wd_bf[...]))
        ys_lo_ref[...] = packed[:, :half]
        ys_hi_ref[...] = packed[:, half:]

    @pl.when(j >= nvalid_ref[0])
    def _():
        ys_lo_ref[...] = jnp.zeros_like(ys_lo_ref)
        ys_hi_ref[...] = jnp.zeros_like(ys_hi_ref)


def _expert_mlp(xs_lo, xs_hi, blk_e, nvalid, w_gate, w_up, w_down, layer):
    n_rows, quarter = xs_lo.shape
    nblk = n_rows // MOE_BLOCK
    d, de = w_gate.shape[2:]

    def x_map(j, be, nv):
        return (jnp.minimum(j, nv[0] - 1), 0)

    w_map = lambda j, be, nv: (layer, be[j], 0, 0)
    grid_spec = pltpu.PrefetchScalarGridSpec(
        num_scalar_prefetch=2,
        grid=(nblk,),
        in_specs=[
            pl.BlockSpec((MOE_BLOCK, quarter), x_map),
            pl.BlockSpec((MOE_BLOCK, quarter), x_map),
            pl.BlockSpec((1, 1, d, de), w_map),
            pl.BlockSpec((1, 1, d, de), w_map),
            pl.BlockSpec((1, 1, de, d), w_map),
        ],
        out_specs=[pl.BlockSpec((MOE_BLOCK, quarter), lambda j, be, nv: (j, 0))] * 2,
        scratch_shapes=[pltpu.VMEM((d, de), BF16), pltpu.VMEM((d, de), BF16), pltpu.VMEM((de, d), BF16)],
    )
    return pl.pallas_call(
        _expert_kernel,
        grid_spec=grid_spec,
        out_shape=[jax.ShapeDtypeStruct((n_rows, quarter), jnp.uint32)] * 2,
        compiler_params=_cparams("arbitrary"),
        name="expert_mlp",
    )(blk_e, nvalid, xs_lo, xs_hi, w_gate, w_up, w_down)


def _final_combine_kernel(x_ref, mod_ref, info_ref, gf_ref, y1_lo_ref, y1_hi_ref, y2_lo_ref, y2_hi_ref, o_ref):
    x2 = _moe_residual(x_ref, mod_ref, info_ref, y1_lo_ref, y1_hi_ref, y2_lo_ref, y2_hi_ref)
    ms = jnp.mean(x2 * x2, axis=-1, keepdims=True)
    o_ref[...] = x2 * lax.rsqrt(ms + EPS) * gf_ref[...]


def _gather_expert_rows(ys, dest1, dest2):
    dest = jnp.concatenate([dest1, dest2])
    return _gather_rows(ys[0], dest), _gather_rows(ys[1], dest)


def _final_combine(x1, mod_l, info, g_lo, g_hi, g_final, seq):
    t, d = x1.shape
    tc = TOKEN_TILE
    tiles_per_batch = seq // tc
    n_tiles = t // tc
    quarter = g_lo.shape[1]
    row_spec = lambda width: pl.BlockSpec((tc, width), lambda i: (i, 0))
    slot2_spec = pl.BlockSpec((tc, quarter), lambda i: (i + n_tiles, 0))
    return pl.pallas_call(
        _final_combine_kernel,
        grid=(n_tiles,),
        in_specs=[
            row_spec(d),
            pl.BlockSpec((1, 6, d), lambda i: (i // tiles_per_batch, 0, 0)),
            row_spec(ROUTER_LANES),
            pl.BlockSpec((1, d), lambda i: (0, 0)),
            row_spec(quarter), row_spec(quarter), slot2_spec, slot2_spec,
        ],
        out_specs=row_spec(d),
        out_shape=jax.ShapeDtypeStruct((t, d), F32),
        compiler_params=_cparams("arbitrary"),
        name="moe_combine",
    )(x1, mod_l, info, g_final.reshape(1, d), g_lo, g_hi, g_lo, g_hi)


def _router_weights(w_rg, w_re):
    d = w_rg.shape[0]
    w_experts = jnp.transpose(w_re, (1, 0, 2)).reshape(d, N_EXPERTS)
    pad = jnp.zeros((d, ROUTER_LANES - N_GROUPS - N_EXPERTS), F32)
    return jnp.concatenate([w_rg, w_experts, pad], axis=1)


def _block_layout(counts_row, n_blocks):
    counts = counts_row[0, N_GROUPS:N_GROUPS + N_EXPERTS].astype(jnp.int32)
    padded = (counts + MOE_BLOCK - 1) // MOE_BLOCK * MOE_BLOCK
    pad_end = jnp.cumsum(padded)
    pad_start = pad_end - padded
    starts = jnp.arange(n_blocks, dtype=jnp.int32) * MOE_BLOCK
    blk_e = jnp.minimum(jnp.sum((pad_end[None, :] <= starts[:, None]).astype(jnp.int32), axis=1), N_EXPERTS - 1)
    nvalid = (pad_end[-1:] // MOE_BLOCK).astype(jnp.int32)
    return counts, pad_start, pad_end, blk_e, nvalid


def _destinations(info, pad_start):
    ids = info[:, 0:4].astype(jnp.int32)
    experts = jnp.arange(N_EXPERTS, dtype=jnp.int32)[None, :]

    def segment_start(e):
        return jnp.sum(jnp.where(e[:, None] == experts, pad_start[None, :], 0), axis=1)

    return segment_start(ids[:, 0]) + ids[:, 2], segment_start(ids[:, 1]) + ids[:, 3]


def kernel(x, c, w_ada, b_ada, g_norm1, g_norm2, w_in, diff_lambda, diff_subln, na_rpb, conv_dw, conv_b,
           conv_norm_g, conv_norm_b, w_out, w_router_group, w_router_expert, w_exp_gate, w_exp_up,
           w_exp_down, g_final):
    batch, seq, d = x.shape
    depth = w_ada.shape[0]
    t = batch * seq
    assert d == D_MODEL and seq == GRID_ROWS * GRID_W
    n_rows = t * 2 + N_EXPERTS * MOE_BLOCK

    mod = _ada_modulation(c, w_ada, b_ada).reshape(depth, batch, 6, d)
    rope_a = _rope_tables(seq, DIFF_DH)
    rope_d = _rope_tables(seq, HEAD_DIM)
    masks = _head_masks()
    gmat = _block_diag_ones(GROUP_WIDTH, HEAD_DIM)

    x2d = x.reshape(t, d)
    pending = None
    for l in range(depth):
        mod_l = mod[l]
        projected = _input_projection(x2d, mod_l, g_norm1[l], w_in[l].astype(BF16), rope_a, rope_d, seq, pending)
        if pending is not None:
            x2d, projected = projected[0], projected[1:]
        qa, ka, va, qb, kb, vb, pc, qd, kd, vd = projected
        ya = _diff_attention(qa, ka, va, diff_lambda[l], diff_subln[l], l, batch, seq, masks, gmat)
        yb = _neighborhood_attention(qb, kb, vb, _na_bias_table(na_rpb[l]), batch, seq, masks)
        yc = _conformer_conv(pc, conv_dw[l], conv_b[l], conv_norm_g[l], conv_norm_b[l], batch, seq, gmat)
        yd = _dilated_attention(qd, kd, vd, batch, seq, masks)
        x1, h2_lo, h2_hi, logits = _output_projection(
            (ya, yb, yc, yd), w_out[l].astype(BF16), x2d, mod_l, g_norm2[l],
            _router_weights(w_router_group[l], w_router_expert[l]).astype(BF16), seq)
        info, counts = _routing(logits)
        counts, pad_start, pad_end, blk_e, nvalid = _block_layout(counts, n_rows // MOE_BLOCK)
        dest1, dest2 = _destinations(info, pad_start)
        targets = _dispatch_targets(dest1, dest2, counts, pad_start, pad_end, n_rows)
        ys = _expert_mlp(_scatter_rows(h2_lo, targets), _scatter_rows(h2_hi, targets), blk_e, nvalid,
                         w_exp_gate, w_exp_up, w_exp_down, l)
        pending = (x1, mod_l, info, *_gather_expert_rows(ys, dest1, dest2))
    return _final_combine(*pending, g_final, seq).reshape(batch, seq, d)
```
